```python
import jax, jax.numpy as jnp
from jax import lax
import numpy as np

D_MODEL = 2048
BATCH = 16
SEQ = 256
DEPTH = 2
DEC_BATCH = 8
DEC_SEQ = 1024
PAST_LEN = 256

GRID_W = 64
D_CONV = D_MODEL // 4
D_POOL = D_MODEL // 4
D_RET = D_MODEL // 2
N_RET_HEADS = 8
HEAD_DIM = D_RET // N_RET_HEADS
CONV_WIDTH = 3
POOL_WINDOWS = (2, 4, 8, 16)
N_POOL_GROUPS = len(POOL_WINDOWS)
POOL_GROUP_DIM = D_POOL // N_POOL_GROUPS
CHUNK = 128
ROPE_BASE = 10000.0
N_EXPERTS = 16
N_EXPERT_GROUPS = 4
EXPERTS_PER_GROUP = N_EXPERTS // N_EXPERT_GROUPS
TOP_K = 2
D_EXPERT = D_MODEL // 4
D_IN_PROJ = 3 * D_CONV + D_POOL + 4 * D_RET
DEEPNORM_ALPHA = (2.0 * DEPTH) ** 0.25
DEEPNORM_BETA = (8.0 * DEPTH) ** -0.25
LN_EPS = 1e-5

kernel_name = "hybrid_diffusion_conv_pool_retention_moe_step"

F32 = jnp.float32


def layer_norm(x, g, b):
    xf = x.astype(F32)
    mu = jnp.mean(xf, axis=-1, keepdims=True)
    var = jnp.mean(jnp.square(xf - mu), axis=-1, keepdims=True)
    y = (xf - mu) * lax.rsqrt(var + LN_EPS)
    return (y * g.astype(F32) + b.astype(F32)).astype(x.dtype)


def short_conv_mixer(b_gate, c_gate, v, conv_w):
    u = c_gate * v
    T = u.shape[1]
    up = jnp.pad(u, ((0, 0), (1, 1), (0, 0)))
    conv = up[:, 0:T] * conv_w[0] + up[:, 1:T + 1] * conv_w[1] + up[:, 2:T + 2] * conv_w[2]
    return b_gate * conv


def pool_mixer(p, pool_w, pool_scale):
    B, T, _ = p.shape
    pf = p.astype(F32)
    csum = jnp.concatenate([jnp.zeros((B, 1, D_POOL), F32), jnp.cumsum(pf, axis=1)], axis=1)
    t = jnp.arange(T)
    groups = []
    for gi, w in enumerate(POOL_WINDOWS):
        lo = jnp.maximum(t - w // 2, 0)
        hi = jnp.minimum(t + w // 2, T)
        sl = slice(gi * POOL_GROUP_DIM, (gi + 1) * POOL_GROUP_DIM)
        cs = csum[:, :, sl]
        win_sum = jnp.take(cs, hi, axis=1) - jnp.take(cs, lo, axis=1)
        cnt = (hi - lo).astype(F32)[None, :, None]
        groups.append(win_sum / cnt - pf[:, :, sl])
    pooled = jnp.stack(groups, axis=2)
    y = jnp.einsum('btgc,gcd->btgd', pooled, pool_w.astype(F32)).reshape(B, T, D_POOL)
    return (y * pool_scale.astype(F32)).astype(p.dtype)


def rope_2d_tables(T):
    rows = T // GRID_W
    row = jnp.repeat(jnp.arange(rows), GRID_W).astype(F32)
    col = jnp.tile(jnp.arange(GRID_W), rows).astype(F32)
    n_freq = HEAD_DIM // 4
    inv_freq = ROPE_BASE ** (-jnp.arange(n_freq, dtype=F32) / n_freq)
    ang = jnp.concatenate([row[:, None] * inv_freq[None], col[:, None] * inv_freq[None]], axis=-1)
    return jnp.cos(ang), jnp.sin(ang)


def apply_rope(x, cos, sin):
    half = HEAD_DIM // 2
    xf = x.astype(F32)
    x1, x2 = xf[..., :half], xf[..., half:]
    c, s = cos[None, :, None, :], sin[None, :, None, :]
    return jnp.concatenate([x1 * c - x2 * s, x1 * s + x2 * c], axis=-1).astype(x.dtype)


def retention_scan(q, k, v, log_gamma, s0):
    B, T, H, _ = q.shape
    N = T // CHUNK

    def chunks(a):
        return jnp.moveaxis(a.astype(F32).reshape(B, N, CHUNK, H, a.shape[-1]), 1, 0)

    pos = jnp.arange(CHUNK, dtype=F32)
    lg = log_gamma.astype(F32)
    diff = pos[:, None] - pos[None, :]
    decay = jnp.where(diff[None] >= 0, jnp.exp(lg[:, None, None] * jnp.maximum(diff, 0.0)[None]), 0.0)
    xi = jnp.exp((pos[:, None] + 1.0) * lg[None, :])
    zeta = jnp.exp((CHUNK - 1.0 - pos)[:, None] * lg[None, :])
    g_chunk = jnp.exp(CHUNK * lg)

    def step(S, qkv):
        qc, kc, vc = qkv
        scores = jnp.einsum('bnhd,bmhd->bhnm', qc, kc) * decay[None]
        inner = jnp.einsum('bhnm,bmhe->bnhe', scores, vc)
        cross = jnp.einsum('bnhd,bhde->bnhe', qc, S) * xi[None, :, :, None]
        S_new = g_chunk[None, :, None, None] * S + jnp.einsum('bmhd,bmhe->bhde', kc * zeta[None, :, :, None], vc)
        return S_new, inner + cross

    S_final, o = lax.scan(step, s0.astype(F32), (chunks(q), chunks(k), chunks(v)))
    o = jnp.moveaxis(o, 0, 1).reshape(B, T, H, v.shape[-1])
    return o, S_final


def bidir_retention(q, k, v, decay_logit, s0):
    lg = jax.nn.log_sigmoid(decay_logit.astype(F32))
    o_f, s_f = retention_scan(q, k, v, lg[0], s0[:, 0])
    o_b, s_b = retention_scan(jnp.flip(q, 1), jnp.flip(k, 1), jnp.flip(v, 1), lg[1], s0[:, 1])
    return o_f + jnp.flip(o_b, 1), jnp.stack([s_f, s_b], axis=1)


def mixer_sublayer(h, s0, rope, w_in, w_out, conv_w, pool_w, pool_scale, ret_decay_logit, ret_gn_gain):
    B, T, _ = h.shape
    z = h @ w_in
    offs = [D_CONV, 2 * D_CONV, 3 * D_CONV, 3 * D_CONV + D_POOL,
            3 * D_CONV + D_POOL + D_RET, 3 * D_CONV + D_POOL + 2 * D_RET, 3 * D_CONV + D_POOL + 3 * D_RET]
    cb, cc, cv, pp, q, k, v, g = jnp.split(z, offs, axis=-1)
    y_conv = short_conv_mixer(cb, cc, cv, conv_w)
    y_pool = pool_mixer(pp, pool_w, pool_scale)
    q = q.reshape(B, T, N_RET_HEADS, HEAD_DIM)
    k = k.reshape(B, T, N_RET_HEADS, HEAD_DIM) * (HEAD_DIM ** -0.5)
    v = v.reshape(B, T, N_RET_HEADS, HEAD_DIM)
    if rope is not None:
        cos, sin = rope
        q = apply_rope(q, cos, sin)
        k = apply_rope(k, cos, sin)
    o, s_final = bidir_retention(q, k, v, ret_decay_logit, s0)
    mu = jnp.mean(o, axis=-1, keepdims=True)
    var = jnp.mean(jnp.square(o - mu), axis=-1, keepdims=True)
    o = (o - mu) * lax.rsqrt(var + LN_EPS) * ret_gn_gain.astype(F32).reshape(N_RET_HEADS, HEAD_DIM)
    y_ret = jax.nn.silu(g) * o.reshape(B, T, D_RET).astype(h.dtype)
    y = jnp.concatenate([y_conv, y_pool, y_ret], axis=-1) @ w_out
    return y, s_final


def moe_ffn(h, w_router, router_bias, w_gate, w_up, w_down):
    B, T, D = h.shape
    t = h.reshape(B * T, D)
    scores = jax.nn.softmax((t @ w_router).astype(F32), axis=-1)
    biased = (scores + router_bias.astype(F32)).reshape(-1, N_EXPERT_GROUPS, EXPERTS_PER_GROUP)
    group_score = lax.top_k(biased, TOP_K)[0].sum(-1)
    _, g_sel = lax.top_k(group_score, 1)
    in_group = jnp.take_along_axis(biased, g_sel[:, :, None], axis=1)[:, 0]
    _, e_local = lax.top_k(in_group, TOP_K)
    e_idx = g_sel * EXPERTS_PER_GROUP + e_local
    w = jnp.take_along_axis(scores, e_idx, axis=1)
    w = w / jnp.sum(w, axis=-1, keepdims=True)
    combine = jnp.einsum('nk,nke->en', w, jax.nn.one_hot(e_idx, N_EXPERTS, dtype=F32)).astype(h.dtype)
    out = jnp.zeros_like(t)
    for e in range(N_EXPERTS):
        y = (jax.nn.silu(t @ w_gate[e]) * (t @ w_up[e])) @ w_down[e]
        out = out + combine[e][:, None] * y
    return out.reshape(B, T, D)


def trunk_layer(x, ada, s0, rope, w_in, w_out, conv_w, pool_w, pool_scale, ret_decay_logit, ret_gn_gain,
                ln1_g, ln1_b, ln2_g, ln2_b, w_router, router_bias, w_gate, w_up, w_down):
    shift1, scale1, gate1, shift2, scale2, gate2 = [a[:, None, :] for a in jnp.split(ada, 6, axis=-1)]
    h = x * (1.0 + scale1) + shift1
    y, s_final = mixer_sublayer(h, s0, rope, w_in, w_out, conv_w, pool_w, pool_scale, ret_decay_logit, ret_gn_gain)
    x = layer_norm(DEEPNORM_ALPHA * x + gate1 * y, ln1_g, ln1_b)
    h2 = x * (1.0 + scale2) + shift2
    y2 = moe_ffn(h2, w_router, router_bias, w_gate, w_up, w_down)
    x = layer_norm(DEEPNORM_ALPHA * x + gate2 * y2, ln2_g, ln2_b)
    return x, s_final


def setup_inputs(seed: int = 0) -> dict:
    key = jax.random.key(seed)
    ks = jax.random.split(key, 24)
    nrm = jax.random.normal
    d = D_MODEL
    base_decay = 1.0 - 2.0 ** (-5.0 - np.arange(N_RET_HEADS, dtype=np.float32))
    base_logit = jnp.asarray(np.log(base_decay / (1.0 - base_decay)), F32)
    return {
        "x_prompt": nrm(ks[0], (BATCH, SEQ, d), F32),
        "x_sample": nrm(ks[1], (DEC_BATCH, DEC_SEQ, d), F32),
        "state_retention": nrm(ks[2], (DEC_BATCH, DEPTH, 2, N_RET_HEADS, HEAD_DIM, HEAD_DIM), F32)
                           * (PAST_LEN ** 0.5) * (HEAD_DIM ** -0.5),
        "c": nrm(ks[3], (DEC_BATCH, d), F32),
        "c_ctx": nrm(ks[4], (d,), F32),
        "w_ada": nrm(ks[5], (DEPTH, d, 6 * d), F32) * (0.5 * d ** -0.5),
        "b_ada": nrm(ks[6], (DEPTH, 6 * d), F32) * 0.02,
        "w_in": nrm(ks[7], (DEPTH, d, D_IN_PROJ), F32) * d ** -0.5,
        "w_out": nrm(ks[8], (DEPTH, d, d), F32) * (d ** -0.5) * DEEPNORM_BETA,
        "conv_w": nrm(ks[9], (DEPTH, CONV_WIDTH, D_CONV), F32) * CONV_WIDTH ** -0.5,
        "pool_w": nrm(ks[10], (DEPTH, N_POOL_GROUPS, POOL_GROUP_DIM, POOL_GROUP_DIM), F32) * POOL_GROUP_DIM ** -0.5,
        "pool_scale": 1.0 + 0.02 * nrm(ks[11], (DEPTH, D_POOL), F32),
        "ret_decay_logit": base_logit[None, None, :] + 0.01 * nrm(ks[12], (DEPTH, 2, N_RET_HEADS), F32),
        "ret_gn_gain": 1.0 + 0.02 * nrm(ks[13], (DEPTH, D_RET), F32),
        "ln1_g": 1.0 + 0.02 * nrm(ks[14], (DEPTH, d), F32),
        "ln1_b": 0.02 * nrm(ks[15], (DEPTH, d), F32),
        "ln2_g": 1.0 + 0.02 * nrm(ks[16], (DEPTH, d), F32),
        "ln2_b": 0.02 * nrm(ks[17], (DEPTH, d), F32),
        "w_router": nrm(ks[18], (d, N_EXPERTS), F32) * d ** -0.5,
        "router_bias": 0.01 * nrm(ks[19], (N_EXPERTS,), F32),
        "w_gate": nrm(ks[20], (DEPTH, N_EXPERTS, d, D_EXPERT), F32) * d ** -0.5,
        "w_up": nrm(ks[21], (DEPTH, N_EXPERTS, d, D_EXPERT), F32) * d ** -0.5,
        "w_down": nrm(ks[22], (DEPTH, N_EXPERTS, D_EXPERT, d), F32) * (D_EXPERT ** -0.5) * DEEPNORM_BETA,
    }


def reference(x_prompt, x_sample, state_retention, c, c_ctx, w_ada, b_ada, w_in, w_out, conv_w, pool_w,
              pool_scale, ret_decay_logit, ret_gn_gain, ln1_g, ln1_b, ln2_g, ln2_b, w_router, router_bias,
              w_gate, w_up, w_down):
    xp, xs = x_prompt, x_sample
    rope = rope_2d_tables(x_sample.shape[1])
    s0_ctx = jnp.zeros((x_prompt.shape[0], 2, N_RET_HEADS, HEAD_DIM, HEAD_DIM), x_prompt.dtype)
    states = []
    for l in range(DEPTH):
        ada_ctx = jax.nn.silu(c_ctx)[None, :] @ w_ada[l] + b_ada[l]
        ada_lat = jax.nn.silu(c) @ w_ada[l] + b_ada[l]
        xp, st = trunk_layer(xp, ada_ctx, s0_ctx, None, w_in[l], w_out[l], conv_w[l], pool_w[l], pool_scale[l],
                             ret_decay_logit[l], ret_gn_gain[l], ln1_g[l], ln1_b[l], ln2_g[l], ln2_b[l],
                             w_router, router_bias, w_gate[l], w_up[l], w_down[l])
        states.append(st)
        xs, _ = trunk_layer(xs, ada_lat, state_retention[:, l], rope, w_in[l], w_out[l], conv_w[l], pool_w[l],
                            pool_scale[l], ret_decay_logit[l], ret_gn_gain[l], ln1_g[l], ln1_b[l], ln2_g[l],
                            ln2_b[l], w_router, router_bias, w_gate[l], w_up[l], w_down[l])
    new_state_retention = jnp.stack(states, axis=1).astype(x_prompt.dtype)
    return (xp, xs, new_state_retention)
```

```python
import functools

import jax
import jax.numpy as jnp
from jax import lax
from jax.experimental import pallas as pl
from jax.experimental.pallas import tpu as pltpu

F32 = jnp.float32
BF16 = jnp.bfloat16
I32 = jnp.int32

D_MODEL = 2048
N_CTX_SEQ, T_CTX = 16, 256
N_LAT_SEQ, T_LAT = 8, 1024
DEPTH = 2
M_CTX = N_CTX_SEQ * T_CTX
M_LAT = N_LAT_SEQ * T_LAT
M_TOK = M_CTX + M_LAT

GRID_W = 64
D_CONV = D_MODEL // 4
D_POOL = D_MODEL // 4
D_RET = D_MODEL // 2
N_RET_HEADS = 8
HEAD_DIM = D_RET // N_RET_HEADS
POOL_WINDOWS = (2, 4, 8, 16)
POOL_GROUP_DIM = D_POOL // len(POOL_WINDOWS)
CHUNK = 128
ROPE_BASE = 10000.0
N_EXPERTS = 16
EXPERTS_PER_GROUP = 4
N_EXPERT_GROUPS = N_EXPERTS // EXPERTS_PER_GROUP
D_EXPERT = D_MODEL // 4
D_IN_PROJ = 3 * D_CONV + D_POOL + 4 * D_RET
DEEPNORM_ALPHA = (2.0 * DEPTH) ** 0.25
LN_EPS = 1e-5
ADA_ROWS = 16

LANES = 128
VMEM_LIMIT = 56 * 1024 * 1024

TM_IN = 1024
TN_IN = 1024
TM_OUT = 256
TM_EXP = 256
N_PAIR = 2 * M_TOK
NP_EXP = N_PAIR + N_EXPERTS * TM_EXP
NT_EXP = NP_EXP // TM_EXP


def _cparams(sem):
    return pltpu.CompilerParams(dimension_semantics=sem, vmem_limit_bytes=VMEM_LIMIT)


def _silu(x):
    return x * jax.nn.sigmoid(x)


def _ada_row(i, tm):
    n_ctx_tiles = M_CTX // tm
    per_batch = T_LAT // tm
    return jnp.where(i < n_ctx_tiles, 0, 1 + (i - n_ctx_tiles) // per_batch)


def _ada_kernel(c_ref, w_ref, b_ref, o_ref):
    s = _silu(c_ref[...]).astype(BF16)
    o_ref[...] = jnp.dot(s, w_ref[...].astype(BF16), preferred_element_type=F32) + b_ref[...]


def _ada_table(c_all, w_ada, b_ada):
    tn = 1024
    n6 = 6 * D_MODEL
    return pl.pallas_call(
        _ada_kernel,
        grid=(DEPTH, n6 // tn),
        in_specs=[
            pl.BlockSpec((ADA_ROWS, D_MODEL), lambda l, j: (0, 0)),
            pl.BlockSpec((None, D_MODEL, tn), lambda l, j: (l, 0, j)),
            pl.BlockSpec((None, 1, tn), lambda l, j: (l, 0, j)),
        ],
        out_specs=pl.BlockSpec((None, ADA_ROWS, tn), lambda l, j: (l, 0, j)),
        out_shape=jax.ShapeDtypeStruct((DEPTH, ADA_ROWS, n6), F32),
        compiler_params=_cparams(("arbitrary", "arbitrary")),
        name="ada_table",
    )(c_all, w_ada, b_ada.reshape(DEPTH, 1, n6))


def _inproj_kernel(x_ref, sc_ref, sh_ref, w_ref, o_ref, h_scr):
    @pl.when(pl.program_id(1) == 0)
    def _():
        h_scr[...] = (x_ref[...] * (1.0 + sc_ref[...]) + sh_ref[...]).astype(BF16)

    o_ref[...] = jnp.dot(h_scr[...], w_ref[...], preferred_element_type=F32).astype(BF16)


def _in_proj(x, ada5, w_in_bf16):
    ada_spec = lambda chunk: pl.BlockSpec(
        (None, None, 1, D_MODEL), lambda i, j: (_ada_row(i, TM_IN), chunk, 0, 0))
    return pl.pallas_call(
        _inproj_kernel,
        grid=(M_TOK // TM_IN, D_IN_PROJ // TN_IN),
        in_specs=[
            pl.BlockSpec((TM_IN, D_MODEL), lambda i, j: (i, 0)),
            ada_spec(1),
            ada_spec(0),
            pl.BlockSpec((D_MODEL, TN_IN), lambda i, j: (0, j)),
        ],
        out_specs=pl.BlockSpec((TM_IN, TN_IN), lambda i, j: (i, j)),
        out_shape=jax.ShapeDtypeStruct((M_TOK, D_IN_PROJ), BF16),
        scratch_shapes=[pltpu.VMEM((TM_IN, D_MODEL), BF16)],
        compiler_params=_cparams(("arbitrary", "arbitrary")),
        name="in_proj",
    )(x, ada5, ada5, w_in_bf16)


def _convpool_kernel(*refs, T, aliased):
    if aliased:
        z_ref, cw_ref, pw_ref, ps_ref, _, o_ref, band_ref = refs
    else:
        z_ref, cw_ref, pw_ref, ps_ref, o_ref, band_ref = refs

    @pl.when(pl.program_id(0) == 0)
    def _():
        row = lax.broadcasted_iota(I32, (T, T), 0)
        col = lax.broadcasted_iota(I32, (T, T), 1)
        d = col - row
        for gi, w in enumerate(POOL_WINDOWS):
            band_ref[gi] = jnp.where((d >= -(w // 2)) & (d < w // 2), 1.0, 0.0).astype(BF16)

    cb = z_ref[:, 0:D_CONV].astype(F32)
    cc = z_ref[:, D_CONV:2 * D_CONV].astype(F32)
    cv = z_ref[:, 2 * D_CONV:3 * D_CONV].astype(F32)
    u = cc * cv
    t = lax.broadcasted_iota(I32, (T, D_CONV), 0)
    u_prev = jnp.where(t == 0, 0.0, pltpu.roll(u, 1, 0))
    u_next = jnp.where(t == T - 1, 0.0, pltpu.roll(u, T - 1, 0))
    conv = u_prev * cw_ref[0:1, :] + u * cw_ref[1:2, :] + u_next * cw_ref[2:3, :]
    o_ref[:, 0:D_CONV] = (cb * conv).astype(BF16)

    tt = lax.broadcasted_iota(I32, (T, POOL_GROUP_DIM), 0)
    for gi, w in enumerate(POOL_WINDOWS):
        lo = 3 * D_CONV + gi * POOL_GROUP_DIM
        p = z_ref[:, lo:lo + POOL_GROUP_DIM]
        win = jnp.dot(band_ref[gi], p, preferred_element_type=F32)
        cnt = (jnp.minimum(tt + w // 2, T) - jnp.maximum(tt - w // 2, 0)).astype(F32)
        pooled = win / cnt - p.astype(F32)
        y = jnp.dot(pooled.astype(BF16), pw_ref[gi].astype(BF16), preferred_element_type=F32)
        y = y * ps_ref[:, gi * POOL_GROUP_DIM:(gi + 1) * POOL_GROUP_DIM]
        o_ref[:, D_CONV + gi * POOL_GROUP_DIM:D_CONV + (gi + 1) * POOL_GROUP_DIM] = y.astype(BF16)


def _conv_pool(z, conv_w, pool_w, pool_scale, prev, *, T, n_seq, row_block0):
    aliased = prev is not None
    in_specs = [
        pl.BlockSpec((T, D_IN_PROJ - 4 * D_RET), lambda b: (row_block0 + b, 0)),
        pl.BlockSpec((3, D_CONV), lambda b: (0, 0)),
        pl.BlockSpec((len(POOL_WINDOWS), POOL_GROUP_DIM, POOL_GROUP_DIM), lambda b: (0, 0, 0)),
        pl.BlockSpec((1, D_POOL), lambda b: (0, 0)),
    ]
    args = [z, conv_w, pool_w, pool_scale.reshape(1, D_POOL)]
    if aliased:
        in_specs.append(pl.BlockSpec(memory_space=pl.ANY))
        args.append(prev)
    return pl.pallas_call(
        functools.partial(_convpool_kernel, T=T, aliased=aliased),
        grid=(n_seq,),
        in_specs=in_specs,
        out_specs=pl.BlockSpec((T, D_CONV + D_POOL), lambda b: (row_block0 + b, 0)),
        out_shape=jax.ShapeDtypeStruct((M_TOK, D_CONV + D_POOL), BF16),
        scratch_shapes=[pltpu.VMEM((len(POOL_WINDOWS), T, T), BF16)],
        input_output_aliases={4: 0} if aliased else {},
        compiler_params=_cparams(("arbitrary",)),
        name="conv_pool_T%d" % T,
    )(*args)


def _log_sigmoid(x):
    return jnp.minimum(x, 0.0) - jnp.log1p(jnp.exp(-jnp.abs(x)))


def _retention_kernel(*refs, T, latent):
    if latent:
        (q_ref, k_ref, v_ref, g_ref, cos_ref, sin_ref, dl_ref, gain_ref, s0_ref, _,
         y_ref, kr_scr, u_scr, s_scr) = refs
    else:
        (q_ref, k_ref, v_ref, g_ref, cos_ref, sin_ref, dl_ref, gain_ref,
         y_ref, sfin_ref, kr_scr, u_scr, s_scr) = refs
    n_chunks = T // CHUNK
    half = HEAD_DIM // 2

    lg_f = _log_sigmoid(dl_ref[0])
    lg_b = _log_sigmoid(dl_ref[1])
    row = lax.broadcasted_iota(I32, (CHUNK, CHUNK), 0).astype(F32)
    col = lax.broadcasted_iota(I32, (CHUNK, CHUNK), 1).astype(F32)
    diff = row - col
    decay = (jnp.where(diff >= 0, jnp.exp(lg_f * jnp.maximum(diff, 0.0)), 0.0)
             + jnp.where(diff <= 0, jnp.exp(lg_b * jnp.maximum(-diff, 0.0)), 0.0))
    xi_f = jnp.exp(lg_f * (row + 1.0))
    xi_b = jnp.exp(lg_b * (CHUNK - row))
    zeta_f = jnp.exp(lg_f * (CHUNK - 1.0 - row))
    zeta_b = jnp.exp(lg_b * row)
    g_f = jnp.exp(lg_f * CHUNK)
    g_b = jnp.exp(lg_b * CHUNK)

    def rope(x, sl):
        return x * cos_ref[sl, :] + pltpu.roll(x, half, 1) * sin_ref[sl, :]

    tn_dims = (((0,), (0,)), ((), ()))
    nt_dims = (((1,), (1,)), ((), ()))

    for c in range(n_chunks):
        sl = slice(c * CHUNK, (c + 1) * CHUNK)
        kr = rope(k_ref[sl, :].astype(F32), sl) * (HEAD_DIM ** -0.5)
        kr_scr[sl, :] = kr.astype(BF16)
        v = v_ref[sl, :]
        u_scr[0, c] = lax.dot_general((kr * zeta_f).astype(BF16), v, tn_dims, preferred_element_type=F32)
        u_scr[1, c] = lax.dot_general((kr * zeta_b).astype(BF16), v, tn_dims, preferred_element_type=F32)

    s = s0_ref[0] if latent else jnp.zeros((HEAD_DIM, HEAD_DIM), F32)
    for c in range(n_chunks):
        s_scr[0, c] = s.astype(BF16)
        s = g_f * s + u_scr[0, c]
    if not latent:
        sfin_ref[0] = s
    s = s0_ref[1] if latent else jnp.zeros((HEAD_DIM, HEAD_DIM), F32)
    for c in reversed(range(n_chunks)):
        s_scr[1, c] = s.astype(BF16)
        s = g_b * s + u_scr[1, c]
    if not latent:
        sfin_ref[1] = s

    for c in range(n_chunks):
        sl = slice(c * CHUNK, (c + 1) * CHUNK)
        qr = rope(q_ref[sl, :].astype(F32), sl)
        scores = lax.dot_general(qr.astype(BF16), kr_scr[sl, :], nt_dims, preferred_element_type=F32)
        o = jnp.dot((scores * decay).astype(BF16), v_ref[sl, :], preferred_element_type=F32)
        o += jnp.dot((qr * xi_f).astype(BF16), s_scr[0, c], preferred_element_type=F32)
        o += jnp.dot((qr * xi_b).astype(BF16), s_scr[1, c], preferred_element_type=F32)
        mu = jnp.mean(o, axis=-1, keepdims=True)
        dev = o - mu
        var = jnp.mean(dev * dev, axis=-1, keepdims=True)
        on = dev * lax.rsqrt(var + LN_EPS) * gain_ref[...]
        y_ref[sl, :] = (_silu(g_ref[sl, :].astype(F32)) * on).astype(BF16)


def _retention(z, cos_t, sin_t, decay_logit_b, gain, s0, prev, *, T, n_seq, row_block0, layer):
    latent = s0 is not None
    qcol0 = (3 * D_CONV + D_POOL) // HEAD_DIM

    def zspec(k):
        return pl.BlockSpec((T, HEAD_DIM), lambda b, h: (row_block0 + b, qcol0 + k * N_RET_HEADS + h))

    in_specs = [
        zspec(0), zspec(1), zspec(2), zspec(3),
        pl.BlockSpec((T, HEAD_DIM), lambda b, h: (0, 0)),
        pl.BlockSpec((T, HEAD_DIM), lambda b, h: (0, 0)),
        pl.BlockSpec((2, None, 1, HEAD_DIM), lambda b, h: (0, h, 0, 0)),
        pl.BlockSpec((1, HEAD_DIM), lambda b, h: (0, h)),
    ]
    args = [z, z, z, z, cos_t, sin_t, decay_logit_b, gain.reshape(1, D_RET)]
    y_spec = pl.BlockSpec((T, HEAD_DIM), lambda b, h: (row_block0 + b, h))
    y_shape = jax.ShapeDtypeStruct((M_TOK, D_RET), BF16)
    scratch = [
        pltpu.VMEM((T, HEAD_DIM), BF16),
        pltpu.VMEM((2, T // CHUNK, HEAD_DIM, HEAD_DIM), F32),
        pltpu.VMEM((2, T // CHUNK, HEAD_DIM, HEAD_DIM), BF16),
    ]
    if latent:
        in_specs += [
            pl.BlockSpec((None, None, 2, None, HEAD_DIM, HEAD_DIM), lambda b, h: (b, layer, 0, h, 0, 0)),
            pl.BlockSpec(memory_space=pl.ANY),
        ]
        args += [s0, prev]
        return pl.pallas_call(
            functools.partial(_retention_kernel, T=T, latent=True),
            grid=(n_seq, N_RET_HEADS),
            in_specs=in_specs,
            out_specs=y_spec,
            out_shape=y_shape,
            scratch_shapes=scratch,
            input_output_aliases={9: 0},
            compiler_params=_cparams(("arbitrary", "arbitrary")),
            name="retention_latent",
        )(*args)
    return pl.pallas_call(
        functools.partial(_retention_kernel, T=T, latent=False),
        grid=(n_seq, N_RET_HEADS),
        in_specs=in_specs,
        out_specs=[y_spec,
                   pl.BlockSpec((None, 2, None, HEAD_DIM, HEAD_DIM), lambda b, h: (b, 0, h, 0, 0))],
        out_shape=[y_shape,
                   jax.ShapeDtypeStruct((n_seq, 2, N_RET_HEADS, HEAD_DIM, HEAD_DIM), F32)],
        scratch_shapes=scratch,
        compiler_params=_cparams(("arbitrary", "arbitrary")),
        name="retention_context",
    )(*args)


def _layer_norm_rows(r, g, b):
    mu = jnp.mean(r, axis=-1, keepdims=True)
    dev = r - mu
    var = jnp.mean(dev * dev, axis=-1, keepdims=True)
    return dev * lax.rsqrt(var + LN_EPS) * g + b


def _top2_of4(vals):
    top1 = jnp.maximum(jnp.maximum(vals[0], vals[1]), jnp.maximum(vals[2], vals[3]))
    idx1 = jnp.where(vals[0] == top1, 0, jnp.where(vals[1] == top1, 1, jnp.where(vals[2] == top1, 2, 3)))
    neg = jnp.float32(-jnp.inf)
    rest = [jnp.where(idx1 == j, neg, vals[j]) for j in range(4)]
    top2 = jnp.maximum(jnp.maximum(rest[0], rest[1]), jnp.maximum(rest[2], rest[3]))
    idx2 = jnp.where(rest[0] == top2, 0, jnp.where(rest[1] == top2, 1, jnp.where(rest[2] == top2, 2, 3)))
    return top1, idx1, top2, idx2


def _outproj_kernel(ycp_ref, yret_ref, x_ref, wo_ref, gate1_ref, lng_ref, lnb_ref, sc2_ref, sh2_ref,
                    wr_ref, rb_ref, x1_ref, h2_ref, ei_ref, ew_ref, rk_ref, cnt_ref, carry_scr):
    tm = x_ref.shape[0]
    half_k = D_CONV + D_POOL

    @pl.when(pl.program_id(0) == 0)
    def _():
        carry_scr[...] = jnp.zeros_like(carry_scr)

    y = jnp.dot(ycp_ref[...], wo_ref[0:half_k, :], preferred_element_type=F32)
    y += jnp.dot(yret_ref[...], wo_ref[half_k:, :], preferred_element_type=F32)
    x1 = _layer_norm_rows(DEEPNORM_ALPHA * x_ref[...] + gate1_ref[...] * y, lng_ref[...], lnb_ref[...])
    x1_ref[...] = x1
    h2 = x1 * (1.0 + sc2_ref[...]) + sh2_ref[...]
    h2_ref[...] = h2

    h_hi = h2.astype(BF16)
    h_lo = (h2 - h_hi.astype(F32)).astype(BF16)
    wr = wr_ref[...]
    w_hi = wr.astype(BF16)
    w_lo = (wr - w_hi.astype(F32)).astype(BF16)
    logits = (jnp.dot(h_hi, w_hi, preferred_element_type=F32)
              + jnp.dot(h_lo, w_hi, preferred_element_type=F32)
              + jnp.dot(h_hi, w_lo, preferred_element_type=F32))
    lt = logits.T
    rows = [lt[e:e + 1, :] for e in range(N_EXPERTS)]

    mx = rows[0]
    for e in range(1, N_EXPERTS):
        mx = jnp.maximum(mx, rows[e])
    ex = [jnp.exp(r - mx) for r in rows]
    den = ex[0]
    for e in range(1, N_EXPERTS):
        den = den + ex[e]
    score = [x / den for x in ex]
    biased = [score[e] + rb_ref[e] for e in range(N_EXPERTS)]

    best = None
    for gi in range(N_EXPERT_GROUPS):
        t1, i1, t2, i2 = _top2_of4(biased[gi * EXPERTS_PER_GROUP:(gi + 1) * EXPERTS_PER_GROUP])
        gs = t1 + t2
        e1 = gi * EXPERTS_PER_GROUP + i1
        e2 = gi * EXPERTS_PER_GROUP + i2
        if best is None:
            best = (gs, e1, e2)
        else:
            take = gs > best[0]
            best = (jnp.where(take, gs, best[0]), jnp.where(take, e1, best[1]), jnp.where(take, e2, best[2]))
    _, e1, e2 = best
    zero = jnp.zeros_like(score[0])
    w1 = zero
    w2 = zero
    for e in range(N_EXPERTS):
        w1 = w1 + jnp.where(e1 == e, score[e], 0.0)
        w2 = w2 + jnp.where(e2 == e, score[e], 0.0)
    wsum = w1 + w2
    ei_ref[0:1, :] = e1
    ei_ref[1:2, :] = e2
    ew_ref[0:1, :] = w1 / wsum
    ew_ref[1:2, :] = w2 / wsum

    onehot = jnp.concatenate(
        [jnp.where((e1 == e) | (e2 == e), 1.0, 0.0) for e in range(N_EXPERTS)], axis=0)
    s_i = lax.broadcasted_iota(I32, (tm, tm), 0)
    t_i = lax.broadcasted_iota(I32, (tm, tm), 1)
    tri = jnp.where(s_i < t_i, 1.0, 0.0).astype(BF16)
    prefix = jnp.dot(onehot.astype(BF16), tri, preferred_element_type=F32) + carry_scr[:, 0:1]
    r1 = zero
    r2 = zero
    for e in range(N_EXPERTS):
        r1 = r1 + jnp.where(e1 == e, prefix[e:e + 1, :], 0.0)
        r2 = r2 + jnp.where(e2 == e, prefix[e:e + 1, :], 0.0)
    rk_ref[0:1, :] = r1.astype(I32)
    rk_ref[1:2, :] = r2.astype(I32)
    carry_scr[...] = carry_scr[...] + jnp.sum(onehot, axis=1, keepdims=True)
    cnt_ref[...] = carry_scr[...]


def _out_proj(ycp, yret, x, w_out_bf16, ada5, ln_g, ln_b, w_router_pad, router_bias):
    tm = TM_OUT
    ada_spec = lambda chunk: pl.BlockSpec(
        (None, None, 1, D_MODEL), lambda i: (_ada_row(i, tm), chunk, 0, 0))
    vec_spec = pl.BlockSpec((1, D_MODEL), lambda i: (0, 0))
    route_spec = pl.BlockSpec((2, tm), lambda i: (0, i))
    return pl.pallas_call(
        _outproj_kernel,
        grid=(M_TOK // tm,),
        in_specs=[
            pl.BlockSpec((tm, D_CONV + D_POOL), lambda i: (i, 0)),
            pl.BlockSpec((tm, D_RET), lambda i: (i, 0)),
            pl.BlockSpec((tm, D_MODEL), lambda i: (i, 0)),
            pl.BlockSpec((D_MODEL, D_MODEL), lambda i: (0, 0)),
            ada_spec(2),
            vec_spec, vec_spec,
            ada_spec(4),
            ada_spec(3),
            pl.BlockSpec((D_MODEL, LANES), lambda i: (0, 0)),
            pl.BlockSpec(memory_space=pltpu.SMEM),
        ],
        out_specs=[
            pl.BlockSpec((tm, D_MODEL), lambda i: (i, 0)),
            pl.BlockSpec((tm, D_MODEL), lambda i: (i, 0)),
            route_spec, route_spec, route_spec,
            pl.BlockSpec((N_EXPERTS, LANES), lambda i: (0, 0)),
        ],
        out_shape=[
            jax.ShapeDtypeStruct((M_TOK, D_MODEL), F32),
            jax.ShapeDtypeStruct((M_TOK, D_MODEL), F32),
            jax.ShapeDtypeStruct((2, M_TOK), I32),
            jax.ShapeDtypeStruct((2, M_TOK), F32),
            jax.ShapeDtypeStruct((2, M_TOK), I32),
            jax.ShapeDtypeStruct((N_EXPERTS, LANES), F32),
        ],
        scratch_shapes=[pltpu.VMEM((N_EXPERTS, LANES), F32)],
        compiler_params=_cparams(("arbitrary",)),
        name="out_proj_router",
    )(ycp, yret, x, w_out_bf16, ada5, ln_g.reshape(1, D_MODEL), ln_b.reshape(1, D_MODEL), ada5, ada5,
      w_router_pad, router_bias)


def _gather_copy(h2_hbm, xbuf, sem, src_ref, tile, slot, r):
    tok = src_ref[tile * TM_EXP + r]
    return pltpu.make_async_copy(h2_hbm.at[pl.ds(tok, 1), :], xbuf.at[slot, pl.ds(r, 1), :], sem.at[slot])


def _scatter_copy(ybuf, out_hbm, sem, dst_ref, tile, slot, r):
    row = dst_ref[tile * TM_EXP + r]
    return pltpu.make_async_copy(ybuf.at[slot, pl.ds(r, 1), :], out_hbm.at[pl.ds(row, 1), :], sem.at[slot])


def _experts_kernel(te_ref, nv_ref, src_ref, dst_ref, h2_hbm, ws_ref, wg_ref, wu_ref, wd_ref, out_hbm,
                    xbuf, ybuf, gsem, ssem):
    i = pl.program_id(0)
    n_valid = nv_ref[0]
    slot = i % 2

    def start_gather(tile, slot_):
        def body(r, carry):
            _gather_copy(h2_hbm, xbuf, gsem, src_ref, tile, slot_, r).start()
            return carry
        lax.fori_loop(0, TM_EXP, body, 0, unroll=8)

    def wait_rows(copy_fn):
        def body(r, carry):
            copy_fn(r).wait()
            return carry
        lax.fori_loop(0, TM_EXP, body, 0, unroll=8)

    @pl.when(i == 0)
    def _():
        start_gather(0, 0)

    @pl.when(i + 1 < n_valid)
    def _():
        start_gather(i + 1, 1 - slot)

    @pl.when(i < n_valid)
    def _():
        wait_rows(lambda r: _gather_copy(h2_hbm, xbuf, gsem, src_ref, i, slot, r))
        x = xbuf[slot].astype(BF16)
        g = jnp.dot(x, wg_ref[...], preferred_element_type=F32)
        u = jnp.dot(x, wu_ref[...], preferred_element_type=F32)
        a = (_silu(g) * u).astype(BF16)
        y = jnp.dot(a, wd_ref[...], preferred_element_type=F32) * ws_ref[...]

        @pl.when(i >= 2)
        def _():
            wait_rows(lambda r: _scatter_copy(ybuf, out_hbm, ssem, dst_ref, i - 2, slot, r))

        ybuf[slot] = y

        def body(r, carry):
            _scatter_copy(ybuf, out_hbm, ssem, dst_ref, i, slot, r).start()
            return carry
        lax.fori_loop(0, TM_EXP, body, 0, unroll=8)

    @pl.when(i == n_valid - 1)
    def _():
        wait_rows(lambda r: _scatter_copy(ybuf, out_hbm, ssem, dst_ref, i, slot, r))

        @pl.when(i >= 1)
        def _():
            wait_rows(lambda r: _scatter_copy(ybuf, out_hbm, ssem, dst_ref, i - 1, 1 - slot, r))


def _experts(h2, tile_expert, n_valid, src_tok, dst_row, w_sorted, wg, wu, wd):
    grid_spec = pltpu.PrefetchScalarGridSpec(
        num_scalar_prefetch=4,
        grid=(NT_EXP,),
        in_specs=[
            pl.BlockSpec(memory_space=pl.ANY),
            pl.BlockSpec((TM_EXP, 1), lambda i, te, nv, src, dst: (i, 0)),
            pl.BlockSpec((None, D_MODEL, D_EXPERT), lambda i, te, nv, src, dst: (te[i], 0, 0)),
            pl.BlockSpec((None, D_MODEL, D_EXPERT), lambda i, te, nv, src, dst: (te[i], 0, 0)),
            pl.BlockSpec((None, D_EXPERT, D_MODEL), lambda i, te, nv, src, dst: (te[i], 0, 0)),
        ],
        out_specs=pl.BlockSpec(memory_space=pl.ANY),
        scratch_shapes=[
            pltpu.VMEM((2, TM_EXP, D_MODEL), F32),
            pltpu.VMEM((2, TM_EXP, D_MODEL), F32),
            pltpu.SemaphoreType.DMA((2,)),
            pltpu.SemaphoreType.DMA((2,)),
        ],
    )
    return pl.pallas_call(
        _experts_kernel,
        grid_spec=grid_spec,
        out_shape=jax.ShapeDtypeStruct((NP_EXP, D_MODEL), F32),
        compiler_params=_cparams(("arbitrary",)),
        name="experts",
    )(tile_expert, n_valid, src_tok, dst_row, h2, w_sorted, wg, wu, wd)


def _routing_tables(ei, ew, rk, cnt):
    counts = cnt[:, 0].astype(I32)
    tiles_e = (counts + TM_EXP - 1) // TM_EXP
    tiles_cum = jnp.cumsum(tiles_e)
    row_start = (tiles_cum - tiles_e) * TM_EXP
    n_valid = tiles_cum[-1]
    pos = (jnp.take(row_start, ei) + rk).reshape(-1)
    tile_ids = jnp.arange(NT_EXP, dtype=I32)
    te = jnp.minimum(jnp.searchsorted(tiles_cum, tile_ids, side="right").astype(I32), N_EXPERTS - 1)
    te_last = jnp.take(te, n_valid - 1)
    te = jnp.where(tile_ids < n_valid, te, te_last)
    tok = jnp.tile(jnp.arange(M_TOK, dtype=I32), 2)
    src_tok = jnp.zeros((NP_EXP,), I32).at[pos].set(tok)
    valid = jnp.zeros((NP_EXP,), jnp.bool_).at[pos].set(True)
    spare = N_PAIR + jnp.cumsum((~valid).astype(I32)) - 1
    dst_row = jnp.where(valid, jnp.zeros((NP_EXP,), I32).at[pos].set(jnp.arange(N_PAIR, dtype=I32)), spare)
    w_sorted = jnp.zeros((NP_EXP,), F32).at[pos].set(ew.reshape(-1))
    return te, n_valid.reshape(1).astype(I32), src_tok, dst_row, w_sorted.reshape(NP_EXP, 1)


def _final_kernel(x1_ref, o0_ref, o1_ref, gate2_ref, lng_ref, lnb_ref, x2_ref):
    y2 = o0_ref[...] + o1_ref[...]
    x2_ref[...] = _layer_norm_rows(DEEPNORM_ALPHA * x1_ref[...] + gate2_ref[...] * y2,
                                   lng_ref[...], lnb_ref[...])


def _final(x1, out2, ada5, ln_g, ln_b):
    tm = TM_OUT
    nb = M_TOK // tm
    vec_spec = pl.BlockSpec((1, D_MODEL), lambda i: (0, 0))
    return pl.pallas_call(
        _final_kernel,
        grid=(nb,),
        in_specs=[
            pl.BlockSpec((tm, D_MODEL), lambda i: (i, 0)),
            pl.BlockSpec((tm, D_MODEL), lambda i: (i, 0)),
            pl.BlockSpec((tm, D_MODEL), lambda i: (nb + i, 0)),
            pl.BlockSpec((None, None, 1, D_MODEL), lambda i: (_ada_row(i, tm), 5, 0, 0)),
            vec_spec, vec_spec,
        ],
        out_specs=pl.BlockSpec((tm, D_MODEL), lambda i: (i, 0)),
        out_shape=jax.ShapeDtypeStruct((M_TOK, D_MODEL), F32),
        compiler_params=_cparams(("arbitrary",)),
        name="final_ln",
    )(x1, out2, out2, ada5, ln_g.reshape(1, D_MODEL), ln_b.reshape(1, D_MODEL))


def _rope_tables():
    rows = T_LAT // GRID_W
    row = jnp.repeat(jnp.arange(rows), GRID_W).astype(F32)
    col = jnp.tile(jnp.arange(GRID_W), rows).astype(F32)
    n_freq = HEAD_DIM // 4
    inv_freq = ROPE_BASE ** (-jnp.arange(n_freq, dtype=F32) / n_freq)
    ang = jnp.concatenate([row[:, None] * inv_freq[None], col[:, None] * inv_freq[None]], axis=-1)
    cos, sin = jnp.cos(ang), jnp.sin(ang)
    return jnp.concatenate([cos, cos], axis=-1), jnp.concatenate([-sin, sin], axis=-1)


def kernel(x_prompt, x_sample, state_retention, c, c_ctx, w_ada, b_ada, w_in, w_out, conv_w, pool_w,
           pool_scale, ret_decay_logit, ret_gn_gain, ln1_g, ln1_b, ln2_g, ln2_b, w_router, router_bias,
           w_gate, w_up, w_down):
    x = jnp.concatenate([x_prompt.reshape(M_CTX, D_MODEL), x_sample.reshape(M_LAT, D_MODEL)], axis=0)
    c_all = jnp.concatenate(
        [c_ctx[None, :], c, jnp.zeros((ADA_ROWS - 1 - N_LAT_SEQ, D_MODEL), F32)], axis=0)
    ada = _ada_table(c_all, w_ada, b_ada).reshape(DEPTH, ADA_ROWS, 6, 1, D_MODEL)

    cos_lat, sin_lat = _rope_tables()
    cos_ctx = jnp.ones((T_CTX, HEAD_DIM), F32)
    sin_ctx = jnp.zeros((T_CTX, HEAD_DIM), F32)
    w_router_pad = jnp.pad(w_router, ((0, 0), (0, LANES - N_EXPERTS)))

    states = []
    for l in range(DEPTH):
        ada5 = ada[l]
        z = _in_proj(x, ada5, w_in[l].astype(BF16))

        ycp = _conv_pool(z, conv_w[l], pool_w[l], pool_scale[l], None, T=T_CTX, n_seq=N_CTX_SEQ, row_block0=0)
        ycp = _conv_pool(z, conv_w[l], pool_w[l], pool_scale[l], ycp, T=T_LAT, n_seq=N_LAT_SEQ,
                         row_block0=M_CTX // T_LAT)

        dl = jnp.broadcast_to(ret_decay_logit[l][:, :, None, None], (2, N_RET_HEADS, 1, HEAD_DIM))
        yret, s_fin = _retention(z, cos_ctx, sin_ctx, dl, ret_gn_gain[l], None, None,
                                 T=T_CTX, n_seq=N_CTX_SEQ, row_block0=0, layer=l)
        yret = _retention(z, cos_lat, sin_lat, dl, ret_gn_gain[l], state_retention, yret,
                          T=T_LAT, n_seq=N_LAT_SEQ, row_block0=M_CTX // T_LAT, layer=l)
        states.append(s_fin)

        x1, h2, ei, ew, rk, cnt = _out_proj(ycp, yret, x, w_out[l].astype(BF16), ada5, ln1_g[l], ln1_b[l],
                                            w_router_pad, router_bias)
        te, n_valid, src_tok, dst_row, w_sorted = _routing_tables(ei, ew, rk, cnt)
        out2 = _experts(h2, te, n_valid, src_tok, dst_row, w_sorted,
                        w_gate[l].astype(BF16), w_up[l].astype(BF16), w_down[l].astype(BF16))
        x = _final(x1, out2, ada5, ln2_g[l], ln2_b[l])

    y_prompt = x[:M_CTX].reshape(N_CTX_SEQ, T_CTX, D_MODEL)
    y_sample = x[M_CTX:].reshape(N_LAT_SEQ, T_LAT, D_MODEL)
    return y_prompt, y_sample, jnp.stack(states, axis=1)
```

```python
import functools

import jax
import jax.numpy as jnp
from jax import lax
from jax.experimental import pallas as pl
from jax.experimental.pallas import tpu as pltpu

F32 = jnp.float32
BF16 = jnp.bfloat16
I32 = jnp.int32

D_MODEL = 2048
N_CTX_SEQ, T_CTX = 16, 256
N_LAT_SEQ, T_LAT = 8, 1024
DEPTH = 2
M_CTX = N_CTX_SEQ * T_CTX
M_LAT = N_LAT_SEQ * T_LAT
M_TOK = M_CTX + M_LAT

GRID_W = 64
D_CONV = D_MODEL // 4
D_POOL = D_MODEL // 4
D_RET = D_MODEL // 2
N_RET_HEADS = 8
HEAD_DIM = D_RET // N_RET_HEADS
POOL_WINDOWS = (2, 4, 8, 16)
POOL_GROUP_DIM = D_POOL // len(POOL_WINDOWS)
CHUNK = 128
ROPE_BASE = 10000.0
N_EXPERTS = 16
EXPERTS_PER_GROUP = 4
N_EXPERT_GROUPS = N_EXPERTS // EXPERTS_PER_GROUP
D_EXPERT = D_MODEL // 4
D_IN_PROJ = 3 * D_CONV + D_POOL + 4 * D_RET
DEEPNORM_ALPHA = (2.0 * DEPTH) ** 0.25
LN_EPS = 1e-5
ADA_ROWS = 16

LANES = 128
VMEM_LIMIT = 56 * 1024 * 1024

TM_IN = 512
TN_IN = 1024
TM_OUT = 256
TM_EXP = 256
N_PAIR = 2 * M_TOK
NP_EXP = N_PAIR + N_EXPERTS * TM_EXP
NT_EXP = NP_EXP // TM_EXP


def _cparams(sem):
    return pltpu.CompilerParams(dimension_semantics=sem, vmem_limit_bytes=VMEM_LIMIT)


def _silu(x):
    return x * jax.nn.sigmoid(x)


def _ada_row(i, tm):
    n_ctx_tiles = M_CTX // tm
    per_batch = T_LAT // tm
    return jnp.where(i < n_ctx_tiles, 0, 1 + (i - n_ctx_tiles) // per_batch)


def _ada_kernel(c_ref, w_ref, b_ref, o_ref):
    s = _silu(c_ref[...]).astype(BF16)
    o_ref[...] = jnp.dot(s, w_ref[...].astype(BF16), preferred_element_type=F32) + b_ref[...]


def _ada_table(c_all, w_ada, b_ada):
    tn = 1024
    n6 = 6 * D_MODEL
    return pl.pallas_call(
        _ada_kernel,
        grid=(DEPTH, n6 // tn),
        in_specs=[
            pl.BlockSpec((ADA_ROWS, D_MODEL), lambda l, j: (0, 0)),
            pl.BlockSpec((None, D_MODEL, tn), lambda l, j: (l, 0, j)),
            pl.BlockSpec((None, 1, tn), lambda l, j: (l, 0, j)),
        ],
        out_specs=pl.BlockSpec((None, ADA_ROWS, tn), lambda l, j: (l, 0, j)),
        out_shape=jax.ShapeDtypeStruct((DEPTH, ADA_ROWS, n6), F32),
        compiler_params=_cparams(("arbitrary", "arbitrary")),
        name="ada_table",
    )(c_all, w_ada, b_ada.reshape(DEPTH, 1, n6))


def _inproj_kernel(xc_ref, xl_ref, sc_ref, sh_ref, w_ref, o_ref, h_scr):
    i = pl.program_id(0)
    first = pl.program_id(1) == 0

    def modulate(x_ref):
        h_scr[...] = (x_ref[...] * (1.0 + sc_ref[...]) + sh_ref[...]).astype(BF16)

    @pl.when(first & (i < M_CTX // TM_IN))
    def _():
        modulate(xc_ref)

    @pl.when(first & (i >= M_CTX // TM_IN))
    def _():
        modulate(xl_ref)

    o_ref[...] = jnp.dot(h_scr[...], w_ref[...], preferred_element_type=F32).astype(BF16)


def _ctx_block(i, tm):
    return jnp.minimum(i, M_CTX // tm - 1)


def _lat_block(i, tm):
    return jnp.maximum(i - M_CTX // tm, 0)


def _in_proj(x_ctx, x_lat, ada5, w_in_bf16):
    ada_spec = lambda chunk: pl.BlockSpec(
        (None, None, 1, D_MODEL), lambda i, j: (_ada_row(i, TM_IN), chunk, 0, 0))
    return pl.pallas_call(
        _inproj_kernel,
        grid=(M_TOK // TM_IN, D_IN_PROJ // TN_IN),
        in_specs=[
            pl.BlockSpec((TM_IN, D_MODEL), lambda i, j: (_ctx_block(i, TM_IN), 0)),
            pl.BlockSpec((TM_IN, D_MODEL), lambda i, j: (_lat_block(i, TM_IN), 0)),
            ada_spec(1),
            ada_spec(0),
            pl.BlockSpec((D_MODEL, TN_IN), lambda i, j: (0, j)),
        ],
        out_specs=pl.BlockSpec((TM_IN, TN_IN), lambda i, j: (i, j)),
        out_shape=jax.ShapeDtypeStruct((M_TOK, D_IN_PROJ), BF16),
        scratch_shapes=[pltpu.VMEM((TM_IN, D_MODEL), BF16)],
        compiler_params=_cparams(("arbitrary", "arbitrary")),
        name="in_proj",
    )(x_ctx, x_lat, ada5, ada5, w_in_bf16)


def _convpool_kernel(*refs, T, aliased):
    if aliased:
        z_ref, cw_ref, pw_ref, ps_ref, _, o_ref, band_ref = refs
    else:
        z_ref, cw_ref, pw_ref, ps_ref, o_ref, band_ref = refs

    @pl.when(pl.program_id(0) == 0)
    def _():
        row = lax.broadcasted_iota(I32, (T, T), 0)
        col = lax.broadcasted_iota(I32, (T, T), 1)
        d = col - row
        for gi, w in enumerate(POOL_WINDOWS):
            band_ref[gi] = jnp.where((d >= -(w // 2)) & (d < w // 2), 1.0, 0.0).astype(BF16)

    cb = z_ref[:, 0:D_CONV].astype(F32)
    cc = z_ref[:, D_CONV:2 * D_CONV].astype(F32)
    cv = z_ref[:, 2 * D_CONV:3 * D_CONV].astype(F32)
    u = cc * cv
    t = lax.broadcasted_iota(I32, (T, D_CONV), 0)
    u_prev = jnp.where(t == 0, 0.0, pltpu.roll(u, 1, 0))
    u_next = jnp.where(t == T - 1, 0.0, pltpu.roll(u, T - 1, 0))
    conv = u_prev * cw_ref[0:1, :] + u * cw_ref[1:2, :] + u_next * cw_ref[2:3, :]
    o_ref[:, 0:D_CONV] = (cb * conv).astype(BF16)

    tt = lax.broadcasted_iota(I32, (T, POOL_GROUP_DIM), 0)
    for gi, w in enumerate(POOL_WINDOWS):
        lo = 3 * D_CONV + gi * POOL_GROUP_DIM
        p = z_ref[:, lo:lo + POOL_GROUP_DIM]
        win = jnp.dot(band_ref[gi], p, preferred_element_type=F32)
        cnt = (jnp.minimum(tt + w // 2, T) - jnp.maximum(tt - w // 2, 0)).astype(F32)
        pooled = win / cnt - p.astype(F32)
        y = jnp.dot(pooled.astype(BF16), pw_ref[gi].astype(BF16), preferred_element_type=F32)
        y = y * ps_ref[:, gi * POOL_GROUP_DIM:(gi + 1) * POOL_GROUP_DIM]
        o_ref[:, D_CONV + gi * POOL_GROUP_DIM:D_CONV + (gi + 1) * POOL_GROUP_DIM] = y.astype(BF16)


def _conv_pool(z, conv_w, pool_w, pool_scale, prev, *, T, n_seq, row_block0):
    aliased = prev is not None
    in_specs = [
        pl.BlockSpec((T, D_IN_PROJ - 4 * D_RET), lambda b: (row_block0 + b, 0)),
        pl.BlockSpec((3, D_CONV), lambda b: (0, 0)),
        pl.BlockSpec((len(POOL_WINDOWS), POOL_GROUP_DIM, POOL_GROUP_DIM), lambda b: (0, 0, 0)),
        pl.BlockSpec((1, D_POOL), lambda b: (0, 0)),
    ]
    args = [z, conv_w, pool_w, pool_scale.reshape(1, D_POOL)]
    if aliased:
        in_specs.append(pl.BlockSpec(memory_space=pl.ANY))
        args.append(prev)
    return pl.pallas_call(
        functools.partial(_convpool_kernel, T=T, aliased=aliased),
        grid=(n_seq,),
        in_specs=in_specs,
        out_specs=pl.BlockSpec((T, D_CONV + D_POOL), lambda b: (row_block0 + b, 0)),
        out_shape=jax.ShapeDtypeStruct((M_TOK, D_CONV + D_POOL), BF16),
        scratch_shapes=[pltpu.VMEM((len(POOL_WINDOWS), T, T), BF16)],
        input_output_aliases={4: 0} if aliased else {},
        compiler_params=_cparams(("arbitrary",)),
        name="conv_pool_T%d" % T,
    )(*args)


def _log_sigmoid(x):
    return jnp.minimum(x, 0.0) - jnp.log1p(jnp.exp(-jnp.abs(x)))


def _retention_kernel(*refs, T, latent):
    if latent:
        (q_ref, k_ref, v_ref, g_ref, cos_ref, sin_ref, dl_ref, gain_ref, s0_ref, _,
         y_ref, kr_scr, u_scr, s_scr) = refs
    else:
        (q_ref, k_ref, v_ref, g_ref, cos_ref, sin_ref, dl_ref, gain_ref,
         y_ref, sfin_ref, kr_scr, u_scr, s_scr) = refs
    n_chunks = T // CHUNK
    half = HEAD_DIM // 2

    lg_f = _log_sigmoid(dl_ref[0])
    lg_b = _log_sigmoid(dl_ref[1])
    row = lax.broadcasted_iota(I32, (CHUNK, CHUNK), 0).astype(F32)
    col = lax.broadcasted_iota(I32, (CHUNK, CHUNK), 1).astype(F32)
    diff = row - col
    decay = (jnp.where(diff >= 0, jnp.exp(lg_f * jnp.maximum(diff, 0.0)), 0.0)
             + jnp.where(diff <= 0, jnp.exp(lg_b * jnp.maximum(-diff, 0.0)), 0.0))
    xi_f = jnp.exp(lg_f * (row + 1.0))
    xi_b = jnp.exp(lg_b * (CHUNK - row))
    zeta_f = jnp.exp(lg_f * (CHUNK - 1.0 - row))
    zeta_b = jnp.exp(lg_b * row)
    g_f = jnp.exp(lg_f * CHUNK)
    g_b = jnp.exp(lg_b * CHUNK)

    def rope(x, sl):
        return x * cos_ref[sl, :] + pltpu.roll(x, half, 1) * sin_ref[sl, :]

    tn_dims = (((0,), (0,)), ((), ()))
    nt_dims = (((1,), (1,)), ((), ()))

    for c in range(n_chunks):
        sl = slice(c * CHUNK, (c + 1) * CHUNK)
        kr = rope(k_ref[sl, :].astype(F32), sl) * (HEAD_DIM ** -0.5)
        kr_scr[sl, :] = kr.astype(BF16)
        v = v_ref[sl, :]
        u_scr[0, c] = lax.dot_general((kr * zeta_f).astype(BF16), v, tn_dims, preferred_element_type=F32)
        u_scr[1, c] = lax.dot_general((kr * zeta_b).astype(BF16), v, tn_dims, preferred_element_type=F32)

    s = s0_ref[0] if latent else jnp.zeros((HEAD_DIM, HEAD_DIM), F32)
    for c in range(n_chunks):
        s_scr[0, c] = s.astype(BF16)
        s = g_f * s + u_scr[0, c]
    if not latent:
        sfin_ref[0] = s
    s = s0_ref[1] if latent else jnp.zeros((HEAD_DIM, HEAD_DIM), F32)
    for c in reversed(range(n_chunks)):
        s_scr[1, c] = s.astype(BF16)
        s = g_b * s + u_scr[1, c]
    if not latent:
        sfin_ref[1] = s

    for c in range(n_chunks):
        sl = slice(c * CHUNK, (c + 1) * CHUNK)
        qr = rope(q_ref[sl, :].astype(F32), sl)
        scores = lax.dot_general(qr.astype(BF16), kr_scr[sl, :], nt_dims, preferred_element_type=F32)
        o = jnp.dot((scores * decay).astype(BF16), v_ref[sl, :], preferred_element_type=F32)
        o += jnp.dot((qr * xi_f).astype(BF16), s_scr[0, c], preferred_element_type=F32)
        o += jnp.dot((qr * xi_b).astype(BF16), s_scr[1, c], preferred_element_type=F32)
        mu = jnp.mean(o, axis=-1, keepdims=True)
        dev = o - mu
        var = jnp.mean(dev * dev, axis=-1, keepdims=True)
        on = dev * lax.rsqrt(var + LN_EPS) * gain_ref[...]
        y_ref[sl, :] = (_silu(g_ref[sl, :].astype(F32)) * on).astype(BF16)


def _retention(z, cos_t, sin_t, decay_logit_b, gain, s0, prev, *, T, n_seq, row_block0, layer):
    latent = s0 is not None
    qcol0 = (3 * D_CONV + D_POOL) // HEAD_DIM

    def zspec(k):
        return pl.BlockSpec((T, HEAD_DIM), lambda b, h: (row_block0 + b, qcol0 + k * N_RET_HEADS + h))

    in_specs = [
        zspec(0), zspec(1), zspec(2), zspec(3),
        pl.BlockSpec((T, HEAD_DIM), lambda b, h: (0, 0)),
        pl.BlockSpec((T, HEAD_DIM), lambda b, h: (0, 0)),
        pl.BlockSpec((2, None, 1, HEAD_DIM), lambda b, h: (0, h, 0, 0)),
        pl.BlockSpec((1, HEAD_DIM), lambda b, h: (0, h)),
    ]
    args = [z, z, z, z, cos_t, sin_t, decay_logit_b, gain.reshape(1, D_RET)]
    y_spec = pl.BlockSpec((T, HEAD_DIM), lambda b, h: (row_block0 + b, h))
    y_shape = jax.ShapeDtypeStruct((M_TOK, D_RET), BF16)
    scratch = [
        pltpu.VMEM((T, HEAD_DIM), BF16),
        pltpu.VMEM((2, T // CHUNK, HEAD_DIM, HEAD_DIM), F32),
        pltpu.VMEM((2, T // CHUNK, HEAD_DIM, HEAD_DIM), BF16),
    ]
    if latent:
        in_specs += [
            pl.BlockSpec((None, None, 2, None, HEAD_DIM, HEAD_DIM), lambda b, h: (b, layer, 0, h, 0, 0)),
            pl.BlockSpec(memory_space=pl.ANY),
        ]
        args += [s0, prev]
        return pl.pallas_call(
            functools.partial(_retention_kernel, T=T, latent=True),
            grid=(n_seq, N_RET_HEADS),
            in_specs=in_specs,
            out_specs=y_spec,
            out_shape=y_shape,
            scratch_shapes=scratch,
            input_output_aliases={9: 0},
            compiler_params=_cparams(("arbitrary", "arbitrary")),
            name="retention_latent",
        )(*args)
    return pl.pallas_call(
        functools.partial(_retention_kernel, T=T, latent=False),
        grid=(n_seq, N_RET_HEADS),
        in_specs=in_specs,
        out_specs=[y_spec,
                   pl.BlockSpec((None, 2, None, HEAD_DIM, HEAD_DIM), lambda b, h: (b, 0, h, 0, 0))],
        out_shape=[y_shape,
                   jax.ShapeDtypeStruct((n_seq, 2, N_RET_HEADS, HEAD_DIM, HEAD_DIM), F32)],
        scratch_shapes=scratch,
        compiler_params=_cparams(("arbitrary", "arbitrary")),
        name="retention_context",
    )(*args)


def _layer_norm_rows(r, g, b):
    mu = jnp.mean(r, axis=-1, keepdims=True)
    dev = r - mu
    var = jnp.mean(dev * dev, axis=-1, keepdims=True)
    return dev * lax.rsqrt(var + LN_EPS) * g + b


def _top2_of4(vals):
    top1 = jnp.maximum(jnp.maximum(vals[0], vals[1]), jnp.maximum(vals[2], vals[3]))
    idx1 = jnp.where(vals[0] == top1, 0, jnp.where(vals[1] == top1, 1, jnp.where(vals[2] == top1, 2, 3)))
    neg = jnp.float32(-jnp.inf)
    rest = [jnp.where(idx1 == j, neg, vals[j]) for j in range(4)]
    top2 = jnp.maximum(jnp.maximum(rest[0], rest[1]), jnp.maximum(rest[2], rest[3]))
    idx2 = jnp.where(rest[0] == top2, 0, jnp.where(rest[1] == top2, 1, jnp.where(rest[2] == top2, 2, 3)))
    return top1, idx1, top2, idx2


def _outproj_kernel(ycp_ref, yret_ref, xc_ref, xl_ref, wo_ref, gate1_ref, lng_ref, lnb_ref, sc2_ref, sh2_ref,
                    wr_ref, rb_ref, x1_ref, h2_ref, ei_ref, ewt_ref, rk_ref, cnt_ref, carry_scr):
    tm = xc_ref.shape[0]
    half_k = D_CONV + D_POOL
    i = pl.program_id(0)

    @pl.when(i == 0)
    def _():
        carry_scr[...] = jnp.zeros_like(carry_scr)

    y = jnp.dot(ycp_ref[...], wo_ref[0:half_k, :], preferred_element_type=F32)
    y += jnp.dot(yret_ref[...], wo_ref[half_k:, :], preferred_element_type=F32)
    x = jnp.where(i < M_CTX // tm, xc_ref[...], xl_ref[...])
    x1 = _layer_norm_rows(DEEPNORM_ALPHA * x + gate1_ref[...] * y, lng_ref[...], lnb_ref[...])
    x1_ref[...] = x1
    h2 = x1 * (1.0 + sc2_ref[...]) + sh2_ref[...]
    h2_ref[...] = h2

    h_hi = h2.astype(BF16)
    h_lo = (h2 - h_hi.astype(F32)).astype(BF16)
    wr = wr_ref[...]
    w_hi = wr.astype(BF16)
    w_lo = (wr - w_hi.astype(F32)).astype(BF16)
    logits = (jnp.dot(h_hi, w_hi, preferred_element_type=F32)
              + jnp.dot(h_lo, w_hi, preferred_element_type=F32)
              + jnp.dot(h_hi, w_lo, preferred_element_type=F32))
    lt = logits.T
    rows = [lt[e:e + 1, :] for e in range(N_EXPERTS)]

    mx = rows[0]
    for e in range(1, N_EXPERTS):
        mx = jnp.maximum(mx, rows[e])
    ex = [jnp.exp(r - mx) for r in rows]
    den = ex[0]
    for e in range(1, N_EXPERTS):
        den = den + ex[e]
    score = [x / den for x in ex]
    biased = [score[e] + rb_ref[e] for e in range(N_EXPERTS)]

    best = None
    for gi in range(N_EXPERT_GROUPS):
        t1, i1, t2, i2 = _top2_of4(biased[gi * EXPERTS_PER_GROUP:(gi + 1) * EXPERTS_PER_GROUP])
        gs = t1 + t2
        e1 = gi * EXPERTS_PER_GROUP + i1
        e2 = gi * EXPERTS_PER_GROUP + i2
        if best is None:
            best = (gs, e1, e2)
        else:
            take = gs > best[0]
            best = (jnp.where(take, gs, best[0]), jnp.where(take, e1, best[1]), jnp.where(take, e2, best[2]))
    _, e1, e2 = best
    zero = jnp.zeros_like(score[0])
    w1 = zero
    w2 = zero
    for e in range(N_EXPERTS):
        w1 = w1 + jnp.where(e1 == e, score[e], 0.0)
        w2 = w2 + jnp.where(e2 == e, score[e], 0.0)
    wsum = w1 + w2
    ei_ref[0:1, :] = e1
    ei_ref[1:2, :] = e2
    ewt_ref[...] = jnp.concatenate(
        [w1 / wsum, w2 / wsum, jnp.zeros((LANES - 2, tm), F32)], axis=0).T

    onehot = jnp.concatenate(
        [jnp.where((e1 == e) | (e2 == e), 1.0, 0.0) for e in range(N_EXPERTS)], axis=0)
    s_i = lax.broadcasted_iota(I32, (tm, tm), 0)
    t_i = lax.broadcasted_iota(I32, (tm, tm), 1)
    tri = jnp.where(s_i < t_i, 1.0, 0.0).astype(BF16)
    prefix = jnp.dot(onehot.astype(BF16), tri, preferred_element_type=F32) + carry_scr[:, 0:1]
    r1 = zero
    r2 = zero
    for e in range(N_EXPERTS):
        r1 = r1 + jnp.where(e1 == e, prefix[e:e + 1, :], 0.0)
        r2 = r2 + jnp.where(e2 == e, prefix[e:e + 1, :], 0.0)
    rk_ref[0:1, :] = r1.astype(I32)
    rk_ref[1:2, :] = r2.astype(I32)
    carry_scr[...] = carry_scr[...] + jnp.sum(onehot, axis=1, keepdims=True)
    cnt_ref[...] = carry_scr[...]


def _out_proj(ycp, yret, x_ctx, x_lat, w_out_bf16, ada5, ln_g, ln_b, w_router_pad, router_bias):
    tm = TM_OUT
    ada_spec = lambda chunk: pl.BlockSpec(
        (None, None, 1, D_MODEL), lambda i: (_ada_row(i, tm), chunk, 0, 0))
    vec_spec = pl.BlockSpec((1, D_MODEL), lambda i: (0, 0))
    route_spec = pl.BlockSpec((2, tm), lambda i: (0, i))
    return pl.pallas_call(
        _outproj_kernel,
        grid=(M_TOK // tm,),
        in_specs=[
            pl.BlockSpec((tm, D_CONV + D_POOL), lambda i: (i, 0)),
            pl.BlockSpec((tm, D_RET), lambda i: (i, 0)),
            pl.BlockSpec((tm, D_MODEL), lambda i: (_ctx_block(i, tm), 0)),
            pl.BlockSpec((tm, D_MODEL), lambda i: (_lat_block(i, tm), 0)),
            pl.BlockSpec((D_MODEL, D_MODEL), lambda i: (0, 0)),
            ada_spec(2),
            vec_spec, vec_spec,
            ada_spec(4),
            ada_spec(3),
            pl.BlockSpec((D_MODEL, LANES), lambda i: (0, 0)),
            pl.BlockSpec(memory_space=pltpu.SMEM),
        ],
        out_specs=[
            pl.BlockSpec((tm, D_MODEL), lambda i: (i, 0)),
            pl.BlockSpec((tm, D_MODEL), lambda i: (i, 0)),
            route_spec,
            pl.BlockSpec((tm, LANES), lambda i: (i, 0)),
            route_spec,
            pl.BlockSpec((N_EXPERTS, LANES), lambda i: (0, 0)),
        ],
        out_shape=[
            jax.ShapeDtypeStruct((M_TOK, D_MODEL), F32),
            jax.ShapeDtypeStruct((M_TOK, D_MODEL), F32),
            jax.ShapeDtypeStruct((2, M_TOK), I32),
            jax.ShapeDtypeStruct((M_TOK, LANES), F32),
            jax.ShapeDtypeStruct((2, M_TOK), I32),
            jax.ShapeDtypeStruct((N_EXPERTS, LANES), F32),
        ],
        scratch_shapes=[pltpu.VMEM((N_EXPERTS, LANES), F32)],
        compiler_params=_cparams(("arbitrary",)),
        name="out_proj_router",
    )(ycp, yret, x_ctx, x_lat, w_out_bf16, ada5, ln_g.reshape(1, D_MODEL), ln_b.reshape(1, D_MODEL),
      ada5, ada5, w_router_pad, router_bias)


def _route_kernel(cnt_ref, ei_ref, rk_ref, src_ref, pos_ref, te_ref, nv_ref, start_scr):
    def zero(r, carry):
        src_ref[r] = 0
        return carry
    lax.fori_loop(0, NP_EXP, zero, 0, unroll=8)

    tile = jnp.int32(0)
    for e in range(N_EXPERTS):
        start_scr[e] = tile * TM_EXP
        n_tiles = lax.shift_right_logical(cnt_ref[e] + (TM_EXP - 1), TM_EXP.bit_length() - 1)

        def mark(j, carry, e=e, tile=tile):
            te_ref[tile + j] = e
            return carry
        lax.fori_loop(0, n_tiles, mark, 0)
        tile = tile + n_tiles
    nv_ref[0] = tile
    last_expert = te_ref[tile - 1]

    def mark_unused(j, carry):
        te_ref[j] = last_expert
        return carry
    lax.fori_loop(tile, NT_EXP, mark_unused, 0)

    def place(t, carry):
        for k in range(2):
            pair = k * M_TOK + t
            row = start_scr[ei_ref[pair]] + rk_ref[pair]
            pos_ref[pair] = row
            src_ref[row] = t
        return carry
    lax.fori_loop(0, M_TOK, place, 0, unroll=4)


def _route_tables(cnt, ei, rk):
    smem = pl.BlockSpec(memory_space=pltpu.SMEM)
    return pl.pallas_call(
        _route_kernel,
        in_specs=[smem, smem, smem],
        out_specs=[smem, smem, smem, smem],
        out_shape=[
            jax.ShapeDtypeStruct((NP_EXP,), I32),
            jax.ShapeDtypeStruct((N_PAIR,), I32),
            jax.ShapeDtypeStruct((NT_EXP,), I32),
            jax.ShapeDtypeStruct((1,), I32),
        ],
        scratch_shapes=[pltpu.SMEM((N_EXPERTS,), I32)],
        name="route_tables",
    )(cnt, ei, rk)


def _row_gather_start(src_hbm, buf, sem, idx_ref, base, n_rows, row0=0):
    for r in range(n_rows):
        pltpu.make_async_copy(src_hbm.at[pl.ds(idx_ref[base + r], 1), :],
                              buf.at[pl.ds(row0 + r, 1), :], sem).start()


def _row_gather_wait(src_hbm, buf, sem):
    pltpu.make_async_copy(src_hbm.at[pl.ds(0, buf.shape[0]), :], buf, sem).wait()


def _experts_kernel(te_ref, nv_ref, src_ref, h2_hbm, wg_ref, wu_ref, wd_ref, o_ref, xbuf0, xbuf1, gsem):
    i = pl.program_id(0)
    n_valid = nv_ref[0]
    bufs = (xbuf0, xbuf1)

    @pl.when(i == 0)
    def _():
        _row_gather_start(h2_hbm, xbuf0, gsem.at[0], src_ref, 0, TM_EXP)

    def step(cur):
        nxt_buf, nxt_sem = bufs[1 - cur], gsem.at[1 - cur]
        _row_gather_wait(h2_hbm, bufs[cur], gsem.at[cur])
        nxt = jnp.minimum(i + 1, NT_EXP - 1)
        _row_gather_start(h2_hbm, nxt_buf, nxt_sem, src_ref, nxt * TM_EXP, TM_EXP)
        x = bufs[cur][...].astype(BF16)
        g = jnp.dot(x, wg_ref[...], preferred_element_type=F32)
        u = jnp.dot(x, wu_ref[...], preferred_element_type=F32)
        a = (_silu(g) * u).astype(BF16)
        o_ref[...] = jnp.dot(a, wd_ref[...], preferred_element_type=F32)

        @pl.when(i == n_valid - 1)
        def _():
            _row_gather_wait(h2_hbm, nxt_buf, nxt_sem)

    for cur in range(2):
        pl.when((i < n_valid) & (i % 2 == cur))(functools.partial(step, cur))

    @pl.when(i >= n_valid)
    def _():
        o_ref[...] = jnp.zeros_like(o_ref)


def _experts(h2, tile_expert, n_valid, src_tok, wg, wu, wd):
    grid_spec = pltpu.PrefetchScalarGridSpec(
        num_scalar_prefetch=3,
        grid=(NT_EXP,),
        in_specs=[
            pl.BlockSpec(memory_space=pl.ANY),
            pl.BlockSpec((None, D_MODEL, D_EXPERT), lambda i, te, nv, src: (te[i], 0, 0)),
            pl.BlockSpec((None, D_MODEL, D_EXPERT), lambda i, te, nv, src: (te[i], 0, 0)),
            pl.BlockSpec((None, D_EXPERT, D_MODEL), lambda i, te, nv, src: (te[i], 0, 0)),
        ],
        out_specs=pl.BlockSpec((TM_EXP, D_MODEL), lambda i, te, nv, src: (i, 0)),
        scratch_shapes=[
            pltpu.VMEM((TM_EXP, D_MODEL), F32),
            pltpu.VMEM((TM_EXP, D_MODEL), F32),
            pltpu.SemaphoreType.DMA((2,)),
        ],
    )
    return pl.pallas_call(
        _experts_kernel,
        grid_spec=grid_spec,
        out_shape=jax.ShapeDtypeStruct((NP_EXP, D_MODEL), F32),
        compiler_params=_cparams(("arbitrary",)),
        name="experts",
    )(tile_expert, n_valid, src_tok, h2, wg, wu, wd)


def _final_kernel(pos_ref, x1_ref, ys_hbm, ewt_ref, gate2_ref, lng_ref, lnb_ref, xc_ref, xl_ref,
                  rbuf0, rbuf1, sem):
    tm = TM_OUT
    i = pl.program_id(0)
    n_blocks = M_TOK // tm
    bufs = (rbuf0, rbuf1)

    def start(tile, buf, buf_sem):
        _row_gather_start(ys_hbm, buf, buf_sem, pos_ref, tile * tm, tm, row0=0)
        _row_gather_start(ys_hbm, buf, buf_sem, pos_ref, M_TOK + tile * tm, tm, row0=tm)

    @pl.when(i == 0)
    def _():
        start(0, rbuf0, sem.at[0])

    def step(cur):
        buf = bufs[cur]
        nxt_buf, nxt_sem = bufs[1 - cur], sem.at[1 - cur]
        _row_gather_wait(ys_hbm, buf, sem.at[cur])
        start(jnp.minimum(i + 1, n_blocks - 1), nxt_buf, nxt_sem)
        w = ewt_ref[...]
        y2 = w[:, 0:1] * buf[0:tm, :] + w[:, 1:2] * buf[tm:2 * tm, :]
        x2 = _layer_norm_rows(DEEPNORM_ALPHA * x1_ref[...] + gate2_ref[...] * y2,
                              lng_ref[...], lnb_ref[...])

        @pl.when(i < M_CTX // tm)
        def _():
            xc_ref[...] = x2

        @pl.when(i >= M_CTX // tm)
        def _():
            xl_ref[...] = x2

        @pl.when(i == n_blocks - 1)
        def _():
            _row_gather_wait(ys_hbm, nxt_buf, nxt_sem)

    for cur in range(2):
        pl.when(i % 2 == cur)(functools.partial(step, cur))


def _final(x1, ys, pos, ewt, ada5, ln_g, ln_b):
    tm = TM_OUT
    vec_spec = pl.BlockSpec((1, D_MODEL), lambda i, pos: (0, 0))
    grid_spec = pltpu.PrefetchScalarGridSpec(
        num_scalar_prefetch=1,
        grid=(M_TOK // tm,),
        in_specs=[
            pl.BlockSpec((tm, D_MODEL), lambda i, pos: (i, 0)),
            pl.BlockSpec(memory_space=pl.ANY),
            pl.BlockSpec((tm, LANES), lambda i, pos: (i, 0)),
            pl.BlockSpec((None, None, 1, D_MODEL), lambda i, pos: (_ada_row(i, tm), 5, 0, 0)),
            vec_spec, vec_spec,
        ],
        out_specs=[
            pl.BlockSpec((tm, D_MODEL), lambda i, pos: (_ctx_block(i, tm), 0)),
            pl.BlockSpec((tm, D_MODEL), lambda i, pos: (_lat_block(i, tm), 0)),
        ],
        scratch_shapes=[
            pltpu.VMEM((2 * tm, D_MODEL), F32),
            pltpu.VMEM((2 * tm, D_MODEL), F32),
            pltpu.SemaphoreType.DMA((2,)),
        ],
    )
    return pl.pallas_call(
        _final_kernel,
        grid_spec=grid_spec,
        out_shape=[jax.ShapeDtypeStruct((M_CTX, D_MODEL), F32),
                   jax.ShapeDtypeStruct((M_LAT, D_MODEL), F32)],
        compiler_params=_cparams(("arbitrary",)),
        name="final_ln",
    )(pos, x1, ys, ewt, ada5, ln_g.reshape(1, D_MODEL), ln_b.reshape(1, D_MODEL))


def _rope_tables():
    rows = T_LAT // GRID_W
    row = jnp.repeat(jnp.arange(rows), GRID_W).astype(F32)
    col = jnp.tile(jnp.arange(GRID_W), rows).astype(F32)
    n_freq = HEAD_DIM // 4
    inv_freq = ROPE_BASE ** (-jnp.arange(n_freq, dtype=F32) / n_freq)
    ang = jnp.concatenate([row[:, None] * inv_freq[None], col[:, None] * inv_freq[None]], axis=-1)
    cos, sin = jnp.cos(ang), jnp.sin(ang)
    return jnp.concatenate([cos, cos], axis=-1), jnp.concatenate([-sin, sin], axis=-1)


def kernel(x_prompt, x_sample, state_retention, c, c_ctx, w_ada, b_ada, w_in, w_out, conv_w, pool_w,
           pool_scale, ret_decay_logit, ret_gn_gain, ln1_g, ln1_b, ln2_g, ln2_b, w_router, router_bias,
           w_gate, w_up, w_down):
    x_ctx = x_prompt.reshape(M_CTX, D_MODEL)
    x_lat = x_sample.reshape(M_LAT, D_MODEL)
    c_all = jnp.concatenate(
        [c_ctx[None, :], c, jnp.zeros((ADA_ROWS - 1 - N_LAT_SEQ, D_MODEL), F32)], axis=0)
    ada = _ada_table(c_all, w_ada, b_ada).reshape(DEPTH, ADA_ROWS, 6, 1, D_MODEL)

    cos_lat, sin_lat = _rope_tables()
    cos_ctx = jnp.ones((T_CTX, HEAD_DIM), F32)
    sin_ctx = jnp.zeros((T_CTX, HEAD_DIM), F32)
    w_router_pad = jnp.pad(w_router, ((0, 0), (0, LANES - N_EXPERTS)))

    states = []
    for l in range(DEPTH):
        ada5 = ada[l]
        z = _in_proj(x_ctx, x_lat, ada5, w_in[l].astype(BF16))

        ycp = _conv_pool(z, conv_w[l], pool_w[l], pool_scale[l], None, T=T_CTX, n_seq=N_CTX_SEQ, row_block0=0)
        ycp = _conv_pool(z, conv_w[l], pool_w[l], pool_scale[l], ycp, T=T_LAT, n_seq=N_LAT_SEQ,
                         row_block0=M_CTX // T_LAT)

        dl = jnp.broadcast_to(ret_decay_logit[l][:, :, None, None], (2, N_RET_HEADS, 1, HEAD_DIM))
        yret, s_fin = _retention(z, cos_ctx, sin_ctx, dl, ret_gn_gain[l], None, None,
                                 T=T_CTX, n_seq=N_CTX_SEQ, row_block0=0, layer=l)
        yret = _retention(z, cos_lat, sin_lat, dl, ret_gn_gain[l], state_retention, yret,
                          T=T_LAT, n_seq=N_LAT_SEQ, row_block0=M_CTX // T_LAT, layer=l)
        states.append(s_fin)

        x1, h2, ei, ewt, rk, cnt = _out_proj(ycp, yret, x_ctx, x_lat, w_out[l].astype(BF16), ada5,
                                             ln1_g[l], ln1_b[l], w_router_pad, router_bias)
        src_tok, pos, te, n_valid = _route_tables(cnt[:, 0].astype(I32), ei.reshape(-1), rk.reshape(-1))
        ys = _experts(h2, te, n_valid, src_tok,
                      w_gate[l].astype(BF16), w_up[l].astype(BF16), w_down[l].astype(BF16))
        x_ctx, x_lat = _final(x1, ys, pos, ewt, ada5, ln2_g[l], ln2_b[l])

    y_prompt = x_ctx.reshape(N_CTX_SEQ, T_CTX, D_MODEL)
    y_sample = x_lat.reshape(N_LAT_SEQ, T_LAT, D_MODEL)
    return y_prompt, y_sample, jnp.stack(states, axis=1)
```

```python
import functools

import jax
import jax.numpy as jnp
from jax import lax
from jax.experimental import pallas as pl
from jax.experimental.pallas import tpu as pltpu

F32 = jnp.float32
BF16 = jnp.bfloat16
I32 = jnp.int32

D_MODEL = 2048
N_CTX_SEQ, T_CTX = 16, 256
N_LAT_SEQ, T_LAT = 8, 1024
DEPTH = 2
M_CTX = N_CTX_SEQ * T_CTX
M_LAT = N_LAT_SEQ * T_LAT
M_TOK = M_CTX + M_LAT

GRID_W = 64
D_CONV = D_MODEL // 4
D_POOL = D_MODEL // 4
D_RET = D_MODEL // 2
N_RET_HEADS = 8
HEAD_DIM = D_RET // N_RET_HEADS
POOL_WINDOWS = (2, 4, 8, 16)
POOL_GROUP_DIM = D_POOL // len(POOL_WINDOWS)
CHUNK = 128
ROPE_BASE = 10000.0
N_EXPERTS = 16
EXPERTS_PER_GROUP = 4
N_EXPERT_GROUPS = N_EXPERTS // EXPERTS_PER_GROUP
D_EXPERT = D_MODEL // 4
D_IN_PROJ = 3 * D_CONV + D_POOL + 4 * D_RET
DEEPNORM_ALPHA = (2.0 * DEPTH) ** 0.25
LN_EPS = 1e-5
ADA_ROWS = 16

LANES = 128
VMEM_LIMIT = 56 * 1024 * 1024

TM_IN = 1024
TN_IN = 1024
TOK_ROWS = D_MODEL // LANES
TM_OUT = 256
TM_EXP = 256
N_PAIR = 2 * M_TOK
NP_EXP = N_PAIR + N_EXPERTS * TM_EXP
NT_EXP = NP_EXP // TM_EXP


def _cparams(sem):
    return pltpu.CompilerParams(dimension_semantics=sem, vmem_limit_bytes=VMEM_LIMIT)


def _silu(x):
    return x * jax.nn.sigmoid(x)


def _ada_row(i, tm):
    n_ctx_tiles = M_CTX // tm
    per_batch = T_LAT // tm
    return jnp.where(i < n_ctx_tiles, 0, 1 + (i - n_ctx_tiles) // per_batch)


def _ada_kernel(c_ref, w_ref, b_ref, o_ref):
    s = _silu(c_ref[...]).astype(BF16)
    o_ref[...] = jnp.dot(s, w_ref[...].astype(BF16), preferred_element_type=F32) + b_ref[...]


def _ada_table(c_all, w_ada, b_ada):
    tn = 1024
    n6 = 6 * D_MODEL
    return pl.pallas_call(
        _ada_kernel,
        grid=(DEPTH, n6 // tn),
        in_specs=[
            pl.BlockSpec((ADA_ROWS, D_MODEL), lambda l, j: (0, 0)),
            pl.BlockSpec((None, D_MODEL, tn), lambda l, j: (l, 0, j)),
            pl.BlockSpec((None, 1, tn), lambda l, j: (l, 0, j)),
        ],
        out_specs=pl.BlockSpec((None, ADA_ROWS, tn), lambda l, j: (l, 0, j)),
        out_shape=jax.ShapeDtypeStruct((DEPTH, ADA_ROWS, n6), F32),
        compiler_params=_cparams(("arbitrary", "arbitrary")),
        name="ada_table",
    )(c_all, w_ada, b_ada.reshape(DEPTH, 1, n6))


def _ctx_block(i, tm):
    return jnp.minimum(i, M_CTX // tm - 1)


def _lat_block(i, tm):
    return jnp.maximum(i - M_CTX // tm, 0)


def _modulate_kernel(xc_ref, xl_ref, sc_ref, sh_ref, h_ref):
    x = jnp.where(pl.program_id(0) < M_CTX // xc_ref.shape[0], xc_ref[...], xl_ref[...])
    h_ref[...] = (x * (1.0 + sc_ref[...]) + sh_ref[...]).astype(BF16)


def _modulate(x_ctx, x_lat, ada5):
    tm = TM_OUT
    ada_spec = lambda chunk: pl.BlockSpec(
        (None, None, 1, D_MODEL), lambda i: (_ada_row(i, tm), chunk, 0, 0))
    return pl.pallas_call(
        _modulate_kernel,
        grid=(M_TOK // tm,),
        in_specs=[
            pl.BlockSpec((tm, D_MODEL), lambda i: (_ctx_block(i, tm), 0)),
            pl.BlockSpec((tm, D_MODEL), lambda i: (_lat_block(i, tm), 0)),
            ada_spec(1), ada_spec(0),
        ],
        out_specs=pl.BlockSpec((tm, D_MODEL), lambda i: (i, 0)),
        out_shape=jax.ShapeDtypeStruct((M_TOK, D_MODEL), BF16),
        compiler_params=_cparams(("arbitrary",)),
        name="modulate",
    )(x_ctx, x_lat, ada5, ada5)


def _inproj_kernel(h_ref, w_ref, o_ref, wbf_scr):
    @pl.when(pl.program_id(1) == 0)
    def _():
        wbf_scr[...] = w_ref[...].astype(BF16)

    o_ref[...] = jnp.dot(h_ref[...], wbf_scr[...], preferred_element_type=F32).astype(BF16)


def _in_proj(h, w_in, layer):
    return pl.pallas_call(
        _inproj_kernel,
        grid=(D_IN_PROJ // TN_IN, M_TOK // TM_IN),
        in_specs=[
            pl.BlockSpec((TM_IN, D_MODEL), lambda j, i: (i, 0)),
            pl.BlockSpec((None, D_MODEL, TN_IN), lambda j, i: (layer, 0, j)),
        ],
        out_specs=pl.BlockSpec((TM_IN, TN_IN), lambda j, i: (i, j)),
        out_shape=jax.ShapeDtypeStruct((M_TOK, D_IN_PROJ), BF16),
        scratch_shapes=[pltpu.VMEM((D_MODEL, TN_IN), BF16)],
        compiler_params=_cparams(("arbitrary", "arbitrary")),
        name="in_proj",
    )(h, w_in)


def _convpool_kernel(*refs, T, aliased):
    if aliased:
        z_ref, cw_ref, pw_ref, ps_ref, _, o_ref, band_ref = refs
    else:
        z_ref, cw_ref, pw_ref, ps_ref, o_ref, band_ref = refs

    @pl.when(pl.program_id(0) == 0)
    def _():
        row = lax.broadcasted_iota(I32, (T, T), 0)
        col = lax.broadcasted_iota(I32, (T, T), 1)
        d = col - row
        for gi, w in enumerate(POOL_WINDOWS):
            band_ref[gi] = jnp.where((d >= -(w // 2)) & (d < w // 2), 1.0, 0.0).astype(BF16)

    cb = z_ref[:, 0:D_CONV].astype(F32)
    cc = z_ref[:, D_CONV:2 * D_CONV].astype(F32)
    cv = z_ref[:, 2 * D_CONV:3 * D_CONV].astype(F32)
    u = cc * cv
    t = lax.broadcasted_iota(I32, (T, D_CONV), 0)
    u_prev = jnp.where(t == 0, 0.0, pltpu.roll(u, 1, 0))
    u_next = jnp.where(t == T - 1, 0.0, pltpu.roll(u, T - 1, 0))
    conv = u_prev * cw_ref[0:1, :] + u * cw_ref[1:2, :] + u_next * cw_ref[2:3, :]
    o_ref[:, 0:D_CONV] = (cb * conv).astype(BF16)

    tt = lax.broadcasted_iota(I32, (T, POOL_GROUP_DIM), 0)
    for gi, w in enumerate(POOL_WINDOWS):
        lo = 3 * D_CONV + gi * POOL_GROUP_DIM
        p = z_ref[:, lo:lo + POOL_GROUP_DIM]
        win = jnp.dot(band_ref[gi], p, preferred_element_type=F32)
        cnt = (jnp.minimum(tt + w // 2, T) - jnp.maximum(tt - w // 2, 0)).astype(F32)
        pooled = win / cnt - p.astype(F32)
        y = jnp.dot(pooled.astype(BF16), pw_ref[gi].astype(BF16), preferred_element_type=F32)
        y = y * ps_ref[:, gi * POOL_GROUP_DIM:(gi + 1) * POOL_GROUP_DIM]
        o_ref[:, D_CONV + gi * POOL_GROUP_DIM:D_CONV + (gi + 1) * POOL_GROUP_DIM] = y.astype(BF16)


def _conv_pool(z, conv_w, pool_w, pool_scale, prev, *, T, n_seq, row_block0):
    aliased = prev is not None
    in_specs = [
        pl.BlockSpec((T, D_IN_PROJ - 4 * D_RET), lambda b: (row_block0 + b, 0)),
        pl.BlockSpec((3, D_CONV), lambda b: (0, 0)),
        pl.BlockSpec((len(POOL_WINDOWS), POOL_GROUP_DIM, POOL_GROUP_DIM), lambda b: (0, 0, 0)),
        pl.BlockSpec((1, D_POOL), lambda b: (0, 0)),
    ]
    args = [z, conv_w, pool_w, pool_scale.reshape(1, D_POOL)]
    if aliased:
        in_specs.append(pl.BlockSpec(memory_space=pl.ANY))
        args.append(prev)
    return pl.pallas_call(
        functools.partial(_convpool_kernel, T=T, aliased=aliased),
        grid=(n_seq,),
        in_specs=in_specs,
        out_specs=pl.BlockSpec((T, D_CONV + D_POOL), lambda b: (row_block0 + b, 0)),
        out_shape=jax.ShapeDtypeStruct((M_TOK, D_CONV + D_POOL), BF16),
        scratch_shapes=[pltpu.VMEM((len(POOL_WINDOWS), T, T), BF16)],
        input_output_aliases={4: 0} if aliased else {},
        compiler_params=_cparams(("arbitrary",)),
        name="conv_pool_T%d" % T,
    )(*args)


def _log_sigmoid(x):
    return jnp.minimum(x, 0.0) - jnp.log1p(jnp.exp(-jnp.abs(x)))


def _retention_kernel(*refs, T, latent):
    if latent:
        (q_ref, k_ref, v_ref, g_ref, cos_ref, sin_ref, dl_ref, gain_ref, s0_ref, _,
         y_ref, kr_scr, u_scr, s_scr) = refs
    else:
        (q_ref, k_ref, v_ref, g_ref, cos_ref, sin_ref, dl_ref, gain_ref,
         y_ref, sfin_ref, kr_scr, u_scr, s_scr) = refs
    n_chunks = T // CHUNK
    half = HEAD_DIM // 2

    lg_f = _log_sigmoid(dl_ref[0])
    lg_b = _log_sigmoid(dl_ref[1])
    row = lax.broadcasted_iota(I32, (CHUNK, CHUNK), 0).astype(F32)
    col = lax.broadcasted_iota(I32, (CHUNK, CHUNK), 1).astype(F32)
    diff = row - col
    decay = (jnp.where(diff >= 0, jnp.exp(lg_f * jnp.maximum(diff, 0.0)), 0.0)
             + jnp.where(diff <= 0, jnp.exp(lg_b * jnp.maximum(-diff, 0.0)), 0.0))
    xi_f = jnp.exp(lg_f * (row + 1.0))
    xi_b = jnp.exp(lg_b * (CHUNK - row))
    zeta_f = jnp.exp(lg_f * (CHUNK - 1.0 - row))
    zeta_b = jnp.exp(lg_b * row)
    g_f = jnp.exp(lg_f * CHUNK)
    g_b = jnp.exp(lg_b * CHUNK)

    def rope(x, sl):
        return x * cos_ref[sl, :] + pltpu.roll(x, half, 1) * sin_ref[sl, :]

    tn_dims = (((0,), (0,)), ((), ()))
    nt_dims = (((1,), (1,)), ((), ()))

    for c in range(n_chunks):
        sl = slice(c * CHUNK, (c + 1) * CHUNK)
        kr = rope(k_ref[sl, :].astype(F32), sl) * (HEAD_DIM ** -0.5)
        kr_scr[sl, :] = kr.astype(BF16)
        v = v_ref[sl, :]
        u_scr[0, c] = lax.dot_general((kr * zeta_f).astype(BF16), v, tn_dims, preferred_element_type=F32)
        u_scr[1, c] = lax.dot_general((kr * zeta_b).astype(BF16), v, tn_dims, preferred_element_type=F32)

    s = s0_ref[0] if latent else jnp.zeros((HEAD_DIM, HEAD_DIM), F32)
    for c in range(n_chunks):
        s_scr[0, c] = s.astype(BF16)
        s = g_f * s + u_scr[0, c]
    if not latent:
        sfin_ref[0] = s
    s = s0_ref[1] if latent else jnp.zeros((HEAD_DIM, HEAD_DIM), F32)
    for c in reversed(range(n_chunks)):
        s_scr[1, c] = s.astype(BF16)
        s = g_b * s + u_scr[1, c]
    if not latent:
        sfin_ref[1] = s

    for c in range(n_chunks):
        sl = slice(c * CHUNK, (c + 1) * CHUNK)
        qr = rope(q_ref[sl, :].astype(F32), sl)
        scores = lax.dot_general(qr.astype(BF16), kr_scr[sl, :], nt_dims, preferred_element_type=F32)
        o = jnp.dot((scores * decay).astype(BF16), v_ref[sl, :], preferred_element_type=F32)
        o += jnp.dot((qr * xi_f).astype(BF16), s_scr[0, c], preferred_element_type=F32)
        o += jnp.dot((qr * xi_b).astype(BF16), s_scr[1, c], preferred_element_type=F32)
        mu = jnp.mean(o, axis=-1, keepdims=True)
        dev = o - mu
        var = jnp.mean(dev * dev, axis=-1, keepdims=True)
        on = dev * lax.rsqrt(var + LN_EPS) * gain_ref[...]
        y_ref[sl, :] = (_silu(g_ref[sl, :].astype(F32)) * on).astype(BF16)


def _retention(z, cos_t, sin_t, decay_logit_b, gain, s0, prev, *, T, n_seq, row_block0, layer):
    latent = s0 is not None
    qcol0 = (3 * D_CONV + D_POOL) // HEAD_DIM

    def zspec(k):
        return pl.BlockSpec((T, HEAD_DIM), lambda b, h: (row_block0 + b, qcol0 + k * N_RET_HEADS + h))

    in_specs = [
        zspec(0), zspec(1), zspec(2), zspec(3),
        pl.BlockSpec((T, HEAD_DIM), lambda b, h: (0, 0)),
        pl.BlockSpec((T, HEAD_DIM), lambda b, h: (0, 0)),
        pl.BlockSpec((2, None, 1, HEAD_DIM), lambda b, h: (0, h, 0, 0)),
        pl.BlockSpec((1, HEAD_DIM), lambda b, h: (0, h)),
    ]
    args = [z, z, z, z, cos_t, sin_t, decay_logit_b, gain.reshape(1, D_RET)]
    y_spec = pl.BlockSpec((T, HEAD_DIM), lambda b, h: (row_block0 + b, h))
    y_shape = jax.ShapeDtypeStruct((M_TOK, D_RET), BF16)
    scratch = [
        pltpu.VMEM((T, HEAD_DIM), BF16),
        pltpu.VMEM((2, T // CHUNK, HEAD_DIM, HEAD_DIM), F32),
        pltpu.VMEM((2, T // CHUNK, HEAD_DIM, HEAD_DIM), BF16),
    ]
    if latent:
        in_specs += [
            pl.BlockSpec((None, None, 2, None, HEAD_DIM, HEAD_DIM), lambda b, h: (b, layer, 0, h, 0, 0)),
            pl.BlockSpec(memory_space=pl.ANY),
        ]
        args += [s0, prev]
        return pl.pallas_call(
            functools.partial(_retention_kernel, T=T, latent=True),
            grid=(n_seq, N_RET_HEADS),
            in_specs=in_specs,
            out_specs=y_spec,
            out_shape=y_shape,
            scratch_shapes=scratch,
            input_output_aliases={9: 0},
            compiler_params=_cparams(("arbitrary", "arbitrary")),
            name="retention_latent",
        )(*args)
    return pl.pallas_call(
        functools.partial(_retention_kernel, T=T, latent=False),
        grid=(n_seq, N_RET_HEADS),
        in_specs=in_specs,
        out_specs=[y_spec,
                   pl.BlockSpec((None, 2, None, HEAD_DIM, HEAD_DIM), lambda b, h: (b, 0, h, 0, 0))],
        out_shape=[y_shape,
                   jax.ShapeDtypeStruct((n_seq, 2, N_RET_HEADS, HEAD_DIM, HEAD_DIM), F32)],
        scratch_shapes=scratch,
        compiler_params=_cparams(("arbitrary", "arbitrary")),
        name="retention_context",
    )(*args)


def _layer_norm_rows(r, g, b):
    mu = jnp.mean(r, axis=-1, keepdims=True)
    dev = r - mu
    var = jnp.mean(dev * dev, axis=-1, keepdims=True)
    return dev * lax.rsqrt(var + LN_EPS) * g + b


def _top2_of4(vals):
    top1 = jnp.maximum(jnp.maximum(vals[0], vals[1]), jnp.maximum(vals[2], vals[3]))
    idx1 = jnp.where(vals[0] == top1, 0, jnp.where(vals[1] == top1, 1, jnp.where(vals[2] == top1, 2, 3)))
    neg = jnp.float32(-jnp.inf)
    rest = [jnp.where(idx1 == j, neg, vals[j]) for j in range(4)]
    top2 = jnp.maximum(jnp.maximum(rest[0], rest[1]), jnp.maximum(rest[2], rest[3]))
    idx2 = jnp.where(rest[0] == top2, 0, jnp.where(rest[1] == top2, 1, jnp.where(rest[2] == top2, 2, 3)))
    return top1, idx1, top2, idx2


def _store_token_tiles(ref, val):
    n = val.shape[0]
    for c in range(TOK_ROWS):
        ref[pl.ds(c, n, stride=TOK_ROWS), :] = val[:, c * LANES:(c + 1) * LANES]


def _load_token_tiles(ref, row0, n):
    return jnp.concatenate(
        [ref[pl.ds(row0 * TOK_ROWS + c, n, stride=TOK_ROWS), :] for c in range(TOK_ROWS)], axis=1)


def _outproj_kernel(ycp_ref, yret_ref, xc_ref, xl_ref, wo_hbm, gate1_ref, lng_ref, lnb_ref, sc2_ref, sh2_ref,
                    wr_ref, rb_ref, x1_ref, h2_ref, ei_ref, ewt_ref, rk_ref, cnt_ref,
                    carry_scr, wo_ref, wstage, wsem, *, layer):
    tm = xc_ref.shape[0]
    half_k = D_CONV + D_POOL
    i = pl.program_id(0)

    @pl.when(i == 0)
    def _():
        carry_scr[...] = jnp.zeros_like(carry_scr)
        rows = wstage.shape[1]
        n_chunks = D_MODEL // rows

        def chunk_copy(c):
            return pltpu.make_async_copy(wo_hbm.at[layer, pl.ds(c * rows, rows), :], wstage.at[c % 2],
                                         wsem.at[c % 2])
        chunk_copy(0).start()
        for c in range(n_chunks):
            if c + 1 < n_chunks:
                chunk_copy(c + 1).start()
            chunk_copy(c).wait()
            wo_ref[c * rows:(c + 1) * rows, :] = wstage[c % 2].astype(BF16)

    y = jnp.dot(ycp_ref[...], wo_ref[0:half_k, :], preferred_element_type=F32)
    y += jnp.dot(yret_ref[...], wo_ref[half_k:, :], preferred_element_type=F32)
    x = jnp.where(i < M_CTX // tm, xc_ref[...], xl_ref[...])
    x1 = _layer_norm_rows(DEEPNORM_ALPHA * x + gate1_ref[...] * y, lng_ref[...], lnb_ref[...])
    x1_ref[...] = x1
    h2 = x1 * (1.0 + sc2_ref[...]) + sh2_ref[...]
    _store_token_tiles(h2_ref, h2)

    h_hi = h2.astype(BF16)
    h_lo = (h2 - h_hi.astype(F32)).astype(BF16)
    wr = wr_ref[...]
    w_hi = wr.astype(BF16)
    w_lo = (wr - w_hi.astype(F32)).astype(BF16)
    logits = (jnp.dot(h_hi, w_hi, preferred_element_type=F32)
              + jnp.dot(h_lo, w_hi, preferred_element_type=F32)
              + jnp.dot(h_hi, w_lo, preferred_element_type=F32))
    lt = logits.T
    rows = [lt[e:e + 1, :] for e in range(N_EXPERTS)]

    mx = rows[0]
    for e in range(1, N_EXPERTS):
        mx = jnp.maximum(mx, rows[e])
    ex = [jnp.exp(r - mx) for r in rows]
    den = ex[0]
    for e in range(1, N_EXPERTS):
        den = den + ex[e]
    score = [x / den for x in ex]
    biased = [score[e] + rb_ref[e] for e in range(N_EXPERTS)]

    best = None
    for gi in range(N_EXPERT_GROUPS):
        t1, i1, t2, i2 = _top2_of4(biased[gi * EXPERTS_PER_GROUP:(gi + 1) * EXPERTS_PER_GROUP])
        gs = t1 + t2
        e1 = gi * EXPERTS_PER_GROUP + i1
        e2 = gi * EXPERTS_PER_GROUP + i2
        if best is None:
            best = (gs, e1, e2)
        else:
            take = gs > best[0]
            best = (jnp.where(take, gs, best[0]), jnp.where(take, e1, best[1]), jnp.where(take, e2, best[2]))
    _, e1, e2 = best
    zero = jnp.zeros_like(score[0])
    w1 = zero
    w2 = zero
    for e in range(N_EXPERTS):
        w1 = w1 + jnp.where(e1 == e, score[e], 0.0)
        w2 = w2 + jnp.where(e2 == e, score[e], 0.0)
    wsum = w1 + w2
    ei_ref[0:1, :] = e1
    ei_ref[1:2, :] = e2
    ewt_ref[...] = jnp.concatenate(
        [w1 / wsum, w2 / wsum, jnp.zeros((LANES - 2, tm), F32)], axis=0).T

    onehot = jnp.concatenate(
        [jnp.where((e1 == e) | (e2 == e), 1.0, 0.0) for e in range(N_EXPERTS)], axis=0)
    s_i = lax.broadcasted_iota(I32, (tm, tm), 0)
    t_i = lax.broadcasted_iota(I32, (tm, tm), 1)
    tri = jnp.where(s_i < t_i, 1.0, 0.0).astype(BF16)
    prefix = jnp.dot(onehot.astype(BF16), tri, preferred_element_type=F32) + carry_scr[:, 0:1]
    r1 = zero
    r2 = zero
    for e in range(N_EXPERTS):
        r1 = r1 + jnp.where(e1 == e, prefix[e:e + 1, :], 0.0)
        r2 = r2 + jnp.where(e2 == e, prefix[e:e + 1, :], 0.0)
    rk_ref[0:1, :] = r1.astype(I32)
    rk_ref[1:2, :] = r2.astype(I32)
    carry_scr[...] = carry_scr[...] + jnp.sum(onehot, axis=1, keepdims=True)
    cnt_ref[...] = carry_scr[...]


def _out_proj(ycp, yret, x_ctx, x_lat, w_out, layer, ada5, ln_g, ln_b, w_router_pad, router_bias):
    tm = TM_OUT
    ada_spec = lambda chunk: pl.BlockSpec(
        (None, None, 1, D_MODEL), lambda i: (_ada_row(i, tm), chunk, 0, 0))
    vec_spec = pl.BlockSpec((1, D_MODEL), lambda i: (0, 0))
    route_spec = pl.BlockSpec((2, tm), lambda i: (0, i))
    return pl.pallas_call(
        functools.partial(_outproj_kernel, layer=layer),
        grid=(M_TOK // tm,),
        in_specs=[
            pl.BlockSpec((tm, D_CONV + D_POOL), lambda i: (i, 0)),
            pl.BlockSpec((tm, D_RET), lambda i: (i, 0)),
            pl.BlockSpec((tm, D_MODEL), lambda i: (_ctx_block(i, tm), 0)),
            pl.BlockSpec((tm, D_MODEL), lambda i: (_lat_block(i, tm), 0)),
            pl.BlockSpec(memory_space=pl.ANY),
            ada_spec(2),
            vec_spec, vec_spec,
            ada_spec(4),
            ada_spec(3),
            pl.BlockSpec((D_MODEL, LANES), lambda i: (0, 0)),
            pl.BlockSpec(memory_space=pltpu.SMEM),
        ],
        out_specs=[
            pl.BlockSpec((tm, D_MODEL), lambda i: (i, 0)),
            pl.BlockSpec((tm * TOK_ROWS, LANES), lambda i: (i, 0)),
            route_spec,
            pl.BlockSpec((tm, LANES), lambda i: (i, 0)),
            route_spec,
            pl.BlockSpec((N_EXPERTS, LANES), lambda i: (0, 0)),
        ],
        out_shape=[
            jax.ShapeDtypeStruct((M_TOK, D_MODEL), F32),
            jax.ShapeDtypeStruct((M_TOK * TOK_ROWS, LANES), F32),
            jax.ShapeDtypeStruct((2, M_TOK), I32),
            jax.ShapeDtypeStruct((M_TOK, LANES), F32),
            jax.ShapeDtypeStruct((2, M_TOK), I32),
            jax.ShapeDtypeStruct((N_EXPERTS, LANES), F32),
        ],
        scratch_shapes=[
            pltpu.VMEM((N_EXPERTS, LANES), F32),
            pltpu.VMEM((D_MODEL, D_MODEL), BF16),
            pltpu.VMEM((2, 256, D_MODEL), F32),
            pltpu.SemaphoreType.DMA((2,)),
        ],
        compiler_params=_cparams(("arbitrary",)),
        name="out_proj_router",
    )(ycp, yret, x_ctx, x_lat, w_out, ada5, ln_g.reshape(1, D_MODEL), ln_b.reshape(1, D_MODEL),
      ada5, ada5, w_router_pad, router_bias)


def _route_kernel(cnt_ref, ei_ref, rk_ref, src_ref, pos_ref, te_ref, nv_ref, start_scr):
    def zero(r, carry):
        src_ref[r] = 0
        return carry
    lax.fori_loop(0, NP_EXP, zero, 0, unroll=8)

    tile = jnp.int32(0)
    for e in range(N_EXPERTS):
        start_scr[e] = tile * TM_EXP
        n_tiles = lax.shift_right_logical(cnt_ref[e] + (TM_EXP - 1), TM_EXP.bit_length() - 1)

        def mark(j, carry, e=e, tile=tile):
            te_ref[tile + j] = e
            return carry
        lax.fori_loop(0, n_tiles, mark, 0)
        tile = tile + n_tiles
    nv_ref[0] = tile
    last_expert = te_ref[tile - 1]

    def mark_unused(j, carry):
        te_ref[j] = last_expert
        return carry
    lax.fori_loop(tile, NT_EXP, mark_unused, 0)

    def place(t, carry):
        for k in range(2):
            pair = k * M_TOK + t
            row = start_scr[ei_ref[pair]] + rk_ref[pair]
            pos_ref[pair] = row
            src_ref[row] = t
        return carry
    lax.fori_loop(0, M_TOK, place, 0, unroll=4)


def _route_tables(cnt, ei, rk):
    smem = pl.BlockSpec(memory_space=pltpu.SMEM)
    return pl.pallas_call(
        _route_kernel,
        in_specs=[smem, smem, smem],
        out_specs=[smem, smem, smem, smem],
        out_shape=[
            jax.ShapeDtypeStruct((NP_EXP,), I32),
            jax.ShapeDtypeStruct((N_PAIR,), I32),
            jax.ShapeDtypeStruct((NT_EXP,), I32),
            jax.ShapeDtypeStruct((1,), I32),
        ],
        scratch_shapes=[pltpu.SMEM((N_EXPERTS,), I32)],
        name="route_tables",
    )(cnt, ei, rk)


def _row_gather_start(src_hbm, buf, sem, idx_ref, base, n_tok, tok0=0):
    for r in range(n_tok):
        src_row = pl.multiple_of(idx_ref[base + r] * TOK_ROWS, TOK_ROWS)
        pltpu.make_async_copy(src_hbm.at[pl.ds(src_row, TOK_ROWS), :],
                              buf.at[pl.ds((tok0 + r) * TOK_ROWS, TOK_ROWS), :], sem).start()


def _row_gather_wait(src_hbm, buf, sem):
    pltpu.make_async_copy(src_hbm.at[pl.ds(0, buf.shape[0]), :], buf, sem).wait()


def _experts_kernel(te_ref, nv_ref, src_ref, h2_hbm, wg_ref, wu_ref, wd_ref, o_ref, xbuf0, xbuf1, gsem,
                    wg_bf, wu_bf, wd_bf):
    i = pl.program_id(0)
    n_valid = nv_ref[0]
    bufs = (xbuf0, xbuf1)

    @pl.when(i == 0)
    def _():
        _row_gather_start(h2_hbm, xbuf0, gsem.at[0], src_ref, 0, TM_EXP)

    @pl.when((i < n_valid) & ((i == 0) | (te_ref[i] != te_ref[jnp.maximum(i - 1, 0)])))
    def _():
        wg_bf[...] = wg_ref[...].astype(BF16)
        wu_bf[...] = wu_ref[...].astype(BF16)
        wd_bf[...] = wd_ref[...].astype(BF16)

    def step(cur):
        nxt_buf, nxt_sem = bufs[1 - cur], gsem.at[1 - cur]
        _row_gather_wait(h2_hbm, bufs[cur], gsem.at[cur])
        nxt = jnp.minimum(i + 1, NT_EXP - 1)
        _row_gather_start(h2_hbm, nxt_buf, nxt_sem, src_ref, nxt * TM_EXP, TM_EXP)
        x = _load_token_tiles(bufs[cur], 0, TM_EXP).astype(BF16)
        g = jnp.dot(x, wg_bf[...], preferred_element_type=F32)
        u = jnp.dot(x, wu_bf[...], preferred_element_type=F32)
        a = (_silu(g) * u).astype(BF16)
        _store_token_tiles(o_ref, jnp.dot(a, wd_bf[...], preferred_element_type=F32))

        @pl.when(i == n_valid - 1)
        def _():
            _row_gather_wait(h2_hbm, nxt_buf, nxt_sem)

    for cur in range(2):
        pl.when((i < n_valid) & (i % 2 == cur))(functools.partial(step, cur))

    @pl.when(i >= n_valid)
    def _():
        o_ref[...] = jnp.zeros_like(o_ref)


def _experts(h2, tile_expert, n_valid, src_tok, wg, wu, wd, layer):
    grid_spec = pltpu.PrefetchScalarGridSpec(
        num_scalar_prefetch=3,
        grid=(NT_EXP,),
        in_specs=[
            pl.BlockSpec(memory_space=pl.ANY),
            pl.BlockSpec((None, None, D_MODEL, D_EXPERT), lambda i, te, nv, src: (layer, te[i], 0, 0)),
            pl.BlockSpec((None, None, D_MODEL, D_EXPERT), lambda i, te, nv, src: (layer, te[i], 0, 0)),
            pl.BlockSpec((None, None, D_EXPERT, D_MODEL), lambda i, te, nv, src: (layer, te[i], 0, 0)),
        ],
        out_specs=pl.BlockSpec((TM_EXP * TOK_ROWS, LANES), lambda i, te, nv, src: (i, 0)),
        scratch_shapes=[
            pltpu.VMEM((TM_EXP * TOK_ROWS, LANES), F32),
            pltpu.VMEM((TM_EXP * TOK_ROWS, LANES), F32),
            pltpu.SemaphoreType.DMA((2,)),
            pltpu.VMEM((D_MODEL, D_EXPERT), BF16),
            pltpu.VMEM((D_MODEL, D_EXPERT), BF16),
            pltpu.VMEM((D_EXPERT, D_MODEL), BF16),
        ],
    )
    return pl.pallas_call(
        _experts_kernel,
        grid_spec=grid_spec,
        out_shape=jax.ShapeDtypeStruct((NP_EXP * TOK_ROWS, LANES), F32),
        compiler_params=_cparams(("arbitrary",)),
        name="experts",
    )(tile_expert, n_valid, src_tok, h2, wg, wu, wd)


def _final_kernel(*refs, emit_h):
    if emit_h:
        (pos_ref, x1_ref, ys_hbm, ewt_ref, gate2_ref, lng_ref, lnb_ref, sc1_ref, sh1_ref,
         xc_ref, xl_ref, h_ref, rbuf0, rbuf1, sem) = refs
    else:
        (pos_ref, x1_ref, ys_hbm, ewt_ref, gate2_ref, lng_ref, lnb_ref,
         xc_ref, xl_ref, rbuf0, rbuf1, sem) = refs
    tm = TM_OUT
    i = pl.program_id(0)
    n_blocks = M_TOK // tm
    bufs = (rbuf0, rbuf1)

    def start(tile, buf, buf_sem):
        _row_gather_start(ys_hbm, buf, buf_sem, pos_ref, tile * tm, tm, tok0=0)
        _row_gather_start(ys_hbm, buf, buf_sem, pos_ref, M_TOK + tile * tm, tm, tok0=tm)

    @pl.when(i == 0)
    def _():
        start(0, rbuf0, sem.at[0])

    def step(cur):
        buf = bufs[cur]
        nxt_buf, nxt_sem = bufs[1 - cur], sem.at[1 - cur]
        _row_gather_wait(ys_hbm, buf, sem.at[cur])
        start(jnp.minimum(i + 1, n_blocks - 1), nxt_buf, nxt_sem)
        w = ewt_ref[...]
        y2 = w[:, 0:1] * _load_token_tiles(buf, 0, tm) + w[:, 1:2] * _load_token_tiles(buf, tm, tm)
        x2 = _layer_norm_rows(DEEPNORM_ALPHA * x1_ref[...] + gate2_ref[...] * y2,
                              lng_ref[...], lnb_ref[...])
        if emit_h:
            h_ref[...] = (x2 * (1.0 + sc1_ref[...]) + sh1_ref[...]).astype(BF16)

        @pl.when(i < M_CTX // tm)
        def _():
            xc_ref[...] = x2

        @pl.when(i >= M_CTX // tm)
        def _():
            xl_ref[...] = x2

        @pl.when(i == n_blocks - 1)
        def _():
            _row_gather_wait(ys_hbm, nxt_buf, nxt_sem)

    for cur in range(2):
        pl.when(i % 2 == cur)(functools.partial(step, cur))


def _final(x1, ys, pos, ewt, ada5, ln_g, ln_b, ada5_next):
    tm = TM_OUT
    emit_h = ada5_next is not None
    vec_spec = pl.BlockSpec((1, D_MODEL), lambda i, pos: (0, 0))
    ada_spec = lambda chunk: pl.BlockSpec(
        (None, None, 1, D_MODEL), lambda i, pos: (_ada_row(i, tm), chunk, 0, 0))
    in_specs = [
        pl.BlockSpec((tm, D_MODEL), lambda i, pos: (i, 0)),
        pl.BlockSpec(memory_space=pl.ANY),
        pl.BlockSpec((tm, LANES), lambda i, pos: (i, 0)),
        ada_spec(5),
        vec_spec, vec_spec,
    ]
    args = [pos, x1, ys, ewt, ada5, ln_g.reshape(1, D_MODEL), ln_b.reshape(1, D_MODEL)]
    out_specs = [
        pl.BlockSpec((tm, D_MODEL), lambda i, pos: (_ctx_block(i, tm), 0)),
        pl.BlockSpec((tm, D_MODEL), lambda i, pos: (_lat_block(i, tm), 0)),
    ]
    out_shape = [jax.ShapeDtypeStruct((M_CTX, D_MODEL), F32),
                 jax.ShapeDtypeStruct((M_LAT, D_MODEL), F32)]
    if emit_h:
        in_specs += [ada_spec(1), ada_spec(0)]
        args += [ada5_next, ada5_next]
        out_specs.append(pl.BlockSpec((tm, D_MODEL), lambda i, pos: (i, 0)))
        out_shape.append(jax.ShapeDtypeStruct((M_TOK, D_MODEL), BF16))
    grid_spec = pltpu.PrefetchScalarGridSpec(
        num_scalar_prefetch=1,
        grid=(M_TOK // tm,),
        in_specs=in_specs,
        out_specs=out_specs,
        scratch_shapes=[
            pltpu.VMEM((2 * tm * TOK_ROWS, LANES), F32),
            pltpu.VMEM((2 * tm * TOK_ROWS, LANES), F32),
            pltpu.SemaphoreType.DMA((2,)),
        ],
    )
    return pl.pallas_call(
        functools.partial(_final_kernel, emit_h=emit_h),
        grid_spec=grid_spec,
        out_shape=out_shape,
        compiler_params=_cparams(("arbitrary",)),
        name="final_ln",
    )(*args)


def _rope_tables():
    rows = T_LAT // GRID_W
    row = jnp.repeat(jnp.arange(rows), GRID_W).astype(F32)
    col = jnp.tile(jnp.arange(GRID_W), rows).astype(F32)
    n_freq = HEAD_DIM // 4
    inv_freq = ROPE_BASE ** (-jnp.arange(n_freq, dtype=F32) / n_freq)
    ang = jnp.concatenate([row[:, None] * inv_freq[None], col[:, None] * inv_freq[None]], axis=-1)
    cos, sin = jnp.cos(ang), jnp.sin(ang)
    return jnp.concatenate([cos, cos], axis=-1), jnp.concatenate([-sin, sin], axis=-1)


def kernel(x_prompt, x_sample, state_retention, c, c_ctx, w_ada, b_ada, w_in, w_out, conv_w, pool_w,
           pool_scale, ret_decay_logit, ret_gn_gain, ln1_g, ln1_b, ln2_g, ln2_b, w_router, router_bias,
           w_gate, w_up, w_down):
    x_ctx = x_prompt.reshape(M_CTX, D_MODEL)
    x_lat = x_sample.reshape(M_LAT, D_MODEL)
    c_all = jnp.concatenate(
        [c_ctx[None, :], c, jnp.zeros((ADA_ROWS - 1 - N_LAT_SEQ, D_MODEL), F32)], axis=0)
    ada = _ada_table(c_all, w_ada, b_ada).reshape(DEPTH, ADA_ROWS, 6, 1, D_MODEL)

    cos_lat, sin_lat = _rope_tables()
    cos_ctx = jnp.ones((T_CTX, HEAD_DIM), F32)
    sin_ctx = jnp.zeros((T_CTX, HEAD_DIM), F32)
    w_router_pad = jnp.pad(w_router, ((0, 0), (0, LANES - N_EXPERTS)))

    states = []
    h = _modulate(x_ctx, x_lat, ada[0])
    for l in range(DEPTH):
        ada5 = ada[l]
        z = _in_proj(h, w_in, l)

        ycp = _conv_pool(z, conv_w[l], pool_w[l], pool_scale[l], None, T=T_CTX, n_seq=N_CTX_SEQ, row_block0=0)
        ycp = _conv_pool(z, conv_w[l], pool_w[l], pool_scale[l], ycp, T=T_LAT, n_seq=N_LAT_SEQ,
                         row_block0=M_CTX // T_LAT)

        dl = jnp.broadcast_to(ret_decay_logit[l][:, :, None, None], (2, N_RET_HEADS, 1, HEAD_DIM))
        yret, s_fin = _retention(z, cos_ctx, sin_ctx, dl, ret_gn_gain[l], None, None,
                                 T=T_CTX, n_seq=N_CTX_SEQ, row_block0=0, layer=l)
        yret = _retention(z, cos_lat, sin_lat, dl, ret_gn_gain[l], state_retention, yret,
                          T=T_LAT, n_seq=N_LAT_SEQ, row_block0=M_CTX // T_LAT, layer=l)
        states.append(s_fin)

        x1, h2, ei, ewt, rk, cnt = _out_proj(ycp, yret, x_ctx, x_lat, w_out, l, ada5,
                                             ln1_g[l], ln1_b[l], w_router_pad, router_bias)
        src_tok, pos, te, n_valid = _route_tables(cnt[:, 0].astype(I32), ei.reshape(-1), rk.reshape(-1))
        ys = _experts(h2, te, n_valid, src_tok, w_gate, w_up, w_down, l)
        if l + 1 < DEPTH:
            x_ctx, x_lat, h = _final(x1, ys, pos, ewt, ada5, ln2_g[l], ln2_b[l], ada[l + 1])
        else:
            x_ctx, x_lat = _final(x1, ys, pos, ewt, ada5, ln2_g[l], ln2_b[l], None)

    y_prompt = x_ctx.reshape(N_CTX_SEQ, T_CTX, D_MODEL)
    y_sample = x_lat.reshape(N_LAT_SEQ, T_LAT, D_MODEL)
    return y_prompt, y_sample, jnp.stack(states, axis=1)
```

```python
import functools

import jax
import jax.numpy as jnp
from jax import lax
from jax.experimental import pallas as pl
from jax.experimental.pallas import tpu as pltpu

F32 = jnp.float32
BF16 = jnp.bfloat16
I32 = jnp.int32

D_MODEL = 2048
N_CTX_SEQ, T_CTX = 16, 256
N_LAT_SEQ, T_LAT = 8, 1024
DEPTH = 2
M_CTX = N_CTX_SEQ * T_CTX
M_LAT = N_LAT_SEQ * T_LAT
M_TOK = M_CTX + M_LAT

GRID_W = 64
D_CONV = D_MODEL // 4
D_POOL = D_MODEL // 4
D_RET = D_MODEL // 2
N_RET_HEADS = 8
HEAD_DIM = D_RET // N_RET_HEADS
POOL_WINDOWS = (2, 4, 8, 16)
POOL_GROUP_DIM = D_POOL // len(POOL_WINDOWS)
CHUNK = 128
ROPE_BASE = 10000.0
N_EXPERTS = 16
EXPERTS_PER_GROUP = 4
N_EXPERT_GROUPS = N_EXPERTS // EXPERTS_PER_GROUP
D_EXPERT = D_MODEL // 4
D_IN_PROJ = 3 * D_CONV + D_POOL + 4 * D_RET
DEEPNORM_ALPHA = (2.0 * DEPTH) ** 0.25
LN_EPS = 1e-5
ADA_ROWS = 16

LANES = 128
VMEM_LIMIT = 56 * 1024 * 1024

TM_IN = 1024
TN_IN = 1024
TOK_ROWS = D_MODEL // LANES
TOK_STRIDE = TOK_ROWS + 1
TM_OUT = 256
TM_EXP = 256
N_PAIR = 2 * M_TOK
NP_EXP = N_PAIR + N_EXPERTS * TM_EXP
NT_EXP = NP_EXP // TM_EXP


def _cparams(sem):
    return pltpu.CompilerParams(dimension_semantics=sem, vmem_limit_bytes=VMEM_LIMIT)


def _silu(x):
    return x * jax.nn.sigmoid(x)


def _ada_row(i, tm):
    n_ctx_tiles = M_CTX // tm
    per_batch = T_LAT // tm
    return jnp.where(i < n_ctx_tiles, 0, 1 + (i - n_ctx_tiles) // per_batch)


def _ada_kernel(c_ref, w_ref, b_ref, o_ref):
    s = _silu(c_ref[...]).astype(BF16)
    o_ref[...] = jnp.dot(s, w_ref[...].astype(BF16), preferred_element_type=F32) + b_ref[...]


def _ada_table(c_all, w_ada, b_ada):
    tn = 1024
    n6 = 6 * D_MODEL
    return pl.pallas_call(
        _ada_kernel,
        grid=(DEPTH, n6 // tn),
        in_specs=[
            pl.BlockSpec((ADA_ROWS, D_MODEL), lambda l, j: (0, 0)),
            pl.BlockSpec((None, D_MODEL, tn), lambda l, j: (l, 0, j)),
            pl.BlockSpec((None, 1, tn), lambda l, j: (l, 0, j)),
        ],
        out_specs=pl.BlockSpec((None, ADA_ROWS, tn), lambda l, j: (l, 0, j)),
        out_shape=jax.ShapeDtypeStruct((DEPTH, ADA_ROWS, n6), F32),
        compiler_params=_cparams(("arbitrary", "arbitrary")),
        name="ada_table",
    )(c_all, w_ada, b_ada.reshape(DEPTH, 1, n6))


def _ctx_block(i, tm):
    return jnp.minimum(i, M_CTX // tm - 1)


def _lat_block(i, tm):
    return jnp.maximum(i - M_CTX // tm, 0)


def _modulate_kernel(xc_ref, xl_ref, sc_ref, sh_ref, h_ref):
    x = jnp.where(pl.program_id(0) < M_CTX // xc_ref.shape[0], xc_ref[...], xl_ref[...])
    h_ref[...] = (x * (1.0 + sc_ref[...]) + sh_ref[...]).astype(BF16)


def _modulate(x_ctx, x_lat, ada5):
    tm = TM_OUT
    ada_spec = lambda chunk: pl.BlockSpec(
        (None, None, 1, D_MODEL), lambda i: (_ada_row(i, tm), chunk, 0, 0))
    return pl.pallas_call(
        _modulate_kernel,
        grid=(M_TOK // tm,),
        in_specs=[
            pl.BlockSpec((tm, D_MODEL), lambda i: (_ctx_block(i, tm), 0)),
            pl.BlockSpec((tm, D_MODEL), lambda i: (_lat_block(i, tm), 0)),
            ada_spec(1), ada_spec(0),
        ],
        out_specs=pl.BlockSpec((tm, D_MODEL), lambda i: (i, 0)),
        out_shape=jax.ShapeDtypeStruct((M_TOK, D_MODEL), BF16),
        compiler_params=_cparams(("arbitrary",)),
        name="modulate",
    )(x_ctx, x_lat, ada5, ada5)


def _inproj_kernel(h_ref, w_ref, o_ref, wbf_scr):
    @pl.when(pl.program_id(1) == 0)
    def _():
        wbf_scr[...] = w_ref[...].astype(BF16)

    o_ref[...] = jnp.dot(h_ref[...], wbf_scr[...], preferred_element_type=F32).astype(BF16)


def _in_proj(h, w_in, layer):
    return pl.pallas_call(
        _inproj_kernel,
        grid=(D_IN_PROJ // TN_IN, M_TOK // TM_IN),
        in_specs=[
            pl.BlockSpec((TM_IN, D_MODEL), lambda j, i: (i, 0)),
            pl.BlockSpec((None, D_MODEL, TN_IN), lambda j, i: (layer, 0, j)),
        ],
        out_specs=pl.BlockSpec((TM_IN, TN_IN), lambda j, i: (i, j)),
        out_shape=jax.ShapeDtypeStruct((M_TOK, D_IN_PROJ), BF16),
        scratch_shapes=[pltpu.VMEM((D_MODEL, TN_IN), BF16)],
        compiler_params=_cparams(("arbitrary", "arbitrary")),
        name="in_proj",
    )(h, w_in)


def _convpool_kernel(*refs, T, aliased):
    if aliased:
        z_ref, cw_ref, pw_ref, ps_ref, _, o_ref, band_ref = refs
    else:
        z_ref, cw_ref, pw_ref, ps_ref, o_ref, band_ref = refs

    @pl.when(pl.program_id(0) == 0)
    def _():
        row = lax.broadcasted_iota(I32, (T, T), 0)
        col = lax.broadcasted_iota(I32, (T, T), 1)
        d = col - row
        for gi, w in enumerate(POOL_WINDOWS):
            band_ref[gi] = jnp.where((d >= -(w // 2)) & (d < w // 2), 1.0, 0.0).astype(BF16)

    cb = z_ref[:, 0:D_CONV].astype(F32)
    cc = z_ref[:, D_CONV:2 * D_CONV].astype(F32)
    cv = z_ref[:, 2 * D_CONV:3 * D_CONV].astype(F32)
    u = cc * cv
    t = lax.broadcasted_iota(I32, (T, D_CONV), 0)
    u_prev = jnp.where(t == 0, 0.0, pltpu.roll(u, 1, 0))
    u_next = jnp.where(t == T - 1, 0.0, pltpu.roll(u, T - 1, 0))
    conv = u_prev * cw_ref[0:1, :] + u * cw_ref[1:2, :] + u_next * cw_ref[2:3, :]
    o_ref[:, 0:D_CONV] = (cb * conv).astype(BF16)

    tt = lax.broadcasted_iota(I32, (T, POOL_GROUP_DIM), 0)
    for gi, w in enumerate(POOL_WINDOWS):
        lo = 3 * D_CONV + gi * POOL_GROUP_DIM
        p = z_ref[:, lo:lo + POOL_GROUP_DIM]
        win = jnp.dot(band_ref[gi], p, preferred_element_type=F32)
        cnt = (jnp.minimum(tt + w // 2, T) - jnp.maximum(tt - w // 2, 0)).astype(F32)
        pooled = win / cnt - p.astype(F32)
        y = jnp.dot(pooled.astype(BF16), pw_ref[gi].astype(BF16), preferred_element_type=F32)
        y = y * ps_ref[:, gi * POOL_GROUP_DIM:(gi + 1) * POOL_GROUP_DIM]
        o_ref[:, D_CONV + gi * POOL_GROUP_DIM:D_CONV + (gi + 1) * POOL_GROUP_DIM] = y.astype(BF16)


def _conv_pool(z, conv_w, pool_w, pool_scale, prev, *, T, n_seq, row_block0):
    aliased = prev is not None
    in_specs = [
        pl.BlockSpec((T, D_IN_PROJ - 4 * D_RET), lambda b: (row_block0 + b, 0)),
        pl.BlockSpec((3, D_CONV), lambda b: (0, 0)),
        pl.BlockSpec((len(POOL_WINDOWS), POOL_GROUP_DIM, POOL_GROUP_DIM), lambda b: (0, 0, 0)),
        pl.BlockSpec((1, D_POOL), lambda b: (0, 0)),
    ]
    args = [z, conv_w, pool_w, pool_scale.reshape(1, D_POOL)]
    if aliased:
        in_specs.append(pl.BlockSpec(memory_space=pl.ANY))
        args.append(prev)
    return pl.pallas_call(
        functools.partial(_convpool_kernel, T=T, aliased=aliased),
        grid=(n_seq,),
        in_specs=in_specs,
        out_specs=pl.BlockSpec((T, D_CONV + D_POOL), lambda b: (row_block0 + b, 0)),
        out_shape=jax.ShapeDtypeStruct((M_TOK, D_CONV + D_POOL), BF16),
        scratch_shapes=[pltpu.VMEM((len(POOL_WINDOWS), T, T), BF16)],
        input_output_aliases={4: 0} if aliased else {},
        compiler_params=_cparams(("arbitrary",)),
        name="conv_pool_T%d" % T,
    )(*args)


def _log_sigmoid(x):
    return jnp.minimum(x, 0.0) - jnp.log1p(jnp.exp(-jnp.abs(x)))


def _retention_kernel(*refs, T, latent):
    if latent:
        (q_ref, k_ref, v_ref, g_ref, cos_ref, sin_ref, dl_ref, gain_ref, s0_ref, _,
         y_ref, kr_scr, u_scr, s_scr) = refs
    else:
        (q_ref, k_ref, v_ref, g_ref, cos_ref, sin_ref, dl_ref, gain_ref,
         y_ref, sfin_ref, kr_scr, u_scr, s_scr) = refs
    n_chunks = T // CHUNK
    half = HEAD_DIM // 2

    lg_f = _log_sigmoid(dl_ref[0])
    lg_b = _log_sigmoid(dl_ref[1])
    row = lax.broadcasted_iota(I32, (CHUNK, CHUNK), 0).astype(F32)
    col = lax.broadcasted_iota(I32, (CHUNK, CHUNK), 1).astype(F32)
    diff = row - col
    decay = (jnp.where(diff >= 0, jnp.exp(lg_f * jnp.maximum(diff, 0.0)), 0.0)
             + jnp.where(diff <= 0, jnp.exp(lg_b * jnp.maximum(-diff, 0.0)), 0.0))
    xi_f = jnp.exp(lg_f * (row + 1.0))
    xi_b = jnp.exp(lg_b * (CHUNK - row))
    zeta_f = jnp.exp(lg_f * (CHUNK - 1.0 - row))
    zeta_b = jnp.exp(lg_b * row)
    g_f = jnp.exp(lg_f * CHUNK)
    g_b = jnp.exp(lg_b * CHUNK)

    def rope(x, sl):
        return x * cos_ref[sl, :] + pltpu.roll(x, half, 1) * sin_ref[sl, :]

    tn_dims = (((0,), (0,)), ((), ()))
    nt_dims = (((1,), (1,)), ((), ()))

    for c in range(n_chunks):
        sl = slice(c * CHUNK, (c + 1) * CHUNK)
        kr = rope(k_ref[sl, :].astype(F32), sl) * (HEAD_DIM ** -0.5)
        kr_scr[sl, :] = kr.astype(BF16)
        v = v_ref[sl, :]
        u_scr[0, c] = lax.dot_general((kr * zeta_f).astype(BF16), v, tn_dims, preferred_element_type=F32)
        u_scr[1, c] = lax.dot_general((kr * zeta_b).astype(BF16), v, tn_dims, preferred_element_type=F32)

    s = s0_ref[0] if latent else jnp.zeros((HEAD_DIM, HEAD_DIM), F32)
    for c in range(n_chunks):
        s_scr[0, c] = s.astype(BF16)
        s = g_f * s + u_scr[0, c]
    if not latent:
        sfin_ref[0] = s
    s = s0_ref[1] if latent else jnp.zeros((HEAD_DIM, HEAD_DIM), F32)
    for c in reversed(range(n_chunks)):
        s_scr[1, c] = s.astype(BF16)
        s = g_b * s + u_scr[1, c]
    if not latent:
        sfin_ref[1] = s

    for c in range(n_chunks):
        sl = slice(c * CHUNK, (c + 1) * CHUNK)
        qr = rope(q_ref[sl, :].astype(F32), sl)
        scores = lax.dot_general(qr.astype(BF16), kr_scr[sl, :], nt_dims, preferred_element_type=F32)
        o = jnp.dot((scores * decay).astype(BF16), v_ref[sl, :], preferred_element_type=F32)
        o += jnp.dot((qr * xi_f).astype(BF16), s_scr[0, c], preferred_element_type=F32)
        o += jnp.dot((qr * xi_b).astype(BF16), s_scr[1, c], preferred_element_type=F32)
        mu = jnp.mean(o, axis=-1, keepdims=True)
        dev = o - mu
        var = jnp.mean(dev * dev, axis=-1, keepdims=True)
        on = dev * lax.rsqrt(var + LN_EPS) * gain_ref[...]
        y_ref[sl, :] = (_silu(g_ref[sl, :].astype(F32)) * on).astype(BF16)


def _retention(z, cos_t, sin_t, decay_logit_b, gain, s0, prev, *, T, n_seq, row_block0, layer):
    latent = s0 is not None
    qcol0 = (3 * D_CONV + D_POOL) // HEAD_DIM

    def zspec(k):
        return pl.BlockSpec((T, HEAD_DIM), lambda b, h: (row_block0 + b, qcol0 + k * N_RET_HEADS + h))

    in_specs = [
        zspec(0), zspec(1), zspec(2), zspec(3),
        pl.BlockSpec((T, HEAD_DIM), lambda b, h: (0, 0)),
        pl.BlockSpec((T, HEAD_DIM), lambda b, h: (0, 0)),
        pl.BlockSpec((2, None, 1, HEAD_DIM), lambda b, h: (0, h, 0, 0)),
        pl.BlockSpec((1, HEAD_DIM), lambda b, h: (0, h)),
    ]
    args = [z, z, z, z, cos_t, sin_t, decay_logit_b, gain.reshape(1, D_RET)]
    y_spec = pl.BlockSpec((T, HEAD_DIM), lambda b, h: (row_block0 + b, h))
    y_shape = jax.ShapeDtypeStruct((M_TOK, D_RET), BF16)
    scratch = [
        pltpu.VMEM((T, HEAD_DIM), BF16),
        pltpu.VMEM((2, T // CHUNK, HEAD_DIM, HEAD_DIM), F32),
        pltpu.VMEM((2, T // CHUNK, HEAD_DIM, HEAD_DIM), BF16),
    ]
    if latent:
        in_specs += [
            pl.BlockSpec((None, None, 2, None, HEAD_DIM, HEAD_DIM), lambda b, h: (b, layer, 0, h, 0, 0)),
            pl.BlockSpec(memory_space=pl.ANY),
        ]
        args += [s0, prev]
        return pl.pallas_call(
            functools.partial(_retention_kernel, T=T, latent=True),
            grid=(n_seq, N_RET_HEADS),
            in_specs=in_specs,
            out_specs=y_spec,
            out_shape=y_shape,
            scratch_shapes=scratch,
            input_output_aliases={9: 0},
            compiler_params=_cparams(("arbitrary", "arbitrary")),
            name="retention_latent",
        )(*args)
    return pl.pallas_call(
        functools.partial(_retention_kernel, T=T, latent=False),
        grid=(n_seq, N_RET_HEADS),
        in_specs=in_specs,
        out_specs=[y_spec,
                   pl.BlockSpec((None, 2, None, HEAD_DIM, HEAD_DIM), lambda b, h: (b, 0, h, 0, 0))],
        out_shape=[y_shape,
                   jax.ShapeDtypeStruct((n_seq, 2, N_RET_HEADS, HEAD_DIM, HEAD_DIM), F32)],
        scratch_shapes=scratch,
        compiler_params=_cparams(("arbitrary", "arbitrary")),
        name="retention_context",
    )(*args)


def _layer_norm_rows(r, g, b):
    mu = jnp.mean(r, axis=-1, keepdims=True)
    dev = r - mu
    var = jnp.mean(dev * dev, axis=-1, keepdims=True)
    return dev * lax.rsqrt(var + LN_EPS) * g + b


def _top2_of4(vals):
    top1 = jnp.maximum(jnp.maximum(vals[0], vals[1]), jnp.maximum(vals[2], vals[3]))
    idx1 = jnp.where(vals[0] == top1, 0, jnp.where(vals[1] == top1, 1, jnp.where(vals[2] == top1, 2, 3)))
    neg = jnp.float32(-jnp.inf)
    rest = [jnp.where(idx1 == j, neg, vals[j]) for j in range(4)]
    top2 = jnp.maximum(jnp.maximum(rest[0], rest[1]), jnp.maximum(rest[2], rest[3]))
    idx2 = jnp.where(rest[0] == top2, 0, jnp.where(rest[1] == top2, 1, jnp.where(rest[2] == top2, 2, 3)))
    return top1, idx1, top2, idx2


def _store_token_tiles(ref, val):
    n = val.shape[0]
    for c in range(TOK_ROWS):
        ref[pl.ds(c, n, stride=TOK_STRIDE), :] = val[:, c * LANES:(c + 1) * LANES]
    ref[pl.ds(TOK_ROWS, n, stride=TOK_STRIDE), :] = jnp.zeros((n, LANES), val.dtype)


def _load_token_tiles(ref, tok0, n):
    return jnp.concatenate(
        [ref[pl.ds(tok0 * TOK_STRIDE + c, n, stride=TOK_STRIDE), :] for c in range(TOK_ROWS)], axis=1)


def _outproj_kernel(ycp_ref, yret_ref, xc_ref, xl_ref, wo_hbm, gate1_ref, lng_ref, lnb_ref, sc2_ref, sh2_ref,
                    wr_ref, rb_ref, x1_ref, h2_ref, ei_ref, ewt_ref, rk_ref, cnt_ref,
                    carry_scr, wo_ref, wstage, wsem, *, layer):
    tm = xc_ref.shape[0]
    half_k = D_CONV + D_POOL
    i = pl.program_id(0)

    @pl.when(i == 0)
    def _():
        carry_scr[...] = jnp.zeros_like(carry_scr)
        rows = wstage.shape[1]
        n_chunks = D_MODEL // rows

        def chunk_copy(c):
            return pltpu.make_async_copy(wo_hbm.at[layer, pl.ds(c * rows, rows), :], wstage.at[c % 2],
                                         wsem.at[c % 2])
        chunk_copy(0).start()
        for c in range(n_chunks):
            if c + 1 < n_chunks:
                chunk_copy(c + 1).start()
            chunk_copy(c).wait()
            wo_ref[c * rows:(c + 1) * rows, :] = wstage[c % 2].astype(BF16)

    y = jnp.dot(ycp_ref[...], wo_ref[0:half_k, :], preferred_element_type=F32)
    y += jnp.dot(yret_ref[...], wo_ref[half_k:, :], preferred_element_type=F32)
    x = jnp.where(i < M_CTX // tm, xc_ref[...], xl_ref[...])
    x1 = _layer_norm_rows(DEEPNORM_ALPHA * x + gate1_ref[...] * y, lng_ref[...], lnb_ref[...])
    x1_ref[...] = x1
    h2 = x1 * (1.0 + sc2_ref[...]) + sh2_ref[...]
    _store_token_tiles(h2_ref, h2)

    h_hi = h2.astype(BF16)
    h_lo = (h2 - h_hi.astype(F32)).astype(BF16)
    wr = wr_ref[...]
    w_hi = wr.astype(BF16)
    w_lo = (wr - w_hi.astype(F32)).astype(BF16)
    logits = (jnp.dot(h_hi, w_hi, preferred_element_type=F32)
              + jnp.dot(h_lo, w_hi, preferred_element_type=F32)
              + jnp.dot(h_hi, w_lo, preferred_element_type=F32))
    lt = logits.T
    rows = [lt[e:e + 1, :] for e in range(N_EXPERTS)]

    mx = rows[0]
    for e in range(1, N_EXPERTS):
        mx = jnp.maximum(mx, rows[e])
    ex = [jnp.exp(r - mx) for r in rows]
    den = ex[0]
    for e in range(1, N_EXPERTS):
        den = den + ex[e]
    score = [x / den for x in ex]
    biased = [score[e] + rb_ref[e] for e in range(N_EXPERTS)]

    best = None
    for gi in range(N_EXPERT_GROUPS):
        t1, i1, t2, i2 = _top2_of4(biased[gi * EXPERTS_PER_GROUP:(gi + 1) * EXPERTS_PER_GROUP])
        gs = t1 + t2
        e1 = gi * EXPERTS_PER_GROUP + i1
        e2 = gi * EXPERTS_PER_GROUP + i2
        if best is None:
            best = (gs, e1, e2)
        else:
            take = gs > best[0]
            best = (jnp.where(take, gs, best[0]), jnp.where(take, e1, best[1]), jnp.where(take, e2, best[2]))
    _, e1, e2 = best
    zero = jnp.zeros_like(score[0])
    w1 = zero
    w2 = zero
    for e in range(N_EXPERTS):
        w1 = w1 + jnp.where(e1 == e, score[e], 0.0)
        w2 = w2 + jnp.where(e2 == e, score[e], 0.0)
    wsum = w1 + w2
    ei_ref[0:1, :] = e1
    ei_ref[1:2, :] = e2
    ewt_ref[...] = jnp.concatenate(
        [w1 / wsum, w2 / wsum, jnp.zeros((LANES - 2, tm), F32)], axis=0).T

    onehot = jnp.concatenate(
        [jnp.where((e1 == e) | (e2 == e), 1.0, 0.0) for e in range(N_EXPERTS)], axis=0)
    s_i = lax.broadcasted_iota(I32, (tm, tm), 0)
    t_i = lax.broadcasted_iota(I32, (tm, tm), 1)
    tri = jnp.where(s_i < t_i, 1.0, 0.0).astype(BF16)
    prefix = jnp.dot(onehot.astype(BF16), tri, preferred_element_type=F32) + carry_scr[:, 0:1]
    r1 = zero
    r2 = zero
    for e in range(N_EXPERTS):
        r1 = r1 + jnp.where(e1 == e, prefix[e:e + 1, :], 0.0)
        r2 = r2 + jnp.where(e2 == e, prefix[e:e + 1, :], 0.0)
    rk_ref[0:1, :] = r1.astype(I32)
    rk_ref[1:2, :] = r2.astype(I32)
    carry_scr[...] = carry_scr[...] + jnp.sum(onehot, axis=1, keepdims=True)
    cnt_ref[...] = carry_scr[...]


def _out_proj(ycp, yret, x_ctx, x_lat, w_out, layer, ada5, ln_g, ln_b, w_router_pad, router_bias):
    tm = TM_OUT
    ada_spec = lambda chunk: pl.BlockSpec(
        (None, None, 1, D_MODEL), lambda i: (_ada_row(i, tm), chunk, 0, 0))
    vec_spec = pl.BlockSpec((1, D_MODEL), lambda i: (0, 0))
    route_spec = pl.BlockSpec((2, tm), lambda i: (0, i))
    return pl.pallas_call(
        functools.partial(_outproj_kernel, layer=layer),
        grid=(M_TOK // tm,),
        in_specs=[
            pl.BlockSpec((tm, D_CONV + D_POOL), lambda i: (i, 0)),
            pl.BlockSpec((tm, D_RET), lambda i: (i, 0)),
            pl.BlockSpec((tm, D_MODEL), lambda i: (_ctx_block(i, tm), 0)),
            pl.BlockSpec((tm, D_MODEL), lambda i: (_lat_block(i, tm), 0)),
            pl.BlockSpec(memory_space=pl.ANY),
            ada_spec(2),
            vec_spec, vec_spec,
            ada_spec(4),
            ada_spec(3),
            pl.BlockSpec((D_MODEL, LANES), lambda i: (0, 0)),
            pl.BlockSpec(memory_space=pltpu.SMEM),
        ],
        out_specs=[
            pl.BlockSpec((tm, D_MODEL), lambda i: (i, 0)),
            pl.BlockSpec((tm * TOK_STRIDE, LANES), lambda i: (i, 0)),
            route_spec,
            pl.BlockSpec((tm, LANES), lambda i: (i, 0)),
            route_spec,
            pl.BlockSpec((N_EXPERTS, LANES), lambda i: (0, 0)),
        ],
        out_shape=[
            jax.ShapeDtypeStruct((M_TOK, D_MODEL), F32),
            jax.ShapeDtypeStruct((M_TOK * TOK_STRIDE, LANES), F32),
            jax.ShapeDtypeStruct((2, M_TOK), I32),
            jax.ShapeDtypeStruct((M_TOK, LANES), F32),
            jax.ShapeDtypeStruct((2, M_TOK), I32),
            jax.ShapeDtypeStruct((N_EXPERTS, LANES), F32),
        ],
        scratch_shapes=[
            pltpu.VMEM((N_EXPERTS, LANES), F32),
            pltpu.VMEM((D_MODEL, D_MODEL), BF16),
            pltpu.VMEM((2, 256, D_MODEL), F32),
            pltpu.SemaphoreType.DMA((2,)),
        ],
        compiler_params=_cparams(("arbitrary",)),
        name="out_proj_router",
    )(ycp, yret, x_ctx, x_lat, w_out, ada5, ln_g.reshape(1, D_MODEL), ln_b.reshape(1, D_MODEL),
      ada5, ada5, w_router_pad, router_bias)


def _route_kernel(cnt_ref, ei_ref, rk_ref, src_ref, pos_ref, te_ref, nv_ref, start_scr):
    def zero(r, carry):
        src_ref[r] = 0
        return carry
    lax.fori_loop(0, NP_EXP, zero, 0, unroll=8)

    tile = jnp.int32(0)
    for e in range(N_EXPERTS):
        start_scr[e] = tile * TM_EXP
        n_tiles = lax.shift_right_logical(cnt_ref[e] + (TM_EXP - 1), TM_EXP.bit_length() - 1)

        def mark(j, carry, e=e, tile=tile):
            te_ref[tile + j] = e
            return carry
        lax.fori_loop(0, n_tiles, mark, 0)
        tile = tile + n_tiles
    nv_ref[0] = tile
    last_expert = te_ref[tile - 1]

    def mark_unused(j, carry):
        te_ref[j] = last_expert
        return carry
    lax.fori_loop(tile, NT_EXP, mark_unused, 0)

    def place(t, carry):
        for k in range(2):
            pair = k * M_TOK + t
            row = start_scr[ei_ref[pair]] + rk_ref[pair]
            pos_ref[pair] = row
            src_ref[row] = t
        return carry
    lax.fori_loop(0, M_TOK, place, 0, unroll=4)


def _route_tables(cnt, ei, rk):
    smem = pl.BlockSpec(memory_space=pltpu.SMEM)
    return pl.pallas_call(
        _route_kernel,
        in_specs=[smem, smem, smem],
        out_specs=[smem, smem, smem, smem],
        out_shape=[
            jax.ShapeDtypeStruct((NP_EXP,), I32),
            jax.ShapeDtypeStruct((N_PAIR,), I32),
            jax.ShapeDtypeStruct((NT_EXP,), I32),
            jax.ShapeDtypeStruct((1,), I32),
        ],
        scratch_shapes=[pltpu.SMEM((N_EXPERTS,), I32)],
        name="route_tables",
    )(cnt, ei, rk)


def _row_gather_start(src_hbm, buf, sem, idx_ref, base, n_tok, tok0=0):
    for r in range(n_tok):
        pltpu.make_async_copy(src_hbm.at[pl.ds(idx_ref[base + r] * TOK_STRIDE, TOK_ROWS), :],
                              buf.at[pl.ds((tok0 + r) * TOK_STRIDE, TOK_ROWS), :], sem).start()


def _row_gather_wait(src_hbm, buf, sem):
    n_rows = buf.shape[0] // TOK_STRIDE * TOK_ROWS
    pltpu.make_async_copy(src_hbm.at[pl.ds(0, n_rows), :], buf.at[pl.ds(0, n_rows), :], sem).wait()


def _experts_kernel(te_ref, nv_ref, src_ref, h2_hbm, wg_ref, wu_ref, wd_ref, o_ref, xbuf0, xbuf1, gsem,
                    wg_bf, wu_bf, wd_bf):
    i = pl.program_id(0)
    n_valid = nv_ref[0]
    bufs = (xbuf0, xbuf1)

    @pl.when(i == 0)
    def _():
        _row_gather_start(h2_hbm, xbuf0, gsem.at[0], src_ref, 0, TM_EXP)

    @pl.when((i < n_valid) & ((i == 0) | (te_ref[i] != te_ref[jnp.maximum(i - 1, 0)])))
    def _():
        wg_bf[...] = wg_ref[...].astype(BF16)
        wu_bf[...] = wu_ref[...].astype(BF16)
        wd_bf[...] = wd_ref[...].astype(BF16)

    def step(cur):
        nxt_buf, nxt_sem = bufs[1 - cur], gsem.at[1 - cur]
        _row_gather_wait(h2_hbm, bufs[cur], gsem.at[cur])
        nxt = jnp.minimum(i + 1, NT_EXP - 1)
        _row_gather_start(h2_hbm, nxt_buf, nxt_sem, src_ref, nxt * TM_EXP, TM_EXP)
        x = _load_token_tiles(bufs[cur], 0, TM_EXP).astype(BF16)
        g = jnp.dot(x, wg_bf[...], preferred_element_type=F32)
        u = jnp.dot(x, wu_bf[...], preferred_element_type=F32)
        a = (_silu(g) * u).astype(BF16)
        _store_token_tiles(o_ref, jnp.dot(a, wd_bf[...], preferred_element_type=F32))

        @pl.when(i == n_valid - 1)
        def _():
            _row_gather_wait(h2_hbm, nxt_buf, nxt_sem)

    for cur in range(2):
        pl.when((i < n_valid) & (i % 2 == cur))(functools.partial(step, cur))

    @pl.when(i >= n_valid)
    def _():
        o_ref[...] = jnp.zeros_like(o_ref)


def _experts(h2, tile_expert, n_valid, src_tok, wg, wu, wd, layer):
    grid_spec = pltpu.PrefetchScalarGridSpec(
        num_scalar_prefetch=3,
        grid=(NT_EXP,),
        in_specs=[
            pl.BlockSpec(memory_space=pl.ANY),
            pl.BlockSpec((None, None, D_MODEL, D_EXPERT), lambda i, te, nv, src: (layer, te[i], 0, 0)),
            pl.BlockSpec((None, None, D_MODEL, D_EXPERT), lambda i, te, nv, src: (layer, te[i], 0, 0)),
            pl.BlockSpec((None, None, D_EXPERT, D_MODEL), lambda i, te, nv, src: (layer, te[i], 0, 0)),
        ],
        out_specs=pl.BlockSpec((TM_EXP * TOK_STRIDE, LANES), lambda i, te, nv, src: (i, 0)),
        scratch_shapes=[
            pltpu.VMEM((TM_EXP * TOK_STRIDE, LANES), F32),
            pltpu.VMEM((TM_EXP * TOK_STRIDE, LANES), F32),
            pltpu.SemaphoreType.DMA((2,)),
            pltpu.VMEM((D_MODEL, D_EXPERT), BF16),
            pltpu.VMEM((D_MODEL, D_EXPERT), BF16),
            pltpu.VMEM((D_EXPERT, D_MODEL), BF16),
        ],
    )
    return pl.pallas_call(
        _experts_kernel,
        grid_spec=grid_spec,
        out_shape=jax.ShapeDtypeStruct((NP_EXP * TOK_STRIDE, LANES), F32),
        compiler_params=_cparams(("arbitrary",)),
        name="experts",
    )(tile_expert, n_valid, src_tok, h2, wg, wu, wd)


def _final_kernel(*refs, emit_h):
    if emit_h:
        (pos_ref, x1_ref, ys_hbm, ewt_ref, gate2_ref, lng_ref, lnb_ref, sc1_ref, sh1_ref,
         xc_ref, xl_ref, h_ref, rbuf0, rbuf1, sem) = refs
    else:
        (pos_ref, x1_ref, ys_hbm, ewt_ref, gate2_ref, lng_ref, lnb_ref,
         xc_ref, xl_ref, rbuf0, rbuf1, sem) = refs
    tm = TM_OUT
    i = pl.program_id(0)
    n_blocks = M_TOK // tm
    bufs = (rbuf0, rbuf1)

    def start(tile, buf, buf_sem):
        _row_gather_start(ys_hbm, buf, buf_sem, pos_ref, tile * tm, tm, tok0=0)
        _row_gather_start(ys_hbm, buf, buf_sem, pos_ref, M_TOK + tile * tm, tm, tok0=tm)

    @pl.when(i == 0)
    def _():
        start(0, rbuf0, sem.at[0])

    def step(cur):
        buf = bufs[cur]
        nxt_buf, nxt_sem = bufs[1 - cur], sem.at[1 - cur]
        _row_gather_wait(ys_hbm, buf, sem.at[cur])
        start(jnp.minimum(i + 1, n_blocks - 1), nxt_buf, nxt_sem)
        w = ewt_ref[...]
        y2 = w[:, 0:1] * _load_token_tiles(buf, 0, tm) + w[:, 1:2] * _load_token_tiles(buf, tm, tm)
        x2 = _layer_norm_rows(DEEPNORM_ALPHA * x1_ref[...] + gate2_ref[...] * y2,
                              lng_ref[...], lnb_ref[...])
        if emit_h:
            h_ref[...] = (x2 * (1.0 + sc1_ref[...]) + sh1_ref[...]).astype(BF16)

        @pl.when(i < M_CTX // tm)
        def _():
            xc_ref[...] = x2

        @pl.when(i >= M_CTX // tm)
        def _():
            xl_ref[...] = x2

        @pl.when(i == n_blocks - 1)
        def _():
            _row_gather_wait(ys_hbm, nxt_buf, nxt_sem)

    for cur in range(2):
        pl.when(i % 2 == cur)(functools.partial(step, cur))


def _final(x1, ys, pos, ewt, ada5, ln_g, ln_b, ada5_next):
    tm = TM_OUT
    emit_h = ada5_next is not None
    vec_spec = pl.BlockSpec((1, D_MODEL), lambda i, pos: (0, 0))
    ada_spec = lambda chunk: pl.BlockSpec(
        (None, None, 1, D_MODEL), lambda i, pos: (_ada_row(i, tm), chunk, 0, 0))
    in_specs = [
        pl.BlockSpec((tm, D_MODEL), lambda i, pos: (i, 0)),
        pl.BlockSpec(memory_space=pl.ANY),
        pl.BlockSpec((tm, LANES), lambda i, pos: (i, 0)),
        ada_spec(5),
        vec_spec, vec_spec,
    ]
    args = [pos, x1, ys, ewt, ada5, ln_g.reshape(1, D_MODEL), ln_b.reshape(1, D_MODEL)]
    out_specs = [
        pl.BlockSpec((tm, D_MODEL), lambda i, pos: (_ctx_block(i, tm), 0)),
        pl.BlockSpec((tm, D_MODEL), lambda i, pos: (_lat_block(i, tm), 0)),
    ]
    out_shape = [jax.ShapeDtypeStruct((M_CTX, D_MODEL), F32),
                 jax.ShapeDtypeStruct((M_LAT, D_MODEL), F32)]
    if emit_h:
        in_specs += [ada_spec(1), ada_spec(0)]
        args += [ada5_next, ada5_next]
        out_specs.append(pl.BlockSpec((tm, D_MODEL), lambda i, pos: (i, 0)))
        out_shape.append(jax.ShapeDtypeStruct((M_TOK, D_MODEL), BF16))
    grid_spec = pltpu.PrefetchScalarGridSpec(
        num_scalar_prefetch=1,
        grid=(M_TOK // tm,),
        in_specs=in_specs,
        out_specs=out_specs,
        scratch_shapes=[
            pltpu.VMEM((2 * tm * TOK_STRIDE, LANES), F32),
            pltpu.VMEM((2 * tm * TOK_STRIDE, LANES), F32),
            pltpu.SemaphoreType.DMA((2,)),
        ],
    )
    return pl.pallas_call(
        functools.partial(_final_kernel, emit_h=emit_h),
        grid_spec=grid_spec,
        out_shape=out_shape,
        compiler_params=_cparams(("arbitrary",)),
        name="final_ln",
    )(*args)


def _rope_tables():
    rows = T_LAT // GRID_W
    row = jnp.repeat(jnp.arange(rows), GRID_W).astype(F32)
    col = jnp.tile(jnp.arange(GRID_W), rows).astype(F32)
    n_freq = HEAD_DIM // 4
    inv_freq = ROPE_BASE ** (-jnp.arange(n_freq, dtype=F32) / n_freq)
    ang = jnp.concatenate([row[:, None] * inv_freq[None], col[:, None] * inv_freq[None]], axis=-1)
    cos, sin = jnp.cos(ang), jnp.sin(ang)
    return jnp.concatenate([cos, cos], axis=-1), jnp.concatenate([-sin, sin], axis=-1)


def kernel(x_prompt, x_sample, state_retention, c, c_ctx, w_ada, b_ada, w_in, w_out, conv_w, pool_w,
           pool_scale, ret_decay_logit, ret_gn_gain, ln1_g, ln1_b, ln2_g, ln2_b, w_router, router_bias,
           w_gate, w_up, w_down):
    x_ctx = x_prompt.reshape(M_CTX, D_MODEL)
    x_lat = x_sample.reshape(M_LAT, D_MODEL)
    c_all = jnp.concatenate(
        [c_ctx[None, :], c, jnp.zeros((ADA_ROWS - 1 - N_LAT_SEQ, D_MODEL), F32)], axis=0)
    ada = _ada_table(c_all, w_ada, b_ada).reshape(DEPTH, ADA_ROWS, 6, 1, D_MODEL)

    cos_lat, sin_lat = _rope_tables()
    cos_ctx = jnp.ones((T_CTX, HEAD_DIM), F32)
    sin_ctx = jnp.zeros((T_CTX, HEAD_DIM), F32)
    w_router_pad = jnp.pad(w_router, ((0, 0), (0, LANES - N_EXPERTS)))

    states = []
    h = _modulate(x_ctx, x_lat, ada[0])
    for l in range(DEPTH):
        ada5 = ada[l]
        z = _in_proj(h, w_in, l)

        ycp = _conv_pool(z, conv_w[l], pool_w[l], pool_scale[l], None, T=T_CTX, n_seq=N_CTX_SEQ, row_block0=0)
        ycp = _conv_pool(z, conv_w[l], pool_w[l], pool_scale[l], ycp, T=T_LAT, n_seq=N_LAT_SEQ,
                         row_block0=M_CTX // T_LAT)

        dl = jnp.broadcast_to(ret_decay_logit[l][:, :, None, None], (2, N_RET_HEADS, 1, HEAD_DIM))
        yret, s_fin = _retention(z, cos_ctx, sin_ctx, dl, ret_gn_gain[l], None, None,
                                 T=T_CTX, n_seq=N_CTX_SEQ, row_block0=0, layer=l)
        yret = _retention(z, cos_lat, sin_lat, dl, ret_gn_gain[l], state_retention, yret,
                          T=T_LAT, n_seq=N_LAT_SEQ, row_block0=M_CTX // T_LAT, layer=l)
        states.append(s_fin)

        x1, h2, ei, ewt, rk, cnt = _out_proj(ycp, yret, x_ctx, x_lat, w_out, l, ada5,
                                             ln1_g[l], ln1_b[l], w_router_pad, router_bias)
        src_tok, pos, te, n_valid = _route_tables(cnt[:, 0].astype(I32), ei.reshape(-1), rk.reshape(-1))
        ys = _experts(h2, te, n_valid, src_tok, w_gate, w_up, w_down, l)
        if l + 1 < DEPTH:
            x_ctx, x_lat, h = _final(x1, ys, pos, ewt, ada5, ln2_g[l], ln2_b[l], ada[l + 1])
        else:
            x_ctx, x_lat = _final(x1, ys, pos, ewt, ada5, ln2_g[l], ln2_b[l], None)

    y_prompt = x_ctx.reshape(N_CTX_SEQ, T_CTX, D_MODEL)
    y_sample = x_lat.reshape(N_LAT_SEQ, T_LAT, D_MODEL)
    return y_prompt, y_sample, jnp.stack(states, axis=1)
```

```python
import functools

import jax
import jax.numpy as jnp
from jax import lax
from jax.experimental import pallas as pl
from jax.experimental.pallas import tpu as pltpu

F32 = jnp.float32
BF16 = jnp.bfloat16
I32 = jnp.int32

D_MODEL = 2048
N_CTX_SEQ, T_CTX = 16, 256
N_LAT_SEQ, T_LAT = 8, 1024
DEPTH = 2
M_CTX = N_CTX_SEQ * T_CTX
M_LAT = N_LAT_SEQ * T_LAT
M_TOK = M_CTX + M_LAT

GRID_W = 64
D_CONV = D_MODEL // 4
D_POOL = D_MODEL // 4
D_RET = D_MODEL // 2
N_RET_HEADS = 8
HEAD_DIM = D_RET // N_RET_HEADS
POOL_WINDOWS = (2, 4, 8, 16)
POOL_GROUP_DIM = D_POOL // len(POOL_WINDOWS)
CHUNK = 128
ROPE_BASE = 10000.0
N_EXPERTS = 16
EXPERTS_PER_GROUP = 4
N_EXPERT_GROUPS = N_EXPERTS // EXPERTS_PER_GROUP
D_EXPERT = D_MODEL // 4
D_IN_PROJ = 3 * D_CONV + D_POOL + 4 * D_RET
DEEPNORM_ALPHA = (2.0 * DEPTH) ** 0.25
LN_EPS = 1e-5
ADA_ROWS = 16

LANES = 128
VMEM_LIMIT = 56 * 1024 * 1024

TM_IN = 1024
TN_IN = 1024
TOK_ROWS = D_MODEL // LANES
TOK_STRIDE = TOK_ROWS + 1
TM_OUT = 256
TM_EXP = 256
N_PAIR = 2 * M_TOK
NP_EXP = N_PAIR + N_EXPERTS * TM_EXP
NT_EXP = NP_EXP // TM_EXP


def _cparams(sem):
    return pltpu.CompilerParams(dimension_semantics=sem, vmem_limit_bytes=VMEM_LIMIT)


def _silu(x):
    return x * jax.nn.sigmoid(x)


def _ada_row(i, tm):
    n_ctx_tiles = M_CTX // tm
    per_batch = T_LAT // tm
    return jnp.where(i < n_ctx_tiles, 0, 1 + (i - n_ctx_tiles) // per_batch)


def _ada_kernel(c_ref, w_ref, b_ref, o_ref):
    s = _silu(c_ref[...]).astype(BF16)
    o_ref[...] = jnp.dot(s, w_ref[...].astype(BF16), preferred_element_type=F32) + b_ref[...]


def _ada_table(c_all, w_ada, b_ada):
    tn = 1024
    n6 = 6 * D_MODEL
    return pl.pallas_call(
        _ada_kernel,
        grid=(DEPTH, n6 // tn),
        in_specs=[
            pl.BlockSpec((ADA_ROWS, D_MODEL), lambda l, j: (0, 0)),
            pl.BlockSpec((None, D_MODEL, tn), lambda l, j: (l, 0, j)),
            pl.BlockSpec((None, 1, tn), lambda l, j: (l, 0, j)),
        ],
        out_specs=pl.BlockSpec((None, ADA_ROWS, tn), lambda l, j: (l, 0, j)),
        out_shape=jax.ShapeDtypeStruct((DEPTH, ADA_ROWS, n6), F32),
        compiler_params=_cparams(("arbitrary", "arbitrary")),
        name="ada_table",
    )(c_all, w_ada, b_ada.reshape(DEPTH, 1, n6))


def _ctx_block(i, tm):
    return jnp.minimum(i, M_CTX // tm - 1)


def _lat_block(i, tm):
    return jnp.maximum(i - M_CTX // tm, 0)


def _modulate_kernel(xc_ref, xl_ref, sc_ref, sh_ref, h_ref):
    x = jnp.where(pl.program_id(0) < M_CTX // xc_ref.shape[0], xc_ref[...], xl_ref[...])
    h_ref[...] = (x * (1.0 + sc_ref[...]) + sh_ref[...]).astype(BF16)


def _modulate(x_ctx, x_lat, ada5):
    tm = TM_OUT
    ada_spec = lambda chunk: pl.BlockSpec(
        (None, None, 1, D_MODEL), lambda i: (_ada_row(i, tm), chunk, 0, 0))
    return pl.pallas_call(
        _modulate_kernel,
        grid=(M_TOK // tm,),
        in_specs=[
            pl.BlockSpec((tm, D_MODEL), lambda i: (_ctx_block(i, tm), 0)),
            pl.BlockSpec((tm, D_MODEL), lambda i: (_lat_block(i, tm), 0)),
            ada_spec(1), ada_spec(0),
        ],
        out_specs=pl.BlockSpec((tm, D_MODEL), lambda i: (i, 0)),
        out_shape=jax.ShapeDtypeStruct((M_TOK, D_MODEL), BF16),
        compiler_params=_cparams(("arbitrary",)),
        name="modulate",
    )(x_ctx, x_lat, ada5, ada5)


def _inproj_kernel(h_ref, w_ref, o_ref, wbf_scr):
    @pl.when(pl.program_id(1) == 0)
    def _():
        wbf_scr[...] = w_ref[...].astype(BF16)

    o_ref[...] = jnp.dot(h_ref[...], wbf_scr[...], preferred_element_type=F32).astype(BF16)


def _in_proj(h, w_in, layer):
    return pl.pallas_call(
        _inproj_kernel,
        grid=(D_IN_PROJ // TN_IN, M_TOK // TM_IN),
        in_specs=[
            pl.BlockSpec((TM_IN, D_MODEL), lambda j, i: (i, 0)),
            pl.BlockSpec((None, D_MODEL, TN_IN), lambda j, i: (layer, 0, j)),
        ],
        out_specs=pl.BlockSpec((TM_IN, TN_IN), lambda j, i: (i, j)),
        out_shape=jax.ShapeDtypeStruct((M_TOK, D_IN_PROJ), BF16),
        scratch_shapes=[pltpu.VMEM((D_MODEL, TN_IN), BF16)],
        compiler_params=_cparams(("arbitrary", "arbitrary")),
        name="in_proj",
    )(h, w_in)


def _convpool_kernel(*refs, T, aliased):
    if aliased:
        z_ref, cw_ref, pw_ref, ps_ref, _, o_ref, band_ref = refs
    else:
        z_ref, cw_ref, pw_ref, ps_ref, o_ref, band_ref = refs

    @pl.when(pl.program_id(0) == 0)
    def _():
        row = lax.broadcasted_iota(I32, (T, T), 0)
        col = lax.broadcasted_iota(I32, (T, T), 1)
        d = col - row
        for gi, w in enumerate(POOL_WINDOWS):
            band_ref[gi] = jnp.where((d >= -(w // 2)) & (d < w // 2), 1.0, 0.0).astype(BF16)

    cb = z_ref[:, 0:D_CONV].astype(F32)
    cc = z_ref[:, D_CONV:2 * D_CONV].astype(F32)
    cv = z_ref[:, 2 * D_CONV:3 * D_CONV].astype(F32)
    u = cc * cv
    t = lax.broadcasted_iota(I32, (T, D_CONV), 0)
    u_prev = jnp.where(t == 0, 0.0, pltpu.roll(u, 1, 0))
    u_next = jnp.where(t == T - 1, 0.0, pltpu.roll(u, T - 1, 0))
    conv = u_prev * cw_ref[0:1, :] + u * cw_ref[1:2, :] + u_next * cw_ref[2:3, :]
    o_ref[:, 0:D_CONV] = (cb * conv).astype(BF16)

    tt = lax.broadcasted_iota(I32, (T, POOL_GROUP_DIM), 0)
    for gi, w in enumerate(POOL_WINDOWS):
        lo = 3 * D_CONV + gi * POOL_GROUP_DIM
        p = z_ref[:, lo:lo + POOL_GROUP_DIM]
        win = jnp.dot(band_ref[gi], p, preferred_element_type=F32)
        cnt = (jnp.minimum(tt + w // 2, T) - jnp.maximum(tt - w // 2, 0)).astype(F32)
        pooled = win / cnt - p.astype(F32)
        y = jnp.dot(pooled.astype(BF16), pw_ref[gi].astype(BF16), preferred_element_type=F32)
        y = y * ps_ref[:, gi * POOL_GROUP_DIM:(gi + 1) * POOL_GROUP_DIM]
        o_ref[:, D_CONV + gi * POOL_GROUP_DIM:D_CONV + (gi + 1) * POOL_GROUP_DIM] = y.astype(BF16)


def _conv_pool(z, conv_w, pool_w, pool_scale, prev, *, T, n_seq, row_block0):
    aliased = prev is not None
    in_specs = [
        pl.BlockSpec((T, D_IN_PROJ - 4 * D_RET), lambda b: (row_block0 + b, 0)),
        pl.BlockSpec((3, D_CONV), lambda b: (0, 0)),
        pl.BlockSpec((len(POOL_WINDOWS), POOL_GROUP_DIM, POOL_GROUP_DIM), lambda b: (0, 0, 0)),
        pl.BlockSpec((1, D_POOL), lambda b: (0, 0)),
    ]
    args = [z, conv_w, pool_w, pool_scale.reshape(1, D_POOL)]
    if aliased:
        in_specs.append(pl.BlockSpec(memory_space=pl.ANY))
        args.append(prev)
    return pl.pallas_call(
        functools.partial(_convpool_kernel, T=T, aliased=aliased),
        grid=(n_seq,),
        in_specs=in_specs,
        out_specs=pl.BlockSpec((T, D_CONV + D_POOL), lambda b: (row_block0 + b, 0)),
        out_shape=jax.ShapeDtypeStruct((M_TOK, D_CONV + D_POOL), BF16),
        scratch_shapes=[pltpu.VMEM((len(POOL_WINDOWS), T, T), BF16)],
        input_output_aliases={4: 0} if aliased else {},
        compiler_params=_cparams(("arbitrary",)),
        name="conv_pool_T%d" % T,
    )(*args)


def _log_sigmoid(x):
    return jnp.minimum(x, 0.0) - jnp.log1p(jnp.exp(-jnp.abs(x)))


RET_ROWS = T_LAT
N_CTX_BLOCKS = M_CTX // RET_ROWS
SEQ_PER_CTX_BLOCK = RET_ROWS // T_CTX


def _retention_kernel(q_ref, k_ref, v_ref, g_ref, cos_ref, sin_ref, dl_ref, gain_ref, s0_ref,
                      y_ref, sfin_ref, tab_scr, kr_scr, u_scr, s_scr):
    blk = pl.program_id(1)
    n_chunks = RET_ROWS // CHUNK
    half = HEAD_DIM // 2

    lg_f = _log_sigmoid(dl_ref[0])
    lg_b = _log_sigmoid(dl_ref[1])
    g_f = jnp.exp(lg_f * CHUNK)
    g_b = jnp.exp(lg_b * CHUNK)

    @pl.when(blk == 0)
    def _():
        row = lax.broadcasted_iota(I32, (CHUNK, CHUNK), 0).astype(F32)
        col = lax.broadcasted_iota(I32, (CHUNK, CHUNK), 1).astype(F32)
        diff = row - col
        tab_scr[0] = (jnp.where(diff >= 0, jnp.exp(lg_f * jnp.maximum(diff, 0.0)), 0.0)
                      + jnp.where(diff <= 0, jnp.exp(lg_b * jnp.maximum(-diff, 0.0)), 0.0))
        tab_scr[1] = jnp.exp(lg_f * (row + 1.0))
        tab_scr[2] = jnp.exp(lg_b * (CHUNK - row))
        tab_scr[3] = jnp.exp(lg_f * (CHUNK - 1.0 - row))
        tab_scr[4] = jnp.exp(lg_b * row)

    tn_dims = (((0,), (0,)), ((), ()))
    nt_dims = (((1,), (1,)), ((), ()))

    def block(seq_chunks, latent):
        def rope(x, sl):
            if not latent:
                return x
            return x * cos_ref[sl, :] + pltpu.roll(x, half, 1) * sin_ref[sl, :]

        for c in range(n_chunks):
            sl = slice(c * CHUNK, (c + 1) * CHUNK)
            kr = rope(k_ref[sl, :].astype(F32), sl) * (HEAD_DIM ** -0.5)
            kr_scr[sl, :] = kr.astype(BF16)
            kz = jnp.concatenate([kr * tab_scr[3], kr * tab_scr[4]], axis=1).astype(BF16)
            u_scr[c] = lax.dot_general(kz, v_ref[sl, :], tn_dims, preferred_element_type=F32)

        has_state = [[False] * n_chunks, [False] * n_chunks]
        for s_i in range(n_chunks // seq_chunks):
            chunks = list(range(s_i * seq_chunks, (s_i + 1) * seq_chunks))
            for d, order, g_d in ((0, chunks, g_f), (1, chunks[::-1], g_b)):
                s = s0_ref[d] if latent else None
                for c in order:
                    u = u_scr[c, d * HEAD_DIM:(d + 1) * HEAD_DIM, :]
                    if s is None:
                        s = u
                    else:
                        s_scr[d, c] = s.astype(BF16)
                        has_state[d][c] = True
                        s = g_d * s + u
                if not latent:
                    sfin_ref[s_i, d] = s

        for c in range(n_chunks):
            sl = slice(c * CHUNK, (c + 1) * CHUNK)
            qr = rope(q_ref[sl, :].astype(F32), sl)
            scores = lax.dot_general(qr.astype(BF16), kr_scr[sl, :], nt_dims, preferred_element_type=F32)
            o = jnp.dot((scores * tab_scr[0]).astype(BF16), v_ref[sl, :], preferred_element_type=F32)
            for d in range(2):
                if has_state[d][c]:
                    o += jnp.dot((qr * tab_scr[1 + d]).astype(BF16), s_scr[d, c],
                                 preferred_element_type=F32)
            mu = jnp.mean(o, axis=-1, keepdims=True)
            dev = o - mu
            var = jnp.mean(dev * dev, axis=-1, keepdims=True)
            on = dev * lax.rsqrt(var + LN_EPS) * gain_ref[...]
            y_ref[sl, :] = (_silu(g_ref[sl, :].astype(F32)) * on).astype(BF16)

    pl.when(blk < N_CTX_BLOCKS)(functools.partial(block, T_CTX // CHUNK, False))
    pl.when(blk >= N_CTX_BLOCKS)(functools.partial(block, T_LAT // CHUNK, True))


def _retention(z, cos_t, sin_t, decay_logit_b, gain, s0, layer):
    qcol0 = (3 * D_CONV + D_POOL) // HEAD_DIM
    n_blocks = M_TOK // RET_ROWS

    def zspec(k):
        return pl.BlockSpec((RET_ROWS, HEAD_DIM), lambda h, b: (b, qcol0 + k * N_RET_HEADS + h))

    def lat_seq(b):
        return jnp.maximum(b - N_CTX_BLOCKS, 0)

    def ctx_block(b):
        return jnp.minimum(b, N_CTX_BLOCKS - 1)

    n_chunks = RET_ROWS // CHUNK
    return pl.pallas_call(
        _retention_kernel,
        grid=(N_RET_HEADS, n_blocks),
        in_specs=[
            zspec(0), zspec(1), zspec(2), zspec(3),
            pl.BlockSpec((RET_ROWS, HEAD_DIM), lambda h, b: (0, 0)),
            pl.BlockSpec((RET_ROWS, HEAD_DIM), lambda h, b: (0, 0)),
            pl.BlockSpec((2, None, 1, HEAD_DIM), lambda h, b: (0, h, 0, 0)),
            pl.BlockSpec((1, HEAD_DIM), lambda h, b: (0, h)),
            pl.BlockSpec((None, None, 2, None, HEAD_DIM, HEAD_DIM),
                         lambda h, b: (lat_seq(b), layer, 0, h, 0, 0)),
        ],
        out_specs=[
            pl.BlockSpec((RET_ROWS, HEAD_DIM), lambda h, b: (b, h)),
            pl.BlockSpec((SEQ_PER_CTX_BLOCK, 2, None, HEAD_DIM, HEAD_DIM),
                         lambda h, b: (ctx_block(b), 0, h, 0, 0)),
        ],
        out_shape=[
            jax.ShapeDtypeStruct((M_TOK, D_RET), BF16),
            jax.ShapeDtypeStruct((N_CTX_SEQ, 2, N_RET_HEADS, HEAD_DIM, HEAD_DIM), F32),
        ],
        scratch_shapes=[
            pltpu.VMEM((5, CHUNK, CHUNK), F32),
            pltpu.VMEM((RET_ROWS, HEAD_DIM), BF16),
            pltpu.VMEM((n_chunks, 2 * HEAD_DIM, HEAD_DIM), F32),
            pltpu.VMEM((2, n_chunks, HEAD_DIM, HEAD_DIM), BF16),
        ],
        compiler_params=_cparams(("arbitrary", "arbitrary")),
        name="retention",
    )(z, z, z, z, cos_t, sin_t, decay_logit_b, gain.reshape(1, D_RET), s0)


def _layer_norm_rows(r, g, b):
    mu = jnp.mean(r, axis=-1, keepdims=True)
    dev = r - mu
    var = jnp.mean(dev * dev, axis=-1, keepdims=True)
    return dev * lax.rsqrt(var + LN_EPS) * g + b


def _top2_of4(vals):
    top1 = jnp.maximum(jnp.maximum(vals[0], vals[1]), jnp.maximum(vals[2], vals[3]))
    idx1 = jnp.where(vals[0] == top1, 0, jnp.where(vals[1] == top1, 1, jnp.where(vals[2] == top1, 2, 3)))
    neg = jnp.float32(-jnp.inf)
    rest = [jnp.where(idx1 == j, neg, vals[j]) for j in range(4)]
    top2 = jnp.maximum(jnp.maximum(rest[0], rest[1]), jnp.maximum(rest[2], rest[3]))
    idx2 = jnp.where(rest[0] == top2, 0, jnp.where(rest[1] == top2, 1, jnp.where(rest[2] == top2, 2, 3)))
    return top1, idx1, top2, idx2


def _store_token_tiles(ref, val):
    n = val.shape[0]
    for c in range(TOK_ROWS):
        ref[pl.ds(c, n, stride=TOK_STRIDE), :] = val[:, c * LANES:(c + 1) * LANES]
    ref[pl.ds(TOK_ROWS, n, stride=TOK_STRIDE), :] = jnp.zeros((n, LANES), val.dtype)


def _load_token_tiles(ref, tok0, n):
    return jnp.concatenate(
        [ref[pl.ds(tok0 * TOK_STRIDE + c, n, stride=TOK_STRIDE), :] for c in range(TOK_ROWS)], axis=1)


def _outproj_kernel(ycp_ref, yret_ref, xc_ref, xl_ref, wo_hbm, gate1_ref, lng_ref, lnb_ref, sc2_ref, sh2_ref,
                    wr_ref, rb_ref, x1_ref, h2_ref, ei_ref, ewt_ref, rk_ref, cnt_ref,
                    carry_scr, wo_ref, wstage, wsem, *, layer):
    tm = xc_ref.shape[0]
    half_k = D_CONV + D_POOL
    i = pl.program_id(0)

    @pl.when(i == 0)
    def _():
        carry_scr[...] = jnp.zeros_like(carry_scr)
        rows = wstage.shape[1]
        n_chunks = D_MODEL // rows

        def chunk_copy(c):
            return pltpu.make_async_copy(wo_hbm.at[layer, pl.ds(c * rows, rows), :], wstage.at[c % 2],
                                         wsem.at[c % 2])
        chunk_copy(0).start()
        for c in range(n_chunks):
            if c + 1 < n_chunks:
                chunk_copy(c + 1).start()
            chunk_copy(c).wait()
            wo_ref[c * rows:(c + 1) * rows, :] = wstage[c % 2].astype(BF16)

    y = jnp.dot(ycp_ref[...], wo_ref[0:half_k, :], preferred_element_type=F32)
    y += jnp.dot(yret_ref[...], wo_ref[half_k:, :], preferred_element_type=F32)
    x = jnp.where(i < M_CTX // tm, xc_ref[...], xl_ref[...])
    x1 = _layer_norm_rows(DEEPNORM_ALPHA * x + gate1_ref[...] * y, lng_ref[...], lnb_ref[...])
    x1_ref[...] = x1
    h2 = x1 * (1.0 + sc2_ref[...]) + sh2_ref[...]
    _store_token_tiles(h2_ref, h2)

    h_hi = h2.astype(BF16)
    h_lo = (h2 - h_hi.astype(F32)).astype(BF16)
    wr = wr_ref[...]
    w_hi = wr.astype(BF16)
    w_lo = (wr - w_hi.astype(F32)).astype(BF16)
    logits = (jnp.dot(h_hi, w_hi, preferred_element_type=F32)
              + jnp.dot(h_lo, w_hi, preferred_element_type=F32)
              + jnp.dot(h_hi, w_lo, preferred_element_type=F32))
    lt = logits.T
    rows = [lt[e:e + 1, :] for e in range(N_EXPERTS)]

    mx = rows[0]
    for e in range(1, N_EXPERTS):
        mx = jnp.maximum(mx, rows[e])
    ex = [jnp.exp(r - mx) for r in rows]
    den = ex[0]
    for e in range(1, N_EXPERTS):
        den = den + ex[e]
    score = [x / den for x in ex]
    biased = [score[e] + rb_ref[e] for e in range(N_EXPERTS)]

    best = None
    for gi in range(N_EXPERT_GROUPS):
        t1, i1, t2, i2 = _top2_of4(biased[gi * EXPERTS_PER_GROUP:(gi + 1) * EXPERTS_PER_GROUP])
        gs = t1 + t2
        e1 = gi * EXPERTS_PER_GROUP + i1
        e2 = gi * EXPERTS_PER_GROUP + i2
        if best is None:
            best = (gs, e1, e2)
        else:
            take = gs > best[0]
            best = (jnp.where(take, gs, best[0]), jnp.where(take, e1, best[1]), jnp.where(take, e2, best[2]))
    _, e1, e2 = best
    zero = jnp.zeros_like(score[0])
    w1 = zero
    w2 = zero
    for e in range(N_EXPERTS):
        w1 = w1 + jnp.where(e1 == e, score[e], 0.0)
        w2 = w2 + jnp.where(e2 == e, score[e], 0.0)
    wsum = w1 + w2
    ei_ref[0:1, :] = e1
    ei_ref[1:2, :] = e2
    ewt_ref[...] = jnp.concatenate(
        [w1 / wsum, w2 / wsum, jnp.zeros((LANES - 2, tm), F32)], axis=0).T

    onehot = jnp.concatenate(
        [jnp.where((e1 == e) | (e2 == e), 1.0, 0.0) for e in range(N_EXPERTS)], axis=0)
    s_i = lax.broadcasted_iota(I32, (tm, tm), 0)
    t_i = lax.broadcasted_iota(I32, (tm, tm), 1)
    tri = jnp.where(s_i < t_i, 1.0, 0.0).astype(BF16)
    prefix = jnp.dot(onehot.astype(BF16), tri, preferred_element_type=F32) + carry_scr[:, 0:1]
    r1 = zero
    r2 = zero
    for e in range(N_EXPERTS):
        r1 = r1 + jnp.where(e1 == e, prefix[e:e + 1, :], 0.0)
        r2 = r2 + jnp.where(e2 == e, prefix[e:e + 1, :], 0.0)
    rk_ref[0:1, :] = r1.astype(I32)
    rk_ref[1:2, :] = r2.astype(I32)
    carry_scr[...] = carry_scr[...] + jnp.sum(onehot, axis=1, keepdims=True)
    cnt_ref[...] = carry_scr[...]


def _out_proj(ycp, yret, x_ctx, x_lat, w_out, layer, ada5, ln_g, ln_b, w_router_pad, router_bias):
    tm = TM_OUT
    ada_spec = lambda chunk: pl.BlockSpec(
        (None, None, 1, D_MODEL), lambda i: (_ada_row(i, tm), chunk, 0, 0))
    vec_spec = pl.BlockSpec((1, D_MODEL), lambda i: (0, 0))
    route_spec = pl.BlockSpec((2, tm), lambda i: (0, i))
    return pl.pallas_call(
        functools.partial(_outproj_kernel, layer=layer),
        grid=(M_TOK // tm,),
        in_specs=[
            pl.BlockSpec((tm, D_CONV + D_POOL), lambda i: (i, 0)),
            pl.BlockSpec((tm, D_RET), lambda i: (i, 0)),
            pl.BlockSpec((tm, D_MODEL), lambda i: (_ctx_block(i, tm), 0)),
            pl.BlockSpec((tm, D_MODEL), lambda i: (_lat_block(i, tm), 0)),
            pl.BlockSpec(memory_space=pl.ANY),
            ada_spec(2),
            vec_spec, vec_spec,
            ada_spec(4),
            ada_spec(3),
            pl.BlockSpec((D_MODEL, LANES), lambda i: (0, 0)),
            pl.BlockSpec(memory_space=pltpu.SMEM),
        ],
        out_specs=[
            pl.BlockSpec((tm, D_MODEL), lambda i: (i, 0)),
            pl.BlockSpec((tm * TOK_STRIDE, LANES), lambda i: (i, 0)),
            route_spec,
            pl.BlockSpec((tm, LANES), lambda i: (i, 0)),
            route_spec,
            pl.BlockSpec((N_EXPERTS, LANES), lambda i: (0, 0)),
        ],
        out_shape=[
            jax.ShapeDtypeStruct((M_TOK, D_MODEL), F32),
            jax.ShapeDtypeStruct((M_TOK * TOK_STRIDE, LANES), F32),
            jax.ShapeDtypeStruct((2, M_TOK), I32),
            jax.ShapeDtypeStruct((M_TOK, LANES), F32),
            jax.ShapeDtypeStruct((2, M_TOK), I32),
            jax.ShapeDtypeStruct((N_EXPERTS, LANES), F32),
        ],
        scratch_shapes=[
            pltpu.VMEM((N_EXPERTS, LANES), F32),
            pltpu.VMEM((D_MODEL, D_MODEL), BF16),
            pltpu.VMEM((2, 256, D_MODEL), F32),
            pltpu.SemaphoreType.DMA((2,)),
        ],
        compiler_params=_cparams(("arbitrary",)),
        name="out_proj_router",
    )(ycp, yret, x_ctx, x_lat, w_out, ada5, ln_g.reshape(1, D_MODEL), ln_b.reshape(1, D_MODEL),
      ada5, ada5, w_router_pad, router_bias)


def _route_kernel(cnt_ref, ei_ref, rk_ref, src_ref, pos_ref, te_ref, nv_ref, start_scr):
    def zero(r, carry):
        src_ref[r] = 0
        return carry

    tile = jnp.int32(0)
    for e in range(N_EXPERTS):
        start_scr[e] = tile * TM_EXP
        n_tiles = lax.shift_right_logical(cnt_ref[e] + (TM_EXP - 1), TM_EXP.bit_length() - 1)

        def mark(j, carry, e=e, tile=tile):
            te_ref[tile + j] = e
            return carry
        lax.fori_loop(0, n_tiles, mark, 0)
        lax.fori_loop(tile * TM_EXP + cnt_ref[e], (tile + n_tiles) * TM_EXP, zero, 0)
        tile = tile + n_tiles
    nv_ref[0] = tile
    last_expert = te_ref[tile - 1]

    def mark_unused(j, carry):
        te_ref[j] = last_expert
        return carry
    lax.fori_loop(tile, NT_EXP, mark_unused, 0)
    lax.fori_loop(tile * TM_EXP, NP_EXP, zero, 0)

    def place(t, carry):
        for k in range(2):
            pair = k * M_TOK + t
            row = start_scr[ei_ref[pair]] + rk_ref[pair]
            pos_ref[pair] = row
            src_ref[row] = t
        return carry
    lax.fori_loop(0, M_TOK, place, 0, unroll=16)


def _route_tables(cnt, ei, rk):
    smem = pl.BlockSpec(memory_space=pltpu.SMEM)
    return pl.pallas_call(
        _route_kernel,
        in_specs=[smem, smem, smem],
        out_specs=[smem, smem, smem, smem],
        out_shape=[
            jax.ShapeDtypeStruct((NP_EXP,), I32),
            jax.ShapeDtypeStruct((N_PAIR,), I32),
            jax.ShapeDtypeStruct((NT_EXP,), I32),
            jax.ShapeDtypeStruct((1,), I32),
        ],
        scratch_shapes=[pltpu.SMEM((N_EXPERTS,), I32)],
        name="route_tables",
    )(cnt, ei, rk)


def _row_gather_start(src_hbm, buf, sem, idx_ref, base, n_tok, tok0=0):
    for r in range(n_tok):
        pltpu.make_async_copy(src_hbm.at[pl.ds(idx_ref[base + r] * TOK_STRIDE, TOK_ROWS), :],
                              buf.at[pl.ds((tok0 + r) * TOK_STRIDE, TOK_ROWS), :], sem).start()


def _row_gather_wait(src_hbm, buf, sem):
    n_rows = buf.shape[0] // TOK_STRIDE * TOK_ROWS
    pltpu.make_async_copy(src_hbm.at[pl.ds(0, n_rows), :], buf.at[pl.ds(0, n_rows), :], sem).wait()


def _experts_kernel(te_ref, nv_ref, src_ref, h2_hbm, wg_ref, wu_ref, wd_ref, o_ref, xbuf0, xbuf1, gsem,
                    wg_bf, wu_bf, wd_bf):
    i = pl.program_id(0)
    n_valid = nv_ref[0]
    bufs = (xbuf0, xbuf1)

    @pl.when(i == 0)
    def _():
        _row_gather_start(h2_hbm, xbuf0, gsem.at[0], src_ref, 0, TM_EXP)

    @pl.when((i < n_valid) & ((i == 0) | (te_ref[i] != te_ref[jnp.maximum(i - 1, 0)])))
    def _():
        wg_bf[...] = wg_ref[...].astype(BF16)
        wu_bf[...] = wu_ref[...].astype(BF16)
        wd_bf[...] = wd_ref[...].astype(BF16)

    def step(cur):
        nxt_buf, nxt_sem = bufs[1 - cur], gsem.at[1 - cur]
        _row_gather_wait(h2_hbm, bufs[cur], gsem.at[cur])
        nxt = jnp.minimum(i + 1, NT_EXP - 1)
        _row_gather_start(h2_hbm, nxt_buf, nxt_sem, src_ref, nxt * TM_EXP, TM_EXP)
        x = _load_token_tiles(bufs[cur], 0, TM_EXP).astype(BF16)
        g = jnp.dot(x, wg_bf[...], preferred_element_type=F32)
        u = jnp.dot(x, wu_bf[...], preferred_element_type=F32)
        a = (_silu(g) * u).astype(BF16)
        _store_token_tiles(o_ref, jnp.dot(a, wd_bf[...], preferred_element_type=F32))

        @pl.when(i == n_valid - 1)
        def _():
            _row_gather_wait(h2_hbm, nxt_buf, nxt_sem)

    for cur in range(2):
        pl.when((i < n_valid) & (i % 2 == cur))(functools.partial(step, cur))

    @pl.when(i >= n_valid)
    def _():
        o_ref[...] = jnp.zeros_like(o_ref)


def _experts(h2, tile_expert, n_valid, src_tok, wg, wu, wd, layer):
    grid_spec = pltpu.PrefetchScalarGridSpec(
        num_scalar_prefetch=3,
        grid=(NT_EXP,),
        in_specs=[
            pl.BlockSpec(memory_space=pl.ANY),
            pl.BlockSpec((None, None, D_MODEL, D_EXPERT), lambda i, te, nv, src: (layer, te[i], 0, 0)),
            pl.BlockSpec((None, None, D_MODEL, D_EXPERT), lambda i, te, nv, src: (layer, te[i], 0, 0)),
            pl.BlockSpec((None, None, D_EXPERT, D_MODEL), lambda i, te, nv, src: (layer, te[i], 0, 0)),
        ],
        out_specs=pl.BlockSpec((TM_EXP * TOK_STRIDE, LANES), lambda i, te, nv, src: (i, 0)),
        scratch_shapes=[
            pltpu.VMEM((TM_EXP * TOK_STRIDE, LANES), F32),
            pltpu.VMEM((TM_EXP * TOK_STRIDE, LANES), F32),
            pltpu.SemaphoreType.DMA((2,)),
            pltpu.VMEM((D_MODEL, D_EXPERT), BF16),
            pltpu.VMEM((D_MODEL, D_EXPERT), BF16),
            pltpu.VMEM((D_EXPERT, D_MODEL), BF16),
        ],
    )
    return pl.pallas_call(
        _experts_kernel,
        grid_spec=grid_spec,
        out_shape=jax.ShapeDtypeStruct((NP_EXP * TOK_STRIDE, LANES), F32),
        compiler_params=_cparams(("arbitrary",)),
        name="experts",
    )(tile_expert, n_valid, src_tok, h2, wg, wu, wd)


def _final_kernel(*refs, emit_h):
    if emit_h:
        (pos_ref, x1_ref, ys_hbm, ewt_ref, gate2_ref, lng_ref, lnb_ref, sc1_ref, sh1_ref,
         xc_ref, xl_ref, h_ref, rbuf0, rbuf1, sem) = refs
    else:
        (pos_ref, x1_ref, ys_hbm, ewt_ref, gate2_ref, lng_ref, lnb_ref,
         xc_ref, xl_ref, rbuf0, rbuf1, sem) = refs
    tm = TM_OUT
    i = pl.program_id(0)
    n_blocks = M_TOK // tm
    bufs = (rbuf0, rbuf1)

    def start(tile, buf, buf_sem):
        _row_gather_start(ys_hbm, buf, buf_sem, pos_ref, tile * tm, tm, tok0=0)
        _row_gather_start(ys_hbm, buf, buf_sem, pos_ref, M_TOK + tile * tm, tm, tok0=tm)

    @pl.when(i == 0)
    def _():
        start(0, rbuf0, sem.at[0])

    def step(cur):
        buf = bufs[cur]
        nxt_buf, nxt_sem = bufs[1 - cur], sem.at[1 - cur]
        _row_gather_wait(ys_hbm, buf, sem.at[cur])
        start(jnp.minimum(i + 1, n_blocks - 1), nxt_buf, nxt_sem)
        w = ewt_ref[...]
        y2 = w[:, 0:1] * _load_token_tiles(buf, 0, tm) + w[:, 1:2] * _load_token_tiles(buf, tm, tm)
        x2 = _layer_norm_rows(DEEPNORM_ALPHA * x1_ref[...] + gate2_ref[...] * y2,
                              lng_ref[...], lnb_ref[...])
        if emit_h:
            h_ref[...] = (x2 * (1.0 + sc1_ref[...]) + sh1_ref[...]).astype(BF16)

        @pl.when(i < M_CTX // tm)
        def _():
            xc_ref[...] = x2

        @pl.when(i >= M_CTX // tm)
        def _():
            xl_ref[...] = x2

        @pl.when(i == n_blocks - 1)
        def _():
            _row_gather_wait(ys_hbm, nxt_buf, nxt_sem)

    for cur in range(2):
        pl.when(i % 2 == cur)(functools.partial(step, cur))


def _final(x1, ys, pos, ewt, ada5, ln_g, ln_b, ada5_next):
    tm = TM_OUT
    emit_h = ada5_next is not None
    vec_spec = pl.BlockSpec((1, D_MODEL), lambda i, pos: (0, 0))
    ada_spec = lambda chunk: pl.BlockSpec(
        (None, None, 1, D_MODEL), lambda i, pos: (_ada_row(i, tm), chunk, 0, 0))
    in_specs = [
        pl.BlockSpec((tm, D_MODEL), lambda i, pos: (i, 0)),
        pl.BlockSpec(memory_space=pl.ANY),
        pl.BlockSpec((tm, LANES), lambda i, pos: (i, 0)),
        ada_spec(5),
        vec_spec, vec_spec,
    ]
    args = [pos, x1, ys, ewt, ada5, ln_g.reshape(1, D_MODEL), ln_b.reshape(1, D_MODEL)]
    out_specs = [
        pl.BlockSpec((tm, D_MODEL), lambda i, pos: (_ctx_block(i, tm), 0)),
        pl.BlockSpec((tm, D_MODEL), lambda i, pos: (_lat_block(i, tm), 0)),
    ]
    out_shape = [jax.ShapeDtypeStruct((M_CTX, D_MODEL), F32),
                 jax.ShapeDtypeStruct((M_LAT, D_MODEL), F32)]
    if emit_h:
        in_specs += [ada_spec(1), ada_spec(0)]
        args += [ada5_next, ada5_next]
        out_specs.append(pl.BlockSpec((tm, D_MODEL), lambda i, pos: (i, 0)))
        out_shape.append(jax.ShapeDtypeStruct((M_TOK, D_MODEL), BF16))
    grid_spec = pltpu.PrefetchScalarGridSpec(
        num_scalar_prefetch=1,
        grid=(M_TOK // tm,),
        in_specs=in_specs,
        out_specs=out_specs,
        scratch_shapes=[
            pltpu.VMEM((2 * tm * TOK_STRIDE, LANES), F32),
            pltpu.VMEM((2 * tm * TOK_STRIDE, LANES), F32),
            pltpu.SemaphoreType.DMA((2,)),
        ],
    )
    return pl.pallas_call(
        functools.partial(_final_kernel, emit_h=emit_h),
        grid_spec=grid_spec,
        out_shape=out_shape,
        compiler_params=_cparams(("arbitrary",)),
        name="final_ln",
    )(*args)


def _rope_tables():
    rows = T_LAT // GRID_W
    row = jnp.repeat(jnp.arange(rows), GRID_W).astype(F32)
    col = jnp.tile(jnp.arange(GRID_W), rows).astype(F32)
    n_freq = HEAD_DIM // 4
    inv_freq = ROPE_BASE ** (-jnp.arange(n_freq, dtype=F32) / n_freq)
    ang = jnp.concatenate([row[:, None] * inv_freq[None], col[:, None] * inv_freq[None]], axis=-1)
    cos, sin = jnp.cos(ang), jnp.sin(ang)
    return jnp.concatenate([cos, cos], axis=-1), jnp.concatenate([-sin, sin], axis=-1)


def kernel(x_prompt, x_sample, state_retention, c, c_ctx, w_ada, b_ada, w_in, w_out, conv_w, pool_w,
           pool_scale, ret_decay_logit, ret_gn_gain, ln1_g, ln1_b, ln2_g, ln2_b, w_router, router_bias,
           w_gate, w_up, w_down):
    x_ctx = x_prompt.reshape(M_CTX, D_MODEL)
    x_lat = x_sample.reshape(M_LAT, D_MODEL)
    c_all = jnp.concatenate(
        [c_ctx[None, :], c, jnp.zeros((ADA_ROWS - 1 - N_LAT_SEQ, D_MODEL), F32)], axis=0)
    ada = _ada_table(c_all, w_ada, b_ada).reshape(DEPTH, ADA_ROWS, 6, 1, D_MODEL)

    cos_lat, sin_lat = _rope_tables()
    w_router_pad = jnp.pad(w_router, ((0, 0), (0, LANES - N_EXPERTS)))

    states = []
    h = _modulate(x_ctx, x_lat, ada[0])
    for l in range(DEPTH):
        ada5 = ada[l]
        z = _in_proj(h, w_in, l)

        ycp = _conv_pool(z, conv_w[l], pool_w[l], pool_scale[l], None, T=T_CTX, n_seq=N_CTX_SEQ, row_block0=0)
        ycp = _conv_pool(z, conv_w[l], pool_w[l], pool_scale[l], ycp, T=T_LAT, n_seq=N_LAT_SEQ,
                         row_block0=M_CTX // T_LAT)

        dl = jnp.broadcast_to(ret_decay_logit[l][:, :, None, None], (2, N_RET_HEADS, 1, HEAD_DIM))
        yret, s_fin = _retention(z, cos_lat, sin_lat, dl, ret_gn_gain[l], state_retention, l)
        states.append(s_fin)

        x1, h2, ei, ewt, rk, cnt = _out_proj(ycp, yret, x_ctx, x_lat, w_out, l, ada5,
                                             ln1_g[l], ln1_b[l], w_router_pad, router_bias)
        src_tok, pos, te, n_valid = _route_tables(cnt[:, 0].astype(I32), ei.reshape(-1), rk.reshape(-1))
        ys = _experts(h2, te, n_valid, src_tok, w_gate, w_up, w_down, l)
        if l + 1 < DEPTH:
            x_ctx, x_lat, h = _final(x1, ys, pos, ewt, ada5, ln2_g[l], ln2_b[l], ada[l + 1])
        else:
            x_ctx, x_lat = _final(x1, ys, pos, ewt, ada5, ln2_g[l], ln2_b[l], None)

    y_prompt = x_ctx.reshape(N_CTX_SEQ, T_CTX, D_MODEL)
    y_sample = x_lat.reshape(N_LAT_SEQ, T_LAT, D_MODEL)
    return y_prompt, y_sample, jnp.stack(states, axis=1)
```

```python
import functools

import jax
import jax.numpy as jnp
from jax import lax
from jax.experimental import pallas as pl
from jax.experimental.pallas import tpu as pltpu

F32 = jnp.float32
BF16 = jnp.bfloat16
I32 = jnp.int32

D_MODEL = 2048
N_CTX_SEQ, T_CTX = 16, 256
N_LAT_SEQ, T_LAT = 8, 1024
DEPTH = 2
M_CTX = N_CTX_SEQ * T_CTX
M_LAT = N_LAT_SEQ * T_LAT
M_TOK = M_CTX + M_LAT

GRID_W = 64
D_CONV = D_MODEL // 4
D_POOL = D_MODEL // 4
D_RET = D_MODEL // 2
N_RET_HEADS = 8
HEAD_DIM = D_RET // N_RET_HEADS
POOL_WINDOWS = (2, 4, 8, 16)
POOL_GROUP_DIM = D_POOL // len(POOL_WINDOWS)
CHUNK = 128
ROPE_BASE = 10000.0
N_EXPERTS = 16
EXPERTS_PER_GROUP = 4
N_EXPERT_GROUPS = N_EXPERTS // EXPERTS_PER_GROUP
D_EXPERT = D_MODEL // 4
D_IN_PROJ = 3 * D_CONV + D_POOL + 4 * D_RET
DEEPNORM_ALPHA = (2.0 * DEPTH) ** 0.25
LN_EPS = 1e-5
ADA_ROWS = 16

LANES = 128
VMEM_LIMIT = 56 * 1024 * 1024

TM_IN = 1024
TN_IN = 1024
TOK_ROWS = D_MODEL // LANES
TOK_STRIDE = TOK_ROWS + 1
TM_OUT = 256
TM_EXP = 256
N_PAIR = 2 * M_TOK
NP_EXP = N_PAIR + N_EXPERTS * TM_EXP
NT_EXP = NP_EXP // TM_EXP


def _cparams(sem):
    return pltpu.CompilerParams(dimension_semantics=sem, vmem_limit_bytes=VMEM_LIMIT)


def _silu(x):
    return x * jax.nn.sigmoid(x)


def _ada_row(i, tm):
    n_ctx_tiles = M_CTX // tm
    per_batch = T_LAT // tm
    return jnp.where(i < n_ctx_tiles, 0, 1 + (i - n_ctx_tiles) // per_batch)


def _ada_kernel(c_ref, w_ref, b_ref, o_ref):
    s = _silu(c_ref[...]).astype(BF16)
    o_ref[...] = jnp.dot(s, w_ref[...].astype(BF16), preferred_element_type=F32) + b_ref[...]


def _ada_table(c_all, w_ada, b_ada):
    tn = 1024
    n6 = 6 * D_MODEL
    return pl.pallas_call(
        _ada_kernel,
        grid=(DEPTH, n6 // tn),
        in_specs=[
            pl.BlockSpec((ADA_ROWS, D_MODEL), lambda l, j: (0, 0)),
            pl.BlockSpec((None, D_MODEL, tn), lambda l, j: (l, 0, j)),
            pl.BlockSpec((None, 1, tn), lambda l, j: (l, 0, j)),
        ],
        out_specs=pl.BlockSpec((None, ADA_ROWS, tn), lambda l, j: (l, 0, j)),
        out_shape=jax.ShapeDtypeStruct((DEPTH, ADA_ROWS, n6), F32),
        compiler_params=_cparams(("arbitrary", "arbitrary")),
        name="ada_table",
    )(c_all, w_ada, b_ada.reshape(DEPTH, 1, n6))


def _ctx_block(i, tm):
    return jnp.minimum(i, M_CTX // tm - 1)


def _lat_block(i, tm):
    return jnp.maximum(i - M_CTX // tm, 0)


def _modulate_kernel(xc_ref, xl_ref, sc_ref, sh_ref, h_ref):
    x = jnp.where(pl.program_id(0) < M_CTX // xc_ref.shape[0], xc_ref[...], xl_ref[...])
    h_ref[...] = (x * (1.0 + sc_ref[...]) + sh_ref[...]).astype(BF16)


def _modulate(x_ctx, x_lat, ada5):
    tm = TM_OUT
    ada_spec = lambda chunk: pl.BlockSpec(
        (None, None, 1, D_MODEL), lambda i: (_ada_row(i, tm), chunk, 0, 0))
    return pl.pallas_call(
        _modulate_kernel,
        grid=(M_TOK // tm,),
        in_specs=[
            pl.BlockSpec((tm, D_MODEL), lambda i: (_ctx_block(i, tm), 0)),
            pl.BlockSpec((tm, D_MODEL), lambda i: (_lat_block(i, tm), 0)),
            ada_spec(1), ada_spec(0),
        ],
        out_specs=pl.BlockSpec((tm, D_MODEL), lambda i: (i, 0)),
        out_shape=jax.ShapeDtypeStruct((M_TOK, D_MODEL), BF16),
        compiler_params=_cparams(("arbitrary",)),
        name="modulate",
    )(x_ctx, x_lat, ada5, ada5)


def _inproj_kernel(h_ref, w_ref, o_ref, wbf_scr):
    @pl.when(pl.program_id(1) == 0)
    def _():
        wbf_scr[...] = w_ref[...].astype(BF16)

    o_ref[...] = jnp.dot(h_ref[...], wbf_scr[...], preferred_element_type=F32).astype(BF16)


def _in_proj(h, w_in, layer):
    return pl.pallas_call(
        _inproj_kernel,
        grid=(D_IN_PROJ // TN_IN, M_TOK // TM_IN),
        in_specs=[
            pl.BlockSpec((TM_IN, D_MODEL), lambda j, i: (i, 0)),
            pl.BlockSpec((None, D_MODEL, TN_IN), lambda j, i: (layer, 0, j)),
        ],
        out_specs=pl.BlockSpec((TM_IN, TN_IN), lambda j, i: (i, j)),
        out_shape=jax.ShapeDtypeStruct((M_TOK, D_IN_PROJ), BF16),
        scratch_shapes=[pltpu.VMEM((D_MODEL, TN_IN), BF16)],
        compiler_params=_cparams(("arbitrary", "arbitrary")),
        name="in_proj",
    )(h, w_in)


MIX_ROWS = T_LAT
N_CTX_BLOCKS = M_CTX // MIX_ROWS
SEQ_PER_CTX_BLOCK = MIX_ROWS // T_CTX
CP_ROWS = T_CTX
CP_HALO = 128


def _convpool_kernel(z_ref, cw_ref, pw_ref, ps_ref, o_ref, band_ref):
    blk = pl.program_id(0)
    g_dim = POOL_GROUP_DIM

    @pl.when(blk == 0)
    def _():
        row = lax.broadcasted_iota(I32, (CP_ROWS, CP_ROWS + 2 * CP_HALO), 0)
        col = lax.broadcasted_iota(I32, (CP_ROWS, CP_ROWS + 2 * CP_HALO), 1)
        d = col - CP_HALO - row
        for gi, w in enumerate(POOL_WINDOWS):
            band_ref[gi] = jnp.where((d >= -(w // 2)) & (d < w // 2), 1.0, 0.0).astype(BF16)

    def block(seq_len):
        t = lax.broadcasted_iota(I32, (CP_ROWS, LANES), 0)
        for ch in range(MIX_ROWS // CP_ROWS):
            r0 = ch * CP_ROWS
            rows = slice(r0, r0 + CP_ROWS)
            pos0 = r0 % seq_len
            at_start = pos0 == 0
            at_end = pos0 + CP_ROWS == seq_len

            for cg in range(D_CONV // LANES):
                lanes = slice(cg * LANES, (cg + 1) * LANES)

                def u_rows(rs):
                    return (z_ref[rs, D_CONV + cg * LANES:D_CONV + (cg + 1) * LANES].astype(F32)
                            * z_ref[rs, 2 * D_CONV + cg * LANES:2 * D_CONV + (cg + 1) * LANES].astype(F32))
                u = u_rows(rows)
                before = 0.0 if at_start else u_rows(slice(r0 - 1, r0))
                after = 0.0 if at_end else u_rows(slice(r0 + CP_ROWS, r0 + CP_ROWS + 1))
                u_prev = jnp.where(t == 0, before, pltpu.roll(u, 1, 0))
                u_next = jnp.where(t == CP_ROWS - 1, after, pltpu.roll(u, CP_ROWS - 1, 0))
                conv = u_prev * cw_ref[0:1, lanes] + u * cw_ref[1:2, lanes] + u_next * cw_ref[2:3, lanes]
                o_ref[rows, lanes] = (z_ref[rows, lanes].astype(F32) * conv).astype(BF16)

            k_rows = slice(r0 if at_start else r0 - CP_HALO,
                           r0 + CP_ROWS if at_end else r0 + CP_ROWS + CP_HALO)
            b_cols = slice(CP_HALO if at_start else 0,
                           CP_HALO + CP_ROWS if at_end else CP_ROWS + 2 * CP_HALO)
            tpos = pos0 + t
            for gi, w in enumerate(POOL_WINDOWS):
                lo = 3 * D_CONV + gi * g_dim
                win = jnp.dot(band_ref[gi, :, b_cols], z_ref[k_rows, lo:lo + g_dim],
                              preferred_element_type=F32)
                cnt = (jnp.minimum(tpos + w // 2, seq_len) - jnp.maximum(tpos - w // 2, 0)).astype(F32)
                pooled = win / cnt - z_ref[rows, lo:lo + g_dim].astype(F32)
                y = jnp.dot(pooled.astype(BF16), pw_ref[gi].astype(BF16), preferred_element_type=F32)
                y = y * ps_ref[:, gi * g_dim:(gi + 1) * g_dim]
                o_ref[rows, D_CONV + gi * g_dim:D_CONV + (gi + 1) * g_dim] = y.astype(BF16)

    pl.when(blk < N_CTX_BLOCKS)(functools.partial(block, T_CTX))
    pl.when(blk >= N_CTX_BLOCKS)(functools.partial(block, T_LAT))


def _conv_pool(z, conv_w, pool_w, pool_scale):
    return pl.pallas_call(
        _convpool_kernel,
        grid=(M_TOK // MIX_ROWS,),
        in_specs=[
            pl.BlockSpec((MIX_ROWS, D_IN_PROJ - 4 * D_RET), lambda b: (b, 0)),
            pl.BlockSpec((3, D_CONV), lambda b: (0, 0)),
            pl.BlockSpec((len(POOL_WINDOWS), POOL_GROUP_DIM, POOL_GROUP_DIM), lambda b: (0, 0, 0)),
            pl.BlockSpec((1, D_POOL), lambda b: (0, 0)),
        ],
        out_specs=pl.BlockSpec((MIX_ROWS, D_CONV + D_POOL), lambda b: (b, 0)),
        out_shape=jax.ShapeDtypeStruct((M_TOK, D_CONV + D_POOL), BF16),
        scratch_shapes=[pltpu.VMEM((len(POOL_WINDOWS), CP_ROWS, CP_ROWS + 2 * CP_HALO), BF16)],
        compiler_params=_cparams(("arbitrary",)),
        name="conv_pool",
    )(z, conv_w, pool_w, pool_scale.reshape(1, D_POOL))


def _log_sigmoid(x):
    return jnp.minimum(x, 0.0) - jnp.log1p(jnp.exp(-jnp.abs(x)))


RET_ROWS = MIX_ROWS


def _retention_kernel(q_ref, k_ref, v_ref, g_ref, cos_ref, sin_ref, dl_ref, gain_ref, s0_ref,
                      y_ref, sfin_ref, tab_scr, kr_scr, u_scr, s_scr):
    blk = pl.program_id(1)
    n_chunks = RET_ROWS // CHUNK
    half = HEAD_DIM // 2

    lg_f = _log_sigmoid(dl_ref[0])
    lg_b = _log_sigmoid(dl_ref[1])
    g_f = jnp.exp(lg_f * CHUNK)
    g_b = jnp.exp(lg_b * CHUNK)

    @pl.when(blk == 0)
    def _():
        row = lax.broadcasted_iota(I32, (CHUNK, CHUNK), 0).astype(F32)
        col = lax.broadcasted_iota(I32, (CHUNK, CHUNK), 1).astype(F32)
        diff = row - col
        tab_scr[0] = (jnp.where(diff >= 0, jnp.exp(lg_f * jnp.maximum(diff, 0.0)), 0.0)
                      + jnp.where(diff <= 0, jnp.exp(lg_b * jnp.maximum(-diff, 0.0)), 0.0))
        tab_scr[1] = jnp.exp(lg_f * (row + 1.0))
        tab_scr[2] = jnp.exp(lg_b * (CHUNK - row))
        tab_scr[3] = jnp.exp(lg_f * (CHUNK - 1.0 - row))
        tab_scr[4] = jnp.exp(lg_b * row)

    tn_dims = (((0,), (0,)), ((), ()))
    nt_dims = (((1,), (1,)), ((), ()))

    def block(seq_chunks, latent):
        def rope(x, sl):
            if not latent:
                return x
            return x * cos_ref[sl, :] + pltpu.roll(x, half, 1) * sin_ref[sl, :]

        for c in range(n_chunks):
            sl = slice(c * CHUNK, (c + 1) * CHUNK)
            kr = rope(k_ref[sl, :].astype(F32), sl) * (HEAD_DIM ** -0.5)
            kr_scr[sl, :] = kr.astype(BF16)
            kz = jnp.concatenate([kr * tab_scr[3], kr * tab_scr[4]], axis=1).astype(BF16)
            u_scr[c] = lax.dot_general(kz, v_ref[sl, :], tn_dims, preferred_element_type=F32)

        has_state = [[False] * n_chunks, [False] * n_chunks]
        for s_i in range(n_chunks // seq_chunks):
            chunks = list(range(s_i * seq_chunks, (s_i + 1) * seq_chunks))
            for d, order, g_d in ((0, chunks, g_f), (1, chunks[::-1], g_b)):
                s = s0_ref[d] if latent else None
                for c in order:
                    u = u_scr[c, d * HEAD_DIM:(d + 1) * HEAD_DIM, :]
                    if s is None:
                        s = u
                    else:
                        s_scr[d, c] = s.astype(BF16)
                        has_state[d][c] = True
                        s = g_d * s + u
                if not latent:
                    sfin_ref[s_i, d] = s

        for c in range(n_chunks):
            sl = slice(c * CHUNK, (c + 1) * CHUNK)
            qr = rope(q_ref[sl, :].astype(F32), sl)
            scores = lax.dot_general(qr.astype(BF16), kr_scr[sl, :], nt_dims, preferred_element_type=F32)
            o = jnp.dot((scores * tab_scr[0]).astype(BF16), v_ref[sl, :], preferred_element_type=F32)
            for d in range(2):
                if has_state[d][c]:
                    o += jnp.dot((qr * tab_scr[1 + d]).astype(BF16), s_scr[d, c],
                                 preferred_element_type=F32)
            mu = jnp.mean(o, axis=-1, keepdims=True)
            dev = o - mu
            var = jnp.mean(dev * dev, axis=-1, keepdims=True)
            on = dev * lax.rsqrt(var + LN_EPS) * gain_ref[...]
            y_ref[sl, :] = (_silu(g_ref[sl, :].astype(F32)) * on).astype(BF16)

    pl.when(blk < N_CTX_BLOCKS)(functools.partial(block, T_CTX // CHUNK, False))
    pl.when(blk >= N_CTX_BLOCKS)(functools.partial(block, T_LAT // CHUNK, True))


def _retention(z, cos_t, sin_t, decay_logit_b, gain, s0, layer):
    qcol0 = (3 * D_CONV + D_POOL) // HEAD_DIM
    n_blocks = M_TOK // RET_ROWS

    def zspec(k):
        return pl.BlockSpec((RET_ROWS, HEAD_DIM), lambda h, b: (b, qcol0 + k * N_RET_HEADS + h))

    def lat_seq(b):
        return jnp.maximum(b - N_CTX_BLOCKS, 0)

    def ctx_block(b):
        return jnp.minimum(b, N_CTX_BLOCKS - 1)

    n_chunks = RET_ROWS // CHUNK
    return pl.pallas_call(
        _retention_kernel,
        grid=(N_RET_HEADS, n_blocks),
        in_specs=[
            zspec(0), zspec(1), zspec(2), zspec(3),
            pl.BlockSpec((RET_ROWS, HEAD_DIM), lambda h, b: (0, 0)),
            pl.BlockSpec((RET_ROWS, HEAD_DIM), lambda h, b: (0, 0)),
            pl.BlockSpec((2, None, 1, HEAD_DIM), lambda h, b: (0, h, 0, 0)),
            pl.BlockSpec((1, HEAD_DIM), lambda h, b: (0, h)),
            pl.BlockSpec((None, None, 2, None, HEAD_DIM, HEAD_DIM),
                         lambda h, b: (lat_seq(b), layer, 0, h, 0, 0)),
        ],
        out_specs=[
            pl.BlockSpec((RET_ROWS, HEAD_DIM), lambda h, b: (b, h)),
            pl.BlockSpec((SEQ_PER_CTX_BLOCK, 2, None, HEAD_DIM, HEAD_DIM),
                         lambda h, b: (ctx_block(b), 0, h, 0, 0)),
        ],
        out_shape=[
            jax.ShapeDtypeStruct((M_TOK, D_RET), BF16),
            jax.ShapeDtypeStruct((N_CTX_SEQ, 2, N_RET_HEADS, HEAD_DIM, HEAD_DIM), F32),
        ],
        scratch_shapes=[
            pltpu.VMEM((5, CHUNK, CHUNK), F32),
            pltpu.VMEM((RET_ROWS, HEAD_DIM), BF16),
            pltpu.VMEM((n_chunks, 2 * HEAD_DIM, HEAD_DIM), F32),
            pltpu.VMEM((2, n_chunks, HEAD_DIM, HEAD_DIM), BF16),
        ],
        compiler_params=_cparams(("arbitrary", "arbitrary")),
        name="retention",
    )(z, z, z, z, cos_t, sin_t, decay_logit_b, gain.reshape(1, D_RET), s0)


def _layer_norm_rows(r, g, b):
    mu = jnp.mean(r, axis=-1, keepdims=True)
    dev = r - mu
    var = jnp.mean(dev * dev, axis=-1, keepdims=True)
    return dev * lax.rsqrt(var + LN_EPS) * g + b


def _top2_of4(vals):
    top1 = jnp.maximum(jnp.maximum(vals[0], vals[1]), jnp.maximum(vals[2], vals[3]))
    idx1 = jnp.where(vals[0] == top1, 0, jnp.where(vals[1] == top1, 1, jnp.where(vals[2] == top1, 2, 3)))
    neg = jnp.float32(-jnp.inf)
    rest = [jnp.where(idx1 == j, neg, vals[j]) for j in range(4)]
    top2 = jnp.maximum(jnp.maximum(rest[0], rest[1]), jnp.maximum(rest[2], rest[3]))
    idx2 = jnp.where(rest[0] == top2, 0, jnp.where(rest[1] == top2, 1, jnp.where(rest[2] == top2, 2, 3)))
    return top1, idx1, top2, idx2


def _store_token_tiles(ref, val):
    n = val.shape[0]
    for c in range(TOK_ROWS):
        ref[pl.ds(c, n, stride=TOK_STRIDE), :] = val[:, c * LANES:(c + 1) * LANES]
    ref[pl.ds(TOK_ROWS, n, stride=TOK_STRIDE), :] = jnp.zeros((n, LANES), val.dtype)


def _load_token_tiles(ref, tok0, n):
    return jnp.concatenate(
        [ref[pl.ds(tok0 * TOK_STRIDE + c, n, stride=TOK_STRIDE), :] for c in range(TOK_ROWS)], axis=1)


def _outproj_kernel(ycp_ref, yret_ref, xc_ref, xl_ref, wo_hbm, gate1_ref, lng_ref, lnb_ref, sc2_ref, sh2_ref,
                    wr_ref, rb_ref, x1_ref, h2_ref, ei_ref, ewt_ref, rk_ref, cnt_ref,
                    carry_scr, wo_ref, wstage, wsem, *, layer):
    tm = xc_ref.shape[0]
    half_k = D_CONV + D_POOL
    i = pl.program_id(0)

    @pl.when(i == 0)
    def _():
        carry_scr[...] = jnp.zeros_like(carry_scr)
        rows = wstage.shape[1]
        n_chunks = D_MODEL // rows

        def chunk_copy(c):
            return pltpu.make_async_copy(wo_hbm.at[layer, pl.ds(c * rows, rows), :], wstage.at[c % 2],
                                         wsem.at[c % 2])
        chunk_copy(0).start()
        for c in range(n_chunks):
            if c + 1 < n_chunks:
                chunk_copy(c + 1).start()
            chunk_copy(c).wait()
            wo_ref[c * rows:(c + 1) * rows, :] = wstage[c % 2].astype(BF16)

    y = jnp.dot(ycp_ref[...], wo_ref[0:half_k, :], preferred_element_type=F32)
    y += jnp.dot(yret_ref[...], wo_ref[half_k:, :], preferred_element_type=F32)
    x = jnp.where(i < M_CTX // tm, xc_ref[...], xl_ref[...])
    x1 = _layer_norm_rows(DEEPNORM_ALPHA * x + gate1_ref[...] * y, lng_ref[...], lnb_ref[...])
    x1_ref[...] = x1
    h2 = x1 * (1.0 + sc2_ref[...]) + sh2_ref[...]
    _store_token_tiles(h2_ref, h2)

    h_hi = h2.astype(BF16)
    h_lo = (h2 - h_hi.astype(F32)).astype(BF16)
    wr = wr_ref[...]
    w_hi = wr.astype(BF16)
    w_lo = (wr - w_hi.astype(F32)).astype(BF16)
    logits = (jnp.dot(h_hi, w_hi, preferred_element_type=F32)
              + jnp.dot(h_lo, w_hi, preferred_element_type=F32)
              + jnp.dot(h_hi, w_lo, preferred_element_type=F32))
    lt = logits.T
    rows = [lt[e:e + 1, :] for e in range(N_EXPERTS)]

    mx = rows[0]
    for e in range(1, N_EXPERTS):
        mx = jnp.maximum(mx, rows[e])
    ex = [jnp.exp(r - mx) for r in rows]
    den = ex[0]
    for e in range(1, N_EXPERTS):
        den = den + ex[e]
    score = [x / den for x in ex]
    biased = [score[e] + rb_ref[e] for e in range(N_EXPERTS)]

    best = None
    for gi in range(N_EXPERT_GROUPS):
        t1, i1, t2, i2 = _top2_of4(biased[gi * EXPERTS_PER_GROUP:(gi + 1) * EXPERTS_PER_GROUP])
        gs = t1 + t2
        e1 = gi * EXPERTS_PER_GROUP + i1
        e2 = gi * EXPERTS_PER_GROUP + i2
        if best is None:
            best = (gs, e1, e2)
        else:
            take = gs > best[0]
            best = (jnp.where(take, gs, best[0]), jnp.where(take, e1, best[1]), jnp.where(take, e2, best[2]))
    _, e1, e2 = best
    zero = jnp.zeros_like(score[0])
    w1 = zero
    w2 = zero
    for e in range(N_EXPERTS):
        w1 = w1 + jnp.where(e1 == e, score[e], 0.0)
        w2 = w2 + jnp.where(e2 == e, score[e], 0.0)
    wsum = w1 + w2
    ei_ref[0:1, :] = e1
    ei_ref[1:2, :] = e2
    ewt_ref[...] = jnp.concatenate(
        [w1 / wsum, w2 / wsum, jnp.zeros((LANES - 2, tm), F32)], axis=0).T

    onehot = jnp.concatenate(
        [jnp.where((e1 == e) | (e2 == e), 1.0, 0.0) for e in range(N_EXPERTS)], axis=0)
    s_i = lax.broadcasted_iota(I32, (tm, tm), 0)
    t_i = lax.broadcasted_iota(I32, (tm, tm), 1)
    tri = jnp.where(s_i < t_i, 1.0, 0.0).astype(BF16)
    prefix = jnp.dot(onehot.astype(BF16), tri, preferred_element_type=F32) + carry_scr[:, 0:1]
    r1 = zero
    r2 = zero
    for e in range(N_EXPERTS):
        r1 = r1 + jnp.where(e1 == e, prefix[e:e + 1, :], 0.0)
        r2 = r2 + jnp.where(e2 == e, prefix[e:e + 1, :], 0.0)
    rk_ref[0:1, :] = r1.astype(I32)
    rk_ref[1:2, :] = r2.astype(I32)
    carry_scr[...] = carry_scr[...] + jnp.sum(onehot, axis=1, keepdims=True)
    cnt_ref[...] = carry_scr[...]


def _out_proj(ycp, yret, x_ctx, x_lat, w_out, layer, ada5, ln_g, ln_b, w_router_pad, router_bias):
    tm = TM_OUT
    ada_spec = lambda chunk: pl.BlockSpec(
        (None, None, 1, D_MODEL), lambda i: (_ada_row(i, tm), chunk, 0, 0))
    vec_spec = pl.BlockSpec((1, D_MODEL), lambda i: (0, 0))
    route_spec = pl.BlockSpec((2, tm), lambda i: (0, i))
    return pl.pallas_call(
        functools.partial(_outproj_kernel, layer=layer),
        grid=(M_TOK // tm,),
        in_specs=[
            pl.BlockSpec((tm, D_CONV + D_POOL), lambda i: (i, 0)),
            pl.BlockSpec((tm, D_RET), lambda i: (i, 0)),
            pl.BlockSpec((tm, D_MODEL), lambda i: (_ctx_block(i, tm), 0)),
            pl.BlockSpec((tm, D_MODEL), lambda i: (_lat_block(i, tm), 0)),
            pl.BlockSpec(memory_space=pl.ANY),
            ada_spec(2),
            vec_spec, vec_spec,
            ada_spec(4),
            ada_spec(3),
            pl.BlockSpec((D_MODEL, LANES), lambda i: (0, 0)),
            pl.BlockSpec(memory_space=pltpu.SMEM),
        ],
        out_specs=[
            pl.BlockSpec((tm, D_MODEL), lambda i: (i, 0)),
            pl.BlockSpec((tm * TOK_STRIDE, LANES), lambda i: (i, 0)),
            route_spec,
            pl.BlockSpec((tm, LANES), lambda i: (i, 0)),
            route_spec,
            pl.BlockSpec((N_EXPERTS, LANES), lambda i: (0, 0)),
        ],
        out_shape=[
            jax.ShapeDtypeStruct((M_TOK, D_MODEL), F32),
            jax.ShapeDtypeStruct((M_TOK * TOK_STRIDE, LANES), F32),
            jax.ShapeDtypeStruct((2, M_TOK), I32),
            jax.ShapeDtypeStruct((M_TOK, LANES), F32),
            jax.ShapeDtypeStruct((2, M_TOK), I32),
            jax.ShapeDtypeStruct((N_EXPERTS, LANES), F32),
        ],
        scratch_shapes=[
            pltpu.VMEM((N_EXPERTS, LANES), F32),
            pltpu.VMEM((D_MODEL, D_MODEL), BF16),
            pltpu.VMEM((2, 256, D_MODEL), F32),
            pltpu.SemaphoreType.DMA((2,)),
        ],
        compiler_params=_cparams(("arbitrary",)),
        name="out_proj_router",
    )(ycp, yret, x_ctx, x_lat, w_out, ada5, ln_g.reshape(1, D_MODEL), ln_b.reshape(1, D_MODEL),
      ada5, ada5, w_router_pad, router_bias)


def _pos_kernel(ei_ref, rk_ref, cnt_ref, pos_ref):
    ei = ei_ref[...]
    pos = rk_ref[...]
    start = jnp.zeros((1, 1), F32)
    for e in range(N_EXPERTS):
        pos = pos + jnp.where(ei == e, start.astype(I32), 0)
        n_tiles = jnp.floor((cnt_ref[e:e + 1, 0:1] + (TM_EXP - 1.0)) * (1.0 / TM_EXP))
        start = start + n_tiles * TM_EXP
    pos_ref[...] = pos


def _pair_rows(ei, rk, cnt):
    return pl.pallas_call(
        _pos_kernel,
        out_shape=jax.ShapeDtypeStruct((2, M_TOK), I32),
        name="pair_rows",
    )(ei, rk, cnt)


def _route_kernel(cnt_ref, pos_ref, src_ref, te_ref, nv_ref):
    def zero(r, carry):
        src_ref[r] = 0
        return carry

    tile = jnp.int32(0)
    for e in range(N_EXPERTS):
        n_tiles = lax.shift_right_logical(cnt_ref[e] + (TM_EXP - 1), TM_EXP.bit_length() - 1)

        def mark(j, carry, e=e, tile=tile):
            te_ref[tile + j] = e
            return carry
        lax.fori_loop(0, n_tiles, mark, 0)
        lax.fori_loop(tile * TM_EXP + cnt_ref[e], (tile + n_tiles) * TM_EXP, zero, 0)
        tile = tile + n_tiles
    nv_ref[0] = tile
    last_expert = te_ref[tile - 1]

    def mark_unused(j, carry):
        te_ref[j] = last_expert
        return carry
    lax.fori_loop(tile, NT_EXP, mark_unused, 0)
    lax.fori_loop(tile * TM_EXP, NP_EXP, zero, 0)

    def place(t, carry):
        src_ref[pos_ref[t]] = t
        src_ref[pos_ref[M_TOK + t]] = t
        return carry
    lax.fori_loop(0, M_TOK, place, 0, unroll=16)


def _route_tables(cnt, pos):
    smem = pl.BlockSpec(memory_space=pltpu.SMEM)
    return pl.pallas_call(
        _route_kernel,
        in_specs=[smem, smem],
        out_specs=[smem, smem, smem],
        out_shape=[
            jax.ShapeDtypeStruct((NP_EXP,), I32),
            jax.ShapeDtypeStruct((NT_EXP,), I32),
            jax.ShapeDtypeStruct((1,), I32),
        ],
        name="route_tables",
    )(cnt, pos)


def _row_gather_start(src_hbm, buf, sem, idx_ref, base, n_tok, tok0=0):
    for r in range(n_tok):
        pltpu.make_async_copy(src_hbm.at[pl.ds(idx_ref[base + r] * TOK_STRIDE, TOK_ROWS), :],
                              buf.at[pl.ds((tok0 + r) * TOK_STRIDE, TOK_ROWS), :], sem).start()


def _row_gather_wait(src_hbm, buf, sem):
    n_rows = buf.shape[0] // TOK_STRIDE * TOK_ROWS
    pltpu.make_async_copy(src_hbm.at[pl.ds(0, n_rows), :], buf.at[pl.ds(0, n_rows), :], sem).wait()


def _experts_kernel(te_ref, nv_ref, src_ref, h2_hbm, wg_ref, wu_ref, wd_ref, o_ref, xbuf0, xbuf1, gsem,
                    wg_bf, wu_bf, wd_bf):
    i = pl.program_id(0)
    n_valid = nv_ref[0]
    bufs = (xbuf0, xbuf1)

    @pl.when(i == 0)
    def _():
        _row_gather_start(h2_hbm, xbuf0, gsem.at[0], src_ref, 0, TM_EXP)

    @pl.when((i < n_valid) & ((i == 0) | (te_ref[i] != te_ref[jnp.maximum(i - 1, 0)])))
    def _():
        wg_bf[...] = wg_ref[...].astype(BF16)
        wu_bf[...] = wu_ref[...].astype(BF16)
        wd_bf[...] = wd_ref[...].astype(BF16)

    def step(cur):
        nxt_buf, nxt_sem = bufs[1 - cur], gsem.at[1 - cur]
        _row_gather_wait(h2_hbm, bufs[cur], gsem.at[cur])
        nxt = jnp.minimum(i + 1, NT_EXP - 1)
        _row_gather_start(h2_hbm, nxt_buf, nxt_sem, src_ref, nxt * TM_EXP, TM_EXP)
        x = _load_token_tiles(bufs[cur], 0, TM_EXP).astype(BF16)
        g = jnp.dot(x, wg_bf[...], preferred_element_type=F32)
        u = jnp.dot(x, wu_bf[...], preferred_element_type=F32)
        a = (_silu(g) * u).astype(BF16)
        _store_token_tiles(o_ref, jnp.dot(a, wd_bf[...], preferred_element_type=F32))

        @pl.when(i == n_valid - 1)
        def _():
            _row_gather_wait(h2_hbm, nxt_buf, nxt_sem)

    for cur in range(2):
        pl.when((i < n_valid) & (i % 2 == cur))(functools.partial(step, cur))

    @pl.when(i >= n_valid)
    def _():
        o_ref[...] = jnp.zeros_like(o_ref)


def _experts(h2, tile_expert, n_valid, src_tok, wg, wu, wd, layer):
    grid_spec = pltpu.PrefetchScalarGridSpec(
        num_scalar_prefetch=3,
        grid=(NT_EXP,),
        in_specs=[
            pl.BlockSpec(memory_space=pl.ANY),
            pl.BlockSpec((None, None, D_MODEL, D_EXPERT), lambda i, te, nv, src: (layer, te[i], 0, 0)),
            pl.BlockSpec((None, None, D_MODEL, D_EXPERT), lambda i, te, nv, src: (layer, te[i], 0, 0)),
            pl.BlockSpec((None, None, D_EXPERT, D_MODEL), lambda i, te, nv, src: (layer, te[i], 0, 0)),
        ],
        out_specs=pl.BlockSpec((TM_EXP * TOK_STRIDE, LANES), lambda i, te, nv, src: (i, 0)),
        scratch_shapes=[
            pltpu.VMEM((TM_EXP * TOK_STRIDE, LANES), F32),
            pltpu.VMEM((TM_EXP * TOK_STRIDE, LANES), F32),
            pltpu.SemaphoreType.DMA((2,)),
            pltpu.VMEM((D_MODEL, D_EXPERT), BF16),
            pltpu.VMEM((D_MODEL, D_EXPERT), BF16),
            pltpu.VMEM((D_EXPERT, D_MODEL), BF16),
        ],
    )
    return pl.pallas_call(
        _experts_kernel,
        grid_spec=grid_spec,
        out_shape=jax.ShapeDtypeStruct((NP_EXP * TOK_STRIDE, LANES), F32),
        compiler_params=_cparams(("arbitrary",)),
        name="experts",
    )(tile_expert, n_valid, src_tok, h2, wg, wu, wd)


def _final_kernel(*refs, emit_h):
    if emit_h:
        (pos_ref, x1_ref, ys_hbm, ewt_ref, gate2_ref, lng_ref, lnb_ref, sc1_ref, sh1_ref,
         xc_ref, xl_ref, h_ref, rbuf0, rbuf1, sem) = refs
    else:
        (pos_ref, x1_ref, ys_hbm, ewt_ref, gate2_ref, lng_ref, lnb_ref,
         xc_ref, xl_ref, rbuf0, rbuf1, sem) = refs
    tm = TM_OUT
    i = pl.program_id(0)
    n_blocks = M_TOK // tm
    bufs = (rbuf0, rbuf1)

    def start(tile, buf, buf_sem):
        _row_gather_start(ys_hbm, buf, buf_sem, pos_ref, tile * tm, tm, tok0=0)
        _row_gather_start(ys_hbm, buf, buf_sem, pos_ref, M_TOK + tile * tm, tm, tok0=tm)

    @pl.when(i == 0)
    def _():
        start(0, rbuf0, sem.at[0])

    def step(cur):
        buf = bufs[cur]
        nxt_buf, nxt_sem = bufs[1 - cur], sem.at[1 - cur]
        _row_gather_wait(ys_hbm, buf, sem.at[cur])
        start(jnp.minimum(i + 1, n_blocks - 1), nxt_buf, nxt_sem)
        w = ewt_ref[...]
        y2 = w[:, 0:1] * _load_token_tiles(buf, 0, tm) + w[:, 1:2] * _load_token_tiles(buf, tm, tm)
        x2 = _layer_norm_rows(DEEPNORM_ALPHA * x1_ref[...] + gate2_ref[...] * y2,
                              lng_ref[...], lnb_ref[...])
        if emit_h:
            h_ref[...] = (x2 * (1.0 + sc1_ref[...]) + sh1_ref[...]).astype(BF16)

        @pl.when(i < M_CTX // tm)
        def _():
            xc_ref[...] = x2

        @pl.when(i >= M_CTX // tm)
        def _():
            xl_ref[...] = x2

        @pl.when(i == n_blocks - 1)
        def _():
            _row_gather_wait(ys_hbm, nxt_buf, nxt_sem)

    for cur in range(2):
        pl.when(i % 2 == cur)(functools.partial(step, cur))


def _final(x1, ys, pos, ewt, ada5, ln_g, ln_b, ada5_next):
    tm = TM_OUT
    emit_h = ada5_next is not None
    vec_spec = pl.BlockSpec((1, D_MODEL), lambda i, pos: (0, 0))
    ada_spec = lambda chunk: pl.BlockSpec(
        (None, None, 1, D_MODEL), lambda i, pos: (_ada_row(i, tm), chunk, 0, 0))
    in_specs = [
        pl.BlockSpec((tm, D_MODEL), lambda i, pos: (i, 0)),
        pl.BlockSpec(memory_space=pl.ANY),
        pl.BlockSpec((tm, LANES), lambda i, pos: (i, 0)),
        ada_spec(5),
        vec_spec, vec_spec,
    ]
    args = [pos, x1, ys, ewt, ada5, ln_g.reshape(1, D_MODEL), ln_b.reshape(1, D_MODEL)]
    out_specs = [
        pl.BlockSpec((tm, D_MODEL), lambda i, pos: (_ctx_block(i, tm), 0)),
        pl.BlockSpec((tm, D_MODEL), lambda i, pos: (_lat_block(i, tm), 0)),
    ]
    out_shape = [jax.ShapeDtypeStruct((M_CTX, D_MODEL), F32),
                 jax.ShapeDtypeStruct((M_LAT, D_MODEL), F32)]
    if emit_h:
        in_specs += [ada_spec(1), ada_spec(0)]
        args += [ada5_next, ada5_next]
        out_specs.append(pl.BlockSpec((tm, D_MODEL), lambda i, pos: (i, 0)))
        out_shape.append(jax.ShapeDtypeStruct((M_TOK, D_MODEL), BF16))
    grid_spec = pltpu.PrefetchScalarGridSpec(
        num_scalar_prefetch=1,
        grid=(M_TOK // tm,),
        in_specs=in_specs,
        out_specs=out_specs,
        scratch_shapes=[
            pltpu.VMEM((2 * tm * TOK_STRIDE, LANES), F32),
            pltpu.VMEM((2 * tm * TOK_STRIDE, LANES), F32),
            pltpu.SemaphoreType.DMA((2,)),
        ],
    )
    return pl.pallas_call(
        functools.partial(_final_kernel, emit_h=emit_h),
        grid_spec=grid_spec,
        out_shape=out_shape,
        compiler_params=_cparams(("arbitrary",)),
        name="final_ln",
    )(*args)


def _rope_tables():
    rows = T_LAT // GRID_W
    row = jnp.repeat(jnp.arange(rows), GRID_W).astype(F32)
    col = jnp.tile(jnp.arange(GRID_W), rows).astype(F32)
    n_freq = HEAD_DIM // 4
    inv_freq = ROPE_BASE ** (-jnp.arange(n_freq, dtype=F32) / n_freq)
    ang = jnp.concatenate([row[:, None] * inv_freq[None], col[:, None] * inv_freq[None]], axis=-1)
    cos, sin = jnp.cos(ang), jnp.sin(ang)
    return jnp.concatenate([cos, cos], axis=-1), jnp.concatenate([-sin, sin], axis=-1)


def kernel(x_prompt, x_sample, state_retention, c, c_ctx, w_ada, b_ada, w_in, w_out, conv_w, pool_w,
           pool_scale, ret_decay_logit, ret_gn_gain, ln1_g, ln1_b, ln2_g, ln2_b, w_router, router_bias,
           w_gate, w_up, w_down):
    x_ctx = x_prompt.reshape(M_CTX, D_MODEL)
    x_lat = x_sample.reshape(M_LAT, D_MODEL)
    c_all = jnp.concatenate(
        [c_ctx[None, :], c, jnp.zeros((ADA_ROWS - 1 - N_LAT_SEQ, D_MODEL), F32)], axis=0)
    ada = _ada_table(c_all, w_ada, b_ada).reshape(DEPTH, ADA_ROWS, 6, 1, D_MODEL)

    cos_lat, sin_lat = _rope_tables()
    w_router_pad = jnp.pad(w_router, ((0, 0), (0, LANES - N_EXPERTS)))

    states = []
    h = _modulate(x_ctx, x_lat, ada[0])
    for l in range(DEPTH):
        ada5 = ada[l]
        z = _in_proj(h, w_in, l)

        ycp = _conv_pool(z, conv_w[l], pool_w[l], pool_scale[l])

        dl = jnp.broadcast_to(ret_decay_logit[l][:, :, None, None], (2, N_RET_HEADS, 1, HEAD_DIM))
        yret, s_fin = _retention(z, cos_lat, sin_lat, dl, ret_gn_gain[l], state_retention, l)
        states.append(s_fin)

        x1, h2, ei, ewt, rk, cnt = _out_proj(ycp, yret, x_ctx, x_lat, w_out, l, ada5,
                                             ln1_g[l], ln1_b[l], w_router_pad, router_bias)
        pos = _pair_rows(ei, rk, cnt).reshape(-1)
        src_tok, te, n_valid = _route_tables(cnt[:, 0].astype(I32), pos)
        ys = _experts(h2, te, n_valid, src_tok, w_gate, w_up, w_down, l)
        if l + 1 < DEPTH:
            x_ctx, x_lat, h = _final(x1, ys, pos, ewt, ada5, ln2_g[l], ln2_b[l], ada[l + 1])
        else:
            x_ctx, x_lat = _final(x1, ys, pos, ewt, ada5, ln2_g[l], ln2_b[l], None)

    y_prompt = x_ctx.reshape(N_CTX_SEQ, T_CTX, D_MODEL)
    y_sample = x_lat.reshape(N_LAT_SEQ, T_LAT, D_MODEL)
    return y_prompt, y_sample, jnp.stack(states, axis=1)
```

```python
import functools

import jax
import jax.numpy as jnp
from jax import lax
from jax.experimental import pallas as pl
from jax.experimental.pallas import tpu as pltpu

F32 = jnp.float32
BF16 = jnp.bfloat16
I32 = jnp.int32

D_MODEL = 2048
N_CTX_SEQ, T_CTX = 16, 256
N_LAT_SEQ, T_LAT = 8, 1024
DEPTH = 2
M_CTX = N_CTX_SEQ * T_CTX
M_LAT = N_LAT_SEQ * T_LAT
M_TOK = M_CTX + M_LAT

GRID_W = 64
D_CONV = D_MODEL // 4
D_POOL = D_MODEL // 4
D_RET = D_MODEL // 2
N_RET_HEADS = 8
HEAD_DIM = D_RET // N_RET_HEADS
POOL_WINDOWS = (2, 4, 8, 16)
POOL_GROUP_DIM = D_POOL // len(POOL_WINDOWS)
CHUNK = 128
ROPE_BASE = 10000.0
N_EXPERTS = 16
EXPERTS_PER_GROUP = 4
N_EXPERT_GROUPS = N_EXPERTS // EXPERTS_PER_GROUP
D_EXPERT = D_MODEL // 4
D_IN_PROJ = 3 * D_CONV + D_POOL + 4 * D_RET
DEEPNORM_ALPHA = (2.0 * DEPTH) ** 0.25
LN_EPS = 1e-5
ADA_ROWS = 16

LANES = 128
VMEM_LIMIT = 56 * 1024 * 1024

TM_IN = 1024
TN_IN = 1024
TOK_ROWS = D_MODEL // LANES
TOK_STRIDE = TOK_ROWS + 1
TM_OP = 512
SUB_OP = 256
TM_OUT = 256
TM_EXP = 256
N_PAIR = 2 * M_TOK
NP_EXP = N_PAIR + N_EXPERTS * TM_EXP
NT_EXP = NP_EXP // TM_EXP


def _cparams(sem):
    return pltpu.CompilerParams(dimension_semantics=sem, vmem_limit_bytes=VMEM_LIMIT)


def _silu(x):
    return x * jax.nn.sigmoid(x)


def _ada_row(i, tm):
    n_ctx_tiles = M_CTX // tm
    per_batch = T_LAT // tm
    return jnp.where(i < n_ctx_tiles, 0, 1 + (i - n_ctx_tiles) // per_batch)


def _ada_kernel(c_ref, w_ref, b_ref, o_ref):
    s = _silu(c_ref[...]).astype(BF16)
    o_ref[...] = jnp.dot(s, w_ref[...].astype(BF16), preferred_element_type=F32) + b_ref[...]


def _ada_table(c_all, w_ada, b_ada):
    tn = 1024
    n6 = 6 * D_MODEL
    return pl.pallas_call(
        _ada_kernel,
        grid=(DEPTH, n6 // tn),
        in_specs=[
            pl.BlockSpec((ADA_ROWS, D_MODEL), lambda l, j: (0, 0)),
            pl.BlockSpec((None, D_MODEL, tn), lambda l, j: (l, 0, j)),
            pl.BlockSpec((None, 1, tn), lambda l, j: (l, 0, j)),
        ],
        out_specs=pl.BlockSpec((None, ADA_ROWS, tn), lambda l, j: (l, 0, j)),
        out_shape=jax.ShapeDtypeStruct((DEPTH, ADA_ROWS, n6), F32),
        compiler_params=_cparams(("arbitrary", "arbitrary")),
        name="ada_table",
    )(c_all, w_ada, b_ada.reshape(DEPTH, 1, n6))


def _ctx_block(i, tm):
    return jnp.minimum(i, M_CTX // tm - 1)


def _lat_block(i, tm):
    return jnp.maximum(i - M_CTX // tm, 0)


def _modulate_kernel(xc_ref, xl_ref, sc_ref, sh_ref, x_ref, h_ref):
    x = jnp.where(pl.program_id(0) < M_CTX // xc_ref.shape[0], xc_ref[...], xl_ref[...])
    x_ref[...] = x
    h_ref[...] = (x * (1.0 + sc_ref[...]) + sh_ref[...]).astype(BF16)


def _modulate(x_ctx, x_lat, ada5):
    tm = TM_OUT
    ada_spec = lambda chunk: pl.BlockSpec(
        (None, None, 1, D_MODEL), lambda i: (_ada_row(i, tm), chunk, 0, 0))
    return pl.pallas_call(
        _modulate_kernel,
        grid=(M_TOK // tm,),
        in_specs=[
            pl.BlockSpec((tm, D_MODEL), lambda i: (_ctx_block(i, tm), 0)),
            pl.BlockSpec((tm, D_MODEL), lambda i: (_lat_block(i, tm), 0)),
            ada_spec(1), ada_spec(0),
        ],
        out_specs=[pl.BlockSpec((tm, D_MODEL), lambda i: (i, 0)),
                   pl.BlockSpec((tm, D_MODEL), lambda i: (i, 0))],
        out_shape=[jax.ShapeDtypeStruct((M_TOK, D_MODEL), F32),
                   jax.ShapeDtypeStruct((M_TOK, D_MODEL), BF16)],
        compiler_params=_cparams(("arbitrary",)),
        name="modulate",
    )(x_ctx, x_lat, ada5, ada5)


def _inproj_kernel(h_ref, w_ref, o_ref, wbf_scr):
    @pl.when(pl.program_id(1) == 0)
    def _():
        wbf_scr[...] = w_ref[...].astype(BF16)

    o_ref[...] = jnp.dot(h_ref[...], wbf_scr[...], preferred_element_type=F32).astype(BF16)


def _in_proj(h, w_in, layer):
    return pl.pallas_call(
        _inproj_kernel,
        grid=(D_IN_PROJ // TN_IN, M_TOK // TM_IN),
        in_specs=[
            pl.BlockSpec((TM_IN, D_MODEL), lambda j, i: (i, 0)),
            pl.BlockSpec((None, D_MODEL, TN_IN), lambda j, i: (layer, 0, j)),
        ],
        out_specs=pl.BlockSpec((TM_IN, TN_IN), lambda j, i: (i, j)),
        out_shape=jax.ShapeDtypeStruct((M_TOK, D_IN_PROJ), BF16),
        scratch_shapes=[pltpu.VMEM((D_MODEL, TN_IN), BF16)],
        compiler_params=_cparams(("arbitrary", "arbitrary")),
        name="in_proj",
    )(h, w_in)


MIX_ROWS = T_LAT
N_CTX_BLOCKS = M_CTX // MIX_ROWS
SEQ_PER_CTX_BLOCK = MIX_ROWS // T_CTX
CP_ROWS = T_CTX
CP_HALO = 128


def _convpool_kernel(z_ref, cw_ref, pw_ref, ps_ref, o_ref, band_ref):
    blk = pl.program_id(0)
    g_dim = POOL_GROUP_DIM

    @pl.when(blk == 0)
    def _():
        row = lax.broadcasted_iota(I32, (CP_ROWS, CP_ROWS + 2 * CP_HALO), 0)
        col = lax.broadcasted_iota(I32, (CP_ROWS, CP_ROWS + 2 * CP_HALO), 1)
        d = col - CP_HALO - row
        for gi, w in enumerate(POOL_WINDOWS):
            band_ref[gi] = jnp.where((d >= -(w // 2)) & (d < w // 2), 1.0, 0.0).astype(BF16)

    def block(seq_len):
        t = lax.broadcasted_iota(I32, (CP_ROWS, LANES), 0)
        for ch in range(MIX_ROWS // CP_ROWS):
            r0 = ch * CP_ROWS
            rows = slice(r0, r0 + CP_ROWS)
            pos0 = r0 % seq_len
            at_start = pos0 == 0
            at_end = pos0 + CP_ROWS == seq_len

            for cg in range(D_CONV // LANES):
                lanes = slice(cg * LANES, (cg + 1) * LANES)

                def u_rows(rs):
                    return (z_ref[rs, D_CONV + cg * LANES:D_CONV + (cg + 1) * LANES].astype(F32)
                            * z_ref[rs, 2 * D_CONV + cg * LANES:2 * D_CONV + (cg + 1) * LANES].astype(F32))
                u = u_rows(rows)
                before = 0.0 if at_start else u_rows(slice(r0 - 1, r0))
                after = 0.0 if at_end else u_rows(slice(r0 + CP_ROWS, r0 + CP_ROWS + 1))
                u_prev = jnp.where(t == 0, before, pltpu.roll(u, 1, 0))
                u_next = jnp.where(t == CP_ROWS - 1, after, pltpu.roll(u, CP_ROWS - 1, 0))
                conv = u_prev * cw_ref[0:1, lanes] + u * cw_ref[1:2, lanes] + u_next * cw_ref[2:3, lanes]
                o_ref[rows, lanes] = (z_ref[rows, lanes].astype(F32) * conv).astype(BF16)

            k_rows = slice(r0 if at_start else r0 - CP_HALO,
                           r0 + CP_ROWS if at_end else r0 + CP_ROWS + CP_HALO)
            b_cols = slice(CP_HALO if at_start else 0,
                           CP_HALO + CP_ROWS if at_end else CP_ROWS + 2 * CP_HALO)
            tpos = pos0 + t
            for gi, w in enumerate(POOL_WINDOWS):
                lo = 3 * D_CONV + gi * g_dim
                win = jnp.dot(band_ref[gi, :, b_cols], z_ref[k_rows, lo:lo + g_dim],
                              preferred_element_type=F32)
                cnt = (jnp.minimum(tpos + w // 2, seq_len) - jnp.maximum(tpos - w // 2, 0)).astype(F32)
                pooled = win / cnt - z_ref[rows, lo:lo + g_dim].astype(F32)
                y = jnp.dot(pooled.astype(BF16), pw_ref[gi].astype(BF16), preferred_element_type=F32)
                y = y * ps_ref[:, gi * g_dim:(gi + 1) * g_dim]
                o_ref[rows, D_CONV + gi * g_dim:D_CONV + (gi + 1) * g_dim] = y.astype(BF16)

    pl.when(blk < N_CTX_BLOCKS)(functools.partial(block, T_CTX))
    pl.when(blk >= N_CTX_BLOCKS)(functools.partial(block, T_LAT))


def _conv_pool(z, conv_w, pool_w, pool_scale):
    return pl.pallas_call(
        _convpool_kernel,
        grid=(M_TOK // MIX_ROWS,),
        in_specs=[
            pl.BlockSpec((MIX_ROWS, D_IN_PROJ - 4 * D_RET), lambda b: (b, 0)),
            pl.BlockSpec((3, D_CONV), lambda b: (0, 0)),
            pl.BlockSpec((len(POOL_WINDOWS), POOL_GROUP_DIM, POOL_GROUP_DIM), lambda b: (0, 0, 0)),
            pl.BlockSpec((1, D_POOL), lambda b: (0, 0)),
        ],
        out_specs=pl.BlockSpec((MIX_ROWS, D_CONV + D_POOL), lambda b: (b, 0)),
        out_shape=jax.ShapeDtypeStruct((M_TOK, D_CONV + D_POOL), BF16),
        scratch_shapes=[pltpu.VMEM((len(POOL_WINDOWS), CP_ROWS, CP_ROWS + 2 * CP_HALO), BF16)],
        compiler_params=_cparams(("arbitrary",)),
        name="conv_pool",
    )(z, conv_w, pool_w, pool_scale.reshape(1, D_POOL))


def _log_sigmoid(x):
    return jnp.minimum(x, 0.0) - jnp.log1p(jnp.exp(-jnp.abs(x)))


RET_ROWS = MIX_ROWS


def _retention_kernel(q_ref, k_ref, v_ref, g_ref, cos_ref, sin_ref, dl_ref, gain_ref, s0_ref,
                      y_ref, sfin_ref, tab_scr, kr_scr, u_scr, s_scr):
    blk = pl.program_id(1)
    n_chunks = RET_ROWS // CHUNK
    half = HEAD_DIM // 2

    lg_f = _log_sigmoid(dl_ref[0])
    lg_b = _log_sigmoid(dl_ref[1])
    g_f = jnp.exp(lg_f * CHUNK)
    g_b = jnp.exp(lg_b * CHUNK)

    @pl.when(blk == 0)
    def _():
        row = lax.broadcasted_iota(I32, (CHUNK, CHUNK), 0).astype(F32)
        col = lax.broadcasted_iota(I32, (CHUNK, CHUNK), 1).astype(F32)
        diff = row - col
        tab_scr[0] = (jnp.where(diff >= 0, jnp.exp(lg_f * jnp.maximum(diff, 0.0)), 0.0)
                      + jnp.where(diff <= 0, jnp.exp(lg_b * jnp.maximum(-diff, 0.0)), 0.0))
        tab_scr[1] = jnp.exp(lg_f * (row + 1.0))
        tab_scr[2] = jnp.exp(lg_b * (CHUNK - row))
        tab_scr[3] = jnp.exp(lg_f * (CHUNK - 1.0 - row))
        tab_scr[4] = jnp.exp(lg_b * row)

    tn_dims = (((0,), (0,)), ((), ()))
    nt_dims = (((1,), (1,)), ((), ()))

    def block(seq_chunks, latent):
        def rope(x, sl):
            if not latent:
                return x
            return x * cos_ref[sl, :] + pltpu.roll(x, half, 1) * sin_ref[sl, :]

        for c in range(n_chunks):
            sl = slice(c * CHUNK, (c + 1) * CHUNK)
            kr = rope(k_ref[sl, :].astype(F32), sl) * (HEAD_DIM ** -0.5)
            kr_scr[sl, :] = kr.astype(BF16)
            kz = jnp.concatenate([kr * tab_scr[3], kr * tab_scr[4]], axis=1).astype(BF16)
            u_scr[c] = lax.dot_general(kz, v_ref[sl, :], tn_dims, preferred_element_type=F32)

        has_state = [[False] * n_chunks, [False] * n_chunks]
        for s_i in range(n_chunks // seq_chunks):
            chunks = list(range(s_i * seq_chunks, (s_i + 1) * seq_chunks))
            for d, order, g_d in ((0, chunks, g_f), (1, chunks[::-1], g_b)):
                s = s0_ref[d] if latent else None
                for c in order:
                    u = u_scr[c, d * HEAD_DIM:(d + 1) * HEAD_DIM, :]
                    if s is None:
                        s = u
                    else:
                        s_scr[d, c] = s.astype(BF16)
                        has_state[d][c] = True
                        s = g_d * s + u
                if not latent:
                    sfin_ref[s_i, d] = s

        for c in range(n_chunks):
            sl = slice(c * CHUNK, (c + 1) * CHUNK)
            qr = rope(q_ref[sl, :].astype(F32), sl)
            scores = lax.dot_general(qr.astype(BF16), kr_scr[sl, :], nt_dims, preferred_element_type=F32)
            o = jnp.dot((scores * tab_scr[0]).astype(BF16), v_ref[sl, :], preferred_element_type=F32)
            for d in range(2):
                if has_state[d][c]:
                    o += jnp.dot((qr * tab_scr[1 + d]).astype(BF16), s_scr[d, c],
                                 preferred_element_type=F32)
            mu = jnp.mean(o, axis=-1, keepdims=True)
            dev = o - mu
            var = jnp.mean(dev * dev, axis=-1, keepdims=True)
            on = dev * lax.rsqrt(var + LN_EPS) * gain_ref[...]
            y_ref[sl, :] = (_silu(g_ref[sl, :].astype(F32)) * on).astype(BF16)

    pl.when(blk < N_CTX_BLOCKS)(functools.partial(block, T_CTX // CHUNK, False))
    pl.when(blk >= N_CTX_BLOCKS)(functools.partial(block, T_LAT // CHUNK, True))


def _retention(z, cos_t, sin_t, decay_logit_b, gain, s0, layer):
    qcol0 = (3 * D_CONV + D_POOL) // HEAD_DIM
    n_blocks = M_TOK // RET_ROWS

    def zspec(k):
        return pl.BlockSpec((RET_ROWS, HEAD_DIM), lambda h, b: (b, qcol0 + k * N_RET_HEADS + h))

    def lat_seq(b):
        return jnp.maximum(b - N_CTX_BLOCKS, 0)

    def ctx_block(b):
        return jnp.minimum(b, N_CTX_BLOCKS - 1)

    n_chunks = RET_ROWS // CHUNK
    return pl.pallas_call(
        _retention_kernel,
        grid=(N_RET_HEADS, n_blocks),
        in_specs=[
            zspec(0), zspec(1), zspec(2), zspec(3),
            pl.BlockSpec((RET_ROWS, HEAD_DIM), lambda h, b: (0, 0)),
            pl.BlockSpec((RET_ROWS, HEAD_DIM), lambda h, b: (0, 0)),
            pl.BlockSpec((2, None, 1, HEAD_DIM), lambda h, b: (0, h, 0, 0)),
            pl.BlockSpec((1, HEAD_DIM), lambda h, b: (0, h)),
            pl.BlockSpec((None, None, 2, None, HEAD_DIM, HEAD_DIM),
                         lambda h, b: (lat_seq(b), layer, 0, h, 0, 0)),
        ],
        out_specs=[
            pl.BlockSpec((RET_ROWS, HEAD_DIM), lambda h, b: (b, h)),
            pl.BlockSpec((SEQ_PER_CTX_BLOCK, 2, None, HEAD_DIM, HEAD_DIM),
                         lambda h, b: (ctx_block(b), 0, h, 0, 0)),
        ],
        out_shape=[
            jax.ShapeDtypeStruct((M_TOK, D_RET), BF16),
            jax.ShapeDtypeStruct((N_CTX_SEQ, 2, N_RET_HEADS, HEAD_DIM, HEAD_DIM), F32),
        ],
        scratch_shapes=[
            pltpu.VMEM((5, CHUNK, CHUNK), F32),
            pltpu.VMEM((RET_ROWS, HEAD_DIM), BF16),
            pltpu.VMEM((n_chunks, 2 * HEAD_DIM, HEAD_DIM), F32),
            pltpu.VMEM((2, n_chunks, HEAD_DIM, HEAD_DIM), BF16),
        ],
        compiler_params=_cparams(("arbitrary", "arbitrary")),
        name="retention",
    )(z, z, z, z, cos_t, sin_t, decay_logit_b, gain.reshape(1, D_RET), s0)


def _layer_norm_rows(r, g, b):
    mu = jnp.mean(r, axis=-1, keepdims=True)
    dev = r - mu
    var = jnp.mean(dev * dev, axis=-1, keepdims=True)
    return dev * lax.rsqrt(var + LN_EPS) * g + b


def _top2_of4(vals):
    top1 = jnp.maximum(jnp.maximum(vals[0], vals[1]), jnp.maximum(vals[2], vals[3]))
    idx1 = jnp.where(vals[0] == top1, 0, jnp.where(vals[1] == top1, 1, jnp.where(vals[2] == top1, 2, 3)))
    neg = jnp.float32(-jnp.inf)
    rest = [jnp.where(idx1 == j, neg, vals[j]) for j in range(4)]
    top2 = jnp.maximum(jnp.maximum(rest[0], rest[1]), jnp.maximum(rest[2], rest[3]))
    idx2 = jnp.where(rest[0] == top2, 0, jnp.where(rest[1] == top2, 1, jnp.where(rest[2] == top2, 2, 3)))
    return top1, idx1, top2, idx2


def _store_token_tiles(ref, val, tok0=0):
    n = val.shape[0]
    base = tok0 * TOK_STRIDE
    for c in range(TOK_ROWS):
        ref[pl.ds(base + c, n, stride=TOK_STRIDE), :] = val[:, c * LANES:(c + 1) * LANES]
    ref[pl.ds(base + TOK_ROWS, n, stride=TOK_STRIDE), :] = jnp.zeros((n, LANES), val.dtype)


def _load_token_tiles(ref, tok0, n):
    return jnp.concatenate(
        [ref[pl.ds(tok0 * TOK_STRIDE + c, n, stride=TOK_STRIDE), :] for c in range(TOK_ROWS)], axis=1)


def _outproj_kernel(ycp_ref, yret_ref, x_ref, wo_hbm, gate1_ref, lng_ref, lnb_ref, sc2_ref, sh2_ref,
                    wr_ref, rb_ref, x1_ref, h2_ref, ei_ref, ewt_ref, rk_ref, cnt_ref,
                    carry_scr, wo_ref, wstage, wsem, wr_hi, wr_lo, *, layer):
    i = pl.program_id(0)

    @pl.when(i == 0)
    def _():
        carry_scr[...] = jnp.zeros_like(carry_scr)
        wr = wr_ref[...]
        hi = wr.astype(BF16)
        wr_hi[...] = hi
        wr_lo[...] = (wr - hi.astype(F32)).astype(BF16)
        rows = wstage.shape[1]
        n_chunks = D_MODEL // rows

        def chunk_copy(c):
            return pltpu.make_async_copy(wo_hbm.at[layer, pl.ds(c * rows, rows), :], wstage.at[c % 2],
                                         wsem.at[c % 2])
        chunk_copy(0).start()
        for c in range(n_chunks):
            if c + 1 < n_chunks:
                chunk_copy(c + 1).start()
            chunk_copy(c).wait()
            wo_ref[c * rows:(c + 1) * rows, :] = wstage[c % 2].astype(BF16)

    s_i = lax.broadcasted_iota(I32, (SUB_OP, SUB_OP), 0)
    t_i = lax.broadcasted_iota(I32, (SUB_OP, SUB_OP), 1)
    tri = jnp.where(s_i < t_i, 1.0, 0.0).astype(BF16)
    half_k = D_CONV + D_POOL
    n_sub = x_ref.shape[0] // SUB_OP
    ys = []
    for sub in range(n_sub):
        rows_sl = slice(sub * SUB_OP, (sub + 1) * SUB_OP)
        y = jnp.dot(ycp_ref[rows_sl, :], wo_ref[0:half_k, :], preferred_element_type=F32)
        ys.append(y + jnp.dot(yret_ref[rows_sl, :], wo_ref[half_k:, :], preferred_element_type=F32))
    lts = [_ln_router_subtile(sub, ys[sub], x_ref, gate1_ref, lng_ref, lnb_ref, sc2_ref, sh2_ref,
                              wr_hi, wr_lo, x1_ref, h2_ref) for sub in range(n_sub)]
    carry = carry_scr[:, 0:1]
    for sub in range(n_sub):
        carry = _route_subtile(sub, lts[sub], carry, tri, rb_ref, ei_ref, ewt_ref, rk_ref)
    carry_scr[...] = jnp.broadcast_to(carry, carry_scr.shape)
    cnt_ref[...] = carry_scr[...]


def _ln_router_subtile(sub, y, x_ref, gate1_ref, lng_ref, lnb_ref, sc2_ref, sh2_ref, wr_hi, wr_lo,
                       x1_ref, h2_ref):
    tm = SUB_OP
    rows_sl = slice(sub * tm, (sub + 1) * tm)
    x1 = _layer_norm_rows(DEEPNORM_ALPHA * x_ref[rows_sl, :] + gate1_ref[...] * y,
                          lng_ref[...], lnb_ref[...])
    x1_ref[rows_sl, :] = x1
    h2 = x1 * (1.0 + sc2_ref[...]) + sh2_ref[...]
    _store_token_tiles(h2_ref, h2, tok0=sub * tm)

    h_hi = h2.astype(BF16)
    h_lo = (h2 - h_hi.astype(F32)).astype(BF16)
    logits = (jnp.dot(h_hi, wr_hi[...], preferred_element_type=F32)
              + jnp.dot(h_lo, wr_hi[...], preferred_element_type=F32)
              + jnp.dot(h_hi, wr_lo[...], preferred_element_type=F32))
    return logits.T


def _route_subtile(sub, lt, carry, tri, rb_ref, ei_ref, ewt_ref, rk_ref):
    tm = SUB_OP
    rows_sl = slice(sub * tm, (sub + 1) * tm)
    rows = [lt[e:e + 1, :] for e in range(N_EXPERTS)]

    mx = rows[0]
    for e in range(1, N_EXPERTS):
        mx = jnp.maximum(mx, rows[e])
    ex = [jnp.exp(r - mx) for r in rows]
    den = ex[0]
    for e in range(1, N_EXPERTS):
        den = den + ex[e]
    score = [x / den for x in ex]
    biased = [score[e] + rb_ref[e] for e in range(N_EXPERTS)]

    best = None
    for gi in range(N_EXPERT_GROUPS):
        t1, i1, t2, i2 = _top2_of4(biased[gi * EXPERTS_PER_GROUP:(gi + 1) * EXPERTS_PER_GROUP])
        gs = t1 + t2
        e1 = gi * EXPERTS_PER_GROUP + i1
        e2 = gi * EXPERTS_PER_GROUP + i2
        if best is None:
            best = (gs, e1, e2)
        else:
            take = gs > best[0]
            best = (jnp.where(take, gs, best[0]), jnp.where(take, e1, best[1]), jnp.where(take, e2, best[2]))
    _, e1, e2 = best
    zero = jnp.zeros_like(score[0])
    w1 = zero
    w2 = zero
    for e in range(N_EXPERTS):
        w1 = w1 + jnp.where(e1 == e, score[e], 0.0)
        w2 = w2 + jnp.where(e2 == e, score[e], 0.0)
    wsum = w1 + w2
    ei_ref[0:1, rows_sl] = e1
    ei_ref[1:2, rows_sl] = e2
    ewt_ref[rows_sl, :] = jnp.concatenate(
        [w1 / wsum, w2 / wsum, jnp.zeros((LANES - 2, tm), F32)], axis=0).T

    onehot = jnp.concatenate(
        [jnp.where((e1 == e) | (e2 == e), 1.0, 0.0) for e in range(N_EXPERTS)], axis=0)
    prefix = jnp.dot(onehot.astype(BF16), tri, preferred_element_type=F32) + carry
    r1 = zero
    r2 = zero
    for e in range(N_EXPERTS):
        r1 = r1 + jnp.where(e1 == e, prefix[e:e + 1, :], 0.0)
        r2 = r2 + jnp.where(e2 == e, prefix[e:e + 1, :], 0.0)
    rk_ref[0:1, rows_sl] = r1.astype(I32)
    rk_ref[1:2, rows_sl] = r2.astype(I32)
    return carry + jnp.sum(onehot, axis=1, keepdims=True)


def _out_proj(ycp, yret, x, w_out, layer, ada5, ln_g, ln_b, w_router_pad, router_bias):
    tm = TM_OP
    ada_spec = lambda chunk: pl.BlockSpec(
        (None, None, 1, D_MODEL), lambda i: (_ada_row(i, tm), chunk, 0, 0))
    vec_spec = pl.BlockSpec((1, D_MODEL), lambda i: (0, 0))
    route_spec = pl.BlockSpec((2, tm), lambda i: (0, i))
    return pl.pallas_call(
        functools.partial(_outproj_kernel, layer=layer),
        grid=(M_TOK // tm,),
        in_specs=[
            pl.BlockSpec((tm, D_CONV + D_POOL), lambda i: (i, 0)),
            pl.BlockSpec((tm, D_RET), lambda i: (i, 0)),
            pl.BlockSpec((tm, D_MODEL), lambda i: (i, 0)),
            pl.BlockSpec(memory_space=pl.ANY),
            ada_spec(2),
            vec_spec, vec_spec,
            ada_spec(4),
            ada_spec(3),
            pl.BlockSpec((D_MODEL, LANES), lambda i: (0, 0)),
            pl.BlockSpec(memory_space=pltpu.SMEM),
        ],
        out_specs=[
            pl.BlockSpec((tm, D_MODEL), lambda i: (i, 0)),
            pl.BlockSpec((tm * TOK_STRIDE, LANES), lambda i: (i, 0)),
            route_spec,
            pl.BlockSpec((tm, LANES), lambda i: (i, 0)),
            route_spec,
            pl.BlockSpec((N_EXPERTS, LANES), lambda i: (0, 0)),
        ],
        out_shape=[
            jax.ShapeDtypeStruct((M_TOK, D_MODEL), F32),
            jax.ShapeDtypeStruct((M_TOK * TOK_STRIDE, LANES), F32),
            jax.ShapeDtypeStruct((2, M_TOK), I32),
            jax.ShapeDtypeStruct((M_TOK, LANES), F32),
            jax.ShapeDtypeStruct((2, M_TOK), I32),
            jax.ShapeDtypeStruct((N_EXPERTS, LANES), F32),
        ],
        scratch_shapes=[
            pltpu.VMEM((N_EXPERTS, LANES), F32),
            pltpu.VMEM((D_MODEL, D_MODEL), BF16),
            pltpu.VMEM((2, 256, D_MODEL), F32),
            pltpu.SemaphoreType.DMA((2,)),
            pltpu.VMEM((D_MODEL, LANES), BF16),
            pltpu.VMEM((D_MODEL, LANES), BF16),
        ],
        compiler_params=_cparams(("arbitrary",)),
        name="out_proj_router",
    )(ycp, yret, x, w_out, ada5, ln_g.reshape(1, D_MODEL), ln_b.reshape(1, D_MODEL),
      ada5, ada5, w_router_pad, router_bias)


def _pos_kernel(ei_ref, rk_ref, cnt_ref, pos_ref):
    ei = ei_ref[...]
    pos = rk_ref[...]
    start = jnp.zeros((1, 1), F32)
    for e in range(N_EXPERTS):
        pos = pos + jnp.where(ei == e, start.astype(I32), 0)
        n_tiles = jnp.floor((cnt_ref[e:e + 1, 0:1] + (TM_EXP - 1.0)) * (1.0 / TM_EXP))
        start = start + n_tiles * TM_EXP
    pos_ref[...] = pos


def _pair_rows(ei, rk, cnt):
    return pl.pallas_call(
        _pos_kernel,
        out_shape=jax.ShapeDtypeStruct((2, M_TOK), I32),
        name="pair_rows",
    )(ei, rk, cnt)


def _route_kernel(cnt_ref, pos_ref, src_ref, te_ref, nv_ref):
    def zero(r, carry):
        src_ref[r] = 0
        return carry

    tile = jnp.int32(0)
    for e in range(N_EXPERTS):
        n_tiles = lax.shift_right_logical(cnt_ref[e] + (TM_EXP - 1), TM_EXP.bit_length() - 1)

        def mark(j, carry, e=e, tile=tile):
            te_ref[tile + j] = e
            return carry
        lax.fori_loop(0, n_tiles, mark, 0)
        lax.fori_loop(tile * TM_EXP + cnt_ref[e], (tile + n_tiles) * TM_EXP, zero, 0)
        tile = tile + n_tiles
    nv_ref[0] = tile
    last_expert = te_ref[tile - 1]

    def mark_unused(j, carry):
        te_ref[j] = last_expert
        return carry
    lax.fori_loop(tile, NT_EXP, mark_unused, 0)
    lax.fori_loop(tile * TM_EXP, NP_EXP, zero, 0)

    def place(t, carry):
        src_ref[pos_ref[t]] = t
        src_ref[pos_ref[M_TOK + t]] = t
        return carry
    lax.fori_loop(0, M_TOK, place, 0, unroll=16)


def _route_tables(cnt, pos):
    smem = pl.BlockSpec(memory_space=pltpu.SMEM)
    return pl.pallas_call(
        _route_kernel,
        in_specs=[smem, smem],
        out_specs=[smem, smem, smem],
        out_shape=[
            jax.ShapeDtypeStruct((NP_EXP,), I32),
            jax.ShapeDtypeStruct((NT_EXP,), I32),
            jax.ShapeDtypeStruct((1,), I32),
        ],
        name="route_tables",
    )(cnt, pos)


def _row_gather_start(src_hbm, buf, sem, idx_ref, base, n_tok, tok0=0):
    for r in range(n_tok):
        pltpu.make_async_copy(src_hbm.at[pl.ds(idx_ref[base + r] * TOK_STRIDE, TOK_ROWS), :],
                              buf.at[pl.ds((tok0 + r) * TOK_STRIDE, TOK_ROWS), :], sem).start()


def _row_gather_wait(src_hbm, buf, sem):
    n_rows = buf.shape[0] // TOK_STRIDE * TOK_ROWS
    pltpu.make_async_copy(src_hbm.at[pl.ds(0, n_rows), :], buf.at[pl.ds(0, n_rows), :], sem).wait()


def _experts_kernel(te_ref, nv_ref, src_ref, h2_hbm, wg_ref, wu_ref, wd_ref, o_ref, xbuf0, xbuf1, gsem,
                    wg_bf, wu_bf, wd_bf):
    i = pl.program_id(0)
    n_valid = nv_ref[0]
    bufs = (xbuf0, xbuf1)

    @pl.when(i == 0)
    def _():
        _row_gather_start(h2_hbm, xbuf0, gsem.at[0], src_ref, 0, TM_EXP)

    @pl.when((i < n_valid) & ((i == 0) | (te_ref[i] != te_ref[jnp.maximum(i - 1, 0)])))
    def _():
        wg_bf[...] = wg_ref[...].astype(BF16)
        wu_bf[...] = wu_ref[...].astype(BF16)
        wd_bf[...] = wd_ref[...].astype(BF16)

    def step(cur):
        nxt_buf, nxt_sem = bufs[1 - cur], gsem.at[1 - cur]
        _row_gather_wait(h2_hbm, bufs[cur], gsem.at[cur])
        nxt = jnp.minimum(i + 1, NT_EXP - 1)
        _row_gather_start(h2_hbm, nxt_buf, nxt_sem, src_ref, nxt * TM_EXP, TM_EXP)
        x = _load_token_tiles(bufs[cur], 0, TM_EXP).astype(BF16)
        g = jnp.dot(x, wg_bf[...], preferred_element_type=F32)
        u = jnp.dot(x, wu_bf[...], preferred_element_type=F32)
        a = (_silu(g) * u).astype(BF16)
        _store_token_tiles(o_ref, jnp.dot(a, wd_bf[...], preferred_element_type=F32))

        @pl.when(i == n_valid - 1)
        def _():
            _row_gather_wait(h2_hbm, nxt_buf, nxt_sem)

    for cur in range(2):
        pl.when((i < n_valid) & (i % 2 == cur))(functools.partial(step, cur))

    @pl.when(i >= n_valid)
    def _():
        o_ref[...] = jnp.zeros_like(o_ref)


def _experts(h2, tile_expert, n_valid, src_tok, wg, wu, wd, layer):
    grid_spec = pltpu.PrefetchScalarGridSpec(
        num_scalar_prefetch=3,
        grid=(NT_EXP,),
        in_specs=[
            pl.BlockSpec(memory_space=pl.ANY),
            pl.BlockSpec((None, None, D_MODEL, D_EXPERT), lambda i, te, nv, src: (layer, te[i], 0, 0)),
            pl.BlockSpec((None, None, D_MODEL, D_EXPERT), lambda i, te, nv, src: (layer, te[i], 0, 0)),
            pl.BlockSpec((None, None, D_EXPERT, D_MODEL), lambda i, te, nv, src: (layer, te[i], 0, 0)),
        ],
        out_specs=pl.BlockSpec((TM_EXP * TOK_STRIDE, LANES), lambda i, te, nv, src: (i, 0)),
        scratch_shapes=[
            pltpu.VMEM((TM_EXP * TOK_STRIDE, LANES), F32),
            pltpu.VMEM((TM_EXP * TOK_STRIDE, LANES), F32),
            pltpu.SemaphoreType.DMA((2,)),
            pltpu.VMEM((D_MODEL, D_EXPERT), BF16),
            pltpu.VMEM((D_MODEL, D_EXPERT), BF16),
            pltpu.VMEM((D_EXPERT, D_MODEL), BF16),
        ],
    )
    return pl.pallas_call(
        _experts_kernel,
        grid_spec=grid_spec,
        out_shape=jax.ShapeDtypeStruct((NP_EXP * TOK_STRIDE, LANES), F32),
        compiler_params=_cparams(("arbitrary",)),
        name="experts",
    )(tile_expert, n_valid, src_tok, h2, wg, wu, wd)


def _final_kernel(*refs, emit_h):
    if emit_h:
        (pos_ref, x1_ref, ys_hbm, ewt_ref, gate2_ref, lng_ref, lnb_ref, sc1_ref, sh1_ref,
         x_ref, h_ref, rbuf0, rbuf1, sem) = refs
    else:
        (pos_ref, x1_ref, ys_hbm, ewt_ref, gate2_ref, lng_ref, lnb_ref,
         xc_ref, xl_ref, rbuf0, rbuf1, sem) = refs
    tm = TM_OUT
    i = pl.program_id(0)
    n_blocks = M_TOK // tm
    bufs = (rbuf0, rbuf1)

    def start(tile, buf, buf_sem):
        _row_gather_start(ys_hbm, buf, buf_sem, pos_ref, tile * tm, tm, tok0=0)
        _row_gather_start(ys_hbm, buf, buf_sem, pos_ref, M_TOK + tile * tm, tm, tok0=tm)

    @pl.when(i == 0)
    def _():
        start(0, rbuf0, sem.at[0])

    def step(cur):
        buf = bufs[cur]
        nxt_buf, nxt_sem = bufs[1 - cur], sem.at[1 - cur]
        _row_gather_wait(ys_hbm, buf, sem.at[cur])
        start(jnp.minimum(i + 1, n_blocks - 1), nxt_buf, nxt_sem)
        w = ewt_ref[...]
        y2 = w[:, 0:1] * _load_token_tiles(buf, 0, tm) + w[:, 1:2] * _load_token_tiles(buf, tm, tm)
        x2 = _layer_norm_rows(DEEPNORM_ALPHA * x1_ref[...] + gate2_ref[...] * y2,
                              lng_ref[...], lnb_ref[...])
        if emit_h:
            x_ref[...] = x2
            h_ref[...] = (x2 * (1.0 + sc1_ref[...]) + sh1_ref[...]).astype(BF16)
        else:
            @pl.when(i < M_CTX // tm)
            def _():
                xc_ref[...] = x2

            @pl.when(i >= M_CTX // tm)
            def _():
                xl_ref[...] = x2

        @pl.when(i == n_blocks - 1)
        def _():
            _row_gather_wait(ys_hbm, nxt_buf, nxt_sem)

    for cur in range(2):
        pl.when(i % 2 == cur)(functools.partial(step, cur))


def _final(x1, ys, pos, ewt, ada5, ln_g, ln_b, ada5_next):
    tm = TM_OUT
    emit_h = ada5_next is not None
    vec_spec = pl.BlockSpec((1, D_MODEL), lambda i, pos: (0, 0))
    ada_spec = lambda chunk: pl.BlockSpec(
        (None, None, 1, D_MODEL), lambda i, pos: (_ada_row(i, tm), chunk, 0, 0))
    in_specs = [
        pl.BlockSpec((tm, D_MODEL), lambda i, pos: (i, 0)),
        pl.BlockSpec(memory_space=pl.ANY),
        pl.BlockSpec((tm, LANES), lambda i, pos: (i, 0)),
        ada_spec(5),
        vec_spec, vec_spec,
    ]
    args = [pos, x1, ys, ewt, ada5, ln_g.reshape(1, D_MODEL), ln_b.reshape(1, D_MODEL)]
    if emit_h:
        in_specs += [ada_spec(1), ada_spec(0)]
        args += [ada5_next, ada5_next]
        row_spec = pl.BlockSpec((tm, D_MODEL), lambda i, pos: (i, 0))
        out_specs = [row_spec, row_spec]
        out_shape = [jax.ShapeDtypeStruct((M_TOK, D_MODEL), F32),
                     jax.ShapeDtypeStruct((M_TOK, D_MODEL), BF16)]
    else:
        out_specs = [
            pl.BlockSpec((tm, D_MODEL), lambda i, pos: (_ctx_block(i, tm), 0)),
            pl.BlockSpec((tm, D_MODEL), lambda i, pos: (_lat_block(i, tm), 0)),
        ]
        out_shape = [jax.ShapeDtypeStruct((M_CTX, D_MODEL), F32),
                     jax.ShapeDtypeStruct((M_LAT, D_MODEL), F32)]
    grid_spec = pltpu.PrefetchScalarGridSpec(
        num_scalar_prefetch=1,
        grid=(M_TOK // tm,),
        in_specs=in_specs,
        out_specs=out_specs,
        scratch_shapes=[
            pltpu.VMEM((2 * tm * TOK_STRIDE, LANES), F32),
            pltpu.VMEM((2 * tm * TOK_STRIDE, LANES), F32),
            pltpu.SemaphoreType.DMA((2,)),
        ],
    )
    return pl.pallas_call(
        functools.partial(_final_kernel, emit_h=emit_h),
        grid_spec=grid_spec,
        out_shape=out_shape,
        compiler_params=_cparams(("arbitrary",)),
        name="final_ln",
    )(*args)


def _rope_tables():
    rows = T_LAT // GRID_W
    row = jnp.repeat(jnp.arange(rows), GRID_W).astype(F32)
    col = jnp.tile(jnp.arange(GRID_W), rows).astype(F32)
    n_freq = HEAD_DIM // 4
    inv_freq = ROPE_BASE ** (-jnp.arange(n_freq, dtype=F32) / n_freq)
    ang = jnp.concatenate([row[:, None] * inv_freq[None], col[:, None] * inv_freq[None]], axis=-1)
    cos, sin = jnp.cos(ang), jnp.sin(ang)
    return jnp.concatenate([cos, cos], axis=-1), jnp.concatenate([-sin, sin], axis=-1)


def kernel(x_prompt, x_sample, state_retention, c, c_ctx, w_ada, b_ada, w_in, w_out, conv_w, pool_w,
           pool_scale, ret_decay_logit, ret_gn_gain, ln1_g, ln1_b, ln2_g, ln2_b, w_router, router_bias,
           w_gate, w_up, w_down):
    x_ctx = x_prompt.reshape(M_CTX, D_MODEL)
    x_lat = x_sample.reshape(M_LAT, D_MODEL)
    c_all = jnp.concatenate(
        [c_ctx[None, :], c, jnp.zeros((ADA_ROWS - 1 - N_LAT_SEQ, D_MODEL), F32)], axis=0)
    ada = _ada_table(c_all, w_ada, b_ada).reshape(DEPTH, ADA_ROWS, 6, 1, D_MODEL)

    cos_lat, sin_lat = _rope_tables()
    w_router_pad = jnp.pad(w_router, ((0, 0), (0, LANES - N_EXPERTS)))

    states = []
    x, h = _modulate(x_ctx, x_lat, ada[0])
    for l in range(DEPTH):
        ada5 = ada[l]
        z = _in_proj(h, w_in, l)

        ycp = _conv_pool(z, conv_w[l], pool_w[l], pool_scale[l])

        dl = jnp.broadcast_to(ret_decay_logit[l][:, :, None, None], (2, N_RET_HEADS, 1, HEAD_DIM))
        yret, s_fin = _retention(z, cos_lat, sin_lat, dl, ret_gn_gain[l], state_retention, l)
        states.append(s_fin)

        x1, h2, ei, ewt, rk, cnt = _out_proj(ycp, yret, x, w_out, l, ada5,
                                             ln1_g[l], ln1_b[l], w_router_pad, router_bias)
        pos = _pair_rows(ei, rk, cnt).reshape(-1)
        src_tok, te, n_valid = _route_tables(cnt[:, 0].astype(I32), pos)
        ys = _experts(h2, te, n_valid, src_tok, w_gate, w_up, w_down, l)
        if l + 1 < DEPTH:
            x, h = _final(x1, ys, pos, ewt, ada5, ln2_g[l], ln2_b[l], ada[l + 1])
        else:
            x_ctx, x_lat = _final(x1, ys, pos, ewt, ada5, ln2_g[l], ln2_b[l], None)

    y_prompt = x_ctx.reshape(N_CTX_SEQ, T_CTX, D_MODEL)
    y_sample = x_lat.reshape(N_LAT_SEQ, T_LAT, D_MODEL)
    return y_prompt, y_sample, jnp.stack(states, axis=1)
```

```python
import functools

import jax
import jax.numpy as jnp
from jax import lax
from jax.experimental import pallas as pl
from jax.experimental.pallas import tpu as pltpu

F32 = jnp.float32
BF16 = jnp.bfloat16
I32 = jnp.int32

D_MODEL = 2048
N_CTX_SEQ, T_CTX = 16, 256
N_LAT_SEQ, T_LAT = 8, 1024
DEPTH = 2
M_CTX = N_CTX_SEQ * T_CTX
M_LAT = N_LAT_SEQ * T_LAT
M_TOK = M_CTX + M_LAT

GRID_W = 64
D_CONV = D_MODEL // 4
D_POOL = D_MODEL // 4
D_RET = D_MODEL // 2
N_RET_HEADS = 8
HEAD_DIM = D_RET // N_RET_HEADS
POOL_WINDOWS = (2, 4, 8, 16)
POOL_GROUP_DIM = D_POOL // len(POOL_WINDOWS)
CHUNK = 128
ROPE_BASE = 10000.0
N_EXPERTS = 16
EXPERTS_PER_GROUP = 4
N_EXPERT_GROUPS = N_EXPERTS // EXPERTS_PER_GROUP
D_EXPERT = D_MODEL // 4
D_IN_PROJ = 3 * D_CONV + D_POOL + 4 * D_RET
DEEPNORM_ALPHA = (2.0 * DEPTH) ** 0.25
LN_EPS = 1e-5
ADA_ROWS = 16

LANES = 128
VMEM_LIMIT = 56 * 1024 * 1024

TM_IN = 1024
TN_IN = 1024
TOK_ROWS = D_MODEL // LANES
TOK_STRIDE = TOK_ROWS + 1
TM_OP = 512
SUB_OP = 256
TM_OUT = 256
TM_EXP = 256
N_PAIR = 2 * M_TOK
NP_EXP = N_PAIR + N_EXPERTS * TM_EXP
NT_EXP = NP_EXP // TM_EXP


def _cparams(sem):
    return pltpu.CompilerParams(dimension_semantics=sem, vmem_limit_bytes=VMEM_LIMIT)


def _silu(x):
    return x * jax.nn.sigmoid(x)


def _ada_row(i, tm):
    n_ctx_tiles = M_CTX // tm
    per_batch = T_LAT // tm
    return jnp.where(i < n_ctx_tiles, 0, 1 + (i - n_ctx_tiles) // per_batch)


def _ada_kernel(c_ref, w_ref, b_ref, o_ref):
    s = _silu(c_ref[...]).astype(BF16)
    o_ref[...] = jnp.dot(s, w_ref[...].astype(BF16), preferred_element_type=F32) + b_ref[...]


def _ada_table(c_all, w_ada, b_ada):
    tn = 1024
    n6 = 6 * D_MODEL
    return pl.pallas_call(
        _ada_kernel,
        grid=(DEPTH, n6 // tn),
        in_specs=[
            pl.BlockSpec((ADA_ROWS, D_MODEL), lambda l, j: (0, 0)),
            pl.BlockSpec((None, D_MODEL, tn), lambda l, j: (l, 0, j)),
            pl.BlockSpec((None, 1, tn), lambda l, j: (l, 0, j)),
        ],
        out_specs=pl.BlockSpec((None, ADA_ROWS, tn), lambda l, j: (l, 0, j)),
        out_shape=jax.ShapeDtypeStruct((DEPTH, ADA_ROWS, n6), F32),
        compiler_params=_cparams(("arbitrary", "arbitrary")),
        name="ada_table",
    )(c_all, w_ada, b_ada.reshape(DEPTH, 1, n6))


def _ctx_block(i, tm):
    return jnp.minimum(i, M_CTX // tm - 1)


def _lat_block(i, tm):
    return jnp.maximum(i - M_CTX // tm, 0)


def _modulate_kernel(xc_ref, xl_ref, sc_ref, sh_ref, x_ref, h_ref):
    x = jnp.where(pl.program_id(0) < M_CTX // xc_ref.shape[0], xc_ref[...], xl_ref[...])
    x_ref[...] = x
    h_ref[...] = (x * (1.0 + sc_ref[...]) + sh_ref[...]).astype(BF16)


def _modulate(x_ctx, x_lat, ada5):
    tm = TM_OUT
    ada_spec = lambda chunk: pl.BlockSpec(
        (None, None, 1, D_MODEL), lambda i: (_ada_row(i, tm), chunk, 0, 0))
    return pl.pallas_call(
        _modulate_kernel,
        grid=(M_TOK // tm,),
        in_specs=[
            pl.BlockSpec((tm, D_MODEL), lambda i: (_ctx_block(i, tm), 0)),
            pl.BlockSpec((tm, D_MODEL), lambda i: (_lat_block(i, tm), 0)),
            ada_spec(1), ada_spec(0),
        ],
        out_specs=[pl.BlockSpec((tm, D_MODEL), lambda i: (i, 0)),
                   pl.BlockSpec((tm, D_MODEL), lambda i: (i, 0))],
        out_shape=[jax.ShapeDtypeStruct((M_TOK, D_MODEL), F32),
                   jax.ShapeDtypeStruct((M_TOK, D_MODEL), BF16)],
        compiler_params=_cparams(("arbitrary",)),
        name="modulate",
    )(x_ctx, x_lat, ada5, ada5)


def _inproj_kernel(h_ref, w_ref, o_ref, wbf_scr):
    @pl.when(pl.program_id(1) == 0)
    def _():
        wbf_scr[...] = w_ref[...].astype(BF16)

    o_ref[...] = jnp.dot(h_ref[...], wbf_scr[...], preferred_element_type=F32).astype(BF16)


def _in_proj(h, w_in, layer):
    return pl.pallas_call(
        _inproj_kernel,
        grid=(D_IN_PROJ // TN_IN, M_TOK // TM_IN),
        in_specs=[
            pl.BlockSpec((TM_IN, D_MODEL), lambda j, i: (i, 0)),
            pl.BlockSpec((None, D_MODEL, TN_IN), lambda j, i: (layer, 0, j)),
        ],
        out_specs=pl.BlockSpec((TM_IN, TN_IN), lambda j, i: (i, j)),
        out_shape=jax.ShapeDtypeStruct((M_TOK, D_IN_PROJ), BF16),
        scratch_shapes=[pltpu.VMEM((D_MODEL, TN_IN), BF16)],
        compiler_params=_cparams(("arbitrary", "arbitrary")),
        name="in_proj",
    )(h, w_in)


MIX_ROWS = T_LAT
N_CTX_BLOCKS = M_CTX // MIX_ROWS
SEQ_PER_CTX_BLOCK = MIX_ROWS // T_CTX
CP_ROWS = T_CTX
CP_HALO = 128


def _convpool_kernel(z_ref, cw_ref, pw_ref, ps_ref, o_ref, band_ref):
    blk = pl.program_id(0)
    g_dim = POOL_GROUP_DIM

    @pl.when(blk == 0)
    def _():
        row = lax.broadcasted_iota(I32, (CP_ROWS, CP_ROWS + 2 * CP_HALO), 0)
        col = lax.broadcasted_iota(I32, (CP_ROWS, CP_ROWS + 2 * CP_HALO), 1)
        d = col - CP_HALO - row
        for gi, w in enumerate(POOL_WINDOWS):
            band_ref[gi] = jnp.where((d >= -(w // 2)) & (d < w // 2), 1.0, 0.0).astype(BF16)

    def block(seq_len):
        t = lax.broadcasted_iota(I32, (CP_ROWS, LANES), 0)
        for ch in range(MIX_ROWS // CP_ROWS):
            r0 = ch * CP_ROWS
            rows = slice(r0, r0 + CP_ROWS)
            pos0 = r0 % seq_len
            at_start = pos0 == 0
            at_end = pos0 + CP_ROWS == seq_len

            for cg in range(D_CONV // LANES):
                lanes = slice(cg * LANES, (cg + 1) * LANES)

                def u_rows(rs):
                    return (z_ref[rs, D_CONV + cg * LANES:D_CONV + (cg + 1) * LANES].astype(F32)
                            * z_ref[rs, 2 * D_CONV + cg * LANES:2 * D_CONV + (cg + 1) * LANES].astype(F32))
                u = u_rows(rows)
                before = 0.0 if at_start else u_rows(slice(r0 - 1, r0))
                after = 0.0 if at_end else u_rows(slice(r0 + CP_ROWS, r0 + CP_ROWS + 1))
                u_prev = jnp.where(t == 0, before, pltpu.roll(u, 1, 0))
                u_next = jnp.where(t == CP_ROWS - 1, after, pltpu.roll(u, CP_ROWS - 1, 0))
                conv = u_prev * cw_ref[0:1, lanes] + u * cw_ref[1:2, lanes] + u_next * cw_ref[2:3, lanes]
                o_ref[rows, lanes] = (z_ref[rows, lanes].astype(F32) * conv).astype(BF16)

            k_rows = slice(r0 if at_start else r0 - CP_HALO,
                           r0 + CP_ROWS if at_end else r0 + CP_ROWS + CP_HALO)
            b_cols = slice(CP_HALO if at_start else 0,
                           CP_HALO + CP_ROWS if at_end else CP_ROWS + 2 * CP_HALO)
            tpos = pos0 + t
            for gi, w in enumerate(POOL_WINDOWS):
                lo = 3 * D_CONV + gi * g_dim
                win = jnp.dot(band_ref[gi, :, b_cols], z_ref[k_rows, lo:lo + g_dim],
                              preferred_element_type=F32)
                cnt = (jnp.minimum(tpos + w // 2, seq_len) - jnp.maximum(tpos - w // 2, 0)).astype(F32)
                pooled = win / cnt - z_ref[rows, lo:lo + g_dim].astype(F32)
                y = jnp.dot(pooled.astype(BF16), pw_ref[gi].astype(BF16), preferred_element_type=F32)
                y = y * ps_ref[:, gi * g_dim:(gi + 1) * g_dim]
                o_ref[rows, D_CONV + gi * g_dim:D_CONV + (gi + 1) * g_dim] = y.astype(BF16)

    pl.when(blk < N_CTX_BLOCKS)(functools.partial(block, T_CTX))
    pl.when(blk >= N_CTX_BLOCKS)(functools.partial(block, T_LAT))


def _conv_pool(z, conv_w, pool_w, pool_scale):
    return pl.pallas_call(
        _convpool_kernel,
        grid=(M_TOK // MIX_ROWS,),
        in_specs=[
            pl.BlockSpec((MIX_ROWS, D_IN_PROJ - 4 * D_RET), lambda b: (b, 0)),
            pl.BlockSpec((3, D_CONV), lambda b: (0, 0)),
            pl.BlockSpec((len(POOL_WINDOWS), POOL_GROUP_DIM, POOL_GROUP_DIM), lambda b: (0, 0, 0)),
            pl.BlockSpec((1, D_POOL), lambda b: (0, 0)),
        ],
        out_specs=pl.BlockSpec((MIX_ROWS, D_CONV + D_POOL), lambda b: (b, 0)),
        out_shape=jax.ShapeDtypeStruct((M_TOK, D_CONV + D_POOL), BF16),
        scratch_shapes=[pltpu.VMEM((len(POOL_WINDOWS), CP_ROWS, CP_ROWS + 2 * CP_HALO), BF16)],
        compiler_params=_cparams(("arbitrary",)),
        name="conv_pool",
    )(z, conv_w, pool_w, pool_scale.reshape(1, D_POOL))


def _log_sigmoid(x):
    return jnp.minimum(x, 0.0) - jnp.log1p(jnp.exp(-jnp.abs(x)))


RET_ROWS = MIX_ROWS


def _retention_kernel(q_ref, k_ref, v_ref, g_ref, cos_ref, sin_ref, dl_ref, gain_ref, s0_ref,
                      y_ref, sfin_ref, tab_scr, kr_scr, u_scr, s_scr):
    blk = pl.program_id(1)
    n_chunks = RET_ROWS // CHUNK
    half = HEAD_DIM // 2

    lg_f = _log_sigmoid(dl_ref[0])
    lg_b = _log_sigmoid(dl_ref[1])
    g_f = jnp.exp(lg_f * CHUNK)
    g_b = jnp.exp(lg_b * CHUNK)

    @pl.when(blk == 0)
    def _():
        row = lax.broadcasted_iota(I32, (CHUNK, CHUNK), 0).astype(F32)
        col = lax.broadcasted_iota(I32, (CHUNK, CHUNK), 1).astype(F32)
        diff = row - col
        tab_scr[0] = (jnp.where(diff >= 0, jnp.exp(lg_f * jnp.maximum(diff, 0.0)), 0.0)
                      + jnp.where(diff <= 0, jnp.exp(lg_b * jnp.maximum(-diff, 0.0)), 0.0))
        tab_scr[1] = jnp.exp(lg_f * (row + 1.0))
        tab_scr[2] = jnp.exp(lg_b * (CHUNK - row))
        tab_scr[3] = jnp.exp(lg_f * (CHUNK - 1.0 - row))
        tab_scr[4] = jnp.exp(lg_b * row)

    tn_dims = (((0,), (0,)), ((), ()))
    nt_dims = (((1,), (1,)), ((), ()))

    def block(seq_chunks, latent):
        def rope(x, sl):
            if not latent:
                return x
            return x * cos_ref[sl, :] + pltpu.roll(x, half, 1) * sin_ref[sl, :]

        for c in range(n_chunks):
            sl = slice(c * CHUNK, (c + 1) * CHUNK)
            kr = rope(k_ref[sl, :].astype(F32), sl) * (HEAD_DIM ** -0.5)
            kr_scr[sl, :] = kr.astype(BF16)
            kz = jnp.concatenate([kr * tab_scr[3], kr * tab_scr[4]], axis=1).astype(BF16)
            u_scr[c] = lax.dot_general(kz, v_ref[sl, :], tn_dims, preferred_element_type=F32)

        has_state = [[False] * n_chunks, [False] * n_chunks]
        for s_i in range(n_chunks // seq_chunks):
            chunks = list(range(s_i * seq_chunks, (s_i + 1) * seq_chunks))
            for d, order, g_d in ((0, chunks, g_f), (1, chunks[::-1], g_b)):
                s = s0_ref[d] if latent else None
                for c in order:
                    u = u_scr[c, d * HEAD_DIM:(d + 1) * HEAD_DIM, :]
                    if s is None:
                        s = u
                    else:
                        s_scr[d, c] = s.astype(BF16)
                        has_state[d][c] = True
                        s = g_d * s + u
                if not latent:
                    sfin_ref[s_i, d] = s

        for c in range(n_chunks):
            sl = slice(c * CHUNK, (c + 1) * CHUNK)
            qr = rope(q_ref[sl, :].astype(F32), sl)
            scores = lax.dot_general(qr.astype(BF16), kr_scr[sl, :], nt_dims, preferred_element_type=F32)
            o = jnp.dot((scores * tab_scr[0]).astype(BF16), v_ref[sl, :], preferred_element_type=F32)
            for d in range(2):
                if has_state[d][c]:
                    o += jnp.dot((qr * tab_scr[1 + d]).astype(BF16), s_scr[d, c],
                                 preferred_element_type=F32)
            mu = jnp.mean(o, axis=-1, keepdims=True)
            dev = o - mu
            var = jnp.mean(dev * dev, axis=-1, keepdims=True)
            on = dev * lax.rsqrt(var + LN_EPS) * gain_ref[...]
            y_ref[sl, :] = (_silu(g_ref[sl, :].astype(F32)) * on).astype(BF16)

    pl.when(blk < N_CTX_BLOCKS)(functools.partial(block, T_CTX // CHUNK, False))
    pl.when(blk >= N_CTX_BLOCKS)(functools.partial(block, T_LAT // CHUNK, True))


def _retention(z, cos_t, sin_t, decay_logit_b, gain, s0, layer):
    qcol0 = (3 * D_CONV + D_POOL) // HEAD_DIM
    n_blocks = M_TOK // RET_ROWS

    def zspec(k):
        return pl.BlockSpec((RET_ROWS, HEAD_DIM), lambda h, b: (b, qcol0 + k * N_RET_HEADS + h))

    def lat_seq(b):
        return jnp.maximum(b - N_CTX_BLOCKS, 0)

    def ctx_block(b):
        return jnp.minimum(b, N_CTX_BLOCKS - 1)

    n_chunks = RET_ROWS // CHUNK
    return pl.pallas_call(
        _retention_kernel,
        grid=(N_RET_HEADS, n_blocks),
        in_specs=[
            zspec(0), zspec(1), zspec(2), zspec(3),
            pl.BlockSpec((RET_ROWS, HEAD_DIM), lambda h, b: (0, 0)),
            pl.BlockSpec((RET_ROWS, HEAD_DIM), lambda h, b: (0, 0)),
            pl.BlockSpec((2, None, 1, HEAD_DIM), lambda h, b: (0, h, 0, 0)),
            pl.BlockSpec((1, HEAD_DIM), lambda h, b: (0, h)),
            pl.BlockSpec((None, None, 2, None, HEAD_DIM, HEAD_DIM),
                         lambda h, b: (lat_seq(b), layer, 0, h, 0, 0)),
        ],
        out_specs=[
            pl.BlockSpec((RET_ROWS, HEAD_DIM), lambda h, b: (b, h)),
            pl.BlockSpec((SEQ_PER_CTX_BLOCK, 2, None, HEAD_DIM, HEAD_DIM),
                         lambda h, b: (ctx_block(b), 0, h, 0, 0)),
        ],
        out_shape=[
            jax.ShapeDtypeStruct((M_TOK, D_RET), BF16),
            jax.ShapeDtypeStruct((N_CTX_SEQ, 2, N_RET_HEADS, HEAD_DIM, HEAD_DIM), F32),
        ],
        scratch_shapes=[
            pltpu.VMEM((5, CHUNK, CHUNK), F32),
            pltpu.VMEM((RET_ROWS, HEAD_DIM), BF16),
            pltpu.VMEM((n_chunks, 2 * HEAD_DIM, HEAD_DIM), F32),
            pltpu.VMEM((2, n_chunks, HEAD_DIM, HEAD_DIM), BF16),
        ],
        compiler_params=_cparams(("arbitrary", "arbitrary")),
        name="retention",
    )(z, z, z, z, cos_t, sin_t, decay_logit_b, gain.reshape(1, D_RET), s0)


def _layer_norm_rows(r, g, b):
    mu = jnp.mean(r, axis=-1, keepdims=True)
    dev = r - mu
    var = jnp.mean(dev * dev, axis=-1, keepdims=True)
    return dev * lax.rsqrt(var + LN_EPS) * g + b


def _top2_of4(vals):
    top1 = jnp.maximum(jnp.maximum(vals[0], vals[1]), jnp.maximum(vals[2], vals[3]))
    idx1 = jnp.where(vals[0] == top1, 0, jnp.where(vals[1] == top1, 1, jnp.where(vals[2] == top1, 2, 3)))
    neg = jnp.float32(-jnp.inf)
    rest = [jnp.where(idx1 == j, neg, vals[j]) for j in range(4)]
    top2 = jnp.maximum(jnp.maximum(rest[0], rest[1]), jnp.maximum(rest[2], rest[3]))
    idx2 = jnp.where(rest[0] == top2, 0, jnp.where(rest[1] == top2, 1, jnp.where(rest[2] == top2, 2, 3)))
    return top1, idx1, top2, idx2


def _store_token_tiles(ref, val, tok0=0):
    n = val.shape[0]
    base = tok0 * TOK_STRIDE
    for c in range(TOK_ROWS):
        ref[pl.ds(base + c, n, stride=TOK_STRIDE), :] = val[:, c * LANES:(c + 1) * LANES]
    ref[pl.ds(base + TOK_ROWS, n, stride=TOK_STRIDE), :] = jnp.zeros((n, LANES), val.dtype)


def _load_token_tiles(ref, tok0, n):
    return jnp.concatenate(
        [ref[pl.ds(tok0 * TOK_STRIDE + c, n, stride=TOK_STRIDE), :] for c in range(TOK_ROWS)], axis=1)


def _outproj_kernel(ycp_ref, yret_ref, x_ref, wo_hbm, gate1_ref, lng_ref, lnb_ref, sc2_ref, sh2_ref,
                    wr_ref, rb_ref, x1_ref, h2_ref, ei_ref, ewt_ref, rk_ref, cnt_ref,
                    carry_scr, wo_ref, wstage, wsem, wr_hi, wr_lo, *, layer):
    i = pl.program_id(0)

    @pl.when(i == 0)
    def _():
        carry_scr[...] = jnp.zeros_like(carry_scr)
        wr = wr_ref[...]
        hi = wr.astype(BF16)
        wr_hi[...] = hi
        wr_lo[...] = (wr - hi.astype(F32)).astype(BF16)
        rows = wstage.shape[1]
        n_chunks = D_MODEL // rows

        def chunk_copy(c):
            return pltpu.make_async_copy(wo_hbm.at[layer, pl.ds(c * rows, rows), :], wstage.at[c % 2],
                                         wsem.at[c % 2])
        chunk_copy(0).start()
        for c in range(n_chunks):
            if c + 1 < n_chunks:
                chunk_copy(c + 1).start()
            chunk_copy(c).wait()
            wo_ref[c * rows:(c + 1) * rows, :] = wstage[c % 2].astype(BF16)

    s_i = lax.broadcasted_iota(I32, (SUB_OP, SUB_OP), 0)
    t_i = lax.broadcasted_iota(I32, (SUB_OP, SUB_OP), 1)
    tri = jnp.where(s_i < t_i, 1.0, 0.0).astype(BF16)
    half_k = D_CONV + D_POOL
    n_sub = x_ref.shape[0] // SUB_OP
    ys = []
    for sub in range(n_sub):
        rows_sl = slice(sub * SUB_OP, (sub + 1) * SUB_OP)
        y = jnp.dot(ycp_ref[rows_sl, :], wo_ref[0:half_k, :], preferred_element_type=F32)
        ys.append(y + jnp.dot(yret_ref[rows_sl, :], wo_ref[half_k:, :], preferred_element_type=F32))
    lts = [_ln_router_subtile(sub, ys[sub], x_ref, gate1_ref, lng_ref, lnb_ref, sc2_ref, sh2_ref,
                              wr_hi, wr_lo, x1_ref, h2_ref) for sub in range(n_sub)]
    carry = carry_scr[:, 0:1]
    for sub in range(n_sub):
        carry = _route_subtile(sub, lts[sub], carry, tri, rb_ref, ei_ref, ewt_ref, rk_ref)
    carry_scr[...] = jnp.broadcast_to(carry, carry_scr.shape)
    cnt_ref[...] = carry_scr[...]


def _ln_router_subtile(sub, y, x_ref, gate1_ref, lng_ref, lnb_ref, sc2_ref, sh2_ref, wr_hi, wr_lo,
                       x1_ref, h2_ref):
    tm = SUB_OP
    rows_sl = slice(sub * tm, (sub + 1) * tm)
    x1 = _layer_norm_rows(DEEPNORM_ALPHA * x_ref[rows_sl, :] + gate1_ref[...] * y,
                          lng_ref[...], lnb_ref[...])
    x1_ref[rows_sl, :] = x1
    h2 = x1 * (1.0 + sc2_ref[...]) + sh2_ref[...]
    _store_token_tiles(h2_ref, h2, tok0=sub * tm)

    h_hi = h2.astype(BF16)
    h_lo = (h2 - h_hi.astype(F32)).astype(BF16)
    logits = (jnp.dot(h_hi, wr_hi[...], preferred_element_type=F32)
              + jnp.dot(h_lo, wr_hi[...], preferred_element_type=F32)
              + jnp.dot(h_hi, wr_lo[...], preferred_element_type=F32))
    return logits.T


def _route_subtile(sub, lt, carry, tri, rb_ref, ei_ref, ewt_ref, rk_ref):
    tm = SUB_OP
    rows_sl = slice(sub * tm, (sub + 1) * tm)
    rows = [lt[e:e + 1, :] for e in range(N_EXPERTS)]

    mx = rows[0]
    for e in range(1, N_EXPERTS):
        mx = jnp.maximum(mx, rows[e])
    ex = [jnp.exp(r - mx) for r in rows]
    den = ex[0]
    for e in range(1, N_EXPERTS):
        den = den + ex[e]
    score = [x / den for x in ex]
    biased = [score[e] + rb_ref[e] for e in range(N_EXPERTS)]

    best = None
    for gi in range(N_EXPERT_GROUPS):
        t1, i1, t2, i2 = _top2_of4(biased[gi * EXPERTS_PER_GROUP:(gi + 1) * EXPERTS_PER_GROUP])
        gs = t1 + t2
        e1 = gi * EXPERTS_PER_GROUP + i1
        e2 = gi * EXPERTS_PER_GROUP + i2
        if best is None:
            best = (gs, e1, e2)
        else:
            take = gs > best[0]
            best = (jnp.where(take, gs, best[0]), jnp.where(take, e1, best[1]), jnp.where(take, e2, best[2]))
    _, e1, e2 = best
    zero = jnp.zeros_like(score[0])
    w1 = zero
    w2 = zero
    for e in range(N_EXPERTS):
        w1 = w1 + jnp.where(e1 == e, score[e], 0.0)
        w2 = w2 + jnp.where(e2 == e, score[e], 0.0)
    wsum = w1 + w2
    ei_ref[0:1, rows_sl] = e1
    ei_ref[1:2, rows_sl] = e2
    ewt_ref[rows_sl, :] = jnp.concatenate(
        [w1 / wsum, w2 / wsum, jnp.zeros((LANES - 2, tm), F32)], axis=0).T

    onehot = jnp.concatenate(
        [jnp.where((e1 == e) | (e2 == e), 1.0, 0.0) for e in range(N_EXPERTS)], axis=0)
    prefix = jnp.dot(onehot.astype(BF16), tri, preferred_element_type=F32) + carry
    r1 = zero
    r2 = zero
    for e in range(N_EXPERTS):
        r1 = r1 + jnp.where(e1 == e, prefix[e:e + 1, :], 0.0)
        r2 = r2 + jnp.where(e2 == e, prefix[e:e + 1, :], 0.0)
    rk_ref[0:1, rows_sl] = r1.astype(I32)
    rk_ref[1:2, rows_sl] = r2.astype(I32)
    return carry + jnp.sum(onehot, axis=1, keepdims=True)


def _out_proj(ycp, yret, x, w_out, layer, ada5, ln_g, ln_b, w_router_pad, router_bias):
    tm = TM_OP
    ada_spec = lambda chunk: pl.BlockSpec(
        (None, None, 1, D_MODEL), lambda i: (_ada_row(i, tm), chunk, 0, 0))
    vec_spec = pl.BlockSpec((1, D_MODEL), lambda i: (0, 0))
    route_spec = pl.BlockSpec((2, tm), lambda i: (0, i))
    return pl.pallas_call(
        functools.partial(_outproj_kernel, layer=layer),
        grid=(M_TOK // tm,),
        in_specs=[
            pl.BlockSpec((tm, D_CONV + D_POOL), lambda i: (i, 0)),
            pl.BlockSpec((tm, D_RET), lambda i: (i, 0)),
            pl.BlockSpec((tm, D_MODEL), lambda i: (i, 0)),
            pl.BlockSpec(memory_space=pl.ANY),
            ada_spec(2),
            vec_spec, vec_spec,
            ada_spec(4),
            ada_spec(3),
            pl.BlockSpec((D_MODEL, LANES), lambda i: (0, 0)),
            pl.BlockSpec(memory_space=pltpu.SMEM),
        ],
        out_specs=[
            pl.BlockSpec((tm, D_MODEL), lambda i: (i, 0)),
            pl.BlockSpec((tm * TOK_STRIDE, LANES), lambda i: (i, 0)),
            route_spec,
            pl.BlockSpec((tm, LANES), lambda i: (i, 0)),
            route_spec,
            pl.BlockSpec((N_EXPERTS, LANES), lambda i: (0, 0)),
        ],
        out_shape=[
            jax.ShapeDtypeStruct((M_TOK, D_MODEL), F32),
            jax.ShapeDtypeStruct((M_TOK * TOK_STRIDE, LANES), F32),
            jax.ShapeDtypeStruct((2, M_TOK), I32),
            jax.ShapeDtypeStruct((M_TOK, LANES), F32),
            jax.ShapeDtypeStruct((2, M_TOK), I32),
            jax.ShapeDtypeStruct((N_EXPERTS, LANES), F32),
        ],
        scratch_shapes=[
            pltpu.VMEM((N_EXPERTS, LANES), F32),
            pltpu.VMEM((D_MODEL, D_MODEL), BF16),
            pltpu.VMEM((2, 256, D_MODEL), F32),
            pltpu.SemaphoreType.DMA((2,)),
            pltpu.VMEM((D_MODEL, LANES), BF16),
            pltpu.VMEM((D_MODEL, LANES), BF16),
        ],
        compiler_params=_cparams(("arbitrary",)),
        name="out_proj_router",
    )(ycp, yret, x, w_out, ada5, ln_g.reshape(1, D_MODEL), ln_b.reshape(1, D_MODEL),
      ada5, ada5, w_router_pad, router_bias)


def _pos_kernel(ei_ref, rk_ref, cnt_ref, pos_ref):
    ei = ei_ref[...]
    pos = rk_ref[...]
    start = jnp.zeros((1, 1), F32)
    for e in range(N_EXPERTS):
        pos = pos + jnp.where(ei == e, start.astype(I32), 0)
        n_tiles = jnp.floor((cnt_ref[e:e + 1, 0:1] + (TM_EXP - 1.0)) * (1.0 / TM_EXP))
        start = start + n_tiles * TM_EXP
    pos_ref[...] = pos


def _pair_rows(ei, rk, cnt):
    return pl.pallas_call(
        _pos_kernel,
        out_shape=jax.ShapeDtypeStruct((2, M_TOK), I32),
        name="pair_rows",
    )(ei, rk, cnt)


def _route_kernel(cnt_ref, pos_ref, src_ref, te_ref, nv_ref, seg_ref, nxt_ref):
    def zero(r, carry):
        src_ref[r] = 0
        return carry

    tile = jnp.int32(0)
    seg = jnp.int32(0)
    prev_tile = jnp.int32(0)
    prev_tiles = jnp.int32(0)
    for e in range(N_EXPERTS):
        n_tiles = lax.shift_right_logical(cnt_ref[e] + (TM_EXP - 1), TM_EXP.bit_length() - 1)

        def mark(j, carry, e=e, tile=tile, seg=seg):
            te_ref[tile + j] = e
            seg_ref[tile + j] = seg
            nxt_ref[tile + j] = -1
            return carry
        lax.fori_loop(0, n_tiles, mark, 0)

        def link(j, carry, e=e, prev_tile=prev_tile):
            nxt_ref[prev_tile + j] = e
            return carry
        lax.fori_loop(0, jnp.where(n_tiles > 0, prev_tiles, 0), link, 0)
        lax.fori_loop(tile * TM_EXP + cnt_ref[e], (tile + n_tiles) * TM_EXP, zero, 0)
        used = n_tiles > 0
        prev_tile = jnp.where(used, tile, prev_tile)
        prev_tiles = jnp.where(used, n_tiles, prev_tiles)
        seg = seg + used.astype(I32)
        tile = tile + n_tiles
    nv_ref[0] = tile
    last_expert = te_ref[tile - 1]

    def mark_unused(j, carry):
        te_ref[j] = last_expert
        seg_ref[j] = 0
        nxt_ref[j] = -1
        return carry
    lax.fori_loop(tile, NT_EXP, mark_unused, 0)
    lax.fori_loop(tile * TM_EXP, NP_EXP, zero, 0)

    def place(t, carry):
        src_ref[pos_ref[t]] = t
        src_ref[pos_ref[M_TOK + t]] = t
        return carry
    lax.fori_loop(0, M_TOK, place, 0, unroll=16)


def _route_tables(cnt, pos):
    smem = pl.BlockSpec(memory_space=pltpu.SMEM)
    return pl.pallas_call(
        _route_kernel,
        in_specs=[smem, smem],
        out_specs=[smem, smem, smem, smem, smem],
        out_shape=[
            jax.ShapeDtypeStruct((NP_EXP,), I32),
            jax.ShapeDtypeStruct((NT_EXP,), I32),
            jax.ShapeDtypeStruct((1,), I32),
            jax.ShapeDtypeStruct((NT_EXP,), I32),
            jax.ShapeDtypeStruct((NT_EXP,), I32),
        ],
        name="route_tables",
    )(cnt, pos)


def _row_gather_start(src_hbm, buf, sem, idx_ref, base, n_tok, tok0=0):
    for r in range(n_tok):
        pltpu.make_async_copy(src_hbm.at[pl.ds(idx_ref[base + r] * TOK_STRIDE, TOK_ROWS), :],
                              buf.at[pl.ds((tok0 + r) * TOK_STRIDE, TOK_ROWS), :], sem).start()


def _row_gather_wait(src_hbm, buf, sem):
    n_rows = buf.shape[0] // TOK_STRIDE * TOK_ROWS
    pltpu.make_async_copy(src_hbm.at[pl.ds(0, n_rows), :], buf.at[pl.ds(0, n_rows), :], sem).wait()


def _experts_kernel(te_ref, nv_ref, src_ref, seg_ref, nxt_ref, h2_hbm, wg_hbm, wu_hbm, wd_hbm, o_ref,
                    xbuf0, xbuf1, gsem, wg_st, wu_st, wd_st, wsem, wg_bf, wu_bf, wd_bf, *, layer):
    i = pl.program_id(0)
    n_valid = nv_ref[0]
    bufs = (xbuf0, xbuf1)

    def weight_copies(expert, slot):
        return [pltpu.make_async_copy(w_hbm.at[layer, expert], w_st.at[slot], wsem.at[slot])
                for w_hbm, w_st in ((wg_hbm, wg_st), (wu_hbm, wu_st), (wd_hbm, wd_st))]

    @pl.when(i == 0)
    def _():
        _row_gather_start(h2_hbm, xbuf0, gsem.at[0], src_ref, 0, TM_EXP)
        for c in weight_copies(te_ref[0], 0):
            c.start()

    @pl.when((i < n_valid) & ((i == 0) | (te_ref[i] != te_ref[jnp.maximum(i - 1, 0)])))
    def _():
        slot = seg_ref[i] % 2
        for c in weight_copies(te_ref[i], slot):
            c.wait()
        wg_bf[...] = wg_st[slot].astype(BF16)
        wu_bf[...] = wu_st[slot].astype(BF16)
        wd_bf[...] = wd_st[slot].astype(BF16)
        nxt_expert = nxt_ref[i]

        @pl.when(nxt_expert >= 0)
        def _():
            for c in weight_copies(nxt_expert, 1 - slot):
                c.start()

    def step(cur):
        nxt_buf, nxt_sem = bufs[1 - cur], gsem.at[1 - cur]
        _row_gather_wait(h2_hbm, bufs[cur], gsem.at[cur])
        nxt = jnp.minimum(i + 1, NT_EXP - 1)
        _row_gather_start(h2_hbm, nxt_buf, nxt_sem, src_ref, nxt * TM_EXP, TM_EXP)
        x = _load_token_tiles(bufs[cur], 0, TM_EXP).astype(BF16)
        g = jnp.dot(x, wg_bf[...], preferred_element_type=F32)
        u = jnp.dot(x, wu_bf[...], preferred_element_type=F32)
        a = (_silu(g) * u).astype(BF16)
        _store_token_tiles(o_ref, jnp.dot(a, wd_bf[...], preferred_element_type=F32))

        @pl.when(i == n_valid - 1)
        def _():
            _row_gather_wait(h2_hbm, nxt_buf, nxt_sem)

    for cur in range(2):
        pl.when((i < n_valid) & (i % 2 == cur))(functools.partial(step, cur))

    @pl.when(i >= n_valid)
    def _():
        o_ref[...] = jnp.zeros_like(o_ref)


def _experts(h2, tile_expert, n_valid, src_tok, tile_seg, next_expert, wg, wu, wd, layer):
    hbm = pl.BlockSpec(memory_space=pl.ANY)
    grid_spec = pltpu.PrefetchScalarGridSpec(
        num_scalar_prefetch=5,
        grid=(NT_EXP,),
        in_specs=[hbm, hbm, hbm, hbm],
        out_specs=pl.BlockSpec((TM_EXP * TOK_STRIDE, LANES), lambda i, *_: (i, 0)),
        scratch_shapes=[
            pltpu.VMEM((TM_EXP * TOK_STRIDE, LANES), F32),
            pltpu.VMEM((TM_EXP * TOK_STRIDE, LANES), F32),
            pltpu.SemaphoreType.DMA((2,)),
            pltpu.VMEM((2, D_MODEL, D_EXPERT), F32),
            pltpu.VMEM((2, D_MODEL, D_EXPERT), F32),
            pltpu.VMEM((2, D_EXPERT, D_MODEL), F32),
            pltpu.SemaphoreType.DMA((2,)),
            pltpu.VMEM((D_MODEL, D_EXPERT), BF16),
            pltpu.VMEM((D_MODEL, D_EXPERT), BF16),
            pltpu.VMEM((D_EXPERT, D_MODEL), BF16),
        ],
    )
    return pl.pallas_call(
        functools.partial(_experts_kernel, layer=layer),
        grid_spec=grid_spec,
        out_shape=jax.ShapeDtypeStruct((NP_EXP * TOK_STRIDE, LANES), F32),
        compiler_params=_cparams(("arbitrary",)),
        name="experts",
    )(tile_expert, n_valid, src_tok, tile_seg, next_expert, h2, wg, wu, wd)


def _final_kernel(*refs, emit_h):
    if emit_h:
        (pos_ref, x1_ref, ys_hbm, ewt_ref, gate2_ref, lng_ref, lnb_ref, sc1_ref, sh1_ref,
         x_ref, h_ref, rbuf0, rbuf1, sem) = refs
    else:
        (pos_ref, x1_ref, ys_hbm, ewt_ref, gate2_ref, lng_ref, lnb_ref,
         xc_ref, xl_ref, rbuf0, rbuf1, sem) = refs
    tm = TM_OUT
    i = pl.program_id(0)
    n_blocks = M_TOK // tm
    bufs = (rbuf0, rbuf1)

    def start(tile, buf, buf_sem):
        _row_gather_start(ys_hbm, buf, buf_sem, pos_ref, tile * tm, tm, tok0=0)
        _row_gather_start(ys_hbm, buf, buf_sem, pos_ref, M_TOK + tile * tm, tm, tok0=tm)

    @pl.when(i == 0)
    def _():
        start(0, rbuf0, sem.at[0])

    def step(cur):
        buf = bufs[cur]
        nxt_buf, nxt_sem = bufs[1 - cur], sem.at[1 - cur]
        _row_gather_wait(ys_hbm, buf, sem.at[cur])
        start(jnp.minimum(i + 1, n_blocks - 1), nxt_buf, nxt_sem)
        w = ewt_ref[...]
        y2 = w[:, 0:1] * _load_token_tiles(buf, 0, tm) + w[:, 1:2] * _load_token_tiles(buf, tm, tm)
        x2 = _layer_norm_rows(DEEPNORM_ALPHA * x1_ref[...] + gate2_ref[...] * y2,
                              lng_ref[...], lnb_ref[...])
        if emit_h:
            x_ref[...] = x2
            h_ref[...] = (x2 * (1.0 + sc1_ref[...]) + sh1_ref[...]).astype(BF16)
        else:
            @pl.when(i < M_CTX // tm)
            def _():
                xc_ref[...] = x2

            @pl.when(i >= M_CTX // tm)
            def _():
                xl_ref[...] = x2

        @pl.when(i == n_blocks - 1)
        def _():
            _row_gather_wait(ys_hbm, nxt_buf, nxt_sem)

    for cur in range(2):
        pl.when(i % 2 == cur)(functools.partial(step, cur))


def _final(x1, ys, pos, ewt, ada5, ln_g, ln_b, ada5_next):
    tm = TM_OUT
    emit_h = ada5_next is not None
    vec_spec = pl.BlockSpec((1, D_MODEL), lambda i, pos: (0, 0))
    ada_spec = lambda chunk: pl.BlockSpec(
        (None, None, 1, D_MODEL), lambda i, pos: (_ada_row(i, tm), chunk, 0, 0))
    in_specs = [
        pl.BlockSpec((tm, D_MODEL), lambda i, pos: (i, 0)),
        pl.BlockSpec(memory_space=pl.ANY),
        pl.BlockSpec((tm, LANES), lambda i, pos: (i, 0)),
        ada_spec(5),
        vec_spec, vec_spec,
    ]
    args = [pos, x1, ys, ewt, ada5, ln_g.reshape(1, D_MODEL), ln_b.reshape(1, D_MODEL)]
    if emit_h:
        in_specs += [ada_spec(1), ada_spec(0)]
        args += [ada5_next, ada5_next]
        row_spec = pl.BlockSpec((tm, D_MODEL), lambda i, pos: (i, 0))
        out_specs = [row_spec, row_spec]
        out_shape = [jax.ShapeDtypeStruct((M_TOK, D_MODEL), F32),
                     jax.ShapeDtypeStruct((M_TOK, D_MODEL), BF16)]
    else:
        out_specs = [
            pl.BlockSpec((tm, D_MODEL), lambda i, pos: (_ctx_block(i, tm), 0)),
            pl.BlockSpec((tm, D_MODEL), lambda i, pos: (_lat_block(i, tm), 0)),
        ]
        out_shape = [jax.ShapeDtypeStruct((M_CTX, D_MODEL), F32),
                     jax.ShapeDtypeStruct((M_LAT, D_MODEL), F32)]
    grid_spec = pltpu.PrefetchScalarGridSpec(
        num_scalar_prefetch=1,
        grid=(M_TOK // tm,),
        in_specs=in_specs,
        out_specs=out_specs,
        scratch_shapes=[
            pltpu.VMEM((2 * tm * TOK_STRIDE, LANES), F32),
            pltpu.VMEM((2 * tm * TOK_STRIDE, LANES), F32),
            pltpu.SemaphoreType.DMA((2,)),
        ],
    )
    return pl.pallas_call(
        functools.partial(_final_kernel, emit_h=emit_h),
        grid_spec=grid_spec,
        out_shape=out_shape,
        compiler_params=_cparams(("arbitrary",)),
        name="final_ln",
    )(*args)


def _rope_tables():
    rows = T_LAT // GRID_W
    row = jnp.repeat(jnp.arange(rows), GRID_W).astype(F32)
    col = jnp.tile(jnp.arange(GRID_W), rows).astype(F32)
    n_freq = HEAD_DIM // 4
    inv_freq = ROPE_BASE ** (-jnp.arange(n_freq, dtype=F32) / n_freq)
    ang = jnp.concatenate([row[:, None] * inv_freq[None], col[:, None] * inv_freq[None]], axis=-1)
    cos, sin = jnp.cos(ang), jnp.sin(ang)
    return jnp.concatenate([cos, cos], axis=-1), jnp.concatenate([-sin, sin], axis=-1)


def kernel(x_prompt, x_sample, state_retention, c, c_ctx, w_ada, b_ada, w_in, w_out, conv_w, pool_w,
           pool_scale, ret_decay_logit, ret_gn_gain, ln1_g, ln1_b, ln2_g, ln2_b, w_router, router_bias,
           w_gate, w_up, w_down):
    x_ctx = x_prompt.reshape(M_CTX, D_MODEL)
    x_lat = x_sample.reshape(M_LAT, D_MODEL)
    c_all = jnp.concatenate(
        [c_ctx[None, :], c, jnp.zeros((ADA_ROWS - 1 - N_LAT_SEQ, D_MODEL), F32)], axis=0)
    ada = _ada_table(c_all, w_ada, b_ada).reshape(DEPTH, ADA_ROWS, 6, 1, D_MODEL)

    cos_lat, sin_lat = _rope_tables()
    w_router_pad = jnp.pad(w_router, ((0, 0), (0, LANES - N_EXPERTS)))

    states = []
    x, h = _modulate(x_ctx, x_lat, ada[0])
    for l in range(DEPTH):
        ada5 = ada[l]
        z = _in_proj(h, w_in, l)

        ycp = _conv_pool(z, conv_w[l], pool_w[l], pool_scale[l])

        dl = jnp.broadcast_to(ret_decay_logit[l][:, :, None, None], (2, N_RET_HEADS, 1, HEAD_DIM))
        yret, s_fin = _retention(z, cos_lat, sin_lat, dl, ret_gn_gain[l], state_retention, l)
        states.append(s_fin)

        x1, h2, ei, ewt, rk, cnt = _out_proj(ycp, yret, x, w_out, l, ada5,
                                             ln1_g[l], ln1_b[l], w_router_pad, router_bias)
        pos = _pair_rows(ei, rk, cnt).reshape(-1)
        src_tok, te, n_valid, tile_seg, next_expert = _route_tables(cnt[:, 0].astype(I32), pos)
        ys = _experts(h2, te, n_valid, src_tok, tile_seg, next_expert, w_gate, w_up, w_down, l)
        if l + 1 < DEPTH:
            x, h = _final(x1, ys, pos, ewt, ada5, ln2_g[l], ln2_b[l], ada[l + 1])
        else:
            x_ctx, x_lat = _final(x1, ys, pos, ewt, ada5, ln2_g[l], ln2_b[l], None)

    y_prompt = x_ctx.reshape(N_CTX_SEQ, T_CTX, D_MODEL)
    y_sample = x_lat.reshape(N_LAT_SEQ, T_LAT, D_MODEL)
    return y_prompt, y_sample, jnp.stack(states, axis=1)
```

```python
import functools

import jax
import jax.numpy as jnp
from jax import lax
from jax.experimental import pallas as pl
from jax.experimental.pallas import tpu as pltpu

F32 = jnp.float32
BF16 = jnp.bfloat16
I32 = jnp.int32

D_MODEL = 2048
N_CTX_SEQ, T_CTX = 16, 256
N_LAT_SEQ, T_LAT = 8, 1024
DEPTH = 2
M_CTX = N_CTX_SEQ * T_CTX
M_LAT = N_LAT_SEQ * T_LAT
M_TOK = M_CTX + M_LAT

GRID_W = 64
D_CONV = D_MODEL // 4
D_POOL = D_MODEL // 4
D_RET = D_MODEL // 2
N_RET_HEADS = 8
HEAD_DIM = D_RET // N_RET_HEADS
POOL_WINDOWS = (2, 4, 8, 16)
POOL_GROUP_DIM = D_POOL // len(POOL_WINDOWS)
CHUNK = 128
ROPE_BASE = 10000.0
N_EXPERTS = 16
EXPERTS_PER_GROUP = 4
N_EXPERT_GROUPS = N_EXPERTS // EXPERTS_PER_GROUP
D_EXPERT = D_MODEL // 4
D_IN_PROJ = 3 * D_CONV + D_POOL + 4 * D_RET
DEEPNORM_ALPHA = (2.0 * DEPTH) ** 0.25
LN_EPS = 1e-5
ADA_ROWS = 16

LANES = 128
VMEM_LIMIT = 56 * 1024 * 1024

TM_IN = 1024
TN_IN = 1024
TOK_ROWS = D_MODEL // LANES
TOK_STRIDE = TOK_ROWS + 1
TM_OP = 512
SUB_OP = 256
TM_OUT = 512
TM_EXP = 256
N_PAIR = 2 * M_TOK
NP_EXP = N_PAIR + N_EXPERTS * TM_EXP
NT_EXP = NP_EXP // TM_EXP


def _cparams(sem):
    return pltpu.CompilerParams(dimension_semantics=sem, vmem_limit_bytes=VMEM_LIMIT)


def _silu(x):
    return x * jax.nn.sigmoid(x)


def _ada_row(i, tm):
    n_ctx_tiles = M_CTX // tm
    per_batch = T_LAT // tm
    return jnp.where(i < n_ctx_tiles, 0, 1 + (i - n_ctx_tiles) // per_batch)


def _ada_kernel(c_ref, w_ref, b_ref, o_ref):
    s = _silu(c_ref[...]).astype(BF16)
    o_ref[...] = jnp.dot(s, w_ref[...].astype(BF16), preferred_element_type=F32) + b_ref[...]


def _ada_table(c_all, w_ada, b_ada):
    tn = 1024
    n6 = 6 * D_MODEL
    return pl.pallas_call(
        _ada_kernel,
        grid=(DEPTH, n6 // tn),
        in_specs=[
            pl.BlockSpec((ADA_ROWS, D_MODEL), lambda l, j: (0, 0)),
            pl.BlockSpec((None, D_MODEL, tn), lambda l, j: (l, 0, j)),
            pl.BlockSpec((None, 1, tn), lambda l, j: (l, 0, j)),
        ],
        out_specs=pl.BlockSpec((None, ADA_ROWS, tn), lambda l, j: (l, 0, j)),
        out_shape=jax.ShapeDtypeStruct((DEPTH, ADA_ROWS, n6), F32),
        compiler_params=_cparams(("arbitrary", "arbitrary")),
        name="ada_table",
    )(c_all, w_ada, b_ada.reshape(DEPTH, 1, n6))


def _ctx_block(i, tm):
    return jnp.minimum(i, M_CTX // tm - 1)


def _lat_block(i, tm):
    return jnp.maximum(i - M_CTX // tm, 0)


def _modulate_kernel(xc_ref, xl_ref, sc_ref, sh_ref, x_ref, h_ref):
    x = jnp.where(pl.program_id(0) < M_CTX // xc_ref.shape[0], xc_ref[...], xl_ref[...])
    x_ref[...] = x
    h_ref[...] = (x * (1.0 + sc_ref[...]) + sh_ref[...]).astype(BF16)


def _modulate(x_ctx, x_lat, ada5):
    tm = TM_OUT
    ada_spec = lambda chunk: pl.BlockSpec(
        (None, None, 1, D_MODEL), lambda i: (_ada_row(i, tm), chunk, 0, 0))
    return pl.pallas_call(
        _modulate_kernel,
        grid=(M_TOK // tm,),
        in_specs=[
            pl.BlockSpec((tm, D_MODEL), lambda i: (_ctx_block(i, tm), 0)),
            pl.BlockSpec((tm, D_MODEL), lambda i: (_lat_block(i, tm), 0)),
            ada_spec(1), ada_spec(0),
        ],
        out_specs=[pl.BlockSpec((tm, D_MODEL), lambda i: (i, 0)),
                   pl.BlockSpec((tm, D_MODEL), lambda i: (i, 0))],
        out_shape=[jax.ShapeDtypeStruct((M_TOK, D_MODEL), F32),
                   jax.ShapeDtypeStruct((M_TOK, D_MODEL), BF16)],
        compiler_params=_cparams(("arbitrary",)),
        name="modulate",
    )(x_ctx, x_lat, ada5, ada5)


def _inproj_kernel(h_ref, w_ref, o_ref, wbf_scr):
    @pl.when(pl.program_id(1) == 0)
    def _():
        wbf_scr[...] = w_ref[...].astype(BF16)

    o_ref[...] = jnp.dot(h_ref[...], wbf_scr[...], preferred_element_type=F32).astype(BF16)


def _in_proj(h, w_in, layer):
    return pl.pallas_call(
        _inproj_kernel,
        grid=(D_IN_PROJ // TN_IN, M_TOK // TM_IN),
        in_specs=[
            pl.BlockSpec((TM_IN, D_MODEL), lambda j, i: (i, 0)),
            pl.BlockSpec((None, D_MODEL, TN_IN), lambda j, i: (layer, 0, j)),
        ],
        out_specs=pl.BlockSpec((TM_IN, TN_IN), lambda j, i: (i, j)),
        out_shape=jax.ShapeDtypeStruct((M_TOK, D_IN_PROJ), BF16),
        scratch_shapes=[pltpu.VMEM((D_MODEL, TN_IN), BF16)],
        compiler_params=_cparams(("arbitrary", "arbitrary")),
        name="in_proj",
    )(h, w_in)


MIX_ROWS = T_LAT
N_CTX_BLOCKS = M_CTX // MIX_ROWS
SEQ_PER_CTX_BLOCK = MIX_ROWS // T_CTX
CP_ROWS = T_CTX
CP_HALO = 128


def _convpool_kernel(z_ref, cw_ref, pw_ref, ps_ref, o_ref, band_ref):
    blk = pl.program_id(0)
    g_dim = POOL_GROUP_DIM

    @pl.when(blk == 0)
    def _():
        row = lax.broadcasted_iota(I32, (CP_ROWS, CP_ROWS + 2 * CP_HALO), 0)
        col = lax.broadcasted_iota(I32, (CP_ROWS, CP_ROWS + 2 * CP_HALO), 1)
        d = col - CP_HALO - row
        for gi, w in enumerate(POOL_WINDOWS):
            band_ref[gi] = jnp.where((d >= -(w // 2)) & (d < w // 2), 1.0, 0.0).astype(BF16)

    def block(seq_len):
        t = lax.broadcasted_iota(I32, (CP_ROWS, LANES), 0)
        for ch in range(MIX_ROWS // CP_ROWS):
            r0 = ch * CP_ROWS
            rows = slice(r0, r0 + CP_ROWS)
            pos0 = r0 % seq_len
            at_start = pos0 == 0
            at_end = pos0 + CP_ROWS == seq_len

            for cg in range(D_CONV // LANES):
                lanes = slice(cg * LANES, (cg + 1) * LANES)

                def u_rows(rs):
                    return (z_ref[rs, D_CONV + cg * LANES:D_CONV + (cg + 1) * LANES].astype(F32)
                            * z_ref[rs, 2 * D_CONV + cg * LANES:2 * D_CONV + (cg + 1) * LANES].astype(F32))
                u = u_rows(rows)
                before = 0.0 if at_start else u_rows(slice(r0 - 1, r0))
                after = 0.0 if at_end else u_rows(slice(r0 + CP_ROWS, r0 + CP_ROWS + 1))
                u_prev = jnp.where(t == 0, before, pltpu.roll(u, 1, 0))
                u_next = jnp.where(t == CP_ROWS - 1, after, pltpu.roll(u, CP_ROWS - 1, 0))
                conv = u_prev * cw_ref[0:1, lanes] + u * cw_ref[1:2, lanes] + u_next * cw_ref[2:3, lanes]
                o_ref[rows, lanes] = (z_ref[rows, lanes].astype(F32) * conv).astype(BF16)

            k_rows = slice(r0 if at_start else r0 - CP_HALO,
                           r0 + CP_ROWS if at_end else r0 + CP_ROWS + CP_HALO)
            b_cols = slice(CP_HALO if at_start else 0,
                           CP_HALO + CP_ROWS if at_end else CP_ROWS + 2 * CP_HALO)
            tpos = pos0 + t
            for gi, w in enumerate(POOL_WINDOWS):
                lo = 3 * D_CONV + gi * g_dim
                win = jnp.dot(band_ref[gi, :, b_cols], z_ref[k_rows, lo:lo + g_dim],
                              preferred_element_type=F32)
                cnt = (jnp.minimum(tpos + w // 2, seq_len) - jnp.maximum(tpos - w // 2, 0)).astype(F32)
                pooled = win / cnt - z_ref[rows, lo:lo + g_dim].astype(F32)
                y = jnp.dot(pooled.astype(BF16), pw_ref[gi].astype(BF16), preferred_element_type=F32)
                y = y * ps_ref[:, gi * g_dim:(gi + 1) * g_dim]
                o_ref[rows, D_CONV + gi * g_dim:D_CONV + (gi + 1) * g_dim] = y.astype(BF16)

    pl.when(blk < N_CTX_BLOCKS)(functools.partial(block, T_CTX))
    pl.when(blk >= N_CTX_BLOCKS)(functools.partial(block, T_LAT))


def _conv_pool(z, conv_w, pool_w, pool_scale):
    return pl.pallas_call(
        _convpool_kernel,
        grid=(M_TOK // MIX_ROWS,),
        in_specs=[
            pl.BlockSpec((MIX_ROWS, D_IN_PROJ - 4 * D_RET), lambda b: (b, 0)),
            pl.BlockSpec((3, D_CONV), lambda b: (0, 0)),
            pl.BlockSpec((len(POOL_WINDOWS), POOL_GROUP_DIM, POOL_GROUP_DIM), lambda b: (0, 0, 0)),
            pl.BlockSpec((1, D_POOL), lambda b: (0, 0)),
        ],
        out_specs=pl.BlockSpec((MIX_ROWS, D_CONV + D_POOL), lambda b: (b, 0)),
        out_shape=jax.ShapeDtypeStruct((M_TOK, D_CONV + D_POOL), BF16),
        scratch_shapes=[pltpu.VMEM((len(POOL_WINDOWS), CP_ROWS, CP_ROWS + 2 * CP_HALO), BF16)],
        compiler_params=_cparams(("arbitrary",)),
        name="conv_pool",
    )(z, conv_w, pool_w, pool_scale.reshape(1, D_POOL))


def _log_sigmoid(x):
    return jnp.minimum(x, 0.0) - jnp.log1p(jnp.exp(-jnp.abs(x)))


RET_ROWS = MIX_ROWS


def _retention_kernel(*refs, layer):
    if layer == 0:
        (q_ref, k_ref, v_ref, g_ref, cos_ref, sin_ref, dl_ref, gain_ref, s0_ref,
         y_ref, sfin_ref, tab_scr, kr_scr, u_scr, s_scr) = refs
    else:
        (q_ref, k_ref, v_ref, g_ref, cos_ref, sin_ref, dl_ref, gain_ref, s0_ref, _,
         y_ref, sfin_ref, tab_scr, kr_scr, u_scr, s_scr) = refs
    blk = pl.program_id(1)
    n_chunks = RET_ROWS // CHUNK
    half = HEAD_DIM // 2

    lg_f = _log_sigmoid(dl_ref[0])
    lg_b = _log_sigmoid(dl_ref[1])
    g_f = jnp.exp(lg_f * CHUNK)
    g_b = jnp.exp(lg_b * CHUNK)

    @pl.when(blk == 0)
    def _():
        row = lax.broadcasted_iota(I32, (CHUNK, CHUNK), 0).astype(F32)
        col = lax.broadcasted_iota(I32, (CHUNK, CHUNK), 1).astype(F32)
        diff = row - col
        tab_scr[0] = (jnp.where(diff >= 0, jnp.exp(lg_f * jnp.maximum(diff, 0.0)), 0.0)
                      + jnp.where(diff <= 0, jnp.exp(lg_b * jnp.maximum(-diff, 0.0)), 0.0))
        tab_scr[1] = jnp.exp(lg_f * (row + 1.0))
        tab_scr[2] = jnp.exp(lg_b * (CHUNK - row))
        tab_scr[3] = jnp.exp(lg_f * (CHUNK - 1.0 - row))
        tab_scr[4] = jnp.exp(lg_b * row)

    tn_dims = (((0,), (0,)), ((), ()))
    nt_dims = (((1,), (1,)), ((), ()))

    def block(seq_chunks, latent):
        def rope(x, sl):
            if not latent:
                return x
            return x * cos_ref[sl, :] + pltpu.roll(x, half, 1) * sin_ref[sl, :]

        for c in range(n_chunks):
            sl = slice(c * CHUNK, (c + 1) * CHUNK)
            kr = rope(k_ref[sl, :].astype(F32), sl) * (HEAD_DIM ** -0.5)
            kr_scr[sl, :] = kr.astype(BF16)
            kz = jnp.concatenate([kr * tab_scr[3], kr * tab_scr[4]], axis=1).astype(BF16)
            u_scr[c] = lax.dot_general(kz, v_ref[sl, :], tn_dims, preferred_element_type=F32)

        has_state = [[False] * n_chunks, [False] * n_chunks]
        for s_i in range(n_chunks // seq_chunks):
            chunks = list(range(s_i * seq_chunks, (s_i + 1) * seq_chunks))
            for d, order, g_d in ((0, chunks, g_f), (1, chunks[::-1], g_b)):
                s = s0_ref[d] if latent else None
                for c in order:
                    u = u_scr[c, d * HEAD_DIM:(d + 1) * HEAD_DIM, :]
                    if s is None:
                        s = u
                    else:
                        s_scr[d, c] = s.astype(BF16)
                        has_state[d][c] = True
                        s = g_d * s + u
                if not latent:
                    if layer == 0:
                        sfin_ref[s_i, 0, d] = s
                        for later in range(1, DEPTH):
                            sfin_ref[s_i, later, d] = jnp.zeros_like(s)
                    else:
                        sfin_ref[s_i, d] = s

        for c in range(n_chunks):
            sl = slice(c * CHUNK, (c + 1) * CHUNK)
            qr = rope(q_ref[sl, :].astype(F32), sl)
            scores = lax.dot_general(qr.astype(BF16), kr_scr[sl, :], nt_dims, preferred_element_type=F32)
            o = jnp.dot((scores * tab_scr[0]).astype(BF16), v_ref[sl, :], preferred_element_type=F32)
            for d in range(2):
                if has_state[d][c]:
                    o += jnp.dot((qr * tab_scr[1 + d]).astype(BF16), s_scr[d, c],
                                 preferred_element_type=F32)
            mu = jnp.mean(o, axis=-1, keepdims=True)
            dev = o - mu
            var = jnp.mean(dev * dev, axis=-1, keepdims=True)
            on = dev * lax.rsqrt(var + LN_EPS) * gain_ref[...]
            y_ref[sl, :] = (_silu(g_ref[sl, :].astype(F32)) * on).astype(BF16)

    pl.when(blk < N_CTX_BLOCKS)(functools.partial(block, T_CTX // CHUNK, False))
    pl.when(blk >= N_CTX_BLOCKS)(functools.partial(block, T_LAT // CHUNK, True))


def _retention(z, cos_t, sin_t, decay_logit_b, gain, s0, layer, states):
    qcol0 = (3 * D_CONV + D_POOL) // HEAD_DIM
    n_blocks = M_TOK // RET_ROWS

    def zspec(k):
        return pl.BlockSpec((RET_ROWS, HEAD_DIM), lambda h, b: (b, qcol0 + k * N_RET_HEADS + h))

    def lat_seq(b):
        return jnp.maximum(b - N_CTX_BLOCKS, 0)

    def ctx_block(b):
        return jnp.minimum(b, N_CTX_BLOCKS - 1)

    n_chunks = RET_ROWS // CHUNK
    in_specs = [
        zspec(0), zspec(1), zspec(2), zspec(3),
        pl.BlockSpec((RET_ROWS, HEAD_DIM), lambda h, b: (0, 0)),
        pl.BlockSpec((RET_ROWS, HEAD_DIM), lambda h, b: (0, 0)),
        pl.BlockSpec((2, None, 1, HEAD_DIM), lambda h, b: (0, h, 0, 0)),
        pl.BlockSpec((1, HEAD_DIM), lambda h, b: (0, h)),
        pl.BlockSpec((None, None, 2, None, HEAD_DIM, HEAD_DIM),
                     lambda h, b: (lat_seq(b), layer, 0, h, 0, 0)),
    ]
    args = [z, z, z, z, cos_t, sin_t, decay_logit_b, gain.reshape(1, D_RET), s0]
    if layer == 0:
        st_spec = pl.BlockSpec((SEQ_PER_CTX_BLOCK, DEPTH, 2, None, HEAD_DIM, HEAD_DIM),
                               lambda h, b: (ctx_block(b), 0, 0, h, 0, 0))
        aliases = {}
    else:
        st_spec = pl.BlockSpec((SEQ_PER_CTX_BLOCK, None, 2, None, HEAD_DIM, HEAD_DIM),
                               lambda h, b: (ctx_block(b), layer, 0, h, 0, 0))
        in_specs.append(pl.BlockSpec(memory_space=pl.ANY))
        args.append(states)
        aliases = {len(args) - 1: 1}
    return pl.pallas_call(
        functools.partial(_retention_kernel, layer=layer),
        grid=(N_RET_HEADS, n_blocks),
        in_specs=in_specs,
        out_specs=[pl.BlockSpec((RET_ROWS, HEAD_DIM), lambda h, b: (b, h)), st_spec],
        out_shape=[
            jax.ShapeDtypeStruct((M_TOK, D_RET), BF16),
            jax.ShapeDtypeStruct((N_CTX_SEQ, DEPTH, 2, N_RET_HEADS, HEAD_DIM, HEAD_DIM), F32),
        ],
        input_output_aliases=aliases,
        scratch_shapes=[
            pltpu.VMEM((5, CHUNK, CHUNK), F32),
            pltpu.VMEM((RET_ROWS, HEAD_DIM), BF16),
            pltpu.VMEM((n_chunks, 2 * HEAD_DIM, HEAD_DIM), F32),
            pltpu.VMEM((2, n_chunks, HEAD_DIM, HEAD_DIM), BF16),
        ],
        compiler_params=_cparams(("arbitrary", "arbitrary")),
        name="retention",
    )(*args)


def _layer_norm_rows(r, g, b):
    mu = jnp.mean(r, axis=-1, keepdims=True)
    dev = r - mu
    var = jnp.mean(dev * dev, axis=-1, keepdims=True)
    return dev * lax.rsqrt(var + LN_EPS) * g + b


def _top2_of4(vals):
    top1 = jnp.maximum(jnp.maximum(vals[0], vals[1]), jnp.maximum(vals[2], vals[3]))
    idx1 = jnp.where(vals[0] == top1, 0, jnp.where(vals[1] == top1, 1, jnp.where(vals[2] == top1, 2, 3)))
    neg = jnp.float32(-jnp.inf)
    rest = [jnp.where(idx1 == j, neg, vals[j]) for j in range(4)]
    top2 = jnp.maximum(jnp.maximum(rest[0], rest[1]), jnp.maximum(rest[2], rest[3]))
    idx2 = jnp.where(rest[0] == top2, 0, jnp.where(rest[1] == top2, 1, jnp.where(rest[2] == top2, 2, 3)))
    return top1, idx1, top2, idx2


def _store_token_tiles(ref, val, tok0=0):
    n = val.shape[0]
    base = tok0 * TOK_STRIDE
    for c in range(TOK_ROWS):
        ref[pl.ds(base + c, n, stride=TOK_STRIDE), :] = val[:, c * LANES:(c + 1) * LANES]
    ref[pl.ds(base + TOK_ROWS, n, stride=TOK_STRIDE), :] = jnp.zeros((n, LANES), val.dtype)


def _load_token_tiles(ref, tok0, n):
    return jnp.concatenate(
        [ref[pl.ds(tok0 * TOK_STRIDE + c, n, stride=TOK_STRIDE), :] for c in range(TOK_ROWS)], axis=1)


def _outproj_kernel(ycp_ref, yret_ref, x_ref, wo_hbm, gate1_ref, lng_ref, lnb_ref, sc2_ref, sh2_ref,
                    wr_ref, rb_ref, x1_ref, h2_ref, ei_ref, ewt_ref, rk_ref, cnt_ref,
                    carry_scr, wo_ref, wstage, wsem, wr_hi, wr_lo, *, layer):
    i = pl.program_id(0)

    @pl.when(i == 0)
    def _():
        carry_scr[...] = jnp.zeros_like(carry_scr)
        wr = wr_ref[...]
        hi = wr.astype(BF16)
        wr_hi[...] = hi
        wr_lo[...] = (wr - hi.astype(F32)).astype(BF16)
        rows = wstage.shape[1]
        n_chunks = D_MODEL // rows

        def chunk_copy(c):
            return pltpu.make_async_copy(wo_hbm.at[layer, pl.ds(c * rows, rows), :], wstage.at[c % 2],
                                         wsem.at[c % 2])
        chunk_copy(0).start()
        for c in range(n_chunks):
            if c + 1 < n_chunks:
                chunk_copy(c + 1).start()
            chunk_copy(c).wait()
            wo_ref[c * rows:(c + 1) * rows, :] = wstage[c % 2].astype(BF16)

    s_i = lax.broadcasted_iota(I32, (SUB_OP, SUB_OP), 0)
    t_i = lax.broadcasted_iota(I32, (SUB_OP, SUB_OP), 1)
    tri = jnp.where(s_i < t_i, 1.0, 0.0).astype(BF16)
    half_k = D_CONV + D_POOL
    n_sub = x_ref.shape[0] // SUB_OP
    ys = []
    for sub in range(n_sub):
        rows_sl = slice(sub * SUB_OP, (sub + 1) * SUB_OP)
        y = jnp.dot(ycp_ref[rows_sl, :], wo_ref[0:half_k, :], preferred_element_type=F32)
        ys.append(y + jnp.dot(yret_ref[rows_sl, :], wo_ref[half_k:, :], preferred_element_type=F32))
    lts = [_ln_router_subtile(sub, ys[sub], x_ref, gate1_ref, lng_ref, lnb_ref, sc2_ref, sh2_ref,
                              wr_hi, wr_lo, x1_ref, h2_ref) for sub in range(n_sub)]
    carry = carry_scr[:, 0:1]
    for sub in range(n_sub):
        carry = _route_subtile(sub, lts[sub], carry, tri, rb_ref, ei_ref, ewt_ref, rk_ref)
    carry_scr[...] = jnp.broadcast_to(carry, carry_scr.shape)
    cnt_ref[...] = carry_scr[...]


def _ln_router_subtile(sub, y, x_ref, gate1_ref, lng_ref, lnb_ref, sc2_ref, sh2_ref, wr_hi, wr_lo,
                       x1_ref, h2_ref):
    tm = SUB_OP
    rows_sl = slice(sub * tm, (sub + 1) * tm)
    x1 = _layer_norm_rows(DEEPNORM_ALPHA * x_ref[rows_sl, :] + gate1_ref[...] * y,
                          lng_ref[...], lnb_ref[...])
    x1_ref[rows_sl, :] = x1
    h2 = x1 * (1.0 + sc2_ref[...]) + sh2_ref[...]
    _store_token_tiles(h2_ref, h2, tok0=sub * tm)

    h_hi = h2.astype(BF16)
    h_lo = (h2 - h_hi.astype(F32)).astype(BF16)
    logits = (jnp.dot(h_hi, wr_hi[...], preferred_element_type=F32)
              + jnp.dot(h_lo, wr_hi[...], preferred_element_type=F32)
              + jnp.dot(h_hi, wr_lo[...], preferred_element_type=F32))
    return logits.T


def _route_subtile(sub, lt, carry, tri, rb_ref, ei_ref, ewt_ref, rk_ref):
    tm = SUB_OP
    rows_sl = slice(sub * tm, (sub + 1) * tm)
    rows = [lt[e:e + 1, :] for e in range(N_EXPERTS)]

    mx = rows[0]
    for e in range(1, N_EXPERTS):
        mx = jnp.maximum(mx, rows[e])
    ex = [jnp.exp(r - mx) for r in rows]
    den = ex[0]
    for e in range(1, N_EXPERTS):
        den = den + ex[e]
    score = [x / den for x in ex]
    biased = [score[e] + rb_ref[e] for e in range(N_EXPERTS)]

    best = None
    for gi in range(N_EXPERT_GROUPS):
        t1, i1, t2, i2 = _top2_of4(biased[gi * EXPERTS_PER_GROUP:(gi + 1) * EXPERTS_PER_GROUP])
        gs = t1 + t2
        e1 = gi * EXPERTS_PER_GROUP + i1
        e2 = gi * EXPERTS_PER_GROUP + i2
        if best is None:
            best = (gs, e1, e2)
        else:
            take = gs > best[0]
            best = (jnp.where(take, gs, best[0]), jnp.where(take, e1, best[1]), jnp.where(take, e2, best[2]))
    _, e1, e2 = best
    zero = jnp.zeros_like(score[0])
    w1 = zero
    w2 = zero
    for e in range(N_EXPERTS):
        w1 = w1 + jnp.where(e1 == e, score[e], 0.0)
        w2 = w2 + jnp.where(e2 == e, score[e], 0.0)
    wsum = w1 + w2
    ei_ref[0:1, rows_sl] = e1
    ei_ref[1:2, rows_sl] = e2
    ewt_ref[rows_sl, :] = jnp.concatenate(
        [w1 / wsum, w2 / wsum, jnp.zeros((LANES - 2, tm), F32)], axis=0).T

    onehot = jnp.concatenate(
        [jnp.where((e1 == e) | (e2 == e), 1.0, 0.0) for e in range(N_EXPERTS)], axis=0)
    prefix = jnp.dot(onehot.astype(BF16), tri, preferred_element_type=F32) + carry
    r1 = zero
    r2 = zero
    for e in range(N_EXPERTS):
        r1 = r1 + jnp.where(e1 == e, prefix[e:e + 1, :], 0.0)
        r2 = r2 + jnp.where(e2 == e, prefix[e:e + 1, :], 0.0)
    rk_ref[0:1, rows_sl] = r1.astype(I32)
    rk_ref[1:2, rows_sl] = r2.astype(I32)
    return carry + jnp.sum(onehot, axis=1, keepdims=True)


def _out_proj(ycp, yret, x, w_out, layer, ada5, ln_g, ln_b, w_router_pad, router_bias):
    tm = TM_OP
    ada_spec = lambda chunk: pl.BlockSpec(
        (None, None, 1, D_MODEL), lambda i: (_ada_row(i, tm), chunk, 0, 0))
    vec_spec = pl.BlockSpec((1, D_MODEL), lambda i: (0, 0))
    route_spec = pl.BlockSpec((2, tm), lambda i: (0, i))
    return pl.pallas_call(
        functools.partial(_outproj_kernel, layer=layer),
        grid=(M_TOK // tm,),
        in_specs=[
            pl.BlockSpec((tm, D_CONV + D_POOL), lambda i: (i, 0)),
            pl.BlockSpec((tm, D_RET), lambda i: (i, 0)),
            pl.BlockSpec((tm, D_MODEL), lambda i: (i, 0)),
            pl.BlockSpec(memory_space=pl.ANY),
            ada_spec(2),
            vec_spec, vec_spec,
            ada_spec(4),
            ada_spec(3),
            pl.BlockSpec((D_MODEL, LANES), lambda i: (0, 0)),
            pl.BlockSpec(memory_space=pltpu.SMEM),
        ],
        out_specs=[
            pl.BlockSpec((tm, D_MODEL), lambda i: (i, 0)),
            pl.BlockSpec((tm * TOK_STRIDE, LANES), lambda i: (i, 0)),
            route_spec,
            pl.BlockSpec((tm, LANES), lambda i: (i, 0)),
            route_spec,
            pl.BlockSpec((N_EXPERTS, LANES), lambda i: (0, 0)),
        ],
        out_shape=[
            jax.ShapeDtypeStruct((M_TOK, D_MODEL), F32),
            jax.ShapeDtypeStruct((M_TOK * TOK_STRIDE, LANES), F32),
            jax.ShapeDtypeStruct((2, M_TOK), I32),
            jax.ShapeDtypeStruct((M_TOK, LANES), F32),
            jax.ShapeDtypeStruct((2, M_TOK), I32),
            jax.ShapeDtypeStruct((N_EXPERTS, LANES), F32),
        ],
        scratch_shapes=[
            pltpu.VMEM((N_EXPERTS, LANES), F32),
            pltpu.VMEM((D_MODEL, D_MODEL), BF16),
            pltpu.VMEM((2, 256, D_MODEL), F32),
            pltpu.SemaphoreType.DMA((2,)),
            pltpu.VMEM((D_MODEL, LANES), BF16),
            pltpu.VMEM((D_MODEL, LANES), BF16),
        ],
        compiler_params=_cparams(("arbitrary",)),
        name="out_proj_router",
    )(ycp, yret, x, w_out, ada5, ln_g.reshape(1, D_MODEL), ln_b.reshape(1, D_MODEL),
      ada5, ada5, w_router_pad, router_bias)


def _pos_kernel(ei_ref, rk_ref, cnt_ref, pos_ref):
    ei = ei_ref[...]
    pos = rk_ref[...]
    start = jnp.zeros((1, 1), F32)
    for e in range(N_EXPERTS):
        pos = pos + jnp.where(ei == e, start.astype(I32), 0)
        n_tiles = jnp.floor((cnt_ref[e:e + 1, 0:1] + (TM_EXP - 1.0)) * (1.0 / TM_EXP))
        start = start + n_tiles * TM_EXP
    pos_ref[...] = pos


def _pair_rows(ei, rk, cnt):
    return pl.pallas_call(
        _pos_kernel,
        out_shape=jax.ShapeDtypeStruct((2, M_TOK), I32),
        name="pair_rows",
    )(ei, rk, cnt)


def _route_kernel(cnt_ref, pos_ref, src_ref, te_ref, nv_ref):
    def zero(r, carry):
        src_ref[r] = 0
        return carry

    tile = jnp.int32(0)
    for e in range(N_EXPERTS):
        n_tiles = lax.shift_right_logical(cnt_ref[e] + (TM_EXP - 1), TM_EXP.bit_length() - 1)

        def mark(j, carry, e=e, tile=tile):
            te_ref[tile + j] = e
            return carry
        lax.fori_loop(0, n_tiles, mark, 0)
        lax.fori_loop(tile * TM_EXP + cnt_ref[e], (tile + n_tiles) * TM_EXP, zero, 0)
        tile = tile + n_tiles
    nv_ref[0] = tile
    last_expert = te_ref[tile - 1]

    def mark_unused(j, carry):
        te_ref[j] = last_expert
        return carry
    lax.fori_loop(tile, NT_EXP, mark_unused, 0)
    lax.fori_loop(tile * TM_EXP, NP_EXP, zero, 0)

    def place(t, carry):
        src_ref[pos_ref[t]] = t
        src_ref[pos_ref[M_TOK + t]] = t
        return carry
    lax.fori_loop(0, M_TOK, place, 0, unroll=16)


def _route_tables(cnt, pos):
    smem = pl.BlockSpec(memory_space=pltpu.SMEM)
    return pl.pallas_call(
        _route_kernel,
        in_specs=[smem, smem],
        out_specs=[smem, smem, smem],
        out_shape=[
            jax.ShapeDtypeStruct((NP_EXP,), I32),
            jax.ShapeDtypeStruct((NT_EXP,), I32),
            jax.ShapeDtypeStruct((1,), I32),
        ],
        name="route_tables",
    )(cnt, pos)


GATHER_PRIORITY = 1


def _row_gather_start(src_hbm, buf, sem, idx_ref, base, n_tok, tok0=0, priority=0):
    for r in range(n_tok):
        pltpu.make_async_copy(src_hbm.at[pl.ds(idx_ref[base + r] * TOK_STRIDE, TOK_ROWS), :],
                              buf.at[pl.ds((tok0 + r) * TOK_STRIDE, TOK_ROWS), :], sem).start(priority)


def _row_gather_wait(src_hbm, buf, sem):
    n_rows = buf.shape[0] // TOK_STRIDE * TOK_ROWS
    pltpu.make_async_copy(src_hbm.at[pl.ds(0, n_rows), :], buf.at[pl.ds(0, n_rows), :], sem).wait()


def _experts_kernel(te_ref, nv_ref, src_ref, h2_hbm, wg_ref, wu_ref, wd_ref, o_ref, xbuf0, xbuf1, gsem,
                    wg_bf, wu_bf, wd_bf):
    i = pl.program_id(0)
    n_valid = nv_ref[0]
    bufs = (xbuf0, xbuf1)

    @pl.when(i == 0)
    def _():
        _row_gather_start(h2_hbm, xbuf0, gsem.at[0], src_ref, 0, TM_EXP, priority=GATHER_PRIORITY)

    @pl.when((i < n_valid) & ((i == 0) | (te_ref[i] != te_ref[jnp.maximum(i - 1, 0)])))
    def _():
        wg_bf[...] = wg_ref[...].astype(BF16)
        wu_bf[...] = wu_ref[...].astype(BF16)
        wd_bf[...] = wd_ref[...].astype(BF16)

    def step(cur):
        nxt_buf, nxt_sem = bufs[1 - cur], gsem.at[1 - cur]
        _row_gather_wait(h2_hbm, bufs[cur], gsem.at[cur])
        nxt = jnp.minimum(i + 1, NT_EXP - 1)
        _row_gather_start(h2_hbm, nxt_buf, nxt_sem, src_ref, nxt * TM_EXP, TM_EXP, priority=GATHER_PRIORITY)
        x = _load_token_tiles(bufs[cur], 0, TM_EXP).astype(BF16)
        g = jnp.dot(x, wg_bf[...], preferred_element_type=F32)
        u = jnp.dot(x, wu_bf[...], preferred_element_type=F32)
        a = (_silu(g) * u).astype(BF16)
        _store_token_tiles(o_ref, jnp.dot(a, wd_bf[...], preferred_element_type=F32))

        @pl.when(i == n_valid - 1)
        def _():
            _row_gather_wait(h2_hbm, nxt_buf, nxt_sem)

    for cur in range(2):
        pl.when((i < n_valid) & (i % 2 == cur))(functools.partial(step, cur))

    @pl.when(i >= n_valid)
    def _():
        o_ref[...] = jnp.zeros_like(o_ref)


def _experts(h2, tile_expert, n_valid, src_tok, wg, wu, wd, layer):
    grid_spec = pltpu.PrefetchScalarGridSpec(
        num_scalar_prefetch=3,
        grid=(NT_EXP,),
        in_specs=[
            pl.BlockSpec(memory_space=pl.ANY),
            pl.BlockSpec((None, None, D_MODEL, D_EXPERT), lambda i, te, nv, src: (layer, te[i], 0, 0)),
            pl.BlockSpec((None, None, D_MODEL, D_EXPERT), lambda i, te, nv, src: (layer, te[i], 0, 0)),
            pl.BlockSpec((None, None, D_EXPERT, D_MODEL), lambda i, te, nv, src: (layer, te[i], 0, 0)),
        ],
        out_specs=pl.BlockSpec((TM_EXP * TOK_STRIDE, LANES), lambda i, te, nv, src: (i, 0)),
        scratch_shapes=[
            pltpu.VMEM((TM_EXP * TOK_STRIDE, LANES), F32),
            pltpu.VMEM((TM_EXP * TOK_STRIDE, LANES), F32),
            pltpu.SemaphoreType.DMA((2,)),
            pltpu.VMEM((D_MODEL, D_EXPERT), BF16),
            pltpu.VMEM((D_MODEL, D_EXPERT), BF16),
            pltpu.VMEM((D_EXPERT, D_MODEL), BF16),
        ],
    )
    return pl.pallas_call(
        _experts_kernel,
        grid_spec=grid_spec,
        out_shape=jax.ShapeDtypeStruct((NP_EXP * TOK_STRIDE, LANES), F32),
        compiler_params=_cparams(("arbitrary",)),
        name="experts",
    )(tile_expert, n_valid, src_tok, h2, wg, wu, wd)


def _final_kernel(*refs, emit_h):
    if emit_h:
        (pos_ref, x1_ref, ys_hbm, ewt_ref, gate2_ref, lng_ref, lnb_ref, sc1_ref, sh1_ref,
         x_ref, h_ref, rbuf0, rbuf1, sem) = refs
    else:
        (pos_ref, x1_ref, ys_hbm, ewt_ref, gate2_ref, lng_ref, lnb_ref,
         xc_ref, xl_ref, rbuf0, rbuf1, sem) = refs
    tm = TM_OUT
    i = pl.program_id(0)
    n_blocks = M_TOK // tm
    bufs = (rbuf0, rbuf1)

    def start(tile, buf, buf_sem):
        _row_gather_start(ys_hbm, buf, buf_sem, pos_ref, tile * tm, tm, tok0=0)
        _row_gather_start(ys_hbm, buf, buf_sem, pos_ref, M_TOK + tile * tm, tm, tok0=tm)

    @pl.when(i == 0)
    def _():
        start(0, rbuf0, sem.at[0])

    def step(cur):
        buf = bufs[cur]
        nxt_buf, nxt_sem = bufs[1 - cur], sem.at[1 - cur]
        _row_gather_wait(ys_hbm, buf, sem.at[cur])
        start(jnp.minimum(i + 1, n_blocks - 1), nxt_buf, nxt_sem)
        w = ewt_ref[...]
        y2 = w[:, 0:1] * _load_token_tiles(buf, 0, tm) + w[:, 1:2] * _load_token_tiles(buf, tm, tm)
        x2 = _layer_norm_rows(DEEPNORM_ALPHA * x1_ref[...] + gate2_ref[...] * y2,
                              lng_ref[...], lnb_ref[...])
        if emit_h:
            x_ref[...] = x2
            h_ref[...] = (x2 * (1.0 + sc1_ref[...]) + sh1_ref[...]).astype(BF16)
        else:
            @pl.when(i < M_CTX // tm)
            def _():
                xc_ref[...] = x2

            @pl.when(i >= M_CTX // tm)
            def _():
                xl_ref[...] = x2

        @pl.when(i == n_blocks - 1)
        def _():
            _row_gather_wait(ys_hbm, nxt_buf, nxt_sem)

    for cur in range(2):
        pl.when(i % 2 == cur)(functools.partial(step, cur))


def _final(x1, ys, pos, ewt, ada5, ln_g, ln_b, ada5_next):
    tm = TM_OUT
    emit_h = ada5_next is not None
    vec_spec = pl.BlockSpec((1, D_MODEL), lambda i, pos: (0, 0))
    ada_spec = lambda chunk: pl.BlockSpec(
        (None, None, 1, D_MODEL), lambda i, pos: (_ada_row(i, tm), chunk, 0, 0))
    in_specs = [
        pl.BlockSpec((tm, D_MODEL), lambda i, pos: (i, 0)),
        pl.BlockSpec(memory_space=pl.ANY),
        pl.BlockSpec((tm, LANES), lambda i, pos: (i, 0)),
        ada_spec(5),
        vec_spec, vec_spec,
    ]
    args = [pos, x1, ys, ewt, ada5, ln_g.reshape(1, D_MODEL), ln_b.reshape(1, D_MODEL)]
    if emit_h:
        in_specs += [ada_spec(1), ada_spec(0)]
        args += [ada5_next, ada5_next]
        row_spec = pl.BlockSpec((tm, D_MODEL), lambda i, pos: (i, 0))
        out_specs = [row_spec, row_spec]
        out_shape = [jax.ShapeDtypeStruct((M_TOK, D_MODEL), F32),
                     jax.ShapeDtypeStruct((M_TOK, D_MODEL), BF16)]
    else:
        out_specs = [
            pl.BlockSpec((tm, D_MODEL), lambda i, pos: (_ctx_block(i, tm), 0)),
            pl.BlockSpec((tm, D_MODEL), lambda i, pos: (_lat_block(i, tm), 0)),
        ]
        out_shape = [jax.ShapeDtypeStruct((M_CTX, D_MODEL), F32),
                     jax.ShapeDtypeStruct((M_LAT, D_MODEL), F32)]
    grid_spec = pltpu.PrefetchScalarGridSpec(
        num_scalar_prefetch=1,
        grid=(M_TOK // tm,),
        in_specs=in_specs,
        out_specs=out_specs,
        scratch_shapes=[
            pltpu.VMEM((2 * tm * TOK_STRIDE, LANES), F32),
            pltpu.VMEM((2 * tm * TOK_STRIDE, LANES), F32),
            pltpu.SemaphoreType.DMA((2,)),
        ],
    )
    return pl.pallas_call(
        functools.partial(_final_kernel, emit_h=emit_h),
        grid_spec=grid_spec,
        out_shape=out_shape,
        compiler_params=_cparams(("arbitrary",)),
        name="final_ln",
    )(*args)


def _rope_tables():
    rows = T_LAT // GRID_W
    row = jnp.repeat(jnp.arange(rows), GRID_W).astype(F32)
    col = jnp.tile(jnp.arange(GRID_W), rows).astype(F32)
    n_freq = HEAD_DIM // 4
    inv_freq = ROPE_BASE ** (-jnp.arange(n_freq, dtype=F32) / n_freq)
    ang = jnp.concatenate([row[:, None] * inv_freq[None], col[:, None] * inv_freq[None]], axis=-1)
    cos, sin = jnp.cos(ang), jnp.sin(ang)
    return jnp.concatenate([cos, cos], axis=-1), jnp.concatenate([-sin, sin], axis=-1)


def kernel(x_prompt, x_sample, state_retention, c, c_ctx, w_ada, b_ada, w_in, w_out, conv_w, pool_w,
           pool_scale, ret_decay_logit, ret_gn_gain, ln1_g, ln1_b, ln2_g, ln2_b, w_router, router_bias,
           w_gate, w_up, w_down):
    x_ctx = x_prompt.reshape(M_CTX, D_MODEL)
    x_lat = x_sample.reshape(M_LAT, D_MODEL)
    c_all = jnp.concatenate(
        [c_ctx[None, :], c, jnp.zeros((ADA_ROWS - 1 - N_LAT_SEQ, D_MODEL), F32)], axis=0)
    ada = _ada_table(c_all, w_ada, b_ada).reshape(DEPTH, ADA_ROWS, 6, 1, D_MODEL)

    cos_lat, sin_lat = _rope_tables()
    w_router_pad = jnp.pad(w_router, ((0, 0), (0, LANES - N_EXPERTS)))

    states = None
    x, h = _modulate(x_ctx, x_lat, ada[0])
    for l in range(DEPTH):
        ada5 = ada[l]
        z = _in_proj(h, w_in, l)

        ycp = _conv_pool(z, conv_w[l], pool_w[l], pool_scale[l])

        dl = jnp.broadcast_to(ret_decay_logit[l][:, :, None, None], (2, N_RET_HEADS, 1, HEAD_DIM))
        yret, states = _retention(z, cos_lat, sin_lat, dl, ret_gn_gain[l], state_retention, l, states)

        x1, h2, ei, ewt, rk, cnt = _out_proj(ycp, yret, x, w_out, l, ada5,
                                             ln1_g[l], ln1_b[l], w_router_pad, router_bias)
        pos = _pair_rows(ei, rk, cnt).reshape(-1)
        src_tok, te, n_valid = _route_tables(cnt[:, 0].astype(I32), pos)
        ys = _experts(h2, te, n_valid, src_tok, w_gate, w_up, w_down, l)
        if l + 1 < DEPTH:
            x, h = _final(x1, ys, pos, ewt, ada5, ln2_g[l], ln2_b[l], ada[l + 1])
        else:
            x_ctx, x_lat = _final(x1, ys, pos, ewt, ada5, ln2_g[l], ln2_b[l], None)

    y_prompt = x_ctx.reshape(N_CTX_SEQ, T_CTX, D_MODEL)
    y_sample = x_lat.reshape(N_LAT_SEQ, T_LAT, D_MODEL)
    return y_prompt, y_sample, states
```

```python
import functools

import jax
import jax.numpy as jnp
from jax import lax
from jax.experimental import pallas as pl
from jax.experimental.pallas import tpu as pltpu

F32 = jnp.float32
BF16 = jnp.bfloat16
I32 = jnp.int32

D_MODEL = 2048
N_CTX_SEQ, T_CTX = 16, 256
N_LAT_SEQ, T_LAT = 8, 1024
DEPTH = 2
M_CTX = N_CTX_SEQ * T_CTX
M_LAT = N_LAT_SEQ * T_LAT
M_TOK = M_CTX + M_LAT

GRID_W = 64
D_CONV = D_MODEL // 4
D_POOL = D_MODEL // 4
D_RET = D_MODEL // 2
N_RET_HEADS = 8
HEAD_DIM = D_RET // N_RET_HEADS
POOL_WINDOWS = (2, 4, 8, 16)
POOL_GROUP_DIM = D_POOL // len(POOL_WINDOWS)
CHUNK = 128
ROPE_BASE = 10000.0
N_EXPERTS = 16
EXPERTS_PER_GROUP = 4
N_EXPERT_GROUPS = N_EXPERTS // EXPERTS_PER_GROUP
D_EXPERT = D_MODEL // 4
D_IN_PROJ = 3 * D_CONV + D_POOL + 4 * D_RET
DEEPNORM_ALPHA = (2.0 * DEPTH) ** 0.25
LN_EPS = 1e-5
ADA_ROWS = 16

LANES = 128
VMEM_LIMIT = 56 * 1024 * 1024

TM_IN = 1024
TN_IN = 1536
TOK_ROWS = D_MODEL // LANES
TOK_STRIDE = TOK_ROWS + 1
TM_OP = 512
SUB_OP = 256
TM_OUT = 512
TM_EXP = 256
N_PAIR = 2 * M_TOK
NP_EXP = N_PAIR + N_EXPERTS * TM_EXP
NT_EXP = NP_EXP // TM_EXP


def _cparams(sem):
    return pltpu.CompilerParams(dimension_semantics=sem, vmem_limit_bytes=VMEM_LIMIT)


def _silu(x):
    return x * jax.nn.sigmoid(x)


def _ada_row(i, tm):
    n_ctx_tiles = M_CTX // tm
    per_batch = T_LAT // tm
    return jnp.where(i < n_ctx_tiles, 0, 1 + (i - n_ctx_tiles) // per_batch)


def _ada_kernel(c_ref, w_ref, b_ref, o_ref):
    s = _silu(c_ref[...]).astype(BF16)
    o_ref[...] = jnp.dot(s, w_ref[...].astype(BF16), preferred_element_type=F32) + b_ref[...]


def _ada_table(c_all, w_ada, b_ada):
    tn = 1024
    n6 = 6 * D_MODEL
    return pl.pallas_call(
        _ada_kernel,
        grid=(DEPTH, n6 // tn),
        in_specs=[
            pl.BlockSpec((ADA_ROWS, D_MODEL), lambda l, j: (0, 0)),
            pl.BlockSpec((None, D_MODEL, tn), lambda l, j: (l, 0, j)),
            pl.BlockSpec((None, 1, tn), lambda l, j: (l, 0, j)),
        ],
        out_specs=pl.BlockSpec((None, ADA_ROWS, tn), lambda l, j: (l, 0, j)),
        out_shape=jax.ShapeDtypeStruct((DEPTH, ADA_ROWS, n6), F32),
        compiler_params=_cparams(("arbitrary", "arbitrary")),
        name="ada_table",
    )(c_all, w_ada, b_ada.reshape(DEPTH, 1, n6))


def _ctx_block(i, tm):
    return jnp.minimum(i, M_CTX // tm - 1)


def _lat_block(i, tm):
    return jnp.maximum(i - M_CTX // tm, 0)


def _modulate_kernel(xc_ref, xl_ref, sc_ref, sh_ref, x_ref, h_ref):
    x = jnp.where(pl.program_id(0) < M_CTX // xc_ref.shape[0], xc_ref[...], xl_ref[...])
    x_ref[...] = x
    h_ref[...] = (x * (1.0 + sc_ref[...]) + sh_ref[...]).astype(BF16)


def _modulate(x_ctx, x_lat, ada5):
    tm = TM_OUT
    ada_spec = lambda chunk: pl.BlockSpec(
        (None, None, 1, D_MODEL), lambda i: (_ada_row(i, tm), chunk, 0, 0))
    return pl.pallas_call(
        _modulate_kernel,
        grid=(M_TOK // tm,),
        in_specs=[
            pl.BlockSpec((tm, D_MODEL), lambda i: (_ctx_block(i, tm), 0)),
            pl.BlockSpec((tm, D_MODEL), lambda i: (_lat_block(i, tm), 0)),
            ada_spec(1), ada_spec(0),
        ],
        out_specs=[pl.BlockSpec((tm, D_MODEL), lambda i: (i, 0)),
                   pl.BlockSpec((tm, D_MODEL), lambda i: (i, 0))],
        out_shape=[jax.ShapeDtypeStruct((M_TOK, D_MODEL), F32),
                   jax.ShapeDtypeStruct((M_TOK, D_MODEL), BF16)],
        compiler_params=_cparams(("arbitrary",)),
        name="modulate",
    )(x_ctx, x_lat, ada5, ada5)


def _inproj_kernel(h_ref, w_ref, o_ref, wbf_scr):
    @pl.when(pl.program_id(1) == 0)
    def _():
        wbf_scr[...] = w_ref[...].astype(BF16)

    o_ref[...] = jnp.dot(h_ref[...], wbf_scr[...], preferred_element_type=F32).astype(BF16)


def _in_proj(h, w_in, layer):
    return pl.pallas_call(
        _inproj_kernel,
        grid=(D_IN_PROJ // TN_IN, M_TOK // TM_IN),
        in_specs=[
            pl.BlockSpec((TM_IN, D_MODEL), lambda j, i: (i, 0)),
            pl.BlockSpec((None, D_MODEL, TN_IN), lambda j, i: (layer, 0, j)),
        ],
        out_specs=pl.BlockSpec((TM_IN, TN_IN), lambda j, i: (i, j)),
        out_shape=jax.ShapeDtypeStruct((M_TOK, D_IN_PROJ), BF16),
        scratch_shapes=[pltpu.VMEM((D_MODEL, TN_IN), BF16)],
        compiler_params=_cparams(("arbitrary", "arbitrary")),
        name="in_proj",
    )(h, w_in)


MIX_ROWS = T_LAT
N_CTX_BLOCKS = M_CTX // MIX_ROWS
SEQ_PER_CTX_BLOCK = MIX_ROWS // T_CTX
CP_ROWS = T_CTX
CP_HALO = 128


def _convpool_kernel(z_ref, cw_ref, pw_ref, ps_ref, o_ref, band_ref):
    blk = pl.program_id(0)
    g_dim = POOL_GROUP_DIM

    @pl.when(blk == 0)
    def _():
        row = lax.broadcasted_iota(I32, (CP_ROWS, CP_ROWS + 2 * CP_HALO), 0)
        col = lax.broadcasted_iota(I32, (CP_ROWS, CP_ROWS + 2 * CP_HALO), 1)
        d = col - CP_HALO - row
        for gi, w in enumerate(POOL_WINDOWS):
            band_ref[gi] = jnp.where((d >= -(w // 2)) & (d < w // 2), 1.0, 0.0).astype(BF16)

    def block(seq_len):
        t = lax.broadcasted_iota(I32, (CP_ROWS, LANES), 0)
        for ch in range(MIX_ROWS // CP_ROWS):
            r0 = ch * CP_ROWS
            rows = slice(r0, r0 + CP_ROWS)
            pos0 = r0 % seq_len
            at_start = pos0 == 0
            at_end = pos0 + CP_ROWS == seq_len

            for cg in range(D_CONV // LANES):
                lanes = slice(cg * LANES, (cg + 1) * LANES)

                def u_rows(rs):
                    return (z_ref[rs, D_CONV + cg * LANES:D_CONV + (cg + 1) * LANES].astype(F32)
                            * z_ref[rs, 2 * D_CONV + cg * LANES:2 * D_CONV + (cg + 1) * LANES].astype(F32))
                u = u_rows(rows)
                before = 0.0 if at_start else u_rows(slice(r0 - 1, r0))
                after = 0.0 if at_end else u_rows(slice(r0 + CP_ROWS, r0 + CP_ROWS + 1))
                u_prev = jnp.where(t == 0, before, pltpu.roll(u, 1, 0))
                u_next = jnp.where(t == CP_ROWS - 1, after, pltpu.roll(u, CP_ROWS - 1, 0))
                conv = u_prev * cw_ref[0:1, lanes] + u * cw_ref[1:2, lanes] + u_next * cw_ref[2:3, lanes]
                o_ref[rows, lanes] = (z_ref[rows, lanes].astype(F32) * conv).astype(BF16)

            k_rows = slice(r0 if at_start else r0 - CP_HALO,
                           r0 + CP_ROWS if at_end else r0 + CP_ROWS + CP_HALO)
            b_cols = slice(CP_HALO if at_start else 0,
                           CP_HALO + CP_ROWS if at_end else CP_ROWS + 2 * CP_HALO)
            tpos = pos0 + t
            for gi, w in enumerate(POOL_WINDOWS):
                lo = 3 * D_CONV + gi * g_dim
                win = jnp.dot(band_ref[gi, :, b_cols], z_ref[k_rows, lo:lo + g_dim],
                              preferred_element_type=F32)
                cnt = (jnp.minimum(tpos + w // 2, seq_len) - jnp.maximum(tpos - w // 2, 0)).astype(F32)
                pooled = win / cnt - z_ref[rows, lo:lo + g_dim].astype(F32)
                y = jnp.dot(pooled.astype(BF16), pw_ref[gi].astype(BF16), preferred_element_type=F32)
                y = y * ps_ref[:, gi * g_dim:(gi + 1) * g_dim]
                o_ref[rows, D_CONV + gi * g_dim:D_CONV + (gi + 1) * g_dim] = y.astype(BF16)

    pl.when(blk < N_CTX_BLOCKS)(functools.partial(block, T_CTX))
    pl.when(blk >= N_CTX_BLOCKS)(functools.partial(block, T_LAT))


def _conv_pool(z, conv_w, pool_w, pool_scale):
    return pl.pallas_call(
        _convpool_kernel,
        grid=(M_TOK // MIX_ROWS,),
        in_specs=[
            pl.BlockSpec((MIX_ROWS, D_IN_PROJ - 4 * D_RET), lambda b: (b, 0)),
            pl.BlockSpec((3, D_CONV), lambda b: (0, 0)),
            pl.BlockSpec((len(POOL_WINDOWS), POOL_GROUP_DIM, POOL_GROUP_DIM), lambda b: (0, 0, 0)),
            pl.BlockSpec((1, D_POOL), lambda b: (0, 0)),
        ],
        out_specs=pl.BlockSpec((MIX_ROWS, D_CONV + D_POOL), lambda b: (b, 0)),
        out_shape=jax.ShapeDtypeStruct((M_TOK, D_CONV + D_POOL), BF16),
        scratch_shapes=[pltpu.VMEM((len(POOL_WINDOWS), CP_ROWS, CP_ROWS + 2 * CP_HALO), BF16)],
        compiler_params=_cparams(("arbitrary",)),
        name="conv_pool",
    )(z, conv_w, pool_w, pool_scale.reshape(1, D_POOL))


def _log_sigmoid(x):
    return jnp.minimum(x, 0.0) - jnp.log1p(jnp.exp(-jnp.abs(x)))


RET_ROWS = MIX_ROWS


def _retention_kernel(*refs, layer):
    if layer == 0:
        (q_ref, k_ref, v_ref, g_ref, cos_ref, sin_ref, dl_ref, gain_ref, s0_ref,
         y_ref, sfin_ref, tab_scr, kr_scr, u_scr, s_scr) = refs
    else:
        (q_ref, k_ref, v_ref, g_ref, cos_ref, sin_ref, dl_ref, gain_ref, s0_ref, _,
         y_ref, sfin_ref, tab_scr, kr_scr, u_scr, s_scr) = refs
    blk = pl.program_id(1)
    n_chunks = RET_ROWS // CHUNK
    half = HEAD_DIM // 2

    lg_f = _log_sigmoid(dl_ref[0])
    lg_b = _log_sigmoid(dl_ref[1])
    g_f = jnp.exp(lg_f * CHUNK)
    g_b = jnp.exp(lg_b * CHUNK)

    @pl.when(blk == 0)
    def _():
        row = lax.broadcasted_iota(I32, (CHUNK, CHUNK), 0).astype(F32)
        col = lax.broadcasted_iota(I32, (CHUNK, CHUNK), 1).astype(F32)
        diff = row - col
        tab_scr[0] = (jnp.where(diff >= 0, jnp.exp(lg_f * jnp.maximum(diff, 0.0)), 0.0)
                      + jnp.where(diff <= 0, jnp.exp(lg_b * jnp.maximum(-diff, 0.0)), 0.0))
        tab_scr[1] = jnp.exp(lg_f * (row + 1.0))
        tab_scr[2] = jnp.exp(lg_b * (CHUNK - row))
        tab_scr[3] = jnp.exp(lg_f * (CHUNK - 1.0 - row))
        tab_scr[4] = jnp.exp(lg_b * row)

    tn_dims = (((0,), (0,)), ((), ()))
    nt_dims = (((1,), (1,)), ((), ()))

    def block(seq_chunks, latent):
        def rope(x, sl):
            if not latent:
                return x
            return x * cos_ref[sl, :] + pltpu.roll(x, half, 1) * sin_ref[sl, :]

        for c in range(n_chunks):
            sl = slice(c * CHUNK, (c + 1) * CHUNK)
            kr = rope(k_ref[sl, :].astype(F32), sl) * (HEAD_DIM ** -0.5)
            kr_scr[sl, :] = kr.astype(BF16)
            kz = jnp.concatenate([kr * tab_scr[3], kr * tab_scr[4]], axis=1).astype(BF16)
            u_scr[c] = lax.dot_general(kz, v_ref[sl, :], tn_dims, preferred_element_type=F32)

        has_state = [[False] * n_chunks, [False] * n_chunks]
        for s_i in range(n_chunks // seq_chunks):
            chunks = list(range(s_i * seq_chunks, (s_i + 1) * seq_chunks))
            for d, order, g_d in ((0, chunks, g_f), (1, chunks[::-1], g_b)):
                s = s0_ref[d] if latent else None
                for c in order:
                    u = u_scr[c, d * HEAD_DIM:(d + 1) * HEAD_DIM, :]
                    if s is None:
                        s = u
                    else:
                        s_scr[d, c] = s.astype(BF16)
                        has_state[d][c] = True
                        s = g_d * s + u
                if not latent:
                    if layer == 0:
                        sfin_ref[s_i, 0, d] = s
                        for later in range(1, DEPTH):
                            sfin_ref[s_i, later, d] = jnp.zeros_like(s)
                    else:
                        sfin_ref[s_i, d] = s

        for c in range(n_chunks):
            sl = slice(c * CHUNK, (c + 1) * CHUNK)
            qr = rope(q_ref[sl, :].astype(F32), sl)
            scores = lax.dot_general(qr.astype(BF16), kr_scr[sl, :], nt_dims, preferred_element_type=F32)
            o = jnp.dot((scores * tab_scr[0]).astype(BF16), v_ref[sl, :], preferred_element_type=F32)
            for d in range(2):
                if has_state[d][c]:
                    o += jnp.dot((qr * tab_scr[1 + d]).astype(BF16), s_scr[d, c],
                                 preferred_element_type=F32)
            mu = jnp.mean(o, axis=-1, keepdims=True)
            dev = o - mu
            var = jnp.mean(dev * dev, axis=-1, keepdims=True)
            on = dev * lax.rsqrt(var + LN_EPS) * gain_ref[...]
            y_ref[sl, :] = (_silu(g_ref[sl, :].astype(F32)) * on).astype(BF16)

    pl.when(blk < N_CTX_BLOCKS)(functools.partial(block, T_CTX // CHUNK, False))
    pl.when(blk >= N_CTX_BLOCKS)(functools.partial(block, T_LAT // CHUNK, True))


def _retention(z, cos_t, sin_t, decay_logit_b, gain, s0, layer, states):
    qcol0 = (3 * D_CONV + D_POOL) // HEAD_DIM
    n_blocks = M_TOK // RET_ROWS

    def zspec(k):
        return pl.BlockSpec((RET_ROWS, HEAD_DIM), lambda h, b: (b, qcol0 + k * N_RET_HEADS + h))

    def lat_seq(b):
        return jnp.maximum(b - N_CTX_BLOCKS, 0)

    def ctx_block(b):
        return jnp.minimum(b, N_CTX_BLOCKS - 1)

    n_chunks = RET_ROWS // CHUNK
    in_specs = [
        zspec(0), zspec(1), zspec(2), zspec(3),
        pl.BlockSpec((RET_ROWS, HEAD_DIM), lambda h, b: (0, 0)),
        pl.BlockSpec((RET_ROWS, HEAD_DIM), lambda h, b: (0, 0)),
        pl.BlockSpec((2, None, 1, HEAD_DIM), lambda h, b: (0, h, 0, 0)),
        pl.BlockSpec((1, HEAD_DIM), lambda h, b: (0, h)),
        pl.BlockSpec((None, None, 2, None, HEAD_DIM, HEAD_DIM),
                     lambda h, b: (lat_seq(b), layer, 0, h, 0, 0)),
    ]
    args = [z, z, z, z, cos_t, sin_t, decay_logit_b, gain.reshape(1, D_RET), s0]
    if layer == 0:
        st_spec = pl.BlockSpec((SEQ_PER_CTX_BLOCK, DEPTH, 2, None, HEAD_DIM, HEAD_DIM),
                               lambda h, b: (ctx_block(b), 0, 0, h, 0, 0))
        aliases = {}
    else:
        st_spec = pl.BlockSpec((SEQ_PER_CTX_BLOCK, None, 2, None, HEAD_DIM, HEAD_DIM),
                               lambda h, b: (ctx_block(b), layer, 0, h, 0, 0))
        in_specs.append(pl.BlockSpec(memory_space=pl.ANY))
        args.append(states)
        aliases = {len(args) - 1: 1}
    return pl.pallas_call(
        functools.partial(_retention_kernel, layer=layer),
        grid=(N_RET_HEADS, n_blocks),
        in_specs=in_specs,
        out_specs=[pl.BlockSpec((RET_ROWS, HEAD_DIM), lambda h, b: (b, h)), st_spec],
        out_shape=[
            jax.ShapeDtypeStruct((M_TOK, D_RET), BF16),
            jax.ShapeDtypeStruct((N_CTX_SEQ, DEPTH, 2, N_RET_HEADS, HEAD_DIM, HEAD_DIM), F32),
        ],
        input_output_aliases=aliases,
        scratch_shapes=[
            pltpu.VMEM((5, CHUNK, CHUNK), F32),
            pltpu.VMEM((RET_ROWS, HEAD_DIM), BF16),
            pltpu.VMEM((n_chunks, 2 * HEAD_DIM, HEAD_DIM), F32),
            pltpu.VMEM((2, n_chunks, HEAD_DIM, HEAD_DIM), BF16),
        ],
        compiler_params=_cparams(("arbitrary", "arbitrary")),
        name="retention",
    )(*args)


def _layer_norm_rows(r, g, b):
    mu = jnp.mean(r, axis=-1, keepdims=True)
    dev = r - mu
    var = jnp.mean(dev * dev, axis=-1, keepdims=True)
    return dev * lax.rsqrt(var + LN_EPS) * g + b


def _top2_of4(vals):
    top1 = jnp.maximum(jnp.maximum(vals[0], vals[1]), jnp.maximum(vals[2], vals[3]))
    idx1 = jnp.where(vals[0] == top1, 0, jnp.where(vals[1] == top1, 1, jnp.where(vals[2] == top1, 2, 3)))
    neg = jnp.float32(-jnp.inf)
    rest = [jnp.where(idx1 == j, neg, vals[j]) for j in range(4)]
    top2 = jnp.maximum(jnp.maximum(rest[0], rest[1]), jnp.maximum(rest[2], rest[3]))
    idx2 = jnp.where(rest[0] == top2, 0, jnp.where(rest[1] == top2, 1, jnp.where(rest[2] == top2, 2, 3)))
    return top1, idx1, top2, idx2


def _store_token_tiles(ref, val, tok0=0):
    n = val.shape[0]
    base = tok0 * TOK_STRIDE
    for c in range(TOK_ROWS):
        ref[pl.ds(base + c, n, stride=TOK_STRIDE), :] = val[:, c * LANES:(c + 1) * LANES]
    ref[pl.ds(base + TOK_ROWS, n, stride=TOK_STRIDE), :] = jnp.zeros((n, LANES), val.dtype)


def _load_token_tiles(ref, tok0, n):
    return jnp.concatenate(
        [ref[pl.ds(tok0 * TOK_STRIDE + c, n, stride=TOK_STRIDE), :] for c in range(TOK_ROWS)], axis=1)


def _outproj_kernel(ycp_ref, yret_ref, x_ref, wo_hbm, gate1_ref, lng_ref, lnb_ref, sc2_ref, sh2_ref,
                    wr_ref, rb_ref, x1_ref, h2_ref, ei_ref, ewt_ref, rk_ref, cnt_ref,
                    carry_scr, wo_ref, wstage, wsem, wr_hi, wr_lo, *, layer):
    i = pl.program_id(0)

    @pl.when(i == 0)
    def _():
        carry_scr[...] = jnp.zeros_like(carry_scr)
        wr = wr_ref[...]
        hi = wr.astype(BF16)
        wr_hi[...] = hi
        wr_lo[...] = (wr - hi.astype(F32)).astype(BF16)
        rows = wstage.shape[1]
        n_chunks = D_MODEL // rows

        def chunk_copy(c):
            return pltpu.make_async_copy(wo_hbm.at[layer, pl.ds(c * rows, rows), :], wstage.at[c % 2],
                                         wsem.at[c % 2])
        chunk_copy(0).start()
        for c in range(n_chunks):
            if c + 1 < n_chunks:
                chunk_copy(c + 1).start()
            chunk_copy(c).wait()
            wo_ref[c * rows:(c + 1) * rows, :] = wstage[c % 2].astype(BF16)

    s_i = lax.broadcasted_iota(I32, (SUB_OP, SUB_OP), 0)
    t_i = lax.broadcasted_iota(I32, (SUB_OP, SUB_OP), 1)
    tri = jnp.where(s_i < t_i, 1.0, 0.0).astype(BF16)
    half_k = D_CONV + D_POOL
    n_sub = x_ref.shape[0] // SUB_OP
    ys = []
    for sub in range(n_sub):
        rows_sl = slice(sub * SUB_OP, (sub + 1) * SUB_OP)
        y = jnp.dot(ycp_ref[rows_sl, :], wo_ref[0:half_k, :], preferred_element_type=F32)
        ys.append(y + jnp.dot(yret_ref[rows_sl, :], wo_ref[half_k:, :], preferred_element_type=F32))
    lts = [_ln_router_subtile(sub, ys[sub], x_ref, gate1_ref, lng_ref, lnb_ref, sc2_ref, sh2_ref,
                              wr_hi, wr_lo, x1_ref, h2_ref) for sub in range(n_sub)]
    carry = carry_scr[:, 0:1]
    for sub in range(n_sub):
        carry = _route_subtile(sub, lts[sub], carry, tri, rb_ref, ei_ref, ewt_ref, rk_ref)
    carry_scr[...] = jnp.broadcast_to(carry, carry_scr.shape)
    cnt_ref[...] = carry_scr[...]


def _ln_router_subtile(sub, y, x_ref, gate1_ref, lng_ref, lnb_ref, sc2_ref, sh2_ref, wr_hi, wr_lo,
                       x1_ref, h2_ref):
    tm = SUB_OP
    rows_sl = slice(sub * tm, (sub + 1) * tm)
    x1 = _layer_norm_rows(DEEPNORM_ALPHA * x_ref[rows_sl, :] + gate1_ref[...] * y,
                          lng_ref[...], lnb_ref[...])
    x1_ref[rows_sl, :] = x1
    h2 = x1 * (1.0 + sc2_ref[...]) + sh2_ref[...]
    _store_token_tiles(h2_ref, h2, tok0=sub * tm)

    h_hi = h2.astype(BF16)
    h_lo = (h2 - h_hi.astype(F32)).astype(BF16)
    logits = (jnp.dot(h_hi, wr_hi[...], preferred_element_type=F32)
              + jnp.dot(h_lo, wr_hi[...], preferred_element_type=F32)
              + jnp.dot(h_hi, wr_lo[...], preferred_element_type=F32))
    return logits.T


def _route_subtile(sub, lt, carry, tri, rb_ref, ei_ref, ewt_ref, rk_ref):
    tm = SUB_OP
    rows_sl = slice(sub * tm, (sub + 1) * tm)
    rows = [lt[e:e + 1, :] for e in range(N_EXPERTS)]

    mx = rows[0]
    for e in range(1, N_EXPERTS):
        mx = jnp.maximum(mx, rows[e])
    ex = [jnp.exp(r - mx) for r in rows]
    den = ex[0]
    for e in range(1, N_EXPERTS):
        den = den + ex[e]
    score = [x / den for x in ex]
    biased = [score[e] + rb_ref[e] for e in range(N_EXPERTS)]

    best = None
    for gi in range(N_EXPERT_GROUPS):
        t1, i1, t2, i2 = _top2_of4(biased[gi * EXPERTS_PER_GROUP:(gi + 1) * EXPERTS_PER_GROUP])
        gs = t1 + t2
        e1 = gi * EXPERTS_PER_GROUP + i1
        e2 = gi * EXPERTS_PER_GROUP + i2
        if best is None:
            best = (gs, e1, e2)
        else:
            take = gs > best[0]
            best = (jnp.where(take, gs, best[0]), jnp.where(take, e1, best[1]), jnp.where(take, e2, best[2]))
    _, e1, e2 = best
    zero = jnp.zeros_like(score[0])
    w1 = zero
    w2 = zero
    for e in range(N_EXPERTS):
        w1 = w1 + jnp.where(e1 == e, score[e], 0.0)
        w2 = w2 + jnp.where(e2 == e, score[e], 0.0)
    wsum = w1 + w2
    ei_ref[0:1, rows_sl] = e1
    ei_ref[1:2, rows_sl] = e2
    ewt_ref[rows_sl, :] = jnp.concatenate(
        [w1 / wsum, w2 / wsum, jnp.zeros((LANES - 2, tm), F32)], axis=0).T

    onehot = jnp.concatenate(
        [jnp.where((e1 == e) | (e2 == e), 1.0, 0.0) for e in range(N_EXPERTS)], axis=0)
    prefix = jnp.dot(onehot.astype(BF16), tri, preferred_element_type=F32) + carry
    r1 = zero
    r2 = zero
    for e in range(N_EXPERTS):
        r1 = r1 + jnp.where(e1 == e, prefix[e:e + 1, :], 0.0)
        r2 = r2 + jnp.where(e2 == e, prefix[e:e + 1, :], 0.0)
    rk_ref[0:1, rows_sl] = r1.astype(I32)
    rk_ref[1:2, rows_sl] = r2.astype(I32)
    return carry + jnp.sum(onehot, axis=1, keepdims=True)


def _out_proj(ycp, yret, x, w_out, layer, ada5, ln_g, ln_b, w_router_pad, router_bias):
    tm = TM_OP
    ada_spec = lambda chunk: pl.BlockSpec(
        (None, None, 1, D_MODEL), lambda i: (_ada_row(i, tm), chunk, 0, 0))
    vec_spec = pl.BlockSpec((1, D_MODEL), lambda i: (0, 0))
    route_spec = pl.BlockSpec((2, tm), lambda i: (0, i))
    return pl.pallas_call(
        functools.partial(_outproj_kernel, layer=layer),
        grid=(M_TOK // tm,),
        in_specs=[
            pl.BlockSpec((tm, D_CONV + D_POOL), lambda i: (i, 0)),
            pl.BlockSpec((tm, D_RET), lambda i: (i, 0)),
            pl.BlockSpec((tm, D_MODEL), lambda i: (i, 0)),
            pl.BlockSpec(memory_space=pl.ANY),
            ada_spec(2),
            vec_spec, vec_spec,
            ada_spec(4),
            ada_spec(3),
            pl.BlockSpec((D_MODEL, LANES), lambda i: (0, 0)),
            pl.BlockSpec(memory_space=pltpu.SMEM),
        ],
        out_specs=[
            pl.BlockSpec((tm, D_MODEL), lambda i: (i, 0)),
            pl.BlockSpec((tm * TOK_STRIDE, LANES), lambda i: (i, 0)),
            route_spec,
            pl.BlockSpec((tm, LANES), lambda i: (i, 0)),
            route_spec,
            pl.BlockSpec((N_EXPERTS, LANES), lambda i: (0, 0)),
        ],
        out_shape=[
            jax.ShapeDtypeStruct((M_TOK, D_MODEL), F32),
            jax.ShapeDtypeStruct((M_TOK * TOK_STRIDE, LANES), F32),
            jax.ShapeDtypeStruct((2, M_TOK), I32),
            jax.ShapeDtypeStruct((M_TOK, LANES), F32),
            jax.ShapeDtypeStruct((2, M_TOK), I32),
            jax.ShapeDtypeStruct((N_EXPERTS, LANES), F32),
        ],
        scratch_shapes=[
            pltpu.VMEM((N_EXPERTS, LANES), F32),
            pltpu.VMEM((D_MODEL, D_MODEL), BF16),
            pltpu.VMEM((2, 256, D_MODEL), F32),
            pltpu.SemaphoreType.DMA((2,)),
            pltpu.VMEM((D_MODEL, LANES), BF16),
            pltpu.VMEM((D_MODEL, LANES), BF16),
        ],
        compiler_params=_cparams(("arbitrary",)),
        name="out_proj_router",
    )(ycp, yret, x, w_out, ada5, ln_g.reshape(1, D_MODEL), ln_b.reshape(1, D_MODEL),
      ada5, ada5, w_router_pad, router_bias)


def _pos_kernel(ei_ref, rk_ref, cnt_ref, pos_ref):
    ei = ei_ref[...]
    pos = rk_ref[...]
    start = jnp.zeros((1, 1), F32)
    for e in range(N_EXPERTS):
        pos = pos + jnp.where(ei == e, start.astype(I32), 0)
        n_tiles = jnp.floor((cnt_ref[e:e + 1, 0:1] + (TM_EXP - 1.0)) * (1.0 / TM_EXP))
        start = start + n_tiles * TM_EXP
    pos_ref[...] = pos


def _pair_rows(ei, rk, cnt):
    return pl.pallas_call(
        _pos_kernel,
        out_shape=jax.ShapeDtypeStruct((2, M_TOK), I32),
        name="pair_rows",
    )(ei, rk, cnt)


def _route_kernel(cnt_ref, pos_ref, src_ref, te_ref, nv_ref):
    def zero(r, carry):
        src_ref[r] = 0
        return carry

    tile = jnp.int32(0)
    for e in range(N_EXPERTS):
        n_tiles = lax.shift_right_logical(cnt_ref[e] + (TM_EXP - 1), TM_EXP.bit_length() - 1)

        def mark(j, carry, e=e, tile=tile):
            te_ref[tile + j] = e
            return carry
        lax.fori_loop(0, n_tiles, mark, 0)
        lax.fori_loop(tile * TM_EXP + cnt_ref[e], (tile + n_tiles) * TM_EXP, zero, 0)
        tile = tile + n_tiles
    nv_ref[0] = tile
    last_expert = te_ref[tile - 1]

    def mark_unused(j, carry):
        te_ref[j] = last_expert
        return carry
    lax.fori_loop(tile, NT_EXP, mark_unused, 0)
    lax.fori_loop(tile * TM_EXP, NP_EXP, zero, 0)

    def place(t, carry):
        src_ref[pos_ref[t]] = t
        src_ref[pos_ref[M_TOK + t]] = t
        return carry
    lax.fori_loop(0, M_TOK, place, 0, unroll=16)


def _route_tables(cnt, pos):
    smem = pl.BlockSpec(memory_space=pltpu.SMEM)
    return pl.pallas_call(
        _route_kernel,
        in_specs=[smem, smem],
        out_specs=[smem, smem, smem],
        out_shape=[
            jax.ShapeDtypeStruct((NP_EXP,), I32),
            jax.ShapeDtypeStruct((NT_EXP,), I32),
            jax.ShapeDtypeStruct((1,), I32),
        ],
        name="route_tables",
    )(cnt, pos)


GATHER_PRIORITY = 1


def _row_gather_start(src_hbm, buf, sem, idx_ref, base, n_tok, tok0=0, priority=0):
    for r in range(n_tok):
        pltpu.make_async_copy(src_hbm.at[pl.ds(idx_ref[base + r] * TOK_STRIDE, TOK_ROWS), :],
                              buf.at[pl.ds((tok0 + r) * TOK_STRIDE, TOK_ROWS), :], sem).start(priority)


def _row_gather_wait(src_hbm, buf, sem):
    n_rows = buf.shape[0] // TOK_STRIDE * TOK_ROWS
    pltpu.make_async_copy(src_hbm.at[pl.ds(0, n_rows), :], buf.at[pl.ds(0, n_rows), :], sem).wait()


N_XBUF = 3


def _experts_kernel(te_ref, nv_ref, src_ref, h2_hbm, wg_ref, wu_ref, wd_ref, o_ref, *scratch):
    bufs = scratch[:N_XBUF]
    gsem, wg_bf, wu_bf, wd_bf = scratch[N_XBUF:]
    i = pl.program_id(0)
    n_valid = nv_ref[0]

    def start_tile(tile, slot):
        tile = jnp.minimum(tile, NT_EXP - 1)
        _row_gather_start(h2_hbm, bufs[slot], gsem.at[slot], src_ref, tile * TM_EXP, TM_EXP,
                          priority=GATHER_PRIORITY)

    @pl.when(i == 0)
    def _():
        for t in range(N_XBUF - 1):
            start_tile(t, t)

    @pl.when((i < n_valid) & ((i == 0) | (te_ref[i] != te_ref[jnp.maximum(i - 1, 0)])))
    def _():
        wg_bf[...] = wg_ref[...].astype(BF16)
        wu_bf[...] = wu_ref[...].astype(BF16)
        wd_bf[...] = wd_ref[...].astype(BF16)

    def step(cur):
        _row_gather_wait(h2_hbm, bufs[cur], gsem.at[cur])
        start_tile(i + N_XBUF - 1, (cur + N_XBUF - 1) % N_XBUF)
        x = _load_token_tiles(bufs[cur], 0, TM_EXP).astype(BF16)
        g = jnp.dot(x, wg_bf[...], preferred_element_type=F32)
        u = jnp.dot(x, wu_bf[...], preferred_element_type=F32)
        a = (_silu(g) * u).astype(BF16)
        _store_token_tiles(o_ref, jnp.dot(a, wd_bf[...], preferred_element_type=F32))

        @pl.when(i == n_valid - 1)
        def _():
            for ahead in range(1, N_XBUF):
                slot = (cur + ahead) % N_XBUF
                _row_gather_wait(h2_hbm, bufs[slot], gsem.at[slot])

    for cur in range(N_XBUF):
        pl.when((i < n_valid) & (i % N_XBUF == cur))(functools.partial(step, cur))

    @pl.when(i >= n_valid)
    def _():
        o_ref[...] = jnp.zeros_like(o_ref)


def _experts(h2, tile_expert, n_valid, src_tok, wg, wu, wd, layer):
    grid_spec = pltpu.PrefetchScalarGridSpec(
        num_scalar_prefetch=3,
        grid=(NT_EXP,),
        in_specs=[
            pl.BlockSpec(memory_space=pl.ANY),
            pl.BlockSpec((None, None, D_MODEL, D_EXPERT), lambda i, te, nv, src: (layer, te[i], 0, 0)),
            pl.BlockSpec((None, None, D_MODEL, D_EXPERT), lambda i, te, nv, src: (layer, te[i], 0, 0)),
            pl.BlockSpec((None, None, D_EXPERT, D_MODEL), lambda i, te, nv, src: (layer, te[i], 0, 0)),
        ],
        out_specs=pl.BlockSpec((TM_EXP * TOK_STRIDE, LANES), lambda i, te, nv, src: (i, 0)),
        scratch_shapes=[pltpu.VMEM((TM_EXP * TOK_STRIDE, LANES), F32) for _ in range(N_XBUF)] + [
            pltpu.SemaphoreType.DMA((N_XBUF,)),
            pltpu.VMEM((D_MODEL, D_EXPERT), BF16),
            pltpu.VMEM((D_MODEL, D_EXPERT), BF16),
            pltpu.VMEM((D_EXPERT, D_MODEL), BF16),
        ],
    )
    return pl.pallas_call(
        _experts_kernel,
        grid_spec=grid_spec,
        out_shape=jax.ShapeDtypeStruct((NP_EXP * TOK_STRIDE, LANES), F32),
        compiler_params=_cparams(("arbitrary",)),
        name="experts",
    )(tile_expert, n_valid, src_tok, h2, wg, wu, wd)


def _final_kernel(*refs, emit_h):
    if emit_h:
        (pos_ref, x1_ref, ys_hbm, ewt_ref, gate2_ref, lng_ref, lnb_ref, sc1_ref, sh1_ref,
         x_ref, h_ref, rbuf0, rbuf1, sem) = refs
    else:
        (pos_ref, x1_ref, ys_hbm, ewt_ref, gate2_ref, lng_ref, lnb_ref,
         xc_ref, xl_ref, rbuf0, rbuf1, sem) = refs
    tm = TM_OUT
    i = pl.program_id(0)
    n_blocks = M_TOK // tm
    bufs = (rbuf0, rbuf1)

    def start(tile, buf, buf_sem):
        _row_gather_start(ys_hbm, buf, buf_sem, pos_ref, tile * tm, tm, tok0=0)
        _row_gather_start(ys_hbm, buf, buf_sem, pos_ref, M_TOK + tile * tm, tm, tok0=tm)

    @pl.when(i == 0)
    def _():
        start(0, rbuf0, sem.at[0])

    def step(cur):
        buf = bufs[cur]
        nxt_buf, nxt_sem = bufs[1 - cur], sem.at[1 - cur]
        _row_gather_wait(ys_hbm, buf, sem.at[cur])
        start(jnp.minimum(i + 1, n_blocks - 1), nxt_buf, nxt_sem)
        w = ewt_ref[...]
        y2 = w[:, 0:1] * _load_token_tiles(buf, 0, tm) + w[:, 1:2] * _load_token_tiles(buf, tm, tm)
        x2 = _layer_norm_rows(DEEPNORM_ALPHA * x1_ref[...] + gate2_ref[...] * y2,
                              lng_ref[...], lnb_ref[...])
        if emit_h:
            x_ref[...] = x2
            h_ref[...] = (x2 * (1.0 + sc1_ref[...]) + sh1_ref[...]).astype(BF16)
        else:
            @pl.when(i < M_CTX // tm)
            def _():
                xc_ref[...] = x2

            @pl.when(i >= M_CTX // tm)
            def _():
                xl_ref[...] = x2

        @pl.when(i == n_blocks - 1)
        def _():
            _row_gather_wait(ys_hbm, nxt_buf, nxt_sem)

    for cur in range(2):
        pl.when(i % 2 == cur)(functools.partial(step, cur))


def _final(x1, ys, pos, ewt, ada5, ln_g, ln_b, ada5_next):
    tm = TM_OUT
    emit_h = ada5_next is not None
    vec_spec = pl.BlockSpec((1, D_MODEL), lambda i, pos: (0, 0))
    ada_spec = lambda chunk: pl.BlockSpec(
        (None, None, 1, D_MODEL), lambda i, pos: (_ada_row(i, tm), chunk, 0, 0))
    in_specs = [
        pl.BlockSpec((tm, D_MODEL), lambda i, pos: (i, 0)),
        pl.BlockSpec(memory_space=pl.ANY),
        pl.BlockSpec((tm, LANES), lambda i, pos: (i, 0)),
        ada_spec(5),
        vec_spec, vec_spec,
    ]
    args = [pos, x1, ys, ewt, ada5, ln_g.reshape(1, D_MODEL), ln_b.reshape(1, D_MODEL)]
    if emit_h:
        in_specs += [ada_spec(1), ada_spec(0)]
        args += [ada5_next, ada5_next]
        row_spec = pl.BlockSpec((tm, D_MODEL), lambda i, pos: (i, 0))
        out_specs = [row_spec, row_spec]
        out_shape = [jax.ShapeDtypeStruct((M_TOK, D_MODEL), F32),
                     jax.ShapeDtypeStruct((M_TOK, D_MODEL), BF16)]
    else:
        out_specs = [
            pl.BlockSpec((tm, D_MODEL), lambda i, pos: (_ctx_block(i, tm), 0)),
            pl.BlockSpec((tm, D_MODEL), lambda i, pos: (_lat_block(i, tm), 0)),
        ]
        out_shape = [jax.ShapeDtypeStruct((M_CTX, D_MODEL), F32),
                     jax.ShapeDtypeStruct((M_LAT, D_MODEL), F32)]
    grid_spec = pltpu.PrefetchScalarGridSpec(
        num_scalar_prefetch=1,
        grid=(M_TOK // tm,),
        in_specs=in_specs,
        out_specs=out_specs,
        scratch_shapes=[
            pltpu.VMEM((2 * tm * TOK_STRIDE, LANES), F32),
            pltpu.VMEM((2 * tm * TOK_STRIDE, LANES), F32),
            pltpu.SemaphoreType.DMA((2,)),
        ],
    )
    return pl.pallas_call(
        functools.partial(_final_kernel, emit_h=emit_h),
        grid_spec=grid_spec,
        out_shape=out_shape,
        compiler_params=_cparams(("arbitrary",)),
        name="final_ln",
    )(*args)


def _rope_tables():
    rows = T_LAT // GRID_W
    row = jnp.repeat(jnp.arange(rows), GRID_W).astype(F32)
    col = jnp.tile(jnp.arange(GRID_W), rows).astype(F32)
    n_freq = HEAD_DIM // 4
    inv_freq = ROPE_BASE ** (-jnp.arange(n_freq, dtype=F32) / n_freq)
    ang = jnp.concatenate([row[:, None] * inv_freq[None], col[:, None] * inv_freq[None]], axis=-1)
    cos, sin = jnp.cos(ang), jnp.sin(ang)
    return jnp.concatenate([cos, cos], axis=-1), jnp.concatenate([-sin, sin], axis=-1)


def kernel(x_prompt, x_sample, state_retention, c, c_ctx, w_ada, b_ada, w_in, w_out, conv_w, pool_w,
           pool_scale, ret_decay_logit, ret_gn_gain, ln1_g, ln1_b, ln2_g, ln2_b, w_router, router_bias,
           w_gate, w_up, w_down):
    x_ctx = x_prompt.reshape(M_CTX, D_MODEL)
    x_lat = x_sample.reshape(M_LAT, D_MODEL)
    c_all = jnp.concatenate(
        [c_ctx[None, :], c, jnp.zeros((ADA_ROWS - 1 - N_LAT_SEQ, D_MODEL), F32)], axis=0)
    ada = _ada_table(c_all, w_ada, b_ada).reshape(DEPTH, ADA_ROWS, 6, 1, D_MODEL)

    cos_lat, sin_lat = _rope_tables()
    w_router_pad = jnp.pad(w_router, ((0, 0), (0, LANES - N_EXPERTS)))

    states = None
    x, h = _modulate(x_ctx, x_lat, ada[0])
    for l in range(DEPTH):
        ada5 = ada[l]
        z = _in_proj(h, w_in, l)

        ycp = _conv_pool(z, conv_w[l], pool_w[l], pool_scale[l])

        dl = jnp.broadcast_to(ret_decay_logit[l][:, :, None, None], (2, N_RET_HEADS, 1, HEAD_DIM))
        yret, states = _retention(z, cos_lat, sin_lat, dl, ret_gn_gain[l], state_retention, l, states)

        x1, h2, ei, ewt, rk, cnt = _out_proj(ycp, yret, x, w_out, l, ada5,
                                             ln1_g[l], ln1_b[l], w_router_pad, router_bias)
        pos = _pair_rows(ei, rk, cnt).reshape(-1)
        src_tok, te, n_valid = _route_tables(cnt[:, 0].astype(I32), pos)
        ys = _experts(h2, te, n_valid, src_tok, w_gate, w_up, w_down, l)
        if l + 1 < DEPTH:
            x, h = _final(x1, ys, pos, ewt, ada5, ln2_g[l], ln2_b[l], ada[l + 1])
        else:
            x_ctx, x_lat = _final(x1, ys, pos, ewt, ada5, ln2_g[l], ln2_b[l], None)

    y_prompt = x_ctx.reshape(N_CTX_SEQ, T_CTX, D_MODEL)
    y_sample = x_lat.reshape(N_LAT_SEQ, T_LAT, D_MODEL)
    return y_prompt, y_sample, states
```

```python
import functools

import jax
import jax.numpy as jnp
from jax import lax
from jax.experimental import pallas as pl
from jax.experimental.pallas import tpu as pltpu

F32 = jnp.float32
BF16 = jnp.bfloat16
I32 = jnp.int32

D_MODEL = 2048
N_CTX_SEQ, T_CTX = 16, 256
N_LAT_SEQ, T_LAT = 8, 1024
DEPTH = 2
M_CTX = N_CTX_SEQ * T_CTX
M_LAT = N_LAT_SEQ * T_LAT
M_TOK = M_CTX + M_LAT

GRID_W = 64
D_CONV = D_MODEL // 4
D_POOL = D_MODEL // 4
D_RET = D_MODEL // 2
N_RET_HEADS = 8
HEAD_DIM = D_RET // N_RET_HEADS
POOL_WINDOWS = (2, 4, 8, 16)
POOL_GROUP_DIM = D_POOL // len(POOL_WINDOWS)
CHUNK = 128
ROPE_BASE = 10000.0
N_EXPERTS = 16
EXPERTS_PER_GROUP = 4
N_EXPERT_GROUPS = N_EXPERTS // EXPERTS_PER_GROUP
D_EXPERT = D_MODEL // 4
D_IN_PROJ = 3 * D_CONV + D_POOL + 4 * D_RET
DEEPNORM_ALPHA = (2.0 * DEPTH) ** 0.25
LN_EPS = 1e-5
ADA_ROWS = 16

LANES = 128
VMEM_LIMIT = 56 * 1024 * 1024

TM_IN = 1024
TN_IN = 1536
TOK_ROWS = D_MODEL // LANES
TOK_STRIDE = TOK_ROWS + 1
TM_OP = 512
SUB_OP = 256
TM_OUT = 512
TM_EXP = 256
N_PAIR = 2 * M_TOK
NP_EXP = N_PAIR + N_EXPERTS * TM_EXP
NT_EXP = NP_EXP // TM_EXP


def _cparams(sem):
    return pltpu.CompilerParams(dimension_semantics=sem, vmem_limit_bytes=VMEM_LIMIT)


def _silu(x):
    return x * jax.nn.sigmoid(x)


def _ada_row(i, tm):
    n_ctx_tiles = M_CTX // tm
    per_batch = T_LAT // tm
    return jnp.where(i < n_ctx_tiles, 0, 1 + (i - n_ctx_tiles) // per_batch)


def _ada_kernel(c_ref, w_ref, b_ref, o_ref):
    s = _silu(c_ref[...]).astype(BF16)
    o_ref[...] = jnp.dot(s, w_ref[...].astype(BF16), preferred_element_type=F32) + b_ref[...]


def _ada_table(c_all, w_ada, b_ada):
    tn = 1024
    n6 = 6 * D_MODEL
    return pl.pallas_call(
        _ada_kernel,
        grid=(DEPTH, n6 // tn),
        in_specs=[
            pl.BlockSpec((ADA_ROWS, D_MODEL), lambda l, j: (0, 0)),
            pl.BlockSpec((None, D_MODEL, tn), lambda l, j: (l, 0, j)),
            pl.BlockSpec((None, 1, tn), lambda l, j: (l, 0, j)),
        ],
        out_specs=pl.BlockSpec((None, ADA_ROWS, tn), lambda l, j: (l, 0, j)),
        out_shape=jax.ShapeDtypeStruct((DEPTH, ADA_ROWS, n6), F32),
        compiler_params=_cparams(("arbitrary", "arbitrary")),
        name="ada_table",
    )(c_all, w_ada, b_ada.reshape(DEPTH, 1, n6))


def _ctx_block(i, tm):
    return jnp.minimum(i, M_CTX // tm - 1)


def _lat_block(i, tm):
    return jnp.maximum(i - M_CTX // tm, 0)


def _modulate_kernel(xc_ref, xl_ref, sc_ref, sh_ref, x_ref, h_ref):
    x = jnp.where(pl.program_id(0) < M_CTX // xc_ref.shape[0], xc_ref[...], xl_ref[...])
    x_ref[...] = x
    h_ref[...] = (x * (1.0 + sc_ref[...]) + sh_ref[...]).astype(BF16)


def _modulate(x_ctx, x_lat, ada5):
    tm = TM_OUT
    ada_spec = lambda chunk: pl.BlockSpec(
        (None, None, 1, D_MODEL), lambda i: (_ada_row(i, tm), chunk, 0, 0))
    return pl.pallas_call(
        _modulate_kernel,
        grid=(M_TOK // tm,),
        in_specs=[
            pl.BlockSpec((tm, D_MODEL), lambda i: (_ctx_block(i, tm), 0)),
            pl.BlockSpec((tm, D_MODEL), lambda i: (_lat_block(i, tm), 0)),
            ada_spec(1), ada_spec(0),
        ],
        out_specs=[pl.BlockSpec((tm, D_MODEL), lambda i: (i, 0)),
                   pl.BlockSpec((tm, D_MODEL), lambda i: (i, 0))],
        out_shape=[jax.ShapeDtypeStruct((M_TOK, D_MODEL), F32),
                   jax.ShapeDtypeStruct((M_TOK, D_MODEL), BF16)],
        compiler_params=_cparams(("arbitrary",)),
        name="modulate",
    )(x_ctx, x_lat, ada5, ada5)


def _inproj_kernel(h_ref, w_ref, o_ref, wbf_scr):
    @pl.when(pl.program_id(1) == 0)
    def _():
        wbf_scr[...] = w_ref[...].astype(BF16)

    o_ref[...] = jnp.dot(h_ref[...], wbf_scr[...], preferred_element_type=F32).astype(BF16)


def _in_proj(h, w_in, layer):
    return pl.pallas_call(
        _inproj_kernel,
        grid=(D_IN_PROJ // TN_IN, M_TOK // TM_IN),
        in_specs=[
            pl.BlockSpec((TM_IN, D_MODEL), lambda j, i: (i, 0)),
            pl.BlockSpec((None, D_MODEL, TN_IN), lambda j, i: (layer, 0, j)),
        ],
        out_specs=pl.BlockSpec((TM_IN, TN_IN), lambda j, i: (i, j)),
        out_shape=jax.ShapeDtypeStruct((M_TOK, D_IN_PROJ), BF16),
        scratch_shapes=[pltpu.VMEM((D_MODEL, TN_IN), BF16)],
        compiler_params=_cparams(("arbitrary", "arbitrary")),
        name="in_proj",
    )(h, w_in)


MIX_ROWS = T_LAT
N_CTX_BLOCKS = M_CTX // MIX_ROWS
SEQ_PER_CTX_BLOCK = MIX_ROWS // T_CTX
CP_ROWS = T_CTX
CP_HALO = 128


def _convpool_kernel(z_ref, cw_ref, pw_ref, ps_ref, o_ref, band_ref):
    blk = pl.program_id(0)
    g_dim = POOL_GROUP_DIM

    @pl.when(blk == 0)
    def _():
        row = lax.broadcasted_iota(I32, (CP_ROWS, CP_ROWS + 2 * CP_HALO), 0)
        col = lax.broadcasted_iota(I32, (CP_ROWS, CP_ROWS + 2 * CP_HALO), 1)
        d = col - CP_HALO - row
        for gi, w in enumerate(POOL_WINDOWS):
            band_ref[gi] = jnp.where((d >= -(w // 2)) & (d < w // 2), 1.0, 0.0).astype(BF16)

    def block(seq_len):
        t = lax.broadcasted_iota(I32, (CP_ROWS, LANES), 0)
        for ch in range(MIX_ROWS // CP_ROWS):
            r0 = ch * CP_ROWS
            rows = slice(r0, r0 + CP_ROWS)
            pos0 = r0 % seq_len
            at_start = pos0 == 0
            at_end = pos0 + CP_ROWS == seq_len

            for cg in range(D_CONV // LANES):
                lanes = slice(cg * LANES, (cg + 1) * LANES)

                def u_rows(rs):
                    return (z_ref[rs, D_CONV + cg * LANES:D_CONV + (cg + 1) * LANES].astype(F32)
                            * z_ref[rs, 2 * D_CONV + cg * LANES:2 * D_CONV + (cg + 1) * LANES].astype(F32))
                u = u_rows(rows)
                before = 0.0 if at_start else u_rows(slice(r0 - 1, r0))
                after = 0.0 if at_end else u_rows(slice(r0 + CP_ROWS, r0 + CP_ROWS + 1))
                u_prev = jnp.where(t == 0, before, pltpu.roll(u, 1, 0))
                u_next = jnp.where(t == CP_ROWS - 1, after, pltpu.roll(u, CP_ROWS - 1, 0))
                conv = u_prev * cw_ref[0:1, lanes] + u * cw_ref[1:2, lanes] + u_next * cw_ref[2:3, lanes]
                o_ref[rows, lanes] = (z_ref[rows, lanes].astype(F32) * conv).astype(BF16)

            k_rows = slice(r0 if at_start else r0 - CP_HALO,
                           r0 + CP_ROWS if at_end else r0 + CP_ROWS + CP_HALO)
            b_cols = slice(CP_HALO if at_start else 0,
                           CP_HALO + CP_ROWS if at_end else CP_ROWS + 2 * CP_HALO)
            tpos = pos0 + t
            for gi, w in enumerate(POOL_WINDOWS):
                lo = 3 * D_CONV + gi * g_dim
                win = jnp.dot(band_ref[gi, :, b_cols], z_ref[k_rows, lo:lo + g_dim],
                              preferred_element_type=F32)
                cnt = (jnp.minimum(tpos + w // 2, seq_len) - jnp.maximum(tpos - w // 2, 0)).astype(F32)
                pooled = win / cnt - z_ref[rows, lo:lo + g_dim].astype(F32)
                y = jnp.dot(pooled.astype(BF16), pw_ref[gi].astype(BF16), preferred_element_type=F32)
                y = y * ps_ref[:, gi * g_dim:(gi + 1) * g_dim]
                o_ref[rows, D_CONV + gi * g_dim:D_CONV + (gi + 1) * g_dim] = y.astype(BF16)

    pl.when(blk < N_CTX_BLOCKS)(functools.partial(block, T_CTX))
    pl.when(blk >= N_CTX_BLOCKS)(functools.partial(block, T_LAT))


def _conv_pool(z, conv_w, pool_w, pool_scale):
    return pl.pallas_call(
        _convpool_kernel,
        grid=(M_TOK // MIX_ROWS,),
        in_specs=[
            pl.BlockSpec((MIX_ROWS, D_IN_PROJ - 4 * D_RET), lambda b: (b, 0)),
            pl.BlockSpec((3, D_CONV), lambda b: (0, 0)),
            pl.BlockSpec((len(POOL_WINDOWS), POOL_GROUP_DIM, POOL_GROUP_DIM), lambda b: (0, 0, 0)),
            pl.BlockSpec((1, D_POOL), lambda b: (0, 0)),
        ],
        out_specs=pl.BlockSpec((MIX_ROWS, D_CONV + D_POOL), lambda b: (b, 0)),
        out_shape=jax.ShapeDtypeStruct((M_TOK, D_CONV + D_POOL), BF16),
        scratch_shapes=[pltpu.VMEM((len(POOL_WINDOWS), CP_ROWS, CP_ROWS + 2 * CP_HALO), BF16)],
        compiler_params=_cparams(("arbitrary",)),
        name="conv_pool",
    )(z, conv_w, pool_w, pool_scale.reshape(1, D_POOL))


def _log_sigmoid(x):
    return jnp.minimum(x, 0.0) - jnp.log1p(jnp.exp(-jnp.abs(x)))


RET_ROWS = MIX_ROWS


def _retention_kernel(*refs, layer):
    if layer == 0:
        (q_ref, k_ref, v_ref, g_ref, cos_ref, sin_ref, dl_ref, gain_ref, s0_ref,
         y_ref, sfin_ref, tab_scr, kr_scr, u_scr, s_scr) = refs
    else:
        (q_ref, k_ref, v_ref, g_ref, cos_ref, sin_ref, dl_ref, gain_ref, s0_ref, _,
         y_ref, sfin_ref, tab_scr, kr_scr, u_scr, s_scr) = refs
    blk = pl.program_id(1)
    n_chunks = RET_ROWS // CHUNK
    half = HEAD_DIM // 2

    lg_f = _log_sigmoid(dl_ref[0])
    lg_b = _log_sigmoid(dl_ref[1])
    g_f = jnp.exp(lg_f * CHUNK)
    g_b = jnp.exp(lg_b * CHUNK)

    @pl.when(blk == 0)
    def _():
        row = lax.broadcasted_iota(I32, (CHUNK, CHUNK), 0).astype(F32)
        col = lax.broadcasted_iota(I32, (CHUNK, CHUNK), 1).astype(F32)
        diff = row - col
        tab_scr[0] = (jnp.where(diff >= 0, jnp.exp(lg_f * jnp.maximum(diff, 0.0)), 0.0)
                      + jnp.where(diff <= 0, jnp.exp(lg_b * jnp.maximum(-diff, 0.0)), 0.0))
        tab_scr[1] = jnp.exp(lg_f * (row + 1.0))
        tab_scr[2] = jnp.exp(lg_b * (CHUNK - row))
        tab_scr[3] = jnp.exp(lg_f * (CHUNK - 1.0 - row))
        tab_scr[4] = jnp.exp(lg_b * row)

    tn_dims = (((0,), (0,)), ((), ()))
    nt_dims = (((1,), (1,)), ((), ()))

    def block(seq_chunks, latent):
        def rope(x, sl):
            if not latent:
                return x
            return x * cos_ref[sl, :] + pltpu.roll(x, half, 1) * sin_ref[sl, :]

        for c in range(n_chunks):
            sl = slice(c * CHUNK, (c + 1) * CHUNK)
            kr = rope(k_ref[sl, :].astype(F32), sl) * (HEAD_DIM ** -0.5)
            kr_scr[sl, :] = kr.astype(BF16)
            kz = jnp.concatenate([kr * tab_scr[3], kr * tab_scr[4]], axis=1).astype(BF16)
            u_scr[c] = lax.dot_general(kz, v_ref[sl, :], tn_dims, preferred_element_type=F32)

        has_state = [[False] * n_chunks, [False] * n_chunks]
        for s_i in range(n_chunks // seq_chunks):
            chunks = list(range(s_i * seq_chunks, (s_i + 1) * seq_chunks))
            for d, order, g_d in ((0, chunks, g_f), (1, chunks[::-1], g_b)):
                s = s0_ref[d] if latent else None
                for c in order:
                    u = u_scr[c, d * HEAD_DIM:(d + 1) * HEAD_DIM, :]
                    if s is None:
                        s = u
                    else:
                        s_scr[d, c] = s.astype(BF16)
                        has_state[d][c] = True
                        s = g_d * s + u
                if not latent:
                    if layer == 0:
                        sfin_ref[s_i, 0, d] = s
                        for later in range(1, DEPTH):
                            sfin_ref[s_i, later, d] = jnp.zeros_like(s)
                    else:
                        sfin_ref[s_i, d] = s

        for c in range(n_chunks):
            sl = slice(c * CHUNK, (c + 1) * CHUNK)
            qr = rope(q_ref[sl, :].astype(F32), sl)
            scores = lax.dot_general(qr.astype(BF16), kr_scr[sl, :], nt_dims, preferred_element_type=F32)
            o = jnp.dot((scores * tab_scr[0]).astype(BF16), v_ref[sl, :], preferred_element_type=F32)
            for d in range(2):
                if has_state[d][c]:
                    o += jnp.dot((qr * tab_scr[1 + d]).astype(BF16), s_scr[d, c],
                                 preferred_element_type=F32)
            mu = jnp.mean(o, axis=-1, keepdims=True)
            dev = o - mu
            var = jnp.mean(dev * dev, axis=-1, keepdims=True)
            on = dev * lax.rsqrt(var + LN_EPS) * gain_ref[...]
            y_ref[sl, :] = (_silu(g_ref[sl, :].astype(F32)) * on).astype(BF16)

    pl.when(blk < N_CTX_BLOCKS)(functools.partial(block, T_CTX // CHUNK, False))
    pl.when(blk >= N_CTX_BLOCKS)(functools.partial(block, T_LAT // CHUNK, True))


def _retention(z, cos_t, sin_t, decay_logit_b, gain, s0, layer, states):
    qcol0 = (3 * D_CONV + D_POOL) // HEAD_DIM
    n_blocks = M_TOK // RET_ROWS

    def zspec(k):
        return pl.BlockSpec((RET_ROWS, HEAD_DIM), lambda h, b: (b, qcol0 + k * N_RET_HEADS + h))

    def lat_seq(b):
        return jnp.maximum(b - N_CTX_BLOCKS, 0)

    def ctx_block(b):
        return jnp.minimum(b, N_CTX_BLOCKS - 1)

    n_chunks = RET_ROWS // CHUNK
    in_specs = [
        zspec(0), zspec(1), zspec(2), zspec(3),
        pl.BlockSpec((RET_ROWS, HEAD_DIM), lambda h, b: (0, 0)),
        pl.BlockSpec((RET_ROWS, HEAD_DIM), lambda h, b: (0, 0)),
        pl.BlockSpec((2, None, 1, HEAD_DIM), lambda h, b: (0, h, 0, 0)),
        pl.BlockSpec((1, HEAD_DIM), lambda h, b: (0, h)),
        pl.BlockSpec((None, None, 2, None, HEAD_DIM, HEAD_DIM),
                     lambda h, b: (lat_seq(b), layer, 0, h, 0, 0)),
    ]
    args = [z, z, z, z, cos_t, sin_t, decay_logit_b, gain.reshape(1, D_RET), s0]
    if layer == 0:
        st_spec = pl.BlockSpec((SEQ_PER_CTX_BLOCK, DEPTH, 2, None, HEAD_DIM, HEAD_DIM),
                               lambda h, b: (ctx_block(b), 0, 0, h, 0, 0))
        aliases = {}
    else:
        st_spec = pl.BlockSpec((SEQ_PER_CTX_BLOCK, None, 2, None, HEAD_DIM, HEAD_DIM),
                               lambda h, b: (ctx_block(b), layer, 0, h, 0, 0))
        in_specs.append(pl.BlockSpec(memory_space=pl.ANY))
        args.append(states)
        aliases = {len(args) - 1: 1}
    return pl.pallas_call(
        functools.partial(_retention_kernel, layer=layer),
        grid=(N_RET_HEADS, n_blocks),
        in_specs=in_specs,
        out_specs=[pl.BlockSpec((RET_ROWS, HEAD_DIM), lambda h, b: (b, h)), st_spec],
        out_shape=[
            jax.ShapeDtypeStruct((M_TOK, D_RET), BF16),
            jax.ShapeDtypeStruct((N_CTX_SEQ, DEPTH, 2, N_RET_HEADS, HEAD_DIM, HEAD_DIM), F32),
        ],
        input_output_aliases=aliases,
        scratch_shapes=[
            pltpu.VMEM((5, CHUNK, CHUNK), F32),
            pltpu.VMEM((RET_ROWS, HEAD_DIM), BF16),
            pltpu.VMEM((n_chunks, 2 * HEAD_DIM, HEAD_DIM), F32),
            pltpu.VMEM((2, n_chunks, HEAD_DIM, HEAD_DIM), BF16),
        ],
        compiler_params=_cparams(("arbitrary", "arbitrary")),
        name="retention",
    )(*args)


def _layer_norm_rows(r, g, b):
    mu = jnp.mean(r, axis=-1, keepdims=True)
    dev = r - mu
    var = jnp.mean(dev * dev, axis=-1, keepdims=True)
    return dev * lax.rsqrt(var + LN_EPS) * g + b


def _top2_of4(vals):
    top1 = jnp.maximum(jnp.maximum(vals[0], vals[1]), jnp.maximum(vals[2], vals[3]))
    idx1 = jnp.where(vals[0] == top1, 0, jnp.where(vals[1] == top1, 1, jnp.where(vals[2] == top1, 2, 3)))
    neg = jnp.float32(-jnp.inf)
    rest = [jnp.where(idx1 == j, neg, vals[j]) for j in range(4)]
    top2 = jnp.maximum(jnp.maximum(rest[0], rest[1]), jnp.maximum(rest[2], rest[3]))
    idx2 = jnp.where(rest[0] == top2, 0, jnp.where(rest[1] == top2, 1, jnp.where(rest[2] == top2, 2, 3)))
    return top1, idx1, top2, idx2


def _store_token_tiles(ref, val, tok0=0):
    n = val.shape[0]
    base = tok0 * TOK_STRIDE
    for c in range(TOK_ROWS):
        ref[pl.ds(base + c, n, stride=TOK_STRIDE), :] = val[:, c * LANES:(c + 1) * LANES]
    ref[pl.ds(base + TOK_ROWS, n, stride=TOK_STRIDE), :] = jnp.zeros((n, LANES), val.dtype)


def _load_token_tiles(ref, tok0, n):
    return jnp.concatenate(
        [ref[pl.ds(tok0 * TOK_STRIDE + c, n, stride=TOK_STRIDE), :] for c in range(TOK_ROWS)], axis=1)


def _outproj_kernel(ycp_ref, yret_ref, x_ref, wo_hbm, gate1_ref, lng_ref, lnb_ref, sc2_ref, sh2_ref,
                    wr_ref, rb_ref, x1_ref, h2_ref, ei_ref, ewt_ref, rk_ref, cnt_ref,
                    carry_scr, wo_ref, wstage, wsem, wr_hi, wr_lo, *, layer):
    i = pl.program_id(0)

    @pl.when(i == 0)
    def _():
        carry_scr[...] = jnp.zeros_like(carry_scr)
        wr = wr_ref[...]
        hi = wr.astype(BF16)
        wr_hi[...] = hi
        wr_lo[...] = (wr - hi.astype(F32)).astype(BF16)
        rows = wstage.shape[1]
        n_chunks = D_MODEL // rows

        def chunk_copy(c):
            return pltpu.make_async_copy(wo_hbm.at[layer, pl.ds(c * rows, rows), :], wstage.at[c % 2],
                                         wsem.at[c % 2])
        chunk_copy(0).start()
        for c in range(n_chunks):
            if c + 1 < n_chunks:
                chunk_copy(c + 1).start()
            chunk_copy(c).wait()
            wo_ref[c * rows:(c + 1) * rows, :] = wstage[c % 2].astype(BF16)

    s_i = lax.broadcasted_iota(I32, (SUB_OP, SUB_OP), 0)
    t_i = lax.broadcasted_iota(I32, (SUB_OP, SUB_OP), 1)
    tri = jnp.where(s_i < t_i, 1.0, 0.0).astype(BF16)
    half_k = D_CONV + D_POOL
    n_sub = x_ref.shape[0] // SUB_OP
    ys = []
    for sub in range(n_sub):
        rows_sl = slice(sub * SUB_OP, (sub + 1) * SUB_OP)
        y = jnp.dot(ycp_ref[rows_sl, :], wo_ref[0:half_k, :], preferred_element_type=F32)
        ys.append(y + jnp.dot(yret_ref[rows_sl, :], wo_ref[half_k:, :], preferred_element_type=F32))
    lts = [_ln_router_subtile(sub, ys[sub], x_ref, gate1_ref, lng_ref, lnb_ref, sc2_ref, sh2_ref,
                              wr_hi, wr_lo, x1_ref, h2_ref) for sub in range(n_sub)]
    carry = carry_scr[:, 0:1]
    for sub in range(n_sub):
        carry = _route_subtile(sub, lts[sub], carry, tri, rb_ref, ei_ref, ewt_ref, rk_ref)
    carry_scr[...] = jnp.broadcast_to(carry, carry_scr.shape)
    cnt_ref[...] = carry_scr[...]


def _ln_router_subtile(sub, y, x_ref, gate1_ref, lng_ref, lnb_ref, sc2_ref, sh2_ref, wr_hi, wr_lo,
                       x1_ref, h2_ref):
    tm = SUB_OP
    rows_sl = slice(sub * tm, (sub + 1) * tm)
    x1 = _layer_norm_rows(DEEPNORM_ALPHA * x_ref[rows_sl, :] + gate1_ref[...] * y,
                          lng_ref[...], lnb_ref[...])
    x1_ref[rows_sl, :] = x1
    h2 = x1 * (1.0 + sc2_ref[...]) + sh2_ref[...]
    _store_token_tiles(h2_ref, h2, tok0=sub * tm)

    h_hi = h2.astype(BF16)
    h_lo = (h2 - h_hi.astype(F32)).astype(BF16)
    logits = (jnp.dot(h_hi, wr_hi[...], preferred_element_type=F32)
              + jnp.dot(h_lo, wr_hi[...], preferred_element_type=F32)
              + jnp.dot(h_hi, wr_lo[...], preferred_element_type=F32))
    return logits.T


def _route_subtile(sub, lt, carry, tri, rb_ref, ei_ref, ewt_ref, rk_ref):
    tm = SUB_OP
    rows_sl = slice(sub * tm, (sub + 1) * tm)
    rows = [lt[e:e + 1, :] for e in range(N_EXPERTS)]

    mx = rows[0]
    for e in range(1, N_EXPERTS):
        mx = jnp.maximum(mx, rows[e])
    ex = [jnp.exp(r - mx) for r in rows]
    den = ex[0]
    for e in range(1, N_EXPERTS):
        den = den + ex[e]
    score = [x / den for x in ex]
    biased = [score[e] + rb_ref[e] for e in range(N_EXPERTS)]

    best = None
    for gi in range(N_EXPERT_GROUPS):
        t1, i1, t2, i2 = _top2_of4(biased[gi * EXPERTS_PER_GROUP:(gi + 1) * EXPERTS_PER_GROUP])
        gs = t1 + t2
        e1 = gi * EXPERTS_PER_GROUP + i1
        e2 = gi * EXPERTS_PER_GROUP + i2
        if best is None:
            best = (gs, e1, e2)
        else:
            take = gs > best[0]
            best = (jnp.where(take, gs, best[0]), jnp.where(take, e1, best[1]), jnp.where(take, e2, best[2]))
    _, e1, e2 = best
    zero = jnp.zeros_like(score[0])
    w1 = zero
    w2 = zero
    for e in range(N_EXPERTS):
        w1 = w1 + jnp.where(e1 == e, score[e], 0.0)
        w2 = w2 + jnp.where(e2 == e, score[e], 0.0)
    wsum = w1 + w2
    ei_ref[0:1, rows_sl] = e1
    ei_ref[1:2, rows_sl] = e2
    ewt_ref[rows_sl, :] = jnp.concatenate(
        [w1 / wsum, w2 / wsum, jnp.zeros((LANES - 2, tm), F32)], axis=0).T

    onehot = jnp.concatenate(
        [jnp.where((e1 == e) | (e2 == e), 1.0, 0.0) for e in range(N_EXPERTS)], axis=0)
    prefix = jnp.dot(onehot.astype(BF16), tri, preferred_element_type=F32) + carry
    r1 = zero
    r2 = zero
    for e in range(N_EXPERTS):
        r1 = r1 + jnp.where(e1 == e, prefix[e:e + 1, :], 0.0)
        r2 = r2 + jnp.where(e2 == e, prefix[e:e + 1, :], 0.0)
    rk_ref[0:1, rows_sl] = r1.astype(I32)
    rk_ref[1:2, rows_sl] = r2.astype(I32)
    return carry + jnp.sum(onehot, axis=1, keepdims=True)


def _out_proj(ycp, yret, x, w_out, layer, ada5, ln_g, ln_b, w_router_pad, router_bias):
    tm = TM_OP
    ada_spec = lambda chunk: pl.BlockSpec(
        (None, None, 1, D_MODEL), lambda i: (_ada_row(i, tm), chunk, 0, 0))
    vec_spec = pl.BlockSpec((1, D_MODEL), lambda i: (0, 0))
    route_spec = pl.BlockSpec((2, tm), lambda i: (0, i))
    return pl.pallas_call(
        functools.partial(_outproj_kernel, layer=layer),
        grid=(M_TOK // tm,),
        in_specs=[
            pl.BlockSpec((tm, D_CONV + D_POOL), lambda i: (i, 0)),
            pl.BlockSpec((tm, D_RET), lambda i: (i, 0)),
            pl.BlockSpec((tm, D_MODEL), lambda i: (i, 0)),
            pl.BlockSpec(memory_space=pl.ANY),
            ada_spec(2),
            vec_spec, vec_spec,
            ada_spec(4),
            ada_spec(3),
            pl.BlockSpec((D_MODEL, LANES), lambda i: (0, 0)),
            pl.BlockSpec(memory_space=pltpu.SMEM),
        ],
        out_specs=[
            pl.BlockSpec((tm, D_MODEL), lambda i: (i, 0)),
            pl.BlockSpec((tm * TOK_STRIDE, LANES), lambda i: (i, 0)),
            route_spec,
            pl.BlockSpec((tm, LANES), lambda i: (i, 0)),
            route_spec,
            pl.BlockSpec((N_EXPERTS, LANES), lambda i: (0, 0)),
        ],
        out_shape=[
            jax.ShapeDtypeStruct((M_TOK, D_MODEL), F32),
            jax.ShapeDtypeStruct((M_TOK * TOK_STRIDE, LANES), F32),
            jax.ShapeDtypeStruct((2, M_TOK), I32),
            jax.ShapeDtypeStruct((M_TOK, LANES), F32),
            jax.ShapeDtypeStruct((2, M_TOK), I32),
            jax.ShapeDtypeStruct((N_EXPERTS, LANES), F32),
        ],
        scratch_shapes=[
            pltpu.VMEM((N_EXPERTS, LANES), F32),
            pltpu.VMEM((D_MODEL, D_MODEL), BF16),
            pltpu.VMEM((2, 256, D_MODEL), F32),
            pltpu.SemaphoreType.DMA((2,)),
            pltpu.VMEM((D_MODEL, LANES), BF16),
            pltpu.VMEM((D_MODEL, LANES), BF16),
        ],
        compiler_params=_cparams(("arbitrary",)),
        name="out_proj_router",
    )(ycp, yret, x, w_out, ada5, ln_g.reshape(1, D_MODEL), ln_b.reshape(1, D_MODEL),
      ada5, ada5, w_router_pad, router_bias)


def _pos_kernel(ei_ref, rk_ref, cnt_ref, pos_ref):
    ei = ei_ref[...]
    pos = rk_ref[...]
    start = jnp.zeros((1, 1), F32)
    for e in range(N_EXPERTS):
        pos = pos + jnp.where(ei == e, start.astype(I32), 0)
        n_tiles = jnp.floor((cnt_ref[e:e + 1, 0:1] + (TM_EXP - 1.0)) * (1.0 / TM_EXP))
        start = start + n_tiles * TM_EXP
    pos_ref[...] = pos


def _pair_rows(ei, rk, cnt):
    return pl.pallas_call(
        _pos_kernel,
        out_shape=jax.ShapeDtypeStruct((2, M_TOK), I32),
        name="pair_rows",
    )(ei, rk, cnt)


def _route_kernel(cnt_ref, pos_ref, src_ref, te_ref, nv_ref):
    def zero(r, carry):
        src_ref[r] = 0
        return carry

    tile = jnp.int32(0)
    for e in range(N_EXPERTS):
        n_tiles = lax.shift_right_logical(cnt_ref[e] + (TM_EXP - 1), TM_EXP.bit_length() - 1)

        def mark(j, carry, e=e, tile=tile):
            te_ref[tile + j] = e
            return carry
        lax.fori_loop(0, n_tiles, mark, 0)
        lax.fori_loop(tile * TM_EXP + cnt_ref[e], (tile + n_tiles) * TM_EXP, zero, 0)
        tile = tile + n_tiles
    nv_ref[0] = tile
    last_expert = te_ref[tile - 1]

    def mark_unused(j, carry):
        te_ref[j] = last_expert
        return carry
    lax.fori_loop(tile, NT_EXP, mark_unused, 0)
    lax.fori_loop(tile * TM_EXP, NP_EXP, zero, 0)

    def place(t, carry):
        src_ref[pos_ref[t]] = t
        src_ref[pos_ref[M_TOK + t]] = t
        return carry
    lax.fori_loop(0, M_TOK, place, 0, unroll=16)


def _route_tables(cnt, pos):
    smem = pl.BlockSpec(memory_space=pltpu.SMEM)
    return pl.pallas_call(
        _route_kernel,
        in_specs=[smem, smem],
        out_specs=[smem, smem, smem],
        out_shape=[
            jax.ShapeDtypeStruct((NP_EXP,), I32),
            jax.ShapeDtypeStruct((NT_EXP,), I32),
            jax.ShapeDtypeStruct((1,), I32),
        ],
        name="route_tables",
    )(cnt, pos)


GATHER_PRIORITY = 1


def _row_gather_start(src_hbm, buf, sem, idx_ref, base, n_tok, tok0=0, priority=0):
    for r in range(n_tok):
        pltpu.make_async_copy(src_hbm.at[pl.ds(idx_ref[base + r] * TOK_STRIDE, TOK_ROWS), :],
                              buf.at[pl.ds((tok0 + r) * TOK_STRIDE, TOK_ROWS), :], sem).start(priority)


def _row_gather_wait(src_hbm, buf, sem):
    n_rows = buf.shape[0] // TOK_STRIDE * TOK_ROWS
    pltpu.make_async_copy(src_hbm.at[pl.ds(0, n_rows), :], buf.at[pl.ds(0, n_rows), :], sem).wait()


N_XBUF = 4


def _experts_kernel(te_ref, nv_ref, src_ref, h2_hbm, wg_ref, wu_ref, wd_ref, o_ref, *scratch):
    bufs = scratch[:N_XBUF]
    gsem, wg_bf, wu_bf, wd_bf = scratch[N_XBUF:]
    i = pl.program_id(0)
    n_valid = nv_ref[0]

    def start_tile(tile, slot):
        tile = jnp.minimum(tile, NT_EXP - 1)
        _row_gather_start(h2_hbm, bufs[slot], gsem.at[slot], src_ref, tile * TM_EXP, TM_EXP,
                          priority=GATHER_PRIORITY)

    @pl.when(i == 0)
    def _():
        for t in range(N_XBUF - 1):
            start_tile(t, t)

    @pl.when((i < n_valid) & ((i == 0) | (te_ref[i] != te_ref[jnp.maximum(i - 1, 0)])))
    def _():
        wg_bf[...] = wg_ref[...].astype(BF16)
        wu_bf[...] = wu_ref[...].astype(BF16)
        wd_bf[...] = wd_ref[...].astype(BF16)

    def step(cur):
        _row_gather_wait(h2_hbm, bufs[cur], gsem.at[cur])
        start_tile(i + N_XBUF - 1, (cur + N_XBUF - 1) % N_XBUF)
        x = _load_token_tiles(bufs[cur], 0, TM_EXP).astype(BF16)
        g = jnp.dot(x, wg_bf[...], preferred_element_type=F32)
        u = jnp.dot(x, wu_bf[...], preferred_element_type=F32)
        a = (_silu(g) * u).astype(BF16)
        _store_token_tiles(o_ref, jnp.dot(a, wd_bf[...], preferred_element_type=F32))

        @pl.when(i == n_valid - 1)
        def _():
            for ahead in range(1, N_XBUF):
                slot = (cur + ahead) % N_XBUF
                _row_gather_wait(h2_hbm, bufs[slot], gsem.at[slot])

    for cur in range(N_XBUF):
        pl.when((i < n_valid) & (i % N_XBUF == cur))(functools.partial(step, cur))

    @pl.when(i >= n_valid)
    def _():
        o_ref[...] = jnp.zeros_like(o_ref)


def _experts(h2, tile_expert, n_valid, src_tok, wg, wu, wd, layer):
    grid_spec = pltpu.PrefetchScalarGridSpec(
        num_scalar_prefetch=3,
        grid=(NT_EXP,),
        in_specs=[
            pl.BlockSpec(memory_space=pl.ANY),
            pl.BlockSpec((None, None, D_MODEL, D_EXPERT), lambda i, te, nv, src: (layer, te[i], 0, 0)),
            pl.BlockSpec((None, None, D_MODEL, D_EXPERT), lambda i, te, nv, src: (layer, te[i], 0, 0)),
            pl.BlockSpec((None, None, D_EXPERT, D_MODEL), lambda i, te, nv, src: (layer, te[i], 0, 0)),
        ],
        out_specs=pl.BlockSpec((TM_EXP * TOK_STRIDE, LANES), lambda i, te, nv, src: (i, 0)),
        scratch_shapes=[pltpu.VMEM((TM_EXP * TOK_STRIDE, LANES), F32) for _ in range(N_XBUF)] + [
            pltpu.SemaphoreType.DMA((N_XBUF,)),
            pltpu.VMEM((D_MODEL, D_EXPERT), BF16),
            pltpu.VMEM((D_MODEL, D_EXPERT), BF16),
            pltpu.VMEM((D_EXPERT, D_MODEL), BF16),
        ],
    )
    return pl.pallas_call(
        _experts_kernel,
        grid_spec=grid_spec,
        out_shape=jax.ShapeDtypeStruct((NP_EXP * TOK_STRIDE, LANES), F32),
        compiler_params=_cparams(("arbitrary",)),
        name="experts",
    )(tile_expert, n_valid, src_tok, h2, wg, wu, wd)


N_RBUF = 3


def _final_kernel(*refs, emit_h):
    n_out = 2
    if emit_h:
        (pos_ref, x1_ref, ys_hbm, ewt_ref, gate2_ref, lng_ref, lnb_ref, sc1_ref, sh1_ref,
         x_ref, h_ref) = refs[:9 + n_out]
    else:
        (pos_ref, x1_ref, ys_hbm, ewt_ref, gate2_ref, lng_ref, lnb_ref, xc_ref, xl_ref) = refs[:7 + n_out]
    bufs = refs[-N_RBUF - 1:-1]
    sem = refs[-1]
    tm = TM_OUT
    i = pl.program_id(0)
    n_blocks = M_TOK // tm

    def start(tile, slot):
        tile = jnp.minimum(tile, n_blocks - 1)
        _row_gather_start(ys_hbm, bufs[slot], sem.at[slot], pos_ref, tile * tm, tm, tok0=0)
        _row_gather_start(ys_hbm, bufs[slot], sem.at[slot], pos_ref, M_TOK + tile * tm, tm, tok0=tm)

    @pl.when(i == 0)
    def _():
        for t in range(N_RBUF - 1):
            start(t, t)

    def step(cur):
        buf = bufs[cur]
        _row_gather_wait(ys_hbm, buf, sem.at[cur])
        start(i + N_RBUF - 1, (cur + N_RBUF - 1) % N_RBUF)
        w = ewt_ref[...]
        y2 = w[:, 0:1] * _load_token_tiles(buf, 0, tm) + w[:, 1:2] * _load_token_tiles(buf, tm, tm)
        x2 = _layer_norm_rows(DEEPNORM_ALPHA * x1_ref[...] + gate2_ref[...] * y2,
                              lng_ref[...], lnb_ref[...])
        if emit_h:
            x_ref[...] = x2
            h_ref[...] = (x2 * (1.0 + sc1_ref[...]) + sh1_ref[...]).astype(BF16)
        else:
            @pl.when(i < M_CTX // tm)
            def _():
                xc_ref[...] = x2

            @pl.when(i >= M_CTX // tm)
            def _():
                xl_ref[...] = x2

        @pl.when(i == n_blocks - 1)
        def _():
            for ahead in range(1, N_RBUF):
                slot = (cur + ahead) % N_RBUF
                _row_gather_wait(ys_hbm, bufs[slot], sem.at[slot])

    for cur in range(N_RBUF):
        pl.when(i % N_RBUF == cur)(functools.partial(step, cur))


def _final(x1, ys, pos, ewt, ada5, ln_g, ln_b, ada5_next):
    tm = TM_OUT
    emit_h = ada5_next is not None
    vec_spec = pl.BlockSpec((1, D_MODEL), lambda i, pos: (0, 0))
    ada_spec = lambda chunk: pl.BlockSpec(
        (None, None, 1, D_MODEL), lambda i, pos: (_ada_row(i, tm), chunk, 0, 0))
    in_specs = [
        pl.BlockSpec((tm, D_MODEL), lambda i, pos: (i, 0)),
        pl.BlockSpec(memory_space=pl.ANY),
        pl.BlockSpec((tm, LANES), lambda i, pos: (i, 0)),
        ada_spec(5),
        vec_spec, vec_spec,
    ]
    args = [pos, x1, ys, ewt, ada5, ln_g.reshape(1, D_MODEL), ln_b.reshape(1, D_MODEL)]
    if emit_h:
        in_specs += [ada_spec(1), ada_spec(0)]
        args += [ada5_next, ada5_next]
        row_spec = pl.BlockSpec((tm, D_MODEL), lambda i, pos: (i, 0))
        out_specs = [row_spec, row_spec]
        out_shape = [jax.ShapeDtypeStruct((M_TOK, D_MODEL), F32),
                     jax.ShapeDtypeStruct((M_TOK, D_MODEL), BF16)]
    else:
        out_specs = [
            pl.BlockSpec((tm, D_MODEL), lambda i, pos: (_ctx_block(i, tm), 0)),
            pl.BlockSpec((tm, D_MODEL), lambda i, pos: (_lat_block(i, tm), 0)),
        ]
        out_shape = [jax.ShapeDtypeStruct((M_CTX, D_MODEL), F32),
                     jax.ShapeDtypeStruct((M_LAT, D_MODEL), F32)]
    grid_spec = pltpu.PrefetchScalarGridSpec(
        num_scalar_prefetch=1,
        grid=(M_TOK // tm,),
        in_specs=in_specs,
        out_specs=out_specs,
        scratch_shapes=[pltpu.VMEM((2 * tm * TOK_STRIDE, LANES), F32) for _ in range(N_RBUF)] + [
            pltpu.SemaphoreType.DMA((N_RBUF,)),
        ],
    )
    return pl.pallas_call(
        functools.partial(_final_kernel, emit_h=emit_h),
        grid_spec=grid_spec,
        out_shape=out_shape,
        compiler_params=_cparams(("arbitrary",)),
        name="final_ln",
    )(*args)


def _rope_tables():
    rows = T_LAT // GRID_W
    row = jnp.repeat(jnp.arange(rows), GRID_W).astype(F32)
    col = jnp.tile(jnp.arange(GRID_W), rows).astype(F32)
    n_freq = HEAD_DIM // 4
    inv_freq = ROPE_BASE ** (-jnp.arange(n_freq, dtype=F32) / n_freq)
    ang = jnp.concatenate([row[:, None] * inv_freq[None], col[:, None] * inv_freq[None]], axis=-1)
    cos, sin = jnp.cos(ang), jnp.sin(ang)
    return jnp.concatenate([cos, cos], axis=-1), jnp.concatenate([-sin, sin], axis=-1)


def kernel(x_prompt, x_sample, state_retention, c, c_ctx, w_ada, b_ada, w_in, w_out, conv_w, pool_w,
           pool_scale, ret_decay_logit, ret_gn_gain, ln1_g, ln1_b, ln2_g, ln2_b, w_router, router_bias,
           w_gate, w_up, w_down):
    x_ctx = x_prompt.reshape(M_CTX, D_MODEL)
    x_lat = x_sample.reshape(M_LAT, D_MODEL)
    c_all = jnp.concatenate(
        [c_ctx[None, :], c, jnp.zeros((ADA_ROWS - 1 - N_LAT_SEQ, D_MODEL), F32)], axis=0)
    ada = _ada_table(c_all, w_ada, b_ada).reshape(DEPTH, ADA_ROWS, 6, 1, D_MODEL)

    cos_lat, sin_lat = _rope_tables()
    w_router_pad = jnp.pad(w_router, ((0, 0), (0, LANES - N_EXPERTS)))

    states = None
    x, h = _modulate(x_ctx, x_lat, ada[0])
    for l in range(DEPTH):
        ada5 = ada[l]
        z = _in_proj(h, w_in, l)

        ycp = _conv_pool(z, conv_w[l], pool_w[l], pool_scale[l])

        dl = jnp.broadcast_to(ret_decay_logit[l][:, :, None, None], (2, N_RET_HEADS, 1, HEAD_DIM))
        yret, states = _retention(z, cos_lat, sin_lat, dl, ret_gn_gain[l], state_retention, l, states)

        x1, h2, ei, ewt, rk, cnt = _out_proj(ycp, yret, x, w_out, l, ada5,
                                             ln1_g[l], ln1_b[l], w_router_pad, router_bias)
        pos = _pair_rows(ei, rk, cnt).reshape(-1)
        src_tok, te, n_valid = _route_tables(cnt[:, 0].astype(I32), pos)
        ys = _experts(h2, te, n_valid, src_tok, w_gate, w_up, w_down, l)
        if l + 1 < DEPTH:
            x, h = _final(x1, ys, pos, ewt, ada5, ln2_g[l], ln2_b[l], ada[l + 1])
        else:
            x_ctx, x_lat = _final(x1, ys, pos, ewt, ada5, ln2_g[l], ln2_b[l], None)

    y_prompt = x_ctx.reshape(N_CTX_SEQ, T_CTX, D_MODEL)
    y_sample = x_lat.reshape(N_LAT_SEQ, T_LAT, D_MODEL)
    return y_prompt, y_sample, states
```

```python
import functools

import jax
import jax.numpy as jnp
from jax import lax
from jax.experimental import pallas as pl
from jax.experimental.pallas import tpu as pltpu

F32 = jnp.float32
BF16 = jnp.bfloat16
I32 = jnp.int32

D_MODEL = 2048
N_CTX_SEQ, T_CTX = 16, 256
N_LAT_SEQ, T_LAT = 8, 1024
DEPTH = 2
M_CTX = N_CTX_SEQ * T_CTX
M_LAT = N_LAT_SEQ * T_LAT
M_TOK = M_CTX + M_LAT

GRID_W = 64
D_CONV = D_MODEL // 4
D_POOL = D_MODEL // 4
D_RET = D_MODEL // 2
N_RET_HEADS = 8
HEAD_DIM = D_RET // N_RET_HEADS
POOL_WINDOWS = (2, 4, 8, 16)
POOL_GROUP_DIM = D_POOL // len(POOL_WINDOWS)
CHUNK = 128
ROPE_BASE = 10000.0
N_EXPERTS = 16
EXPERTS_PER_GROUP = 4
N_EXPERT_GROUPS = N_EXPERTS // EXPERTS_PER_GROUP
D_EXPERT = D_MODEL // 4
D_IN_PROJ = 3 * D_CONV + D_POOL + 4 * D_RET
DEEPNORM_ALPHA = (2.0 * DEPTH) ** 0.25
LN_EPS = 1e-5
ADA_ROWS = 16

LANES = 128
VMEM_LIMIT = 56 * 1024 * 1024

TM_IN = 1024
TN_IN = 1536
TOK_ROWS = D_MODEL // LANES
TOK_STRIDE = TOK_ROWS + 1
TM_OP = 512
SUB_OP = 256
TM_OUT = 512
TM_EXP = 256
N_PAIR = 2 * M_TOK
NP_EXP = N_PAIR + N_EXPERTS * TM_EXP
NT_EXP = NP_EXP // TM_EXP


def _cparams(sem):
    return pltpu.CompilerParams(dimension_semantics=sem, vmem_limit_bytes=VMEM_LIMIT)


def _silu(x):
    return x * jax.nn.sigmoid(x)


def _ada_row(i, tm):
    n_ctx_tiles = M_CTX // tm
    per_batch = T_LAT // tm
    return jnp.where(i < n_ctx_tiles, 0, 1 + (i - n_ctx_tiles) // per_batch)


def _ada_kernel(c_ref, w_ref, b_ref, o_ref):
    s = _silu(c_ref[...]).astype(BF16)
    o_ref[...] = jnp.dot(s, w_ref[...].astype(BF16), preferred_element_type=F32) + b_ref[...]


def _ada_table(c_all, w_ada, b_ada):
    tn = 1024
    n6 = 6 * D_MODEL
    return pl.pallas_call(
        _ada_kernel,
        grid=(DEPTH, n6 // tn),
        in_specs=[
            pl.BlockSpec((ADA_ROWS, D_MODEL), lambda l, j: (0, 0)),
            pl.BlockSpec((None, D_MODEL, tn), lambda l, j: (l, 0, j)),
            pl.BlockSpec((None, 1, tn), lambda l, j: (l, 0, j)),
        ],
        out_specs=pl.BlockSpec((None, ADA_ROWS, tn), lambda l, j: (l, 0, j)),
        out_shape=jax.ShapeDtypeStruct((DEPTH, ADA_ROWS, n6), F32),
        compiler_params=_cparams(("arbitrary", "arbitrary")),
        name="ada_table",
    )(c_all, w_ada, b_ada.reshape(DEPTH, 1, n6))


def _ctx_block(i, tm):
    return jnp.minimum(i, M_CTX // tm - 1)


def _lat_block(i, tm):
    return jnp.maximum(i - M_CTX // tm, 0)


def _modulate_kernel(xc_ref, xl_ref, sc_ref, sh_ref, x_ref, h_ref):
    x = jnp.where(pl.program_id(0) < M_CTX // xc_ref.shape[0], xc_ref[...], xl_ref[...])
    x_ref[...] = x
    h_ref[...] = (x * (1.0 + sc_ref[...]) + sh_ref[...]).astype(BF16)


def _modulate(x_ctx, x_lat, ada5):
    tm = TM_OUT
    ada_spec = lambda chunk: pl.BlockSpec(
        (None, None, 1, D_MODEL), lambda i: (_ada_row(i, tm), chunk, 0, 0))
    return pl.pallas_call(
        _modulate_kernel,
        grid=(M_TOK // tm,),
        in_specs=[
            pl.BlockSpec((tm, D_MODEL), lambda i: (_ctx_block(i, tm), 0)),
            pl.BlockSpec((tm, D_MODEL), lambda i: (_lat_block(i, tm), 0)),
            ada_spec(1), ada_spec(0),
        ],
        out_specs=[pl.BlockSpec((tm, D_MODEL), lambda i: (i, 0)),
                   pl.BlockSpec((tm, D_MODEL), lambda i: (i, 0))],
        out_shape=[jax.ShapeDtypeStruct((M_TOK, D_MODEL), F32),
                   jax.ShapeDtypeStruct((M_TOK, D_MODEL), BF16)],
        compiler_params=_cparams(("arbitrary",)),
        name="modulate",
    )(x_ctx, x_lat, ada5, ada5)


def _inproj_kernel(h_ref, w_ref, o_ref, wbf_scr):
    @pl.when(pl.program_id(1) == 0)
    def _():
        wbf_scr[...] = w_ref[...].astype(BF16)

    o_ref[...] = jnp.dot(h_ref[...], wbf_scr[...], preferred_element_type=F32).astype(BF16)


def _in_proj(h, w_in, layer):
    return pl.pallas_call(
        _inproj_kernel,
        grid=(D_IN_PROJ // TN_IN, M_TOK // TM_IN),
        in_specs=[
            pl.BlockSpec((TM_IN, D_MODEL), lambda j, i: (i, 0)),
            pl.BlockSpec((None, D_MODEL, TN_IN), lambda j, i: (layer, 0, j)),
        ],
        out_specs=pl.BlockSpec((TM_IN, TN_IN), lambda j, i: (i, j)),
        out_shape=jax.ShapeDtypeStruct((M_TOK, D_IN_PROJ), BF16),
        scratch_shapes=[pltpu.VMEM((D_MODEL, TN_IN), BF16)],
        compiler_params=_cparams(("arbitrary", "arbitrary")),
        name="in_proj",
    )(h, w_in)


MIX_ROWS = T_LAT
N_CTX_BLOCKS = M_CTX // MIX_ROWS
SEQ_PER_CTX_BLOCK = MIX_ROWS // T_CTX
CP_ROWS = T_CTX
CP_HALO = 128


def _convpool_kernel(z_ref, cw_ref, pw_ref, ps_ref, o_ref, band_ref):
    blk = pl.program_id(0)
    g_dim = POOL_GROUP_DIM

    @pl.when(blk == 0)
    def _():
        row = lax.broadcasted_iota(I32, (CP_ROWS, CP_ROWS + 2 * CP_HALO), 0)
        col = lax.broadcasted_iota(I32, (CP_ROWS, CP_ROWS + 2 * CP_HALO), 1)
        d = col - CP_HALO - row
        for gi, w in enumerate(POOL_WINDOWS):
            band_ref[gi] = jnp.where((d >= -(w // 2)) & (d < w // 2), 1.0, 0.0).astype(BF16)

    def block(seq_len):
        t = lax.broadcasted_iota(I32, (CP_ROWS, LANES), 0)
        for ch in range(MIX_ROWS // CP_ROWS):
            r0 = ch * CP_ROWS
            rows = slice(r0, r0 + CP_ROWS)
            pos0 = r0 % seq_len
            at_start = pos0 == 0
            at_end = pos0 + CP_ROWS == seq_len

            for cg in range(D_CONV // LANES):
                lanes = slice(cg * LANES, (cg + 1) * LANES)

                def u_rows(rs):
                    return (z_ref[rs, D_CONV + cg * LANES:D_CONV + (cg + 1) * LANES].astype(F32)
                            * z_ref[rs, 2 * D_CONV + cg * LANES:2 * D_CONV + (cg + 1) * LANES].astype(F32))
                u = u_rows(rows)
                before = 0.0 if at_start else u_rows(slice(r0 - 1, r0))
                after = 0.0 if at_end else u_rows(slice(r0 + CP_ROWS, r0 + CP_ROWS + 1))
                u_prev = jnp.where(t == 0, before, pltpu.roll(u, 1, 0))
                u_next = jnp.where(t == CP_ROWS - 1, after, pltpu.roll(u, CP_ROWS - 1, 0))
                conv = u_prev * cw_ref[0:1, lanes] + u * cw_ref[1:2, lanes] + u_next * cw_ref[2:3, lanes]
                o_ref[rows, lanes] = (z_ref[rows, lanes].astype(F32) * conv).astype(BF16)

            k_rows = slice(r0 if at_start else r0 - CP_HALO,
                           r0 + CP_ROWS if at_end else r0 + CP_ROWS + CP_HALO)
            b_cols = slice(CP_HALO if at_start else 0,
                           CP_HALO + CP_ROWS if at_end else CP_ROWS + 2 * CP_HALO)
            tpos = pos0 + t
            for gi, w in enumerate(POOL_WINDOWS):
                lo = 3 * D_CONV + gi * g_dim
                win = jnp.dot(band_ref[gi, :, b_cols], z_ref[k_rows, lo:lo + g_dim],
                              preferred_element_type=F32)
                cnt = (jnp.minimum(tpos + w // 2, seq_len) - jnp.maximum(tpos - w // 2, 0)).astype(F32)
                pooled = win / cnt - z_ref[rows, lo:lo + g_dim].astype(F32)
                y = jnp.dot(pooled.astype(BF16), pw_ref[gi].astype(BF16), preferred_element_type=F32)
                y = y * ps_ref[:, gi * g_dim:(gi + 1) * g_dim]
                o_ref[rows, D_CONV + gi * g_dim:D_CONV + (gi + 1) * g_dim] = y.astype(BF16)

    pl.when(blk < N_CTX_BLOCKS)(functools.partial(block, T_CTX))
    pl.when(blk >= N_CTX_BLOCKS)(functools.partial(block, T_LAT))


def _conv_pool(z, conv_w, pool_w, pool_scale):
    return pl.pallas_call(
        _convpool_kernel,
        grid=(M_TOK // MIX_ROWS,),
        in_specs=[
            pl.BlockSpec((MIX_ROWS, D_IN_PROJ - 4 * D_RET), lambda b: (b, 0)),
            pl.BlockSpec((3, D_CONV), lambda b: (0, 0)),
            pl.BlockSpec((len(POOL_WINDOWS), POOL_GROUP_DIM, POOL_GROUP_DIM), lambda b: (0, 0, 0)),
            pl.BlockSpec((1, D_POOL), lambda b: (0, 0)),
        ],
        out_specs=pl.BlockSpec((MIX_ROWS, D_CONV + D_POOL), lambda b: (b, 0)),
        out_shape=jax.ShapeDtypeStruct((M_TOK, D_CONV + D_POOL), BF16),
        scratch_shapes=[pltpu.VMEM((len(POOL_WINDOWS), CP_ROWS, CP_ROWS + 2 * CP_HALO), BF16)],
        compiler_params=_cparams(("arbitrary",)),
        name="conv_pool",
    )(z, conv_w, pool_w, pool_scale.reshape(1, D_POOL))


def _log_sigmoid(x):
    return jnp.minimum(x, 0.0) - jnp.log1p(jnp.exp(-jnp.abs(x)))


RET_ROWS = MIX_ROWS


def _retention_kernel(*refs, layer):
    if layer == 0:
        (q_ref, k_ref, v_ref, g_ref, cos_ref, sin_ref, dl_ref, gain_ref, s0_ref,
         y_ref, sfin_ref, tab_scr, kr_scr, u_scr, s_scr) = refs
    else:
        (q_ref, k_ref, v_ref, g_ref, cos_ref, sin_ref, dl_ref, gain_ref, s0_ref, _,
         y_ref, sfin_ref, tab_scr, kr_scr, u_scr, s_scr) = refs
    blk = pl.program_id(1)
    n_chunks = RET_ROWS // CHUNK
    half = HEAD_DIM // 2

    lg_f = _log_sigmoid(dl_ref[0])
    lg_b = _log_sigmoid(dl_ref[1])
    g_f = jnp.exp(lg_f * CHUNK)
    g_b = jnp.exp(lg_b * CHUNK)

    @pl.when(blk == 0)
    def _():
        row = lax.broadcasted_iota(I32, (CHUNK, CHUNK), 0).astype(F32)
        col = lax.broadcasted_iota(I32, (CHUNK, CHUNK), 1).astype(F32)
        diff = row - col
        tab_scr[0] = (jnp.where(diff >= 0, jnp.exp(lg_f * jnp.maximum(diff, 0.0)), 0.0)
                      + jnp.where(diff <= 0, jnp.exp(lg_b * jnp.maximum(-diff, 0.0)), 0.0))
        tab_scr[1] = jnp.exp(lg_f * (row + 1.0))
        tab_scr[2] = jnp.exp(lg_b * (CHUNK - row))
        tab_scr[3] = jnp.exp(lg_f * (CHUNK - 1.0 - row))
        tab_scr[4] = jnp.exp(lg_b * row)

    tn_dims = (((0,), (0,)), ((), ()))
    nt_dims = (((1,), (1,)), ((), ()))

    def block(seq_chunks, latent):
        def rope(x, sl):
            if not latent:
                return x
            return x * cos_ref[sl, :] + pltpu.roll(x, half, 1) * sin_ref[sl, :]

        for c in range(n_chunks):
            sl = slice(c * CHUNK, (c + 1) * CHUNK)
            kr = rope(k_ref[sl, :].astype(F32), sl) * (HEAD_DIM ** -0.5)
            kr_scr[sl, :] = kr.astype(BF16)
            kz = jnp.concatenate([kr * tab_scr[3], kr * tab_scr[4]], axis=1).astype(BF16)
            u_scr[c] = lax.dot_general(kz, v_ref[sl, :], tn_dims, preferred_element_type=F32)

        has_state = [[False] * n_chunks, [False] * n_chunks]
        for s_i in range(n_chunks // seq_chunks):
            chunks = list(range(s_i * seq_chunks, (s_i + 1) * seq_chunks))
            for d, order, g_d in ((0, chunks, g_f), (1, chunks[::-1], g_b)):
                s = s0_ref[d] if latent else None
                for c in order:
                    u = u_scr[c, d * HEAD_DIM:(d + 1) * HEAD_DIM, :]
                    if s is None:
                        s = u
                    else:
                        s_scr[d, c] = s.astype(BF16)
                        has_state[d][c] = True
                        s = g_d * s + u
                if not latent:
                    if layer == 0:
                        sfin_ref[s_i, 0, d] = s
                        for later in range(1, DEPTH):
                            sfin_ref[s_i, later, d] = jnp.zeros_like(s)
                    else:
                        sfin_ref[s_i, d] = s

        for c in range(n_chunks):
            sl = slice(c * CHUNK, (c + 1) * CHUNK)
            qr = rope(q_ref[sl, :].astype(F32), sl)
            scores = lax.dot_general(qr.astype(BF16), kr_scr[sl, :], nt_dims, preferred_element_type=F32)
            o = jnp.dot((scores * tab_scr[0]).astype(BF16), v_ref[sl, :], preferred_element_type=F32)
            for d in range(2):
                if has_state[d][c]:
                    o += jnp.dot((qr * tab_scr[1 + d]).astype(BF16), s_scr[d, c],
                                 preferred_element_type=F32)
            mu = jnp.mean(o, axis=-1, keepdims=True)
            dev = o - mu
            var = jnp.mean(dev * dev, axis=-1, keepdims=True)
            on = dev * lax.rsqrt(var + LN_EPS) * gain_ref[...]
            y_ref[sl, :] = (_silu(g_ref[sl, :].astype(F32)) * on).astype(BF16)

    pl.when(blk < N_CTX_BLOCKS)(functools.partial(block, T_CTX // CHUNK, False))
    pl.when(blk >= N_CTX_BLOCKS)(functools.partial(block, T_LAT // CHUNK, True))


def _retention(z, cos_t, sin_t, decay_logit_b, gain, s0, layer, states):
    qcol0 = (3 * D_CONV + D_POOL) // HEAD_DIM
    n_blocks = M_TOK // RET_ROWS

    def zspec(k):
        return pl.BlockSpec((RET_ROWS, HEAD_DIM), lambda h, b: (b, qcol0 + k * N_RET_HEADS + h))

    def lat_seq(b):
        return jnp.maximum(b - N_CTX_BLOCKS, 0)

    def ctx_block(b):
        return jnp.minimum(b, N_CTX_BLOCKS - 1)

    n_chunks = RET_ROWS // CHUNK
    in_specs = [
        zspec(0), zspec(1), zspec(2), zspec(3),
        pl.BlockSpec((RET_ROWS, HEAD_DIM), lambda h, b: (0, 0)),
        pl.BlockSpec((RET_ROWS, HEAD_DIM), lambda h, b: (0, 0)),
        pl.BlockSpec((2, None, 1, HEAD_DIM), lambda h, b: (0, h, 0, 0)),
        pl.BlockSpec((1, HEAD_DIM), lambda h, b: (0, h)),
        pl.BlockSpec((None, None, 2, None, HEAD_DIM, HEAD_DIM),
                     lambda h, b: (lat_seq(b), layer, 0, h, 0, 0)),
    ]
    args = [z, z, z, z, cos_t, sin_t, decay_logit_b, gain.reshape(1, D_RET), s0]
    if layer == 0:
        st_spec = pl.BlockSpec((SEQ_PER_CTX_BLOCK, DEPTH, 2, None, HEAD_DIM, HEAD_DIM),
                               lambda h, b: (ctx_block(b), 0, 0, h, 0, 0))
        aliases = {}
    else:
        st_spec = pl.BlockSpec((SEQ_PER_CTX_BLOCK, None, 2, None, HEAD_DIM, HEAD_DIM),
                               lambda h, b: (ctx_block(b), layer, 0, h, 0, 0))
        in_specs.append(pl.BlockSpec(memory_space=pl.ANY))
        args.append(states)
        aliases = {len(args) - 1: 1}
    return pl.pallas_call(
        functools.partial(_retention_kernel, layer=layer),
        grid=(N_RET_HEADS, n_blocks),
        in_specs=in_specs,
        out_specs=[pl.BlockSpec((RET_ROWS, HEAD_DIM), lambda h, b: (b, h)), st_spec],
        out_shape=[
            jax.ShapeDtypeStruct((M_TOK, D_RET), BF16),
            jax.ShapeDtypeStruct((N_CTX_SEQ, DEPTH, 2, N_RET_HEADS, HEAD_DIM, HEAD_DIM), F32),
        ],
        input_output_aliases=aliases,
        scratch_shapes=[
            pltpu.VMEM((5, CHUNK, CHUNK), F32),
            pltpu.VMEM((RET_ROWS, HEAD_DIM), BF16),
            pltpu.VMEM((n_chunks, 2 * HEAD_DIM, HEAD_DIM), F32),
            pltpu.VMEM((2, n_chunks, HEAD_DIM, HEAD_DIM), BF16),
        ],
        compiler_params=_cparams(("arbitrary", "arbitrary")),
        name="retention",
    )(*args)


def _layer_norm_rows(r, g, b):
    mu = jnp.mean(r, axis=-1, keepdims=True)
    dev = r - mu
    var = jnp.mean(dev * dev, axis=-1, keepdims=True)
    return dev * lax.rsqrt(var + LN_EPS) * g + b


def _top2_of4(vals):
    top1 = jnp.maximum(jnp.maximum(vals[0], vals[1]), jnp.maximum(vals[2], vals[3]))
    idx1 = jnp.where(vals[0] == top1, 0, jnp.where(vals[1] == top1, 1, jnp.where(vals[2] == top1, 2, 3)))
    neg = jnp.float32(-jnp.inf)
    rest = [jnp.where(idx1 == j, neg, vals[j]) for j in range(4)]
    top2 = jnp.maximum(jnp.maximum(rest[0], rest[1]), jnp.maximum(rest[2], rest[3]))
    idx2 = jnp.where(rest[0] == top2, 0, jnp.where(rest[1] == top2, 1, jnp.where(rest[2] == top2, 2, 3)))
    return top1, idx1, top2, idx2


def _store_token_tiles(ref, val, tok0=0):
    n = val.shape[0]
    base = tok0 * TOK_STRIDE
    for c in range(TOK_ROWS):
        ref[pl.ds(base + c, n, stride=TOK_STRIDE), :] = val[:, c * LANES:(c + 1) * LANES]
    ref[pl.ds(base + TOK_ROWS, n, stride=TOK_STRIDE), :] = jnp.zeros((n, LANES), val.dtype)


def _load_token_tiles(ref, tok0, n):
    return jnp.concatenate(
        [ref[pl.ds(tok0 * TOK_STRIDE + c, n, stride=TOK_STRIDE), :] for c in range(TOK_ROWS)], axis=1)


def _outproj_kernel(ycp_ref, yret_ref, x_ref, wo_hbm, gate1_ref, lng_ref, lnb_ref, sc2_ref, sh2_ref,
                    wr_ref, rb_ref, x1_ref, h2_ref, ei_ref, ewt_ref, rk_ref, cnt_ref,
                    carry_scr, wo_ref, wstage, wsem, wr_hi, wr_lo, *, layer):
    i = pl.program_id(0)

    @pl.when(i == 0)
    def _():
        carry_scr[...] = jnp.zeros_like(carry_scr)
        wr = wr_ref[...]
        hi = wr.astype(BF16)
        wr_hi[...] = hi
        wr_lo[...] = (wr - hi.astype(F32)).astype(BF16)
        rows = wstage.shape[1]
        n_chunks = D_MODEL // rows

        def chunk_copy(c):
            return pltpu.make_async_copy(wo_hbm.at[layer, pl.ds(c * rows, rows), :], wstage.at[c % 2],
                                         wsem.at[c % 2])
        chunk_copy(0).start()
        for c in range(n_chunks):
            if c + 1 < n_chunks:
                chunk_copy(c + 1).start()
            chunk_copy(c).wait()
            wo_ref[c * rows:(c + 1) * rows, :] = wstage[c % 2].astype(BF16)

    s_i = lax.broadcasted_iota(I32, (SUB_OP, SUB_OP), 0)
    t_i = lax.broadcasted_iota(I32, (SUB_OP, SUB_OP), 1)
    tri = jnp.where(s_i < t_i, 1.0, 0.0).astype(BF16)
    half_k = D_CONV + D_POOL
    n_sub = x_ref.shape[0] // SUB_OP
    ys = []
    for sub in range(n_sub):
        rows_sl = slice(sub * SUB_OP, (sub + 1) * SUB_OP)
        y = jnp.dot(ycp_ref[rows_sl, :], wo_ref[0:half_k, :], preferred_element_type=F32)
        ys.append(y + jnp.dot(yret_ref[rows_sl, :], wo_ref[half_k:, :], preferred_element_type=F32))
    lts = [_ln_router_subtile(sub, ys[sub], x_ref, gate1_ref, lng_ref, lnb_ref, sc2_ref, sh2_ref,
                              wr_hi, wr_lo, x1_ref, h2_ref) for sub in range(n_sub)]
    carry = carry_scr[:, 0:1]
    for sub in range(n_sub):
        carry = _route_subtile(sub, lts[sub], carry, tri, rb_ref, ei_ref, ewt_ref, rk_ref)
    carry_scr[...] = jnp.broadcast_to(carry, carry_scr.shape)
    cnt_ref[...] = carry_scr[...]


def _ln_router_subtile(sub, y, x_ref, gate1_ref, lng_ref, lnb_ref, sc2_ref, sh2_ref, wr_hi, wr_lo,
                       x1_ref, h2_ref):
    tm = SUB_OP
    rows_sl = slice(sub * tm, (sub + 1) * tm)
    x1 = _layer_norm_rows(DEEPNORM_ALPHA * x_ref[rows_sl, :] + gate1_ref[...] * y,
                          lng_ref[...], lnb_ref[...])
    x1_ref[rows_sl, :] = x1
    h2 = x1 * (1.0 + sc2_ref[...]) + sh2_ref[...]
    _store_token_tiles(h2_ref, h2, tok0=sub * tm)

    h_hi = h2.astype(BF16)
    h_lo = (h2 - h_hi.astype(F32)).astype(BF16)
    logits = (jnp.dot(h_hi, wr_hi[...], preferred_element_type=F32)
              + jnp.dot(h_lo, wr_hi[...], preferred_element_type=F32)
              + jnp.dot(h_hi, wr_lo[...], preferred_element_type=F32))
    return logits.T


def _route_subtile(sub, lt, carry, tri, rb_ref, ei_ref, ewt_ref, rk_ref):
    tm = SUB_OP
    rows_sl = slice(sub * tm, (sub + 1) * tm)
    rows = [lt[e:e + 1, :] for e in range(N_EXPERTS)]

    mx = rows[0]
    for e in range(1, N_EXPERTS):
        mx = jnp.maximum(mx, rows[e])
    ex = [jnp.exp(r - mx) for r in rows]
    den = ex[0]
    for e in range(1, N_EXPERTS):
        den = den + ex[e]
    score = [x / den for x in ex]
    biased = [score[e] + rb_ref[e] for e in range(N_EXPERTS)]

    best = None
    for gi in range(N_EXPERT_GROUPS):
        t1, i1, t2, i2 = _top2_of4(biased[gi * EXPERTS_PER_GROUP:(gi + 1) * EXPERTS_PER_GROUP])
        gs = t1 + t2
        e1 = gi * EXPERTS_PER_GROUP + i1
        e2 = gi * EXPERTS_PER_GROUP + i2
        if best is None:
            best = (gs, e1, e2)
        else:
            take = gs > best[0]
            best = (jnp.where(take, gs, best[0]), jnp.where(take, e1, best[1]), jnp.where(take, e2, best[2]))
    _, e1, e2 = best
    zero = jnp.zeros_like(score[0])
    w1 = zero
    w2 = zero
    for e in range(N_EXPERTS):
        w1 = w1 + jnp.where(e1 == e, score[e], 0.0)
        w2 = w2 + jnp.where(e2 == e, score[e], 0.0)
    wsum = w1 + w2
    ei_ref[0:1, rows_sl] = e1
    ei_ref[1:2, rows_sl] = e2
    ewt_ref[rows_sl, :] = jnp.concatenate(
        [w1 / wsum, w2 / wsum, jnp.zeros((LANES - 2, tm), F32)], axis=0).T

    onehot = jnp.concatenate(
        [jnp.where((e1 == e) | (e2 == e), 1.0, 0.0) for e in range(N_EXPERTS)], axis=0)
    prefix = jnp.dot(onehot.astype(BF16), tri, preferred_element_type=F32) + carry
    r1 = zero
    r2 = zero
    for e in range(N_EXPERTS):
        r1 = r1 + jnp.where(e1 == e, prefix[e:e + 1, :], 0.0)
        r2 = r2 + jnp.where(e2 == e, prefix[e:e + 1, :], 0.0)
    rk_ref[0:1, rows_sl] = r1.astype(I32)
    rk_ref[1:2, rows_sl] = r2.astype(I32)
    return carry + jnp.sum(onehot, axis=1, keepdims=True)


def _out_proj(ycp, yret, x, w_out, layer, ada5, ln_g, ln_b, w_router_pad, router_bias):
    tm = TM_OP
    ada_spec = lambda chunk: pl.BlockSpec(
        (None, None, 1, D_MODEL), lambda i: (_ada_row(i, tm), chunk, 0, 0))
    vec_spec = pl.BlockSpec((1, D_MODEL), lambda i: (0, 0))
    route_spec = pl.BlockSpec((2, tm), lambda i: (0, i))
    return pl.pallas_call(
        functools.partial(_outproj_kernel, layer=layer),
        grid=(M_TOK // tm,),
        in_specs=[
            pl.BlockSpec((tm, D_CONV + D_POOL), lambda i: (i, 0)),
            pl.BlockSpec((tm, D_RET), lambda i: (i, 0)),
            pl.BlockSpec((tm, D_MODEL), lambda i: (i, 0)),
            pl.BlockSpec(memory_space=pl.ANY),
            ada_spec(2),
            vec_spec, vec_spec,
            ada_spec(4),
            ada_spec(3),
            pl.BlockSpec((D_MODEL, LANES), lambda i: (0, 0)),
            pl.BlockSpec(memory_space=pltpu.SMEM),
        ],
        out_specs=[
            pl.BlockSpec((tm, D_MODEL), lambda i: (i, 0)),
            pl.BlockSpec((tm * TOK_STRIDE, LANES), lambda i: (i, 0)),
            route_spec,
            pl.BlockSpec((tm, LANES), lambda i: (i, 0)),
            route_spec,
            pl.BlockSpec((N_EXPERTS, LANES), lambda i: (0, 0)),
        ],
        out_shape=[
            jax.ShapeDtypeStruct((M_TOK, D_MODEL), F32),
            jax.ShapeDtypeStruct((M_TOK * TOK_STRIDE, LANES), F32),
            jax.ShapeDtypeStruct((2, M_TOK), I32),
            jax.ShapeDtypeStruct((M_TOK, LANES), F32),
            jax.ShapeDtypeStruct((2, M_TOK), I32),
            jax.ShapeDtypeStruct((N_EXPERTS, LANES), F32),
        ],
        scratch_shapes=[
            pltpu.VMEM((N_EXPERTS, LANES), F32),
            pltpu.VMEM((D_MODEL, D_MODEL), BF16),
            pltpu.VMEM((2, 128, D_MODEL), F32),
            pltpu.SemaphoreType.DMA((2,)),
            pltpu.VMEM((D_MODEL, LANES), BF16),
            pltpu.VMEM((D_MODEL, LANES), BF16),
        ],
        compiler_params=_cparams(("arbitrary",)),
        name="out_proj_router",
    )(ycp, yret, x, w_out, ada5, ln_g.reshape(1, D_MODEL), ln_b.reshape(1, D_MODEL),
      ada5, ada5, w_router_pad, router_bias)


def _pos_kernel(ei_ref, rk_ref, cnt_ref, pos_ref):
    ei = ei_ref[...]
    pos = rk_ref[...]
    start = jnp.zeros((1, 1), F32)
    for e in range(N_EXPERTS):
        pos = pos + jnp.where(ei == e, start.astype(I32), 0)
        n_tiles = jnp.floor((cnt_ref[e:e + 1, 0:1] + (TM_EXP - 1.0)) * (1.0 / TM_EXP))
        start = start + n_tiles * TM_EXP
    pos_ref[...] = pos


def _pair_rows(ei, rk, cnt):
    return pl.pallas_call(
        _pos_kernel,
        out_shape=jax.ShapeDtypeStruct((2, M_TOK), I32),
        name="pair_rows",
    )(ei, rk, cnt)


def _route_kernel(cnt_ref, pos_ref, src_ref, te_ref, nv_ref):
    def zero(r, carry):
        src_ref[r] = 0
        return carry

    tile = jnp.int32(0)
    for e in range(N_EXPERTS):
        n_tiles = lax.shift_right_logical(cnt_ref[e] + (TM_EXP - 1), TM_EXP.bit_length() - 1)

        def mark(j, carry, e=e, tile=tile):
            te_ref[tile + j] = e
            return carry
        lax.fori_loop(0, n_tiles, mark, 0)
        lax.fori_loop(tile * TM_EXP + cnt_ref[e], (tile + n_tiles) * TM_EXP, zero, 0)
        tile = tile + n_tiles
    nv_ref[0] = tile
    last_expert = te_ref[tile - 1]

    def mark_unused(j, carry):
        te_ref[j] = last_expert
        return carry
    lax.fori_loop(tile, NT_EXP, mark_unused, 0)
    lax.fori_loop(tile * TM_EXP, NP_EXP, zero, 0)

    def place(t, carry):
        src_ref[pos_ref[t]] = t
        src_ref[pos_ref[M_TOK + t]] = t
        return carry
    lax.fori_loop(0, M_TOK, place, 0, unroll=16)


def _route_tables(cnt, pos):
    smem = pl.BlockSpec(memory_space=pltpu.SMEM)
    return pl.pallas_call(
        _route_kernel,
        in_specs=[smem, smem],
        out_specs=[smem, smem, smem],
        out_shape=[
            jax.ShapeDtypeStruct((NP_EXP,), I32),
            jax.ShapeDtypeStruct((NT_EXP,), I32),
            jax.ShapeDtypeStruct((1,), I32),
        ],
        name="route_tables",
    )(cnt, pos)


GATHER_PRIORITY = 1


def _row_gather_start(src_hbm, buf, sem, idx_ref, base, n_tok, tok0=0, priority=0):
    for r in range(n_tok):
        pltpu.make_async_copy(src_hbm.at[pl.ds(idx_ref[base + r] * TOK_STRIDE, TOK_ROWS), :],
                              buf.at[pl.ds((tok0 + r) * TOK_STRIDE, TOK_ROWS), :], sem).start(priority)


def _row_gather_wait(src_hbm, buf, sem):
    n_rows = buf.shape[0] // TOK_STRIDE * TOK_ROWS
    pltpu.make_async_copy(src_hbm.at[pl.ds(0, n_rows), :], buf.at[pl.ds(0, n_rows), :], sem).wait()


N_XBUF = 6


def _experts_kernel(te_ref, nv_ref, src_ref, h2_hbm, wg_ref, wu_ref, wd_ref, o_ref, *scratch):
    bufs = scratch[:N_XBUF]
    gsem, wg_bf, wu_bf, wd_bf = scratch[N_XBUF:]
    i = pl.program_id(0)
    n_valid = nv_ref[0]

    def start_tile(tile, slot):
        tile = jnp.minimum(tile, NT_EXP - 1)
        _row_gather_start(h2_hbm, bufs[slot], gsem.at[slot], src_ref, tile * TM_EXP, TM_EXP,
                          priority=GATHER_PRIORITY)

    @pl.when(i == 0)
    def _():
        for t in range(N_XBUF - 1):
            start_tile(t, t)

    @pl.when((i < n_valid) & ((i == 0) | (te_ref[i] != te_ref[jnp.maximum(i - 1, 0)])))
    def _():
        wg_bf[...] = wg_ref[...].astype(BF16)
        wu_bf[...] = wu_ref[...].astype(BF16)
        wd_bf[...] = wd_ref[...].astype(BF16)

    def step(cur):
        _row_gather_wait(h2_hbm, bufs[cur], gsem.at[cur])
        start_tile(i + N_XBUF - 1, (cur + N_XBUF - 1) % N_XBUF)
        x = _load_token_tiles(bufs[cur], 0, TM_EXP).astype(BF16)
        g = jnp.dot(x, wg_bf[...], preferred_element_type=F32)
        u = jnp.dot(x, wu_bf[...], preferred_element_type=F32)
        a = (_silu(g) * u).astype(BF16)
        _store_token_tiles(o_ref, jnp.dot(a, wd_bf[...], preferred_element_type=F32))

        @pl.when(i == n_valid - 1)
        def _():
            for ahead in range(1, N_XBUF):
                slot = (cur + ahead) % N_XBUF
                _row_gather_wait(h2_hbm, bufs[slot], gsem.at[slot])

    for cur in range(N_XBUF):
        pl.when((i < n_valid) & (i % N_XBUF == cur))(functools.partial(step, cur))

    @pl.when(i >= n_valid)
    def _():
        o_ref[...] = jnp.zeros_like(o_ref)


def _experts(h2, tile_expert, n_valid, src_tok, wg, wu, wd, layer):
    grid_spec = pltpu.PrefetchScalarGridSpec(
        num_scalar_prefetch=3,
        grid=(NT_EXP,),
        in_specs=[
            pl.BlockSpec(memory_space=pl.ANY),
            pl.BlockSpec((None, None, D_MODEL, D_EXPERT), lambda i, te, nv, src: (layer, te[i], 0, 0)),
            pl.BlockSpec((None, None, D_MODEL, D_EXPERT), lambda i, te, nv, src: (layer, te[i], 0, 0)),
            pl.BlockSpec((None, None, D_EXPERT, D_MODEL), lambda i, te, nv, src: (layer, te[i], 0, 0)),
        ],
        out_specs=pl.BlockSpec((TM_EXP * TOK_STRIDE, LANES), lambda i, te, nv, src: (i, 0)),
        scratch_shapes=[pltpu.VMEM((TM_EXP * TOK_STRIDE, LANES), F32) for _ in range(N_XBUF)] + [
            pltpu.SemaphoreType.DMA((N_XBUF,)),
            pltpu.VMEM((D_MODEL, D_EXPERT), BF16),
            pltpu.VMEM((D_MODEL, D_EXPERT), BF16),
            pltpu.VMEM((D_EXPERT, D_MODEL), BF16),
        ],
    )
    return pl.pallas_call(
        _experts_kernel,
        grid_spec=grid_spec,
        out_shape=jax.ShapeDtypeStruct((NP_EXP * TOK_STRIDE, LANES), F32),
        compiler_params=_cparams(("arbitrary",)),
        name="experts",
    )(tile_expert, n_valid, src_tok, h2, wg, wu, wd)


N_RBUF = 3


def _final_kernel(*refs, emit_h):
    n_out = 2
    if emit_h:
        (pos_ref, x1_ref, ys_hbm, ewt_ref, gate2_ref, lng_ref, lnb_ref, sc1_ref, sh1_ref,
         x_ref, h_ref) = refs[:9 + n_out]
    else:
        (pos_ref, x1_ref, ys_hbm, ewt_ref, gate2_ref, lng_ref, lnb_ref, xc_ref, xl_ref) = refs[:7 + n_out]
    bufs = refs[-N_RBUF - 1:-1]
    sem = refs[-1]
    tm = TM_OUT
    i = pl.program_id(0)
    n_blocks = M_TOK // tm

    def start(tile, slot):
        tile = jnp.minimum(tile, n_blocks - 1)
        _row_gather_start(ys_hbm, bufs[slot], sem.at[slot], pos_ref, tile * tm, tm, tok0=0)
        _row_gather_start(ys_hbm, bufs[slot], sem.at[slot], pos_ref, M_TOK + tile * tm, tm, tok0=tm)

    @pl.when(i == 0)
    def _():
        for t in range(N_RBUF - 1):
            start(t, t)

    def step(cur):
        buf = bufs[cur]
        _row_gather_wait(ys_hbm, buf, sem.at[cur])
        start(i + N_RBUF - 1, (cur + N_RBUF - 1) % N_RBUF)
        w = ewt_ref[...]
        y2 = w[:, 0:1] * _load_token_tiles(buf, 0, tm) + w[:, 1:2] * _load_token_tiles(buf, tm, tm)
        x2 = _layer_norm_rows(DEEPNORM_ALPHA * x1_ref[...] + gate2_ref[...] * y2,
                              lng_ref[...], lnb_ref[...])
        if emit_h:
            x_ref[...] = x2
            h_ref[...] = (x2 * (1.0 + sc1_ref[...]) + sh1_ref[...]).astype(BF16)
        else:
            @pl.when(i < M_CTX // tm)
            def _():
                xc_ref[...] = x2

            @pl.when(i >= M_CTX // tm)
            def _():
                xl_ref[...] = x2

        @pl.when(i == n_blocks - 1)
        def _():
            for ahead in range(1, N_RBUF):
                slot = (cur + ahead) % N_RBUF
                _row_gather_wait(ys_hbm, bufs[slot], sem.at[slot])

    for cur in range(N_RBUF):
        pl.when(i % N_RBUF == cur)(functools.partial(step, cur))


def _final(x1, ys, pos, ewt, ada5, ln_g, ln_b, ada5_next):
    tm = TM_OUT
    emit_h = ada5_next is not None
    vec_spec = pl.BlockSpec((1, D_MODEL), lambda i, pos: (0, 0))
    ada_spec = lambda chunk: pl.BlockSpec(
        (None, None, 1, D_MODEL), lambda i, pos: (_ada_row(i, tm), chunk, 0, 0))
    in_specs = [
        pl.BlockSpec((tm, D_MODEL), lambda i, pos: (i, 0)),
        pl.BlockSpec(memory_space=pl.ANY),
        pl.BlockSpec((tm, LANES), lambda i, pos: (i, 0)),
        ada_spec(5),
        vec_spec, vec_spec,
    ]
    args = [pos, x1, ys, ewt, ada5, ln_g.reshape(1, D_MODEL), ln_b.reshape(1, D_MODEL)]
    if emit_h:
        in_specs += [ada_spec(1), ada_spec(0)]
        args += [ada5_next, ada5_next]
        row_spec = pl.BlockSpec((tm, D_MODEL), lambda i, pos: (i, 0))
        out_specs = [row_spec, row_spec]
        out_shape = [jax.ShapeDtypeStruct((M_TOK, D_MODEL), F32),
                     jax.ShapeDtypeStruct((M_TOK, D_MODEL), BF16)]
    else:
        out_specs = [
            pl.BlockSpec((tm, D_MODEL), lambda i, pos: (_ctx_block(i, tm), 0)),
            pl.BlockSpec((tm, D_MODEL), lambda i, pos: (_lat_block(i, tm), 0)),
        ]
        out_shape = [jax.ShapeDtypeStruct((M_CTX, D_MODEL), F32),
                     jax.ShapeDtypeStruct((M_LAT, D_MODEL), F32)]
    grid_spec = pltpu.PrefetchScalarGridSpec(
        num_scalar_prefetch=1,
        grid=(M_TOK // tm,),
        in_specs=in_specs,
        out_specs=out_specs,
        scratch_shapes=[pltpu.VMEM((2 * tm * TOK_STRIDE, LANES), F32) for _ in range(N_RBUF)] + [
            pltpu.SemaphoreType.DMA((N_RBUF,)),
        ],
    )
    return pl.pallas_call(
        functools.partial(_final_kernel, emit_h=emit_h),
        grid_spec=grid_spec,
        out_shape=out_shape,
        compiler_params=_cparams(("arbitrary",)),
        name="final_ln",
    )(*args)


def _rope_tables():
    rows = T_LAT // GRID_W
    row = jnp.repeat(jnp.arange(rows), GRID_W).astype(F32)
    col = jnp.tile(jnp.arange(GRID_W), rows).astype(F32)
    n_freq = HEAD_DIM // 4
    inv_freq = ROPE_BASE ** (-jnp.arange(n_freq, dtype=F32) / n_freq)
    ang = jnp.concatenate([row[:, None] * inv_freq[None], col[:, None] * inv_freq[None]], axis=-1)
    cos, sin = jnp.cos(ang), jnp.sin(ang)
    return jnp.concatenate([cos, cos], axis=-1), jnp.concatenate([-sin, sin], axis=-1)


def kernel(x_prompt, x_sample, state_retention, c, c_ctx, w_ada, b_ada, w_in, w_out, conv_w, pool_w,
           pool_scale, ret_decay_logit, ret_gn_gain, ln1_g, ln1_b, ln2_g, ln2_b, w_router, router_bias,
           w_gate, w_up, w_down):
    x_ctx = x_prompt.reshape(M_CTX, D_MODEL)
    x_lat = x_sample.reshape(M_LAT, D_MODEL)
    c_all = jnp.concatenate(
        [c_ctx[None, :], c, jnp.zeros((ADA_ROWS - 1 - N_LAT_SEQ, D_MODEL), F32)], axis=0)
    ada = _ada_table(c_all, w_ada, b_ada).reshape(DEPTH, ADA_ROWS, 6, 1, D_MODEL)

    cos_lat, sin_lat = _rope_tables()
    w_router_pad = jnp.pad(w_router, ((0, 0), (0, LANES - N_EXPERTS)))

    states = None
    x, h = _modulate(x_ctx, x_lat, ada[0])
    for l in range(DEPTH):
        ada5 = ada[l]
        z = _in_proj(h, w_in, l)

        ycp = _conv_pool(z, conv_w[l], pool_w[l], pool_scale[l])

        dl = jnp.broadcast_to(ret_decay_logit[l][:, :, None, None], (2, N_RET_HEADS, 1, HEAD_DIM))
        yret, states = _retention(z, cos_lat, sin_lat, dl, ret_gn_gain[l], state_retention, l, states)

        x1, h2, ei, ewt, rk, cnt = _out_proj(ycp, yret, x, w_out, l, ada5,
                                             ln1_g[l], ln1_b[l], w_router_pad, router_bias)
        pos = _pair_rows(ei, rk, cnt).reshape(-1)
        src_tok, te, n_valid = _route_tables(cnt[:, 0].astype(I32), pos)
        ys = _experts(h2, te, n_valid, src_tok, w_gate, w_up, w_down, l)
        if l + 1 < DEPTH:
            x, h = _final(x1, ys, pos, ewt, ada5, ln2_g[l], ln2_b[l], ada[l + 1])
        else:
            x_ctx, x_lat = _final(x1, ys, pos, ewt, ada5, ln2_g[l], ln2_b[l], None)

    y_prompt = x_ctx.reshape(N_CTX_SEQ, T_CTX, D_MODEL)
    y_sample = x_lat.reshape(N_LAT_SEQ, T_LAT, D_MODEL)
    return y_prompt, y_sample, states
```

```python
import functools

import jax
import jax.numpy as jnp
from jax import lax
from jax.experimental import pallas as pl
from jax.experimental.pallas import tpu as pltpu

F32 = jnp.float32
BF16 = jnp.bfloat16
I32 = jnp.int32

D_MODEL = 2048
N_CTX_SEQ, T_CTX = 16, 256
N_LAT_SEQ, T_LAT = 8, 1024
DEPTH = 2
M_CTX = N_CTX_SEQ * T_CTX
M_LAT = N_LAT_SEQ * T_LAT
M_TOK = M_CTX + M_LAT

GRID_W = 64
D_CONV = D_MODEL // 4
D_POOL = D_MODEL // 4
D_RET = D_MODEL // 2
N_RET_HEADS = 8
HEAD_DIM = D_RET // N_RET_HEADS
POOL_WINDOWS = (2, 4, 8, 16)
POOL_GROUP_DIM = D_POOL // len(POOL_WINDOWS)
CHUNK = 128
ROPE_BASE = 10000.0
N_EXPERTS = 16
EXPERTS_PER_GROUP = 4
N_EXPERT_GROUPS = N_EXPERTS // EXPERTS_PER_GROUP
D_EXPERT = D_MODEL // 4
D_IN_PROJ = 3 * D_CONV + D_POOL + 4 * D_RET
DEEPNORM_ALPHA = (2.0 * DEPTH) ** 0.25
LN_EPS = 1e-5
ADA_ROWS = 16

LANES = 128
VMEM_LIMIT = 56 * 1024 * 1024

TM_IN = 1024
TN_IN = 1536
TOK_ROWS = D_MODEL // LANES
TOK_STRIDE = TOK_ROWS + 1
TM_OP = 512
SUB_OP = 256
W_OUT_STAGE_ROWS = 256
TM_OUT = 512
TM_EXP = 256
N_PAIR = 2 * M_TOK
NP_EXP = N_PAIR + N_EXPERTS * TM_EXP
NT_EXP = NP_EXP // TM_EXP


def _cparams(sem):
    return pltpu.CompilerParams(dimension_semantics=sem, vmem_limit_bytes=VMEM_LIMIT)


def _silu(x):
    return x * jax.nn.sigmoid(x)


def _ada_row(i, tm):
    n_ctx_tiles = M_CTX // tm
    per_batch = T_LAT // tm
    return jnp.where(i < n_ctx_tiles, 0, 1 + (i - n_ctx_tiles) // per_batch)


def _ada_kernel(c_ref, w_ref, b_ref, o_ref):
    s = _silu(c_ref[...]).astype(BF16)
    o_ref[...] = jnp.dot(s, w_ref[...].astype(BF16), preferred_element_type=F32) + b_ref[...]


def _ada_table(c_all, w_ada, b_ada):
    tn = 1024
    n6 = 6 * D_MODEL
    return pl.pallas_call(
        _ada_kernel,
        grid=(DEPTH, n6 // tn),
        in_specs=[
            pl.BlockSpec((ADA_ROWS, D_MODEL), lambda l, j: (0, 0)),
            pl.BlockSpec((None, D_MODEL, tn), lambda l, j: (l, 0, j)),
            pl.BlockSpec((None, 1, tn), lambda l, j: (l, 0, j)),
        ],
        out_specs=pl.BlockSpec((None, ADA_ROWS, tn), lambda l, j: (l, 0, j)),
        out_shape=jax.ShapeDtypeStruct((DEPTH, ADA_ROWS, n6), F32),
        compiler_params=_cparams(("arbitrary", "arbitrary")),
        name="ada_table",
    )(c_all, w_ada, b_ada.reshape(DEPTH, 1, n6))


def _ctx_block(i, tm):
    return jnp.minimum(i, M_CTX // tm - 1)


def _lat_block(i, tm):
    return jnp.maximum(i - M_CTX // tm, 0)


def _modulate_kernel(xc_ref, xl_ref, sc_ref, sh_ref, x_ref, h_ref):
    x = jnp.where(pl.program_id(0) < M_CTX // xc_ref.shape[0], xc_ref[...], xl_ref[...])
    x_ref[...] = x
    h_ref[...] = (x * (1.0 + sc_ref[...]) + sh_ref[...]).astype(BF16)


def _modulate(x_ctx, x_lat, ada5):
    tm = TM_OUT
    ada_spec = lambda chunk: pl.BlockSpec(
        (None, None, 1, D_MODEL), lambda i: (_ada_row(i, tm), chunk, 0, 0))
    return pl.pallas_call(
        _modulate_kernel,
        grid=(M_TOK // tm,),
        in_specs=[
            pl.BlockSpec((tm, D_MODEL), lambda i: (_ctx_block(i, tm), 0)),
            pl.BlockSpec((tm, D_MODEL), lambda i: (_lat_block(i, tm), 0)),
            ada_spec(1), ada_spec(0),
        ],
        out_specs=[pl.BlockSpec((tm, D_MODEL), lambda i: (i, 0)),
                   pl.BlockSpec((tm, D_MODEL), lambda i: (i, 0))],
        out_shape=[jax.ShapeDtypeStruct((M_TOK, D_MODEL), F32),
                   jax.ShapeDtypeStruct((M_TOK, D_MODEL), BF16)],
        compiler_params=_cparams(("arbitrary",)),
        name="modulate",
    )(x_ctx, x_lat, ada5, ada5)


def _inproj_kernel(h_ref, w_ref, o_ref, wbf_scr):
    @pl.when(pl.program_id(1) == 0)
    def _():
        wbf_scr[...] = w_ref[...].astype(BF16)

    o_ref[...] = jnp.dot(h_ref[...], wbf_scr[...], preferred_element_type=F32).astype(BF16)


def _in_proj(h, w_in, layer):
    return pl.pallas_call(
        _inproj_kernel,
        grid=(D_IN_PROJ // TN_IN, M_TOK // TM_IN),
        in_specs=[
            pl.BlockSpec((TM_IN, D_MODEL), lambda j, i: (i, 0)),
            pl.BlockSpec((None, D_MODEL, TN_IN), lambda j, i: (layer, 0, j)),
        ],
        out_specs=pl.BlockSpec((TM_IN, TN_IN), lambda j, i: (i, j)),
        out_shape=jax.ShapeDtypeStruct((M_TOK, D_IN_PROJ), BF16),
        scratch_shapes=[pltpu.VMEM((D_MODEL, TN_IN), BF16)],
        compiler_params=_cparams(("arbitrary", "arbitrary")),
        name="in_proj",
    )(h, w_in)


MIX_ROWS = T_LAT
N_CTX_BLOCKS = M_CTX // MIX_ROWS
SEQ_PER_CTX_BLOCK = MIX_ROWS // T_CTX
CP_ROWS = T_CTX
CP_HALO = 128


def _convpool_kernel(z_ref, cw_ref, pw_ref, ps_ref, o_ref, band_ref):
    blk = pl.program_id(0)
    g_dim = POOL_GROUP_DIM

    @pl.when(blk == 0)
    def _():
        row = lax.broadcasted_iota(I32, (CP_ROWS, CP_ROWS + 2 * CP_HALO), 0)
        col = lax.broadcasted_iota(I32, (CP_ROWS, CP_ROWS + 2 * CP_HALO), 1)
        d = col - CP_HALO - row
        for gi, w in enumerate(POOL_WINDOWS):
            band_ref[gi] = jnp.where((d >= -(w // 2)) & (d < w // 2), 1.0, 0.0).astype(BF16)

    def block(seq_len):
        t = lax.broadcasted_iota(I32, (CP_ROWS, LANES), 0)
        for ch in range(MIX_ROWS // CP_ROWS):
            r0 = ch * CP_ROWS
            rows = slice(r0, r0 + CP_ROWS)
            pos0 = r0 % seq_len
            at_start = pos0 == 0
            at_end = pos0 + CP_ROWS == seq_len

            for cg in range(D_CONV // LANES):
                lanes = slice(cg * LANES, (cg + 1) * LANES)

                def u_rows(rs):
                    return (z_ref[rs, D_CONV + cg * LANES:D_CONV + (cg + 1) * LANES].astype(F32)
                            * z_ref[rs, 2 * D_CONV + cg * LANES:2 * D_CONV + (cg + 1) * LANES].astype(F32))
                u = u_rows(rows)
                before = 0.0 if at_start else u_rows(slice(r0 - 1, r0))
                after = 0.0 if at_end else u_rows(slice(r0 + CP_ROWS, r0 + CP_ROWS + 1))
                u_prev = jnp.where(t == 0, before, pltpu.roll(u, 1, 0))
                u_next = jnp.where(t == CP_ROWS - 1, after, pltpu.roll(u, CP_ROWS - 1, 0))
                conv = u_prev * cw_ref[0:1, lanes] + u * cw_ref[1:2, lanes] + u_next * cw_ref[2:3, lanes]
                o_ref[rows, lanes] = (z_ref[rows, lanes].astype(F32) * conv).astype(BF16)

            k_rows = slice(r0 if at_start else r0 - CP_HALO,
                           r0 + CP_ROWS if at_end else r0 + CP_ROWS + CP_HALO)
            b_cols = slice(CP_HALO if at_start else 0,
                           CP_HALO + CP_ROWS if at_end else CP_ROWS + 2 * CP_HALO)
            tpos = pos0 + t
            for gi, w in enumerate(POOL_WINDOWS):
                lo = 3 * D_CONV + gi * g_dim
                win = jnp.dot(band_ref[gi, :, b_cols], z_ref[k_rows, lo:lo + g_dim],
                              preferred_element_type=F32)
                cnt = (jnp.minimum(tpos + w // 2, seq_len) - jnp.maximum(tpos - w // 2, 0)).astype(F32)
                pooled = win / cnt - z_ref[rows, lo:lo + g_dim].astype(F32)
                y = jnp.dot(pooled.astype(BF16), pw_ref[gi].astype(BF16), preferred_element_type=F32)
                y = y * ps_ref[:, gi * g_dim:(gi + 1) * g_dim]
                o_ref[rows, D_CONV + gi * g_dim:D_CONV + (gi + 1) * g_dim] = y.astype(BF16)

    pl.when(blk < N_CTX_BLOCKS)(functools.partial(block, T_CTX))
    pl.when(blk >= N_CTX_BLOCKS)(functools.partial(block, T_LAT))


def _conv_pool(z, conv_w, pool_w, pool_scale):
    return pl.pallas_call(
        _convpool_kernel,
        grid=(M_TOK // MIX_ROWS,),
        in_specs=[
            pl.BlockSpec((MIX_ROWS, D_IN_PROJ - 4 * D_RET), lambda b: (b, 0)),
            pl.BlockSpec((3, D_CONV), lambda b: (0, 0)),
            pl.BlockSpec((len(POOL_WINDOWS), POOL_GROUP_DIM, POOL_GROUP_DIM), lambda b: (0, 0, 0)),
            pl.BlockSpec((1, D_POOL), lambda b: (0, 0)),
        ],
        out_specs=pl.BlockSpec((MIX_ROWS, D_CONV + D_POOL), lambda b: (b, 0)),
        out_shape=jax.ShapeDtypeStruct((M_TOK, D_CONV + D_POOL), BF16),
        scratch_shapes=[pltpu.VMEM((len(POOL_WINDOWS), CP_ROWS, CP_ROWS + 2 * CP_HALO), BF16)],
        compiler_params=_cparams(("arbitrary",)),
        name="conv_pool",
    )(z, conv_w, pool_w, pool_scale.reshape(1, D_POOL))


def _log_sigmoid(x):
    return jnp.minimum(x, 0.0) - jnp.log1p(jnp.exp(-jnp.abs(x)))


RET_ROWS = MIX_ROWS


def _retention_kernel(*refs, layer):
    if layer == 0:
        (q_ref, k_ref, v_ref, g_ref, cos_ref, sin_ref, dl_ref, gain_ref, s0_ref,
         y_ref, sfin_ref, tab_scr, kr_scr, u_scr, s_scr) = refs
    else:
        (q_ref, k_ref, v_ref, g_ref, cos_ref, sin_ref, dl_ref, gain_ref, s0_ref, _,
         y_ref, sfin_ref, tab_scr, kr_scr, u_scr, s_scr) = refs
    blk = pl.program_id(1)
    n_chunks = RET_ROWS // CHUNK
    half = HEAD_DIM // 2

    lg_f = _log_sigmoid(dl_ref[0])
    lg_b = _log_sigmoid(dl_ref[1])
    g_f = jnp.exp(lg_f * CHUNK)
    g_b = jnp.exp(lg_b * CHUNK)

    @pl.when(blk == 0)
    def _():
        row = lax.broadcasted_iota(I32, (CHUNK, CHUNK), 0).astype(F32)
        col = lax.broadcasted_iota(I32, (CHUNK, CHUNK), 1).astype(F32)
        diff = row - col
        tab_scr[0] = (jnp.where(diff >= 0, jnp.exp(lg_f * jnp.maximum(diff, 0.0)), 0.0)
                      + jnp.where(diff <= 0, jnp.exp(lg_b * jnp.maximum(-diff, 0.0)), 0.0))
        tab_scr[1] = jnp.exp(lg_f * (row + 1.0))
        tab_scr[2] = jnp.exp(lg_b * (CHUNK - row))
        tab_scr[3] = jnp.exp(lg_f * (CHUNK - 1.0 - row))
        tab_scr[4] = jnp.exp(lg_b * row)

    tn_dims = (((0,), (0,)), ((), ()))
    nt_dims = (((1,), (1,)), ((), ()))

    def block(seq_chunks, latent):
        def rope(x, sl):
            if not latent:
                return x
            return x * cos_ref[sl, :] + pltpu.roll(x, half, 1) * sin_ref[sl, :]

        for c in range(n_chunks):
            sl = slice(c * CHUNK, (c + 1) * CHUNK)
            kr = rope(k_ref[sl, :].astype(F32), sl) * (HEAD_DIM ** -0.5)
            kr_scr[sl, :] = kr.astype(BF16)
            kz = jnp.concatenate([kr * tab_scr[3], kr * tab_scr[4]], axis=1).astype(BF16)
            u_scr[c] = lax.dot_general(kz, v_ref[sl, :], tn_dims, preferred_element_type=F32)

        has_state = [[False] * n_chunks, [False] * n_chunks]
        for s_i in range(n_chunks // seq_chunks):
            chunks = list(range(s_i * seq_chunks, (s_i + 1) * seq_chunks))
            for d, order, g_d in ((0, chunks, g_f), (1, chunks[::-1], g_b)):
                s = s0_ref[d] if latent else None
                for c in order:
                    u = u_scr[c, d * HEAD_DIM:(d + 1) * HEAD_DIM, :]
                    if s is None:
                        s = u
                    else:
                        s_scr[d, c] = s.astype(BF16)
                        has_state[d][c] = True
                        s = g_d * s + u
                if not latent:
                    if layer == 0:
                        sfin_ref[s_i, 0, d] = s
                        for later in range(1, DEPTH):
                            sfin_ref[s_i, later, d] = jnp.zeros_like(s)
                    else:
                        sfin_ref[s_i, d] = s

        for c in range(n_chunks):
            sl = slice(c * CHUNK, (c + 1) * CHUNK)
            qr = rope(q_ref[sl, :].astype(F32), sl)
            scores = lax.dot_general(qr.astype(BF16), kr_scr[sl, :], nt_dims, preferred_element_type=F32)
            o = jnp.dot((scores * tab_scr[0]).astype(BF16), v_ref[sl, :], preferred_element_type=F32)
            for d in range(2):
                if has_state[d][c]:
                    o += jnp.dot((qr * tab_scr[1 + d]).astype(BF16), s_scr[d, c],
                                 preferred_element_type=F32)
            mu = jnp.mean(o, axis=-1, keepdims=True)
            dev = o - mu
            var = jnp.mean(dev * dev, axis=-1, keepdims=True)
            on = dev * lax.rsqrt(var + LN_EPS) * gain_ref[...]
            y_ref[sl, :] = (_silu(g_ref[sl, :].astype(F32)) * on).astype(BF16)

    pl.when(blk < N_CTX_BLOCKS)(functools.partial(block, T_CTX // CHUNK, False))
    pl.when(blk >= N_CTX_BLOCKS)(functools.partial(block, T_LAT // CHUNK, True))


def _retention(z, cos_t, sin_t, decay_logit_b, gain, s0, layer, states):
    qcol0 = (3 * D_CONV + D_POOL) // HEAD_DIM
    n_blocks = M_TOK // RET_ROWS

    def zspec(k):
        return pl.BlockSpec((RET_ROWS, HEAD_DIM), lambda h, b: (b, qcol0 + k * N_RET_HEADS + h))

    def lat_seq(b):
        return jnp.maximum(b - N_CTX_BLOCKS, 0)

    def ctx_block(b):
        return jnp.minimum(b, N_CTX_BLOCKS - 1)

    n_chunks = RET_ROWS // CHUNK
    in_specs = [
        zspec(0), zspec(1), zspec(2), zspec(3),
        pl.BlockSpec((RET_ROWS, HEAD_DIM), lambda h, b: (0, 0)),
        pl.BlockSpec((RET_ROWS, HEAD_DIM), lambda h, b: (0, 0)),
        pl.BlockSpec((2, None, 1, HEAD_DIM), lambda h, b: (0, h, 0, 0)),
        pl.BlockSpec((1, HEAD_DIM), lambda h, b: (0, h)),
        pl.BlockSpec((None, None, 2, None, HEAD_DIM, HEAD_DIM),
                     lambda h, b: (lat_seq(b), layer, 0, h, 0, 0)),
    ]
    args = [z, z, z, z, cos_t, sin_t, decay_logit_b, gain.reshape(1, D_RET), s0]
    if layer == 0:
        st_spec = pl.BlockSpec((SEQ_PER_CTX_BLOCK, DEPTH, 2, None, HEAD_DIM, HEAD_DIM),
                               lambda h, b: (ctx_block(b), 0, 0, h, 0, 0))
        aliases = {}
    else:
        st_spec = pl.BlockSpec((SEQ_PER_CTX_BLOCK, None, 2, None, HEAD_DIM, HEAD_DIM),
                               lambda h, b: (ctx_block(b), layer, 0, h, 0, 0))
        in_specs.append(pl.BlockSpec(memory_space=pl.ANY))
        args.append(states)
        aliases = {len(args) - 1: 1}
    return pl.pallas_call(
        functools.partial(_retention_kernel, layer=layer),
        grid=(N_RET_HEADS, n_blocks),
        in_specs=in_specs,
        out_specs=[pl.BlockSpec((RET_ROWS, HEAD_DIM), lambda h, b: (b, h)), st_spec],
        out_shape=[
            jax.ShapeDtypeStruct((M_TOK, D_RET), BF16),
            jax.ShapeDtypeStruct((N_CTX_SEQ, DEPTH, 2, N_RET_HEADS, HEAD_DIM, HEAD_DIM), F32),
        ],
        input_output_aliases=aliases,
        scratch_shapes=[
            pltpu.VMEM((5, CHUNK, CHUNK), F32),
            pltpu.VMEM((RET_ROWS, HEAD_DIM), BF16),
            pltpu.VMEM((n_chunks, 2 * HEAD_DIM, HEAD_DIM), F32),
            pltpu.VMEM((2, n_chunks, HEAD_DIM, HEAD_DIM), BF16),
        ],
        compiler_params=_cparams(("arbitrary", "arbitrary")),
        name="retention",
    )(*args)


def _layer_norm_rows(r, g, b):
    mu = jnp.mean(r, axis=-1, keepdims=True)
    dev = r - mu
    var = jnp.mean(dev * dev, axis=-1, keepdims=True)
    return dev * lax.rsqrt(var + LN_EPS) * g + b


def _top2_of4(vals):
    top1 = jnp.maximum(jnp.maximum(vals[0], vals[1]), jnp.maximum(vals[2], vals[3]))
    idx1 = jnp.where(vals[0] == top1, 0, jnp.where(vals[1] == top1, 1, jnp.where(vals[2] == top1, 2, 3)))
    neg = jnp.float32(-jnp.inf)
    rest = [jnp.where(idx1 == j, neg, vals[j]) for j in range(4)]
    top2 = jnp.maximum(jnp.maximum(rest[0], rest[1]), jnp.maximum(rest[2], rest[3]))
    idx2 = jnp.where(rest[0] == top2, 0, jnp.where(rest[1] == top2, 1, jnp.where(rest[2] == top2, 2, 3)))
    return top1, idx1, top2, idx2


def _store_token_tiles(ref, val, tok0=0):
    n = val.shape[0]
    base = tok0 * TOK_STRIDE
    for c in range(TOK_ROWS):
        ref[pl.ds(base + c, n, stride=TOK_STRIDE), :] = val[:, c * LANES:(c + 1) * LANES]
    ref[pl.ds(base + TOK_ROWS, n, stride=TOK_STRIDE), :] = jnp.zeros((n, LANES), val.dtype)


def _load_token_tiles(ref, tok0, n):
    return jnp.concatenate(
        [ref[pl.ds(tok0 * TOK_STRIDE + c, n, stride=TOK_STRIDE), :] for c in range(TOK_ROWS)], axis=1)


def _outproj_kernel(ycp_ref, yret_ref, x_ref, wo_hbm, gate1_ref, lng_ref, lnb_ref, sc2_ref, sh2_ref,
                    wr_ref, rb_ref, x1_ref, h2_ref, ei_ref, ewt_ref, rk_ref, cnt_ref,
                    carry_scr, wo_ref, wstage, wsem, wr_hi, wr_lo, *, layer):
    i = pl.program_id(0)

    @pl.when(i == 0)
    def _():
        carry_scr[...] = jnp.zeros_like(carry_scr)
        wr = wr_ref[...]
        hi = wr.astype(BF16)
        wr_hi[...] = hi
        wr_lo[...] = (wr - hi.astype(F32)).astype(BF16)
        rows = wstage.shape[1]
        n_chunks = D_MODEL // rows

        def chunk_copy(c):
            return pltpu.make_async_copy(wo_hbm.at[layer, pl.ds(c * rows, rows), :], wstage.at[c % 2],
                                         wsem.at[c % 2])
        chunk_copy(0).start()
        for c in range(n_chunks):
            if c + 1 < n_chunks:
                chunk_copy(c + 1).start()
            chunk_copy(c).wait()
            wo_ref[c * rows:(c + 1) * rows, :] = wstage[c % 2].astype(BF16)

    s_i = lax.broadcasted_iota(I32, (SUB_OP, SUB_OP), 0)
    t_i = lax.broadcasted_iota(I32, (SUB_OP, SUB_OP), 1)
    tri = jnp.where(s_i < t_i, 1.0, 0.0).astype(BF16)
    half_k = D_CONV + D_POOL
    n_sub = x_ref.shape[0] // SUB_OP
    ys = []
    for sub in range(n_sub):
        rows_sl = slice(sub * SUB_OP, (sub + 1) * SUB_OP)
        y = jnp.dot(ycp_ref[rows_sl, :], wo_ref[0:half_k, :], preferred_element_type=F32)
        ys.append(y + jnp.dot(yret_ref[rows_sl, :], wo_ref[half_k:, :], preferred_element_type=F32))
    lts = [_ln_router_subtile(sub, ys[sub], x_ref, gate1_ref, lng_ref, lnb_ref, sc2_ref, sh2_ref,
                              wr_hi, wr_lo, x1_ref, h2_ref) for sub in range(n_sub)]
    carry = carry_scr[:, 0:1]
    for sub in range(n_sub):
        carry = _route_subtile(sub, lts[sub], carry, tri, rb_ref, ei_ref, ewt_ref, rk_ref)
    carry_scr[...] = jnp.broadcast_to(carry, carry_scr.shape)
    cnt_ref[...] = carry_scr[...]


def _ln_router_subtile(sub, y, x_ref, gate1_ref, lng_ref, lnb_ref, sc2_ref, sh2_ref, wr_hi, wr_lo,
                       x1_ref, h2_ref):
    tm = SUB_OP
    rows_sl = slice(sub * tm, (sub + 1) * tm)
    x1 = _layer_norm_rows(DEEPNORM_ALPHA * x_ref[rows_sl, :] + gate1_ref[...] * y,
                          lng_ref[...], lnb_ref[...])
    x1_ref[rows_sl, :] = x1
    h2 = x1 * (1.0 + sc2_ref[...]) + sh2_ref[...]
    _store_token_tiles(h2_ref, h2, tok0=sub * tm)

    h_hi = h2.astype(BF16)
    h_lo = (h2 - h_hi.astype(F32)).astype(BF16)
    logits = (jnp.dot(h_hi, wr_hi[...], preferred_element_type=F32)
              + jnp.dot(h_lo, wr_hi[...], preferred_element_type=F32)
              + jnp.dot(h_hi, wr_lo[...], preferred_element_type=F32))
    return logits.T


def _route_subtile(sub, lt, carry, tri, rb_ref, ei_ref, ewt_ref, rk_ref):
    tm = SUB_OP
    rows_sl = slice(sub * tm, (sub + 1) * tm)
    rows = [lt[e:e + 1, :] for e in range(N_EXPERTS)]

    mx = rows[0]
    for e in range(1, N_EXPERTS):
        mx = jnp.maximum(mx, rows[e])
    ex = [jnp.exp(r - mx) for r in rows]
    den = ex[0]
    for e in range(1, N_EXPERTS):
        den = den + ex[e]
    score = [x / den for x in ex]
    biased = [score[e] + rb_ref[e] for e in range(N_EXPERTS)]

    best = None
    for gi in range(N_EXPERT_GROUPS):
        t1, i1, t2, i2 = _top2_of4(biased[gi * EXPERTS_PER_GROUP:(gi + 1) * EXPERTS_PER_GROUP])
        gs = t1 + t2
        e1 = gi * EXPERTS_PER_GROUP + i1
        e2 = gi * EXPERTS_PER_GROUP + i2
        if best is None:
            best = (gs, e1, e2)
        else:
            take = gs > best[0]
            best = (jnp.where(take, gs, best[0]), jnp.where(take, e1, best[1]), jnp.where(take, e2, best[2]))
    _, e1, e2 = best
    zero = jnp.zeros_like(score[0])
    w1 = zero
    w2 = zero
    for e in range(N_EXPERTS):
        w1 = w1 + jnp.where(e1 == e, score[e], 0.0)
        w2 = w2 + jnp.where(e2 == e, score[e], 0.0)
    wsum = w1 + w2
    ei_ref[0:1, rows_sl] = e1
    ei_ref[1:2, rows_sl] = e2
    ewt_ref[rows_sl, :] = jnp.concatenate(
        [w1 / wsum, w2 / wsum, jnp.zeros((LANES - 2, tm), F32)], axis=0).T

    onehot = jnp.concatenate(
        [jnp.where((e1 == e) | (e2 == e), 1.0, 0.0) for e in range(N_EXPERTS)], axis=0)
    prefix = jnp.dot(onehot.astype(BF16), tri, preferred_element_type=F32) + carry
    r1 = zero
    r2 = zero
    for e in range(N_EXPERTS):
        r1 = r1 + jnp.where(e1 == e, prefix[e:e + 1, :], 0.0)
        r2 = r2 + jnp.where(e2 == e, prefix[e:e + 1, :], 0.0)
    rk_ref[0:1, rows_sl] = r1.astype(I32)
    rk_ref[1:2, rows_sl] = r2.astype(I32)
    return carry + jnp.sum(onehot, axis=1, keepdims=True)


def _out_proj(ycp, yret, x, w_out, layer, ada5, ln_g, ln_b, w_router_pad, router_bias):
    tm = TM_OP
    ada_spec = lambda chunk: pl.BlockSpec(
        (None, None, 1, D_MODEL), lambda i: (_ada_row(i, tm), chunk, 0, 0))
    vec_spec = pl.BlockSpec((1, D_MODEL), lambda i: (0, 0))
    route_spec = pl.BlockSpec((2, tm), lambda i: (0, i))
    return pl.pallas_call(
        functools.partial(_outproj_kernel, layer=layer),
        grid=(M_TOK // tm,),
        in_specs=[
            pl.BlockSpec((tm, D_CONV + D_POOL), lambda i: (i, 0)),
            pl.BlockSpec((tm, D_RET), lambda i: (i, 0)),
            pl.BlockSpec((tm, D_MODEL), lambda i: (i, 0)),
            pl.BlockSpec(memory_space=pl.ANY),
            ada_spec(2),
            vec_spec, vec_spec,
            ada_spec(4),
            ada_spec(3),
            pl.BlockSpec((D_MODEL, LANES), lambda i: (0, 0)),
            pl.BlockSpec(memory_space=pltpu.SMEM),
        ],
        out_specs=[
            pl.BlockSpec((tm, D_MODEL), lambda i: (i, 0)),
            pl.BlockSpec((tm * TOK_STRIDE, LANES), lambda i: (i, 0)),
            route_spec,
            pl.BlockSpec((tm, LANES), lambda i: (i, 0)),
            route_spec,
            pl.BlockSpec((N_EXPERTS, LANES), lambda i: (0, 0)),
        ],
        out_shape=[
            jax.ShapeDtypeStruct((M_TOK, D_MODEL), F32),
            jax.ShapeDtypeStruct((M_TOK * TOK_STRIDE, LANES), F32),
            jax.ShapeDtypeStruct((2, M_TOK), I32),
            jax.ShapeDtypeStruct((M_TOK, LANES), F32),
            jax.ShapeDtypeStruct((2, M_TOK), I32),
            jax.ShapeDtypeStruct((N_EXPERTS, LANES), F32),
        ],
        scratch_shapes=[
            pltpu.VMEM((N_EXPERTS, LANES), F32),
            pltpu.VMEM((D_MODEL, D_MODEL), BF16),
            pltpu.VMEM((2, W_OUT_STAGE_ROWS, D_MODEL), F32),
            pltpu.SemaphoreType.DMA((2,)),
            pltpu.VMEM((D_MODEL, LANES), BF16),
            pltpu.VMEM((D_MODEL, LANES), BF16),
        ],
        compiler_params=_cparams(("arbitrary",)),
        name="out_proj_router",
    )(ycp, yret, x, w_out, ada5, ln_g.reshape(1, D_MODEL), ln_b.reshape(1, D_MODEL),
      ada5, ada5, w_router_pad, router_bias)


def _pos_kernel(ei_ref, rk_ref, cnt_ref, pos_ref):
    ei = ei_ref[...]
    pos = rk_ref[...]
    start = jnp.zeros((1, 1), F32)
    for e in range(N_EXPERTS):
        pos = pos + jnp.where(ei == e, start.astype(I32), 0)
        n_tiles = jnp.floor((cnt_ref[e:e + 1, 0:1] + (TM_EXP - 1.0)) * (1.0 / TM_EXP))
        start = start + n_tiles * TM_EXP
    pos_ref[...] = pos


def _pair_rows(ei, rk, cnt):
    return pl.pallas_call(
        _pos_kernel,
        out_shape=jax.ShapeDtypeStruct((2, M_TOK), I32),
        name="pair_rows",
    )(ei, rk, cnt)


def _route_kernel(cnt_ref, pos_ref, src_ref, te_ref, nv_ref):
    def zero(r, carry):
        src_ref[r] = 0
        return carry

    tile = jnp.int32(0)
    for e in range(N_EXPERTS):
        n_tiles = lax.shift_right_logical(cnt_ref[e] + (TM_EXP - 1), TM_EXP.bit_length() - 1)

        def mark(j, carry, e=e, tile=tile):
            te_ref[tile + j] = e
            return carry
        lax.fori_loop(0, n_tiles, mark, 0)
        lax.fori_loop(tile * TM_EXP + cnt_ref[e], (tile + n_tiles) * TM_EXP, zero, 0)
        tile = tile + n_tiles
    nv_ref[0] = tile
    last_expert = te_ref[tile - 1]

    def mark_unused(j, carry):
        te_ref[j] = last_expert
        return carry
    lax.fori_loop(tile, NT_EXP, mark_unused, 0)
    lax.fori_loop(tile * TM_EXP, NP_EXP, zero, 0)

    def place(t, carry):
        src_ref[pos_ref[t]] = t
        src_ref[pos_ref[M_TOK + t]] = t
        return carry
    lax.fori_loop(0, M_TOK, place, 0, unroll=16)


def _route_tables(cnt, pos):
    smem = pl.BlockSpec(memory_space=pltpu.SMEM)
    return pl.pallas_call(
        _route_kernel,
        in_specs=[smem, smem],
        out_specs=[smem, smem, smem],
        out_shape=[
            jax.ShapeDtypeStruct((NP_EXP,), I32),
            jax.ShapeDtypeStruct((NT_EXP,), I32),
            jax.ShapeDtypeStruct((1,), I32),
        ],
        name="route_tables",
    )(cnt, pos)


GATHER_PRIORITY = 1


def _row_gather_start(src_hbm, buf, sem, idx_ref, base, n_tok, tok0=0, priority=0):
    for r in range(n_tok):
        pltpu.make_async_copy(src_hbm.at[pl.ds(idx_ref[base + r] * TOK_STRIDE, TOK_ROWS), :],
                              buf.at[pl.ds((tok0 + r) * TOK_STRIDE, TOK_ROWS), :], sem).start(priority)


def _row_gather_wait(src_hbm, buf, sem):
    n_rows = buf.shape[0] // TOK_STRIDE * TOK_ROWS
    pltpu.make_async_copy(src_hbm.at[pl.ds(0, n_rows), :], buf.at[pl.ds(0, n_rows), :], sem).wait()


N_XBUF = 4


def _experts_kernel(te_ref, nv_ref, src_ref, h2_hbm, wg_ref, wu_ref, wd_ref, o_ref, *scratch):
    bufs = scratch[:N_XBUF]
    gsem, wg_bf, wu_bf, wd_bf = scratch[N_XBUF:]
    i = pl.program_id(0)
    n_valid = nv_ref[0]

    def start_tile(tile, slot):
        tile = jnp.minimum(tile, NT_EXP - 1)
        _row_gather_start(h2_hbm, bufs[slot], gsem.at[slot], src_ref, tile * TM_EXP, TM_EXP,
                          priority=GATHER_PRIORITY)

    @pl.when(i == 0)
    def _():
        for t in range(N_XBUF - 1):
            start_tile(t, t)

    @pl.when((i < n_valid) & ((i == 0) | (te_ref[i] != te_ref[jnp.maximum(i - 1, 0)])))
    def _():
        wg_bf[...] = wg_ref[...].astype(BF16)
        wu_bf[...] = wu_ref[...].astype(BF16)
        wd_bf[...] = wd_ref[...].astype(BF16)

    def step(cur):
        _row_gather_wait(h2_hbm, bufs[cur], gsem.at[cur])
        start_tile(i + N_XBUF - 1, (cur + N_XBUF - 1) % N_XBUF)
        x = _load_token_tiles(bufs[cur], 0, TM_EXP).astype(BF16)
        g = jnp.dot(x, wg_bf[...], preferred_element_type=F32)
        u = jnp.dot(x, wu_bf[...], preferred_element_type=F32)
        a = (_silu(g) * u).astype(BF16)
        _store_token_tiles(o_ref, jnp.dot(a, wd_bf[...], preferred_element_type=F32))

        @pl.when(i == n_valid - 1)
        def _():
            for ahead in range(1, N_XBUF):
                slot = (cur + ahead) % N_XBUF
                _row_gather_wait(h2_hbm, bufs[slot], gsem.at[slot])

    for cur in range(N_XBUF):
        pl.when((i < n_valid) & (i % N_XBUF == cur))(functools.partial(step, cur))

    @pl.when(i >= n_valid)
    def _():
        o_ref[...] = jnp.zeros_like(o_ref)


def _experts(h2, tile_expert, n_valid, src_tok, wg, wu, wd, layer):
    grid_spec = pltpu.PrefetchScalarGridSpec(
        num_scalar_prefetch=3,
        grid=(NT_EXP,),
        in_specs=[
            pl.BlockSpec(memory_space=pl.ANY),
            pl.BlockSpec((None, None, D_MODEL, D_EXPERT), lambda i, te, nv, src: (layer, te[i], 0, 0)),
            pl.BlockSpec((None, None, D_MODEL, D_EXPERT), lambda i, te, nv, src: (layer, te[i], 0, 0)),
            pl.BlockSpec((None, None, D_EXPERT, D_MODEL), lambda i, te, nv, src: (layer, te[i], 0, 0)),
        ],
        out_specs=pl.BlockSpec((TM_EXP * TOK_STRIDE, LANES), lambda i, te, nv, src: (i, 0)),
        scratch_shapes=[pltpu.VMEM((TM_EXP * TOK_STRIDE, LANES), F32) for _ in range(N_XBUF)] + [
            pltpu.SemaphoreType.DMA((N_XBUF,)),
            pltpu.VMEM((D_MODEL, D_EXPERT), BF16),
            pltpu.VMEM((D_MODEL, D_EXPERT), BF16),
            pltpu.VMEM((D_EXPERT, D_MODEL), BF16),
        ],
    )
    return pl.pallas_call(
        _experts_kernel,
        grid_spec=grid_spec,
        out_shape=jax.ShapeDtypeStruct((NP_EXP * TOK_STRIDE, LANES), F32),
        compiler_params=_cparams(("arbitrary",)),
        name="experts",
    )(tile_expert, n_valid, src_tok, h2, wg, wu, wd)


N_RBUF = 3


def _final_kernel(*refs, emit_h):
    n_out = 2
    if emit_h:
        (pos_ref, x1_ref, ys_hbm, ewt_ref, gate2_ref, lng_ref, lnb_ref, sc1_ref, sh1_ref,
         x_ref, h_ref) = refs[:9 + n_out]
    else:
        (pos_ref, x1_ref, ys_hbm, ewt_ref, gate2_ref, lng_ref, lnb_ref, xc_ref, xl_ref) = refs[:7 + n_out]
    bufs = refs[-N_RBUF - 1:-1]
    sem = refs[-1]
    tm = TM_OUT
    i = pl.program_id(0)
    n_blocks = M_TOK // tm

    def start(tile, slot):
        tile = jnp.minimum(tile, n_blocks - 1)
        _row_gather_start(ys_hbm, bufs[slot], sem.at[slot], pos_ref, tile * tm, tm, tok0=0)
        _row_gather_start(ys_hbm, bufs[slot], sem.at[slot], pos_ref, M_TOK + tile * tm, tm, tok0=tm)

    @pl.when(i == 0)
    def _():
        for t in range(N_RBUF - 1):
            start(t, t)

    def step(cur):
        buf = bufs[cur]
        _row_gather_wait(ys_hbm, buf, sem.at[cur])
        start(i + N_RBUF - 1, (cur + N_RBUF - 1) % N_RBUF)
        w = ewt_ref[...]
        y2 = w[:, 0:1] * _load_token_tiles(buf, 0, tm) + w[:, 1:2] * _load_token_tiles(buf, tm, tm)
        x2 = _layer_norm_rows(DEEPNORM_ALPHA * x1_ref[...] + gate2_ref[...] * y2,
                              lng_ref[...], lnb_ref[...])
        if emit_h:
            x_ref[...] = x2
            h_ref[...] = (x2 * (1.0 + sc1_ref[...]) + sh1_ref[...]).astype(BF16)
        else:
            @pl.when(i < M_CTX // tm)
            def _():
                xc_ref[...] = x2

            @pl.when(i >= M_CTX // tm)
            def _():
                xl_ref[...] = x2

        @pl.when(i == n_blocks - 1)
        def _():
            for ahead in range(1, N_RBUF):
                slot = (cur + ahead) % N_RBUF
                _row_gather_wait(ys_hbm, bufs[slot], sem.at[slot])

    for cur in range(N_RBUF):
        pl.when(i % N_RBUF == cur)(functools.partial(step, cur))


def _final(x1, ys, pos, ewt, ada5, ln_g, ln_b, ada5_next):
    tm = TM_OUT
    emit_h = ada5_next is not None
    vec_spec = pl.BlockSpec((1, D_MODEL), lambda i, pos: (0, 0))
    ada_spec = lambda chunk: pl.BlockSpec(
        (None, None, 1, D_MODEL), lambda i, pos: (_ada_row(i, tm), chunk, 0, 0))
    in_specs = [
        pl.BlockSpec((tm, D_MODEL), lambda i, pos: (i, 0)),
        pl.BlockSpec(memory_space=pl.ANY),
        pl.BlockSpec((tm, LANES), lambda i, pos: (i, 0)),
        ada_spec(5),
        vec_spec, vec_spec,
    ]
    args = [pos, x1, ys, ewt, ada5, ln_g.reshape(1, D_MODEL), ln_b.reshape(1, D_MODEL)]
    if emit_h:
        in_specs += [ada_spec(1), ada_spec(0)]
        args += [ada5_next, ada5_next]
        row_spec = pl.BlockSpec((tm, D_MODEL), lambda i, pos: (i, 0))
        out_specs = [row_spec, row_spec]
        out_shape = [jax.ShapeDtypeStruct((M_TOK, D_MODEL), F32),
                     jax.ShapeDtypeStruct((M_TOK, D_MODEL), BF16)]
    else:
        out_specs = [
            pl.BlockSpec((tm, D_MODEL), lambda i, pos: (_ctx_block(i, tm), 0)),
            pl.BlockSpec((tm, D_MODEL), lambda i, pos: (_lat_block(i, tm), 0)),
        ]
        out_shape = [jax.ShapeDtypeStruct((M_CTX, D_MODEL), F32),
                     jax.ShapeDtypeStruct((M_LAT, D_MODEL), F32)]
    grid_spec = pltpu.PrefetchScalarGridSpec(
        num_scalar_prefetch=1,
        grid=(M_TOK // tm,),
        in_specs=in_specs,
        out_specs=out_specs,
        scratch_shapes=[pltpu.VMEM((2 * tm * TOK_STRIDE, LANES), F32) for _ in range(N_RBUF)] + [
            pltpu.SemaphoreType.DMA((N_RBUF,)),
        ],
    )
    return pl.pallas_call(
        functools.partial(_final_kernel, emit_h=emit_h),
        grid_spec=grid_spec,
        out_shape=out_shape,
        compiler_params=_cparams(("arbitrary",)),
        name="final_ln",
    )(*args)


def _rope_tables():
    rows = T_LAT // GRID_W
    row = jnp.repeat(jnp.arange(rows), GRID_W).astype(F32)
    col = jnp.tile(jnp.arange(GRID_W), rows).astype(F32)
    n_freq = HEAD_DIM // 4
    inv_freq = ROPE_BASE ** (-jnp.arange(n_freq, dtype=F32) / n_freq)
    ang = jnp.concatenate([row[:, None] * inv_freq[None], col[:, None] * inv_freq[None]], axis=-1)
    cos, sin = jnp.cos(ang), jnp.sin(ang)
    return jnp.concatenate([cos, cos], axis=-1), jnp.concatenate([-sin, sin], axis=-1)


def kernel(x_prompt, x_sample, state_retention, c, c_ctx, w_ada, b_ada, w_in, w_out, conv_w, pool_w,
           pool_scale, ret_decay_logit, ret_gn_gain, ln1_g, ln1_b, ln2_g, ln2_b, w_router, router_bias,
           w_gate, w_up, w_down):
    x_ctx = x_prompt.reshape(M_CTX, D_MODEL)
    x_lat = x_sample.reshape(M_LAT, D_MODEL)
    c_all = jnp.concatenate(
        [c_ctx[None, :], c, jnp.zeros((ADA_ROWS - 1 - N_LAT_SEQ, D_MODEL), F32)], axis=0)
    ada = _ada_table(c_all, w_ada, b_ada).reshape(DEPTH, ADA_ROWS, 6, 1, D_MODEL)

    cos_lat, sin_lat = _rope_tables()
    w_router_pad = jnp.pad(w_router, ((0, 0), (0, LANES - N_EXPERTS)))

    states = None
    x, h = _modulate(x_ctx, x_lat, ada[0])
    for l in range(DEPTH):
        ada5 = ada[l]
        z = _in_proj(h, w_in, l)

        ycp = _conv_pool(z, conv_w[l], pool_w[l], pool_scale[l])

        dl = jnp.broadcast_to(ret_decay_logit[l][:, :, None, None], (2, N_RET_HEADS, 1, HEAD_DIM))
        yret, states = _retention(z, cos_lat, sin_lat, dl, ret_gn_gain[l], state_retention, l, states)

        x1, h2, ei, ewt, rk, cnt = _out_proj(ycp, yret, x, w_out, l, ada5,
                                             ln1_g[l], ln1_b[l], w_router_pad, router_bias)
        pos = _pair_rows(ei, rk, cnt).reshape(-1)
        src_tok, te, n_valid = _route_tables(cnt[:, 0].astype(I32), pos)
        ys = _experts(h2, te, n_valid, src_tok, w_gate, w_up, w_down, l)
        if l + 1 < DEPTH:
            x, h = _final(x1, ys, pos, ewt, ada5, ln2_g[l], ln2_b[l], ada[l + 1])
        else:
            x_ctx, x_lat = _final(x1, ys, pos, ewt, ada5, ln2_g[l], ln2_b[l], None)

    y_prompt = x_ctx.reshape(N_CTX_SEQ, T_CTX, D_MODEL)
    y_sample = x_lat.reshape(N_LAT_SEQ, T_LAT, D_MODEL)
    return y_prompt, y_sample, states
```

```python
import functools

import jax
import jax.numpy as jnp
from jax import lax
from jax.experimental import pallas as pl
from jax.experimental.pallas import tpu as pltpu

F32 = jnp.float32
BF16 = jnp.bfloat16
I32 = jnp.int32

D_MODEL = 2048
N_CTX_SEQ, T_CTX = 16, 256
N_LAT_SEQ, T_LAT = 8, 1024
DEPTH = 2
M_CTX = N_CTX_SEQ * T_CTX
M_LAT = N_LAT_SEQ * T_LAT
M_TOK = M_CTX + M_LAT

GRID_W = 64
D_CONV = D_MODEL // 4
D_POOL = D_MODEL // 4
D_RET = D_MODEL // 2
N_RET_HEADS = 8
HEAD_DIM = D_RET // N_RET_HEADS
POOL_WINDOWS = (2, 4, 8, 16)
POOL_GROUP_DIM = D_POOL // len(POOL_WINDOWS)
CHUNK = 128
ROPE_BASE = 10000.0
N_EXPERTS = 16
EXPERTS_PER_GROUP = 4
N_EXPERT_GROUPS = N_EXPERTS // EXPERTS_PER_GROUP
D_EXPERT = D_MODEL // 4
D_IN_PROJ = 3 * D_CONV + D_POOL + 4 * D_RET
DEEPNORM_ALPHA = (2.0 * DEPTH) ** 0.25
LN_EPS = 1e-5
ADA_ROWS = 16

LANES = 128
VMEM_LIMIT = 56 * 1024 * 1024

TM_IN = 1024
TN_IN = 1536
TOK_ROWS = D_MODEL // LANES
TOK_STRIDE = TOK_ROWS + 1
TM_OP = 512
SUB_OP = 256
W_OUT_STAGE_ROWS = 256
TM_OUT = 512
TM_EXP = 256
N_PAIR = 2 * M_TOK
NP_EXP = N_PAIR + N_EXPERTS * TM_EXP
NT_EXP = NP_EXP // TM_EXP


def _cparams(sem):
    return pltpu.CompilerParams(dimension_semantics=sem, vmem_limit_bytes=VMEM_LIMIT)


def _silu(x):
    return x * jax.nn.sigmoid(x)


def _ada_row(i, tm):
    n_ctx_tiles = M_CTX // tm
    per_batch = T_LAT // tm
    return jnp.where(i < n_ctx_tiles, 0, 1 + (i - n_ctx_tiles) // per_batch)


def _ada_kernel(c_ref, w_ref, b_ref, o_ref):
    s = _silu(c_ref[...]).astype(BF16)
    o_ref[...] = jnp.dot(s, w_ref[...].astype(BF16), preferred_element_type=F32) + b_ref[...]


def _ada_table(c_all, w_ada, b_ada):
    tn = 1024
    n6 = 6 * D_MODEL
    return pl.pallas_call(
        _ada_kernel,
        grid=(DEPTH, n6 // tn),
        in_specs=[
            pl.BlockSpec((ADA_ROWS, D_MODEL), lambda l, j: (0, 0)),
            pl.BlockSpec((None, D_MODEL, tn), lambda l, j: (l, 0, j)),
            pl.BlockSpec((None, 1, tn), lambda l, j: (l, 0, j)),
        ],
        out_specs=pl.BlockSpec((None, ADA_ROWS, tn), lambda l, j: (l, 0, j)),
        out_shape=jax.ShapeDtypeStruct((DEPTH, ADA_ROWS, n6), F32),
        compiler_params=_cparams(("arbitrary", "arbitrary")),
        name="ada_table",
    )(c_all, w_ada, b_ada.reshape(DEPTH, 1, n6))


def _ctx_block(i, tm):
    return jnp.minimum(i, M_CTX // tm - 1)


def _lat_block(i, tm):
    return jnp.maximum(i - M_CTX // tm, 0)


def _modulate_kernel(xc_ref, xl_ref, sc_ref, sh_ref, x_ref, h_ref):
    x = jnp.where(pl.program_id(0) < M_CTX // xc_ref.shape[0], xc_ref[...], xl_ref[...])
    x_ref[...] = x
    h_ref[...] = (x * (1.0 + sc_ref[...]) + sh_ref[...]).astype(BF16)


def _modulate(x_ctx, x_lat, ada5):
    tm = TM_OUT
    ada_spec = lambda chunk: pl.BlockSpec(
        (None, None, 1, D_MODEL), lambda i: (_ada_row(i, tm), chunk, 0, 0))
    return pl.pallas_call(
        _modulate_kernel,
        grid=(M_TOK // tm,),
        in_specs=[
            pl.BlockSpec((tm, D_MODEL), lambda i: (_ctx_block(i, tm), 0)),
            pl.BlockSpec((tm, D_MODEL), lambda i: (_lat_block(i, tm), 0)),
            ada_spec(1), ada_spec(0),
        ],
        out_specs=[pl.BlockSpec((tm, D_MODEL), lambda i: (i, 0)),
                   pl.BlockSpec((tm, D_MODEL), lambda i: (i, 0))],
        out_shape=[jax.ShapeDtypeStruct((M_TOK, D_MODEL), F32),
                   jax.ShapeDtypeStruct((M_TOK, D_MODEL), BF16)],
        compiler_params=_cparams(("arbitrary",)),
        name="modulate",
    )(x_ctx, x_lat, ada5, ada5)


def _inproj_kernel(h_ref, w_ref, o_ref, wbf_scr):
    @pl.when(pl.program_id(1) == 0)
    def _():
        wbf_scr[...] = w_ref[...].astype(BF16)

    o_ref[...] = jnp.dot(h_ref[...], wbf_scr[...], preferred_element_type=F32).astype(BF16)


def _in_proj(h, w_in, layer):
    return pl.pallas_call(
        _inproj_kernel,
        grid=(D_IN_PROJ // TN_IN, M_TOK // TM_IN),
        in_specs=[
            pl.BlockSpec((TM_IN, D_MODEL), lambda j, i: (i, 0)),
            pl.BlockSpec((None, D_MODEL, TN_IN), lambda j, i: (layer, 0, j)),
        ],
        out_specs=pl.BlockSpec((TM_IN, TN_IN), lambda j, i: (i, j)),
        out_shape=jax.ShapeDtypeStruct((M_TOK, D_IN_PROJ), BF16),
        scratch_shapes=[pltpu.VMEM((D_MODEL, TN_IN), BF16)],
        compiler_params=_cparams(("arbitrary", "arbitrary")),
        name="in_proj",
    )(h, w_in)


MIX_ROWS = T_LAT
N_CTX_BLOCKS = M_CTX // MIX_ROWS
SEQ_PER_CTX_BLOCK = MIX_ROWS // T_CTX
CP_ROWS = T_CTX
CP_HALO = 128


def _convpool_kernel(z_ref, cw_ref, pw_ref, ps_ref, o_ref, band_ref):
    blk = pl.program_id(0)
    g_dim = POOL_GROUP_DIM

    @pl.when(blk == 0)
    def _():
        row = lax.broadcasted_iota(I32, (CP_ROWS, CP_ROWS + 2 * CP_HALO), 0)
        col = lax.broadcasted_iota(I32, (CP_ROWS, CP_ROWS + 2 * CP_HALO), 1)
        d = col - CP_HALO - row
        for gi, w in enumerate(POOL_WINDOWS):
            band_ref[gi] = jnp.where((d >= -(w // 2)) & (d < w // 2), 1.0, 0.0).astype(BF16)

    def block(seq_len):
        t = lax.broadcasted_iota(I32, (CP_ROWS, LANES), 0)
        for ch in range(MIX_ROWS // CP_ROWS):
            r0 = ch * CP_ROWS
            rows = slice(r0, r0 + CP_ROWS)
            pos0 = r0 % seq_len
            at_start = pos0 == 0
            at_end = pos0 + CP_ROWS == seq_len

            for cg in range(D_CONV // LANES):
                lanes = slice(cg * LANES, (cg + 1) * LANES)

                def u_rows(rs):
                    return (z_ref[rs, D_CONV + cg * LANES:D_CONV + (cg + 1) * LANES].astype(F32)
                            * z_ref[rs, 2 * D_CONV + cg * LANES:2 * D_CONV + (cg + 1) * LANES].astype(F32))
                u = u_rows(rows)
                before = 0.0 if at_start else u_rows(slice(r0 - 1, r0))
                after = 0.0 if at_end else u_rows(slice(r0 + CP_ROWS, r0 + CP_ROWS + 1))
                u_prev = jnp.where(t == 0, before, pltpu.roll(u, 1, 0))
                u_next = jnp.where(t == CP_ROWS - 1, after, pltpu.roll(u, CP_ROWS - 1, 0))
                conv = u_prev * cw_ref[0:1, lanes] + u * cw_ref[1:2, lanes] + u_next * cw_ref[2:3, lanes]
                o_ref[rows, lanes] = (z_ref[rows, lanes].astype(F32) * conv).astype(BF16)

            k_rows = slice(r0 if at_start else r0 - CP_HALO,
                           r0 + CP_ROWS if at_end else r0 + CP_ROWS + CP_HALO)
            b_cols = slice(CP_HALO if at_start else 0,
                           CP_HALO + CP_ROWS if at_end else CP_ROWS + 2 * CP_HALO)
            tpos = pos0 + t
            for gi, w in enumerate(POOL_WINDOWS):
                lo = 3 * D_CONV + gi * g_dim
                win = jnp.dot(band_ref[gi, :, b_cols], z_ref[k_rows, lo:lo + g_dim],
                              preferred_element_type=F32)
                cnt = (jnp.minimum(tpos + w // 2, seq_len) - jnp.maximum(tpos - w // 2, 0)).astype(F32)
                pooled = win / cnt - z_ref[rows, lo:lo + g_dim].astype(F32)
                y = jnp.dot(pooled.astype(BF16), pw_ref[gi].astype(BF16), preferred_element_type=F32)
                y = y * ps_ref[:, gi * g_dim:(gi + 1) * g_dim]
                o_ref[rows, D_CONV + gi * g_dim:D_CONV + (gi + 1) * g_dim] = y.astype(BF16)

    pl.when(blk < N_CTX_BLOCKS)(functools.partial(block, T_CTX))
    pl.when(blk >= N_CTX_BLOCKS)(functools.partial(block, T_LAT))


def _conv_pool(z, conv_w, pool_w, pool_scale):
    return pl.pallas_call(
        _convpool_kernel,
        grid=(M_TOK // MIX_ROWS,),
        in_specs=[
            pl.BlockSpec((MIX_ROWS, D_IN_PROJ - 4 * D_RET), lambda b: (b, 0)),
            pl.BlockSpec((3, D_CONV), lambda b: (0, 0)),
            pl.BlockSpec((len(POOL_WINDOWS), POOL_GROUP_DIM, POOL_GROUP_DIM), lambda b: (0, 0, 0)),
            pl.BlockSpec((1, D_POOL), lambda b: (0, 0)),
        ],
        out_specs=pl.BlockSpec((MIX_ROWS, D_CONV + D_POOL), lambda b: (b, 0)),
        out_shape=jax.ShapeDtypeStruct((M_TOK, D_CONV + D_POOL), BF16),
        scratch_shapes=[pltpu.VMEM((len(POOL_WINDOWS), CP_ROWS, CP_ROWS + 2 * CP_HALO), BF16)],
        compiler_params=_cparams(("arbitrary",)),
        name="conv_pool",
    )(z, conv_w, pool_w, pool_scale.reshape(1, D_POOL))


def _log_sigmoid(x):
    return jnp.minimum(x, 0.0) - jnp.log1p(jnp.exp(-jnp.abs(x)))


RET_ROWS = MIX_ROWS


def _retention_kernel(*refs, layer):
    if layer == 0:
        (q_ref, k_ref, v_ref, g_ref, cos_ref, sin_ref, dl_ref, gain_ref, s0_ref,
         y_ref, sfin_ref, tab_scr, kr_scr, u_scr, s_scr) = refs
    else:
        (q_ref, k_ref, v_ref, g_ref, cos_ref, sin_ref, dl_ref, gain_ref, s0_ref, _,
         y_ref, sfin_ref, tab_scr, kr_scr, u_scr, s_scr) = refs
    blk = pl.program_id(1)
    n_chunks = RET_ROWS // CHUNK
    half = HEAD_DIM // 2

    lg_f = _log_sigmoid(dl_ref[0])
    lg_b = _log_sigmoid(dl_ref[1])
    g_f = jnp.exp(lg_f * CHUNK)
    g_b = jnp.exp(lg_b * CHUNK)

    @pl.when(blk == 0)
    def _():
        row = lax.broadcasted_iota(I32, (CHUNK, CHUNK), 0).astype(F32)
        col = lax.broadcasted_iota(I32, (CHUNK, CHUNK), 1).astype(F32)
        diff = row - col
        tab_scr[0] = (jnp.where(diff >= 0, jnp.exp(lg_f * jnp.maximum(diff, 0.0)), 0.0)
                      + jnp.where(diff <= 0, jnp.exp(lg_b * jnp.maximum(-diff, 0.0)), 0.0))
        tab_scr[1] = jnp.exp(lg_f * (row + 1.0))
        tab_scr[2] = jnp.exp(lg_b * (CHUNK - row))
        tab_scr[3] = jnp.exp(lg_f * (CHUNK - 1.0 - row))
        tab_scr[4] = jnp.exp(lg_b * row)

    tn_dims = (((0,), (0,)), ((), ()))
    nt_dims = (((1,), (1,)), ((), ()))

    def block(seq_chunks, latent):
        def rope(x, sl):
            if not latent:
                return x
            return x * cos_ref[sl, :] + pltpu.roll(x, half, 1) * sin_ref[sl, :]

        for c in range(n_chunks):
            sl = slice(c * CHUNK, (c + 1) * CHUNK)
            kr = rope(k_ref[sl, :].astype(F32), sl) * (HEAD_DIM ** -0.5)
            kr_scr[sl, :] = kr.astype(BF16)
            kz = jnp.concatenate([kr * tab_scr[3], kr * tab_scr[4]], axis=1).astype(BF16)
            u_scr[c] = lax.dot_general(kz, v_ref[sl, :], tn_dims, preferred_element_type=F32)

        has_state = [[False] * n_chunks, [False] * n_chunks]
        for s_i in range(n_chunks // seq_chunks):
            chunks = list(range(s_i * seq_chunks, (s_i + 1) * seq_chunks))
            for d, order, g_d in ((0, chunks, g_f), (1, chunks[::-1], g_b)):
                s = s0_ref[d] if latent else None
                for c in order:
                    u = u_scr[c, d * HEAD_DIM:(d + 1) * HEAD_DIM, :]
                    if s is None:
                        s = u
                    else:
                        s_scr[d, c] = s.astype(BF16)
                        has_state[d][c] = True
                        s = g_d * s + u
                if not latent:
                    if layer == 0:
                        sfin_ref[s_i, 0, d] = s
                        for later in range(1, DEPTH):
                            sfin_ref[s_i, later, d] = jnp.zeros_like(s)
                    else:
                        sfin_ref[s_i, d] = s

        for c in range(n_chunks):
            sl = slice(c * CHUNK, (c + 1) * CHUNK)
            qr = rope(q_ref[sl, :].astype(F32), sl)
            scores = lax.dot_general(qr.astype(BF16), kr_scr[sl, :], nt_dims, preferred_element_type=F32)
            o = jnp.dot((scores * tab_scr[0]).astype(BF16), v_ref[sl, :], preferred_element_type=F32)
            for d in range(2):
                if has_state[d][c]:
                    o += jnp.dot((qr * tab_scr[1 + d]).astype(BF16), s_scr[d, c],
                                 preferred_element_type=F32)
            mu = jnp.mean(o, axis=-1, keepdims=True)
            dev = o - mu
            var = jnp.mean(dev * dev, axis=-1, keepdims=True)
            on = dev * lax.rsqrt(var + LN_EPS) * gain_ref[...]
            y_ref[sl, :] = (_silu(g_ref[sl, :].astype(F32)) * on).astype(BF16)

    pl.when(blk < N_CTX_BLOCKS)(functools.partial(block, T_CTX // CHUNK, False))
    pl.when(blk >= N_CTX_BLOCKS)(functools.partial(block, T_LAT // CHUNK, True))


def _retention(z, cos_t, sin_t, decay_logit_b, gain, s0, layer, states):
    qcol0 = (3 * D_CONV + D_POOL) // HEAD_DIM
    n_blocks = M_TOK // RET_ROWS

    def zspec(k):
        return pl.BlockSpec((RET_ROWS, HEAD_DIM), lambda h, b: (b, qcol0 + k * N_RET_HEADS + h))

    def lat_seq(b):
        return jnp.maximum(b - N_CTX_BLOCKS, 0)

    def ctx_block(b):
        return jnp.minimum(b, N_CTX_BLOCKS - 1)

    n_chunks = RET_ROWS // CHUNK
    in_specs = [
        zspec(0), zspec(1), zspec(2), zspec(3),
        pl.BlockSpec((RET_ROWS, HEAD_DIM), lambda h, b: (0, 0)),
        pl.BlockSpec((RET_ROWS, HEAD_DIM), lambda h, b: (0, 0)),
        pl.BlockSpec((2, None, 1, HEAD_DIM), lambda h, b: (0, h, 0, 0)),
        pl.BlockSpec((1, HEAD_DIM), lambda h, b: (0, h)),
        pl.BlockSpec((None, None, 2, None, HEAD_DIM, HEAD_DIM),
                     lambda h, b: (lat_seq(b), layer, 0, h, 0, 0)),
    ]
    args = [z, z, z, z, cos_t, sin_t, decay_logit_b, gain.reshape(1, D_RET), s0]
    if layer == 0:
        st_spec = pl.BlockSpec((SEQ_PER_CTX_BLOCK, DEPTH, 2, None, HEAD_DIM, HEAD_DIM),
                               lambda h, b: (ctx_block(b), 0, 0, h, 0, 0))
        aliases = {}
    else:
        st_spec = pl.BlockSpec((SEQ_PER_CTX_BLOCK, None, 2, None, HEAD_DIM, HEAD_DIM),
                               lambda h, b: (ctx_block(b), layer, 0, h, 0, 0))
        in_specs.append(pl.BlockSpec(memory_space=pl.ANY))
        args.append(states)
        aliases = {len(args) - 1: 1}
    return pl.pallas_call(
        functools.partial(_retention_kernel, layer=layer),
        grid=(N_RET_HEADS, n_blocks),
        in_specs=in_specs,
        out_specs=[pl.BlockSpec((RET_ROWS, HEAD_DIM), lambda h, b: (b, h)), st_spec],
        out_shape=[
            jax.ShapeDtypeStruct((M_TOK, D_RET), BF16),
            jax.ShapeDtypeStruct((N_CTX_SEQ, DEPTH, 2, N_RET_HEADS, HEAD_DIM, HEAD_DIM), F32),
        ],
        input_output_aliases=aliases,
        scratch_shapes=[
            pltpu.VMEM((5, CHUNK, CHUNK), F32),
            pltpu.VMEM((RET_ROWS, HEAD_DIM), BF16),
            pltpu.VMEM((n_chunks, 2 * HEAD_DIM, HEAD_DIM), F32),
            pltpu.VMEM((2, n_chunks, HEAD_DIM, HEAD_DIM), BF16),
        ],
        compiler_params=_cparams(("arbitrary", "arbitrary")),
        name="retention",
    )(*args)


def _layer_norm_rows(r, g, b):
    mu = jnp.mean(r, axis=-1, keepdims=True)
    dev = r - mu
    var = jnp.mean(dev * dev, axis=-1, keepdims=True)
    return dev * lax.rsqrt(var + LN_EPS) * g + b


def _top2_of4(vals):
    top1 = jnp.maximum(jnp.maximum(vals[0], vals[1]), jnp.maximum(vals[2], vals[3]))
    idx1 = jnp.where(vals[0] == top1, 0, jnp.where(vals[1] == top1, 1, jnp.where(vals[2] == top1, 2, 3)))
    neg = jnp.float32(-jnp.inf)
    rest = [jnp.where(idx1 == j, neg, vals[j]) for j in range(4)]
    top2 = jnp.maximum(jnp.maximum(rest[0], rest[1]), jnp.maximum(rest[2], rest[3]))
    idx2 = jnp.where(rest[0] == top2, 0, jnp.where(rest[1] == top2, 1, jnp.where(rest[2] == top2, 2, 3)))
    return top1, idx1, top2, idx2


def _store_token_tiles(ref, val, tok0=0):
    n = val.shape[0]
    base = tok0 * TOK_STRIDE
    for c in range(TOK_ROWS):
        ref[pl.ds(base + c, n, stride=TOK_STRIDE), :] = val[:, c * LANES:(c + 1) * LANES]
    ref[pl.ds(base + TOK_ROWS, n, stride=TOK_STRIDE), :] = jnp.zeros((n, LANES), val.dtype)


def _load_token_tiles(ref, tok0, n):
    return jnp.concatenate(
        [ref[pl.ds(tok0 * TOK_STRIDE + c, n, stride=TOK_STRIDE), :] for c in range(TOK_ROWS)], axis=1)


def _outproj_kernel(ycp_ref, yret_ref, x_ref, wo_hbm, gate1_ref, lng_ref, lnb_ref, sc2_ref, sh2_ref,
                    wr_ref, rb_ref, x1_ref, h2_ref, ei_ref, ewt_ref, rk_ref, cnt_ref,
                    carry_scr, wo_ref, wstage, wsem, wr_hi, wr_lo, *, layer):
    i = pl.program_id(0)

    @pl.when(i == 0)
    def _():
        carry_scr[...] = jnp.zeros_like(carry_scr)
        wr = wr_ref[...]
        hi = wr.astype(BF16)
        wr_hi[...] = hi
        wr_lo[...] = (wr - hi.astype(F32)).astype(BF16)
        rows = wstage.shape[1]
        n_chunks = D_MODEL // rows

        def chunk_copy(c):
            return pltpu.make_async_copy(wo_hbm.at[layer, pl.ds(c * rows, rows), :], wstage.at[c % 2],
                                         wsem.at[c % 2])
        chunk_copy(0).start()
        for c in range(n_chunks):
            if c + 1 < n_chunks:
                chunk_copy(c + 1).start()
            chunk_copy(c).wait()
            wo_ref[c * rows:(c + 1) * rows, :] = wstage[c % 2].astype(BF16)

    s_i = lax.broadcasted_iota(I32, (SUB_OP, SUB_OP), 0)
    t_i = lax.broadcasted_iota(I32, (SUB_OP, SUB_OP), 1)
    tri = jnp.where(s_i < t_i, 1.0, 0.0).astype(BF16)
    half_k = D_CONV + D_POOL
    n_sub = x_ref.shape[0] // SUB_OP
    ys = []
    for sub in range(n_sub):
        rows_sl = slice(sub * SUB_OP, (sub + 1) * SUB_OP)
        y = jnp.dot(ycp_ref[rows_sl, :], wo_ref[0:half_k, :], preferred_element_type=F32)
        ys.append(y + jnp.dot(yret_ref[rows_sl, :], wo_ref[half_k:, :], preferred_element_type=F32))
    lts = [_ln_router_subtile(sub, ys[sub], x_ref, gate1_ref, lng_ref, lnb_ref, sc2_ref, sh2_ref,
                              wr_hi, wr_lo, x1_ref, h2_ref) for sub in range(n_sub)]
    carry = carry_scr[:, 0:1]
    for sub in range(n_sub):
        carry = _route_subtile(sub, lts[sub], carry, tri, rb_ref, ei_ref, ewt_ref, rk_ref)
    carry_scr[...] = jnp.broadcast_to(carry, carry_scr.shape)
    cnt_ref[...] = carry_scr[...]


def _ln_router_subtile(sub, y, x_ref, gate1_ref, lng_ref, lnb_ref, sc2_ref, sh2_ref, wr_hi, wr_lo,
                       x1_ref, h2_ref):
    tm = SUB_OP
    rows_sl = slice(sub * tm, (sub + 1) * tm)
    x1 = _layer_norm_rows(DEEPNORM_ALPHA * x_ref[rows_sl, :] + gate1_ref[...] * y,
                          lng_ref[...], lnb_ref[...])
    x1_ref[rows_sl, :] = x1
    h2 = x1 * (1.0 + sc2_ref[...]) + sh2_ref[...]
    _store_token_tiles(h2_ref, h2, tok0=sub * tm)

    h_hi = h2.astype(BF16)
    h_lo = (h2 - h_hi.astype(F32)).astype(BF16)
    logits = (jnp.dot(h_hi, wr_hi[...], preferred_element_type=F32)
              + jnp.dot(h_lo, wr_hi[...], preferred_element_type=F32)
              + jnp.dot(h_hi, wr_lo[...], preferred_element_type=F32))
    return logits.T


def _route_subtile(sub, lt, carry, tri, rb_ref, ei_ref, ewt_ref, rk_ref):
    tm = SUB_OP
    rows_sl = slice(sub * tm, (sub + 1) * tm)
    rows = [lt[e:e + 1, :] for e in range(N_EXPERTS)]

    mx = rows[0]
    for e in range(1, N_EXPERTS):
        mx = jnp.maximum(mx, rows[e])
    ex = [jnp.exp(r - mx) for r in rows]
    den = ex[0]
    for e in range(1, N_EXPERTS):
        den = den + ex[e]
    score = [x / den for x in ex]
    biased = [score[e] + rb_ref[e] for e in range(N_EXPERTS)]

    best = None
    for gi in range(N_EXPERT_GROUPS):
        t1, i1, t2, i2 = _top2_of4(biased[gi * EXPERTS_PER_GROUP:(gi + 1) * EXPERTS_PER_GROUP])
        gs = t1 + t2
        e1 = gi * EXPERTS_PER_GROUP + i1
        e2 = gi * EXPERTS_PER_GROUP + i2
        if best is None:
            best = (gs, e1, e2)
        else:
            take = gs > best[0]
            best = (jnp.where(take, gs, best[0]), jnp.where(take, e1, best[1]), jnp.where(take, e2, best[2]))
    _, e1, e2 = best
    zero = jnp.zeros_like(score[0])
    w1 = zero
    w2 = zero
    for e in range(N_EXPERTS):
        w1 = w1 + jnp.where(e1 == e, score[e], 0.0)
        w2 = w2 + jnp.where(e2 == e, score[e], 0.0)
    wsum = w1 + w2
    ei_ref[0:1, rows_sl] = e1
    ei_ref[1:2, rows_sl] = e2
    ewt_ref[rows_sl, :] = jnp.concatenate(
        [w1 / wsum, w2 / wsum, jnp.zeros((LANES - 2, tm), F32)], axis=0).T

    onehot = jnp.concatenate(
        [jnp.where((e1 == e) | (e2 == e), 1.0, 0.0) for e in range(N_EXPERTS)], axis=0)
    prefix = jnp.dot(onehot.astype(BF16), tri, preferred_element_type=F32) + carry
    r1 = zero
    r2 = zero
    for e in range(N_EXPERTS):
        r1 = r1 + jnp.where(e1 == e, prefix[e:e + 1, :], 0.0)
        r2 = r2 + jnp.where(e2 == e, prefix[e:e + 1, :], 0.0)
    rk_ref[0:1, rows_sl] = r1.astype(I32)
    rk_ref[1:2, rows_sl] = r2.astype(I32)
    return carry + jnp.sum(onehot, axis=1, keepdims=True)


def _out_proj(ycp, yret, x, w_out, layer, ada5, ln_g, ln_b, w_router_pad, router_bias):
    tm = TM_OP
    ada_spec = lambda chunk: pl.BlockSpec(
        (None, None, 1, D_MODEL), lambda i: (_ada_row(i, tm), chunk, 0, 0))
    vec_spec = pl.BlockSpec((1, D_MODEL), lambda i: (0, 0))
    route_spec = pl.BlockSpec((2, tm), lambda i: (0, i))
    return pl.pallas_call(
        functools.partial(_outproj_kernel, layer=layer),
        grid=(M_TOK // tm,),
        in_specs=[
            pl.BlockSpec((tm, D_CONV + D_POOL), lambda i: (i, 0)),
            pl.BlockSpec((tm, D_RET), lambda i: (i, 0)),
            pl.BlockSpec((tm, D_MODEL), lambda i: (i, 0)),
            pl.BlockSpec(memory_space=pl.ANY),
            ada_spec(2),
            vec_spec, vec_spec,
            ada_spec(4),
            ada_spec(3),
            pl.BlockSpec((D_MODEL, LANES), lambda i: (0, 0)),
            pl.BlockSpec(memory_space=pltpu.SMEM),
        ],
        out_specs=[
            pl.BlockSpec((tm, D_MODEL), lambda i: (i, 0)),
            pl.BlockSpec((tm * TOK_STRIDE, LANES), lambda i: (i, 0)),
            route_spec,
            pl.BlockSpec((tm, LANES), lambda i: (i, 0)),
            route_spec,
            pl.BlockSpec((N_EXPERTS, LANES), lambda i: (0, 0)),
        ],
        out_shape=[
            jax.ShapeDtypeStruct((M_TOK, D_MODEL), F32),
            jax.ShapeDtypeStruct((M_TOK * TOK_STRIDE, LANES), F32),
            jax.ShapeDtypeStruct((2, M_TOK), I32),
            jax.ShapeDtypeStruct((M_TOK, LANES), F32),
            jax.ShapeDtypeStruct((2, M_TOK), I32),
            jax.ShapeDtypeStruct((N_EXPERTS, LANES), F32),
        ],
        scratch_shapes=[
            pltpu.VMEM((N_EXPERTS, LANES), F32),
            pltpu.VMEM((D_MODEL, D_MODEL), BF16),
            pltpu.VMEM((2, W_OUT_STAGE_ROWS, D_MODEL), F32),
            pltpu.SemaphoreType.DMA((2,)),
            pltpu.VMEM((D_MODEL, LANES), BF16),
            pltpu.VMEM((D_MODEL, LANES), BF16),
        ],
        compiler_params=_cparams(("arbitrary",)),
        name="out_proj_router",
    )(ycp, yret, x, w_out, ada5, ln_g.reshape(1, D_MODEL), ln_b.reshape(1, D_MODEL),
      ada5, ada5, w_router_pad, router_bias)


def _pos_kernel(ei_ref, rk_ref, cnt_ref, pos_ref):
    ei = ei_ref[...]
    pos = rk_ref[...]
    start = jnp.zeros((1, 1), F32)
    for e in range(N_EXPERTS):
        pos = pos + jnp.where(ei == e, start.astype(I32), 0)
        n_tiles = jnp.floor((cnt_ref[e:e + 1, 0:1] + (TM_EXP - 1.0)) * (1.0 / TM_EXP))
        start = start + n_tiles * TM_EXP
    pos_ref[...] = pos


def _pair_rows(ei, rk, cnt):
    return pl.pallas_call(
        _pos_kernel,
        out_shape=jax.ShapeDtypeStruct((2, M_TOK), I32),
        name="pair_rows",
    )(ei, rk, cnt)


def _route_kernel(cnt_ref, pos_ref, src_ref, te_ref, nv_ref):
    def zero(r, carry):
        src_ref[r] = 0
        return carry

    tile = jnp.int32(0)
    for e in range(N_EXPERTS):
        n_tiles = lax.shift_right_logical(cnt_ref[e] + (TM_EXP - 1), TM_EXP.bit_length() - 1)

        def mark(j, carry, e=e, tile=tile):
            te_ref[tile + j] = e
            return carry
        lax.fori_loop(0, n_tiles, mark, 0)
        lax.fori_loop(tile * TM_EXP + cnt_ref[e], (tile + n_tiles) * TM_EXP, zero, 0)
        tile = tile + n_tiles
    nv_ref[0] = tile
    last_expert = te_ref[tile - 1]

    def mark_unused(j, carry):
        te_ref[j] = last_expert
        return carry
    lax.fori_loop(tile, NT_EXP, mark_unused, 0)
    lax.fori_loop(tile * TM_EXP, NP_EXP, zero, 0)

    def place(t, carry):
        src_ref[pos_ref[t]] = t
        src_ref[pos_ref[M_TOK + t]] = t
        return carry
    lax.fori_loop(0, M_TOK, place, 0, unroll=16)


def _route_tables(cnt, pos):
    smem = pl.BlockSpec(memory_space=pltpu.SMEM)
    return pl.pallas_call(
        _route_kernel,
        in_specs=[smem, smem],
        out_specs=[smem, smem, smem],
        out_shape=[
            jax.ShapeDtypeStruct((NP_EXP,), I32),
            jax.ShapeDtypeStruct((NT_EXP,), I32),
            jax.ShapeDtypeStruct((1,), I32),
        ],
        name="route_tables",
    )(cnt, pos)


GATHER_PRIORITY = 1


def _row_gather_start(src_hbm, buf, sem, idx_ref, base, n_tok, tok0=0, priority=0):
    for r in range(n_tok):
        pltpu.make_async_copy(src_hbm.at[pl.ds(idx_ref[base + r] * TOK_STRIDE, TOK_ROWS), :],
                              buf.at[pl.ds((tok0 + r) * TOK_STRIDE, TOK_ROWS), :], sem).start(priority)


def _row_gather_wait(src_hbm, buf, sem):
    n_rows = buf.shape[0] // TOK_STRIDE * TOK_ROWS
    pltpu.make_async_copy(src_hbm.at[pl.ds(0, n_rows), :], buf.at[pl.ds(0, n_rows), :], sem).wait()


N_XBUF = 4


def _experts_kernel(te_ref, nv_ref, src_ref, h2_hbm, wg_ref, wu_ref, wd_ref, o_ref, *scratch):
    bufs = scratch[:N_XBUF]
    gsem, wg_bf, wu_bf, wd_bf = scratch[N_XBUF:]
    i = pl.program_id(0)
    n_valid = nv_ref[0]

    def start_tile(tile, slot):
        tile = jnp.minimum(tile, NT_EXP - 1)
        _row_gather_start(h2_hbm, bufs[slot], gsem.at[slot], src_ref, tile * TM_EXP, TM_EXP,
                          priority=GATHER_PRIORITY)

    @pl.when(i == 0)
    def _():
        for t in range(N_XBUF - 1):
            start_tile(t, t)

    @pl.when((i < n_valid) & ((i == 0) | (te_ref[i] != te_ref[jnp.maximum(i - 1, 0)])))
    def _():
        wg_bf[...] = wg_ref[...].astype(BF16)
        wu_bf[...] = wu_ref[...].astype(BF16)
        wd_bf[...] = wd_ref[...].astype(BF16)

    def step(cur):
        _row_gather_wait(h2_hbm, bufs[cur], gsem.at[cur])
        start_tile(i + N_XBUF - 1, (cur + N_XBUF - 1) % N_XBUF)
        x = _load_token_tiles(bufs[cur], 0, TM_EXP).astype(BF16)
        g = jnp.dot(x, wg_bf[...], preferred_element_type=F32)
        u = jnp.dot(x, wu_bf[...], preferred_element_type=F32)
        a = (_silu(g) * u).astype(BF16)
        _store_token_tiles(o_ref, jnp.dot(a, wd_bf[...], preferred_element_type=F32))

        @pl.when(i == n_valid - 1)
        def _():
            for ahead in range(1, N_XBUF):
                slot = (cur + ahead) % N_XBUF
                _row_gather_wait(h2_hbm, bufs[slot], gsem.at[slot])

    for cur in range(N_XBUF):
        pl.when((i < n_valid) & (i % N_XBUF == cur))(functools.partial(step, cur))

    @pl.when(i >= n_valid)
    def _():
        o_ref[...] = jnp.zeros_like(o_ref)


def _experts(h2, tile_expert, n_valid, src_tok, wg, wu, wd, layer):
    grid_spec = pltpu.PrefetchScalarGridSpec(
        num_scalar_prefetch=3,
        grid=(NT_EXP,),
        in_specs=[
            pl.BlockSpec(memory_space=pl.ANY),
            pl.BlockSpec((None, None, D_MODEL, D_EXPERT), lambda i, te, nv, src: (layer, te[i], 0, 0)),
            pl.BlockSpec((None, None, D_MODEL, D_EXPERT), lambda i, te, nv, src: (layer, te[i], 0, 0)),
            pl.BlockSpec((None, None, D_EXPERT, D_MODEL), lambda i, te, nv, src: (layer, te[i], 0, 0)),
        ],
        out_specs=pl.BlockSpec((TM_EXP * TOK_STRIDE, LANES), lambda i, te, nv, src: (i, 0)),
        scratch_shapes=[pltpu.VMEM((TM_EXP * TOK_STRIDE, LANES), F32) for _ in range(N_XBUF)] + [
            pltpu.SemaphoreType.DMA((N_XBUF,)),
            pltpu.VMEM((D_MODEL, D_EXPERT), BF16),
            pltpu.VMEM((D_MODEL, D_EXPERT), BF16),
            pltpu.VMEM((D_EXPERT, D_MODEL), BF16),
        ],
    )
    return pl.pallas_call(
        _experts_kernel,
        grid_spec=grid_spec,
        out_shape=jax.ShapeDtypeStruct((NP_EXP * TOK_STRIDE, LANES), F32),
        compiler_params=_cparams(("arbitrary",)),
        name="experts",
    )(tile_expert, n_valid, src_tok, h2, wg, wu, wd)


N_RBUF = 3


def _final_kernel(*refs, emit_h):
    n_out = 2
    if emit_h:
        (pos_ref, x1_ref, ys_hbm, ewt_ref, gate2_ref, lng_ref, lnb_ref, sc1_ref, sh1_ref,
         x_ref, h_ref) = refs[:9 + n_out]
    else:
        (pos_ref, x1_ref, ys_hbm, ewt_ref, gate2_ref, lng_ref, lnb_ref, xc_ref, xl_ref) = refs[:7 + n_out]
    bufs = refs[-N_RBUF - 1:-1]
    sem = refs[-1]
    tm = TM_OUT
    i = pl.program_id(0)
    n_blocks = M_TOK // tm

    def start(tile, slot):
        tile = jnp.minimum(tile, n_blocks - 1)
        _row_gather_start(ys_hbm, bufs[slot], sem.at[slot], pos_ref, tile * tm, tm, tok0=0,
                          priority=GATHER_PRIORITY)
        _row_gather_start(ys_hbm, bufs[slot], sem.at[slot], pos_ref, M_TOK + tile * tm, tm, tok0=tm,
                          priority=GATHER_PRIORITY)

    @pl.when(i == 0)
    def _():
        for t in range(N_RBUF - 1):
            start(t, t)

    def step(cur):
        buf = bufs[cur]
        _row_gather_wait(ys_hbm, buf, sem.at[cur])
        start(i + N_RBUF - 1, (cur + N_RBUF - 1) % N_RBUF)
        w = ewt_ref[...]
        y2 = w[:, 0:1] * _load_token_tiles(buf, 0, tm) + w[:, 1:2] * _load_token_tiles(buf, tm, tm)
        x2 = _layer_norm_rows(DEEPNORM_ALPHA * x1_ref[...] + gate2_ref[...] * y2,
                              lng_ref[...], lnb_ref[...])
        if emit_h:
            x_ref[...] = x2
            h_ref[...] = (x2 * (1.0 + sc1_ref[...]) + sh1_ref[...]).astype(BF16)
        else:
            @pl.when(i < M_CTX // tm)
            def _():
                xc_ref[...] = x2

            @pl.when(i >= M_CTX // tm)
            def _():
                xl_ref[...] = x2

        @pl.when(i == n_blocks - 1)
        def _():
            for ahead in range(1, N_RBUF):
                slot = (cur + ahead) % N_RBUF
                _row_gather_wait(ys_hbm, bufs[slot], sem.at[slot])

    for cur in range(N_RBUF):
        pl.when(i % N_RBUF == cur)(functools.partial(step, cur))


def _final(x1, ys, pos, ewt, ada5, ln_g, ln_b, ada5_next):
    tm = TM_OUT
    emit_h = ada5_next is not None
    vec_spec = pl.BlockSpec((1, D_MODEL), lambda i, pos: (0, 0))
    ada_spec = lambda chunk: pl.BlockSpec(
        (None, None, 1, D_MODEL), lambda i, pos: (_ada_row(i, tm), chunk, 0, 0))
    in_specs = [
        pl.BlockSpec((tm, D_MODEL), lambda i, pos: (i, 0)),
        pl.BlockSpec(memory_space=pl.ANY),
        pl.BlockSpec((tm, LANES), lambda i, pos: (i, 0)),
        ada_spec(5),
        vec_spec, vec_spec,
    ]
    args = [pos, x1, ys, ewt, ada5, ln_g.reshape(1, D_MODEL), ln_b.reshape(1, D_MODEL)]
    if emit_h:
        in_specs += [ada_spec(1), ada_spec(0)]
        args += [ada5_next, ada5_next]
        row_spec = pl.BlockSpec((tm, D_MODEL), lambda i, pos: (i, 0))
        out_specs = [row_spec, row_spec]
        out_shape = [jax.ShapeDtypeStruct((M_TOK, D_MODEL), F32),
                     jax.ShapeDtypeStruct((M_TOK, D_MODEL), BF16)]
    else:
        out_specs = [
            pl.BlockSpec((tm, D_MODEL), lambda i, pos: (_ctx_block(i, tm), 0)),
            pl.BlockSpec((tm, D_MODEL), lambda i, pos: (_lat_block(i, tm), 0)),
        ]
        out_shape = [jax.ShapeDtypeStruct((M_CTX, D_MODEL), F32),
                     jax.ShapeDtypeStruct((M_LAT, D_MODEL), F32)]
    grid_spec = pltpu.PrefetchScalarGridSpec(
        num_scalar_prefetch=1,
        grid=(M_TOK // tm,),
        in_specs=in_specs,
        out_specs=out_specs,
        scratch_shapes=[pltpu.VMEM((2 * tm * TOK_STRIDE, LANES), F32) for _ in range(N_RBUF)] + [
            pltpu.SemaphoreType.DMA((N_RBUF,)),
        ],
    )
    return pl.pallas_call(
        functools.partial(_final_kernel, emit_h=emit_h),
        grid_spec=grid_spec,
        out_shape=out_shape,
        compiler_params=_cparams(("arbitrary",)),
        name="final_ln",
    )(*args)


def _rope_tables():
    rows = T_LAT // GRID_W
    row = jnp.repeat(jnp.arange(rows), GRID_W).astype(F32)
    col = jnp.tile(jnp.arange(GRID_W), rows).astype(F32)
    n_freq = HEAD_DIM // 4
    inv_freq = ROPE_BASE ** (-jnp.arange(n_freq, dtype=F32) / n_freq)
    ang = jnp.concatenate([row[:, None] * inv_freq[None], col[:, None] * inv_freq[None]], axis=-1)
    cos, sin = jnp.cos(ang), jnp.sin(ang)
    return jnp.concatenate([cos, cos], axis=-1), jnp.concatenate([-sin, sin], axis=-1)


def kernel(x_prompt, x_sample, state_retention, c, c_ctx, w_ada, b_ada, w_in, w_out, conv_w, pool_w,
           pool_scale, ret_decay_logit, ret_gn_gain, ln1_g, ln1_b, ln2_g, ln2_b, w_router, router_bias,
           w_gate, w_up, w_down):
    x_ctx = x_prompt.reshape(M_CTX, D_MODEL)
    x_lat = x_sample.reshape(M_LAT, D_MODEL)
    c_all = jnp.concatenate(
        [c_ctx[None, :], c, jnp.zeros((ADA_ROWS - 1 - N_LAT_SEQ, D_MODEL), F32)], axis=0)
    ada = _ada_table(c_all, w_ada, b_ada).reshape(DEPTH, ADA_ROWS, 6, 1, D_MODEL)

    cos_lat, sin_lat = _rope_tables()
    w_router_pad = jnp.pad(w_router, ((0, 0), (0, LANES - N_EXPERTS)))

    states = None
    x, h = _modulate(x_ctx, x_lat, ada[0])
    for l in range(DEPTH):
        ada5 = ada[l]
        z = _in_proj(h, w_in, l)

        ycp = _conv_pool(z, conv_w[l], pool_w[l], pool_scale[l])

        dl = jnp.broadcast_to(ret_decay_logit[l][:, :, None, None], (2, N_RET_HEADS, 1, HEAD_DIM))
        yret, states = _retention(z, cos_lat, sin_lat, dl, ret_gn_gain[l], state_retention, l, states)

        x1, h2, ei, ewt, rk, cnt = _out_proj(ycp, yret, x, w_out, l, ada5,
                                             ln1_g[l], ln1_b[l], w_router_pad, router_bias)
        pos = _pair_rows(ei, rk, cnt).reshape(-1)
        src_tok, te, n_valid = _route_tables(cnt[:, 0].astype(I32), pos)
        ys = _experts(h2, te, n_valid, src_tok, w_gate, w_up, w_down, l)
        if l + 1 < DEPTH:
            x, h = _final(x1, ys, pos, ewt, ada5, ln2_g[l], ln2_b[l], ada[l + 1])
        else:
            x_ctx, x_lat = _final(x1, ys, pos, ewt, ada5, ln2_g[l], ln2_b[l], None)

    y_prompt = x_ctx.reshape(N_CTX_SEQ, T_CTX, D_MODEL)
    y_sample = x_lat.reshape(N_LAT_SEQ, T_LAT, D_MODEL)
    return y_prompt, y_sample, states
```

```python
import functools

import jax
import jax.numpy as jnp
from jax import lax
from jax.experimental import pallas as pl
from jax.experimental.pallas import tpu as pltpu

F32 = jnp.float32
BF16 = jnp.bfloat16
I32 = jnp.int32

D_MODEL = 2048
N_CTX_SEQ, T_CTX = 16, 256
N_LAT_SEQ, T_LAT = 8, 1024
DEPTH = 2
M_CTX = N_CTX_SEQ * T_CTX
M_LAT = N_LAT_SEQ * T_LAT
M_TOK = M_CTX + M_LAT

GRID_W = 64
D_CONV = D_MODEL // 4
D_POOL = D_MODEL // 4
D_RET = D_MODEL // 2
N_RET_HEADS = 8
HEAD_DIM = D_RET // N_RET_HEADS
POOL_WINDOWS = (2, 4, 8, 16)
POOL_GROUP_DIM = D_POOL // len(POOL_WINDOWS)
CHUNK = 128
ROPE_BASE = 10000.0
N_EXPERTS = 16
EXPERTS_PER_GROUP = 4
N_EXPERT_GROUPS = N_EXPERTS // EXPERTS_PER_GROUP
D_EXPERT = D_MODEL // 4
D_IN_PROJ = 3 * D_CONV + D_POOL + 4 * D_RET
DEEPNORM_ALPHA = (2.0 * DEPTH) ** 0.25
LN_EPS = 1e-5
ADA_ROWS = 16

LANES = 128
VMEM_LIMIT = 56 * 1024 * 1024

TM_IN = 1024
TN_IN = 1536
TOK_ROWS = D_MODEL // LANES
TOK_STRIDE = TOK_ROWS + 1
TM_OP = 512
SUB_OP = 256
W_OUT_STAGE_ROWS = 256
TM_OUT = 512
TM_EXP = 256
N_PAIR = 2 * M_TOK
NP_EXP = N_PAIR + N_EXPERTS * TM_EXP
NT_EXP = NP_EXP // TM_EXP


def _cparams(sem):
    return pltpu.CompilerParams(dimension_semantics=sem, vmem_limit_bytes=VMEM_LIMIT)


def _silu(x):
    return x * jax.nn.sigmoid(x)


def _ada_row(i, tm):
    n_ctx_tiles = M_CTX // tm
    per_batch = T_LAT // tm
    return jnp.where(i < n_ctx_tiles, 0, 1 + (i - n_ctx_tiles) // per_batch)


def _ada_kernel(c_ref, w_ref, b_ref, o_ref):
    s = _silu(c_ref[...]).astype(BF16)
    o_ref[...] = jnp.dot(s, w_ref[...].astype(BF16), preferred_element_type=F32) + b_ref[...]


def _ada_table(c_all, w_ada, b_ada):
    tn = 1024
    n6 = 6 * D_MODEL
    return pl.pallas_call(
        _ada_kernel,
        grid=(DEPTH, n6 // tn),
        in_specs=[
            pl.BlockSpec((ADA_ROWS, D_MODEL), lambda l, j: (0, 0)),
            pl.BlockSpec((None, D_MODEL, tn), lambda l, j: (l, 0, j)),
            pl.BlockSpec((None, 1, tn), lambda l, j: (l, 0, j)),
        ],
        out_specs=pl.BlockSpec((None, ADA_ROWS, tn), lambda l, j: (l, 0, j)),
        out_shape=jax.ShapeDtypeStruct((DEPTH, ADA_ROWS, n6), F32),
        compiler_params=_cparams(("arbitrary", "arbitrary")),
        name="ada_table",
    )(c_all, w_ada, b_ada.reshape(DEPTH, 1, n6))


def _ctx_block(i, tm):
    return jnp.minimum(i, M_CTX // tm - 1)


def _lat_block(i, tm):
    return jnp.maximum(i - M_CTX // tm, 0)


def _modulate_kernel(xc_ref, xl_ref, sc_ref, sh_ref, x_ref, h_ref):
    x = jnp.where(pl.program_id(0) < M_CTX // xc_ref.shape[0], xc_ref[...], xl_ref[...])
    x_ref[...] = x
    h_ref[...] = (x * (1.0 + sc_ref[...]) + sh_ref[...]).astype(BF16)


def _modulate(x_ctx, x_lat, ada5):
    tm = TM_OUT
    ada_spec = lambda chunk: pl.BlockSpec(
        (None, None, 1, D_MODEL), lambda i: (_ada_row(i, tm), chunk, 0, 0))
    return pl.pallas_call(
        _modulate_kernel,
        grid=(M_TOK // tm,),
        in_specs=[
            pl.BlockSpec((tm, D_MODEL), lambda i: (_ctx_block(i, tm), 0)),
            pl.BlockSpec((tm, D_MODEL), lambda i: (_lat_block(i, tm), 0)),
            ada_spec(1), ada_spec(0),
        ],
        out_specs=[pl.BlockSpec((tm, D_MODEL), lambda i: (i, 0)),
                   pl.BlockSpec((tm, D_MODEL), lambda i: (i, 0))],
        out_shape=[jax.ShapeDtypeStruct((M_TOK, D_MODEL), F32),
                   jax.ShapeDtypeStruct((M_TOK, D_MODEL), BF16)],
        compiler_params=_cparams(("arbitrary",)),
        name="modulate",
    )(x_ctx, x_lat, ada5, ada5)


def _inproj_kernel(h_ref, w_ref, o_ref, wbf_scr):
    @pl.when(pl.program_id(1) == 0)
    def _():
        wbf_scr[...] = w_ref[...].astype(BF16)

    o_ref[...] = jnp.dot(h_ref[...], wbf_scr[...], preferred_element_type=F32).astype(BF16)


def _in_proj(h, w_in, layer):
    return pl.pallas_call(
        _inproj_kernel,
        grid=(D_IN_PROJ // TN_IN, M_TOK // TM_IN),
        in_specs=[
            pl.BlockSpec((TM_IN, D_MODEL), lambda j, i: (i, 0)),
            pl.BlockSpec((None, D_MODEL, TN_IN), lambda j, i: (layer, 0, j)),
        ],
        out_specs=pl.BlockSpec((TM_IN, TN_IN), lambda j, i: (i, j)),
        out_shape=jax.ShapeDtypeStruct((M_TOK, D_IN_PROJ), BF16),
        scratch_shapes=[pltpu.VMEM((D_MODEL, TN_IN), BF16)],
        compiler_params=_cparams(("arbitrary", "arbitrary")),
        name="in_proj",
    )(h, w_in)


MIX_ROWS = T_LAT
N_CTX_BLOCKS = M_CTX // MIX_ROWS
SEQ_PER_CTX_BLOCK = MIX_ROWS // T_CTX
CP_ROWS = T_CTX
CP_HALO = 128


def _convpool_kernel(z_ref, cw_ref, pw_ref, ps_ref, o_ref, band_ref):
    blk = pl.program_id(0)
    g_dim = POOL_GROUP_DIM

    @pl.when(blk == 0)
    def _():
        row = lax.broadcasted_iota(I32, (CP_ROWS, CP_ROWS + 2 * CP_HALO), 0)
        col = lax.broadcasted_iota(I32, (CP_ROWS, CP_ROWS + 2 * CP_HALO), 1)
        d = col - CP_HALO - row
        for gi, w in enumerate(POOL_WINDOWS):
            band_ref[gi] = jnp.where((d >= -(w // 2)) & (d < w // 2), 1.0, 0.0).astype(BF16)

    def block(seq_len):
        t = lax.broadcasted_iota(I32, (CP_ROWS, LANES), 0)
        for ch in range(MIX_ROWS // CP_ROWS):
            r0 = ch * CP_ROWS
            rows = slice(r0, r0 + CP_ROWS)
            pos0 = r0 % seq_len
            at_start = pos0 == 0
            at_end = pos0 + CP_ROWS == seq_len

            for cg in range(D_CONV // LANES):
                lanes = slice(cg * LANES, (cg + 1) * LANES)

                def u_rows(rs):
                    return (z_ref[rs, D_CONV + cg * LANES:D_CONV + (cg + 1) * LANES].astype(F32)
                            * z_ref[rs, 2 * D_CONV + cg * LANES:2 * D_CONV + (cg + 1) * LANES].astype(F32))
                u = u_rows(rows)
                before = 0.0 if at_start else u_rows(slice(r0 - 1, r0))
                after = 0.0 if at_end else u_rows(slice(r0 + CP_ROWS, r0 + CP_ROWS + 1))
                u_prev = jnp.where(t == 0, before, pltpu.roll(u, 1, 0))
                u_next = jnp.where(t == CP_ROWS - 1, after, pltpu.roll(u, CP_ROWS - 1, 0))
                conv = u_prev * cw_ref[0:1, lanes] + u * cw_ref[1:2, lanes] + u_next * cw_ref[2:3, lanes]
                o_ref[rows, lanes] = (z_ref[rows, lanes].astype(F32) * conv).astype(BF16)

            k_rows = slice(r0 if at_start else r0 - CP_HALO,
                           r0 + CP_ROWS if at_end else r0 + CP_ROWS + CP_HALO)
            b_cols = slice(CP_HALO if at_start else 0,
                           CP_HALO + CP_ROWS if at_end else CP_ROWS + 2 * CP_HALO)
            tpos = pos0 + t
            for gi, w in enumerate(POOL_WINDOWS):
                lo = 3 * D_CONV + gi * g_dim
                win = jnp.dot(band_ref[gi, :, b_cols], z_ref[k_rows, lo:lo + g_dim],
                              preferred_element_type=F32)
                cnt = (jnp.minimum(tpos + w // 2, seq_len) - jnp.maximum(tpos - w // 2, 0)).astype(F32)
                pooled = win / cnt - z_ref[rows, lo:lo + g_dim].astype(F32)
                y = jnp.dot(pooled.astype(BF16), pw_ref[gi].astype(BF16), preferred_element_type=F32)
                y = y * ps_ref[:, gi * g_dim:(gi + 1) * g_dim]
                o_ref[rows, D_CONV + gi * g_dim:D_CONV + (gi + 1) * g_dim] = y.astype(BF16)

    pl.when(blk < N_CTX_BLOCKS)(functools.partial(block, T_CTX))
    pl.when(blk >= N_CTX_BLOCKS)(functools.partial(block, T_LAT))


def _conv_pool(z, conv_w, pool_w, pool_scale):
    return pl.pallas_call(
        _convpool_kernel,
        grid=(M_TOK // MIX_ROWS,),
        in_specs=[
            pl.BlockSpec((MIX_ROWS, D_IN_PROJ - 4 * D_RET), lambda b: (b, 0)),
            pl.BlockSpec((3, D_CONV), lambda b: (0, 0)),
            pl.BlockSpec((len(POOL_WINDOWS), POOL_GROUP_DIM, POOL_GROUP_DIM), lambda b: (0, 0, 0)),
            pl.BlockSpec((1, D_POOL), lambda b: (0, 0)),
        ],
        out_specs=pl.BlockSpec((MIX_ROWS, D_CONV + D_POOL), lambda b: (b, 0)),
        out_shape=jax.ShapeDtypeStruct((M_TOK, D_CONV + D_POOL), BF16),
        scratch_shapes=[pltpu.VMEM((len(POOL_WINDOWS), CP_ROWS, CP_ROWS + 2 * CP_HALO), BF16)],
        compiler_params=_cparams(("arbitrary",)),
        name="conv_pool",
    )(z, conv_w, pool_w, pool_scale.reshape(1, D_POOL))


def _log_sigmoid(x):
    return jnp.minimum(x, 0.0) - jnp.log1p(jnp.exp(-jnp.abs(x)))


RET_ROWS = MIX_ROWS


def _retention_kernel(*refs, layer):
    if layer == 0:
        (q_ref, k_ref, v_ref, g_ref, cos_ref, sin_ref, dl_ref, gain_ref, s0_ref,
         y_ref, sfin_ref, tab_scr, kr_scr, u_scr, s_scr) = refs
    else:
        (q_ref, k_ref, v_ref, g_ref, cos_ref, sin_ref, dl_ref, gain_ref, s0_ref, _,
         y_ref, sfin_ref, tab_scr, kr_scr, u_scr, s_scr) = refs
    blk = pl.program_id(1)
    n_chunks = RET_ROWS // CHUNK
    half = HEAD_DIM // 2

    lg_f = _log_sigmoid(dl_ref[0])
    lg_b = _log_sigmoid(dl_ref[1])
    g_f = jnp.exp(lg_f * CHUNK)
    g_b = jnp.exp(lg_b * CHUNK)

    @pl.when(blk == 0)
    def _():
        row = lax.broadcasted_iota(I32, (CHUNK, CHUNK), 0).astype(F32)
        col = lax.broadcasted_iota(I32, (CHUNK, CHUNK), 1).astype(F32)
        diff = row - col
        tab_scr[0] = (jnp.where(diff >= 0, jnp.exp(lg_f * jnp.maximum(diff, 0.0)), 0.0)
                      + jnp.where(diff <= 0, jnp.exp(lg_b * jnp.maximum(-diff, 0.0)), 0.0))
        tab_scr[1] = jnp.exp(lg_f * (row + 1.0))
        tab_scr[2] = jnp.exp(lg_b * (CHUNK - row))
        tab_scr[3] = jnp.exp(lg_f * (CHUNK - 1.0 - row))
        tab_scr[4] = jnp.exp(lg_b * row)

    tn_dims = (((0,), (0,)), ((), ()))
    nt_dims = (((1,), (1,)), ((), ()))

    def block(seq_chunks, latent):
        def rope(x, sl):
            if not latent:
                return x
            return x * cos_ref[sl, :] + pltpu.roll(x, half, 1) * sin_ref[sl, :]

        for c in range(n_chunks):
            sl = slice(c * CHUNK, (c + 1) * CHUNK)
            kr = rope(k_ref[sl, :].astype(F32), sl) * (HEAD_DIM ** -0.5)
            kr_scr[sl, :] = kr.astype(BF16)
            kz = jnp.concatenate([kr * tab_scr[3], kr * tab_scr[4]], axis=1).astype(BF16)
            u_scr[c] = lax.dot_general(kz, v_ref[sl, :], tn_dims, preferred_element_type=F32)

        has_state = [[False] * n_chunks, [False] * n_chunks]
        for s_i in range(n_chunks // seq_chunks):
            chunks = list(range(s_i * seq_chunks, (s_i + 1) * seq_chunks))
            for d, order, g_d in ((0, chunks, g_f), (1, chunks[::-1], g_b)):
                s = s0_ref[d] if latent else None
                for c in order:
                    u = u_scr[c, d * HEAD_DIM:(d + 1) * HEAD_DIM, :]
                    if s is None:
                        s = u
                    else:
                        s_scr[d, c] = s.astype(BF16)
                        has_state[d][c] = True
                        s = g_d * s + u
                if not latent:
                    if layer == 0:
                        sfin_ref[s_i, 0, d] = s
                        for later in range(1, DEPTH):
                            sfin_ref[s_i, later, d] = jnp.zeros_like(s)
                    else:
                        sfin_ref[s_i, d] = s

        for c in range(n_chunks):
            sl = slice(c * CHUNK, (c + 1) * CHUNK)
            qr = rope(q_ref[sl, :].astype(F32), sl)
            scores = lax.dot_general(qr.astype(BF16), kr_scr[sl, :], nt_dims, preferred_element_type=F32)
            o = jnp.dot((scores * tab_scr[0]).astype(BF16), v_ref[sl, :], preferred_element_type=F32)
            for d in range(2):
                if has_state[d][c]:
                    o += jnp.dot((qr * tab_scr[1 + d]).astype(BF16), s_scr[d, c],
                                 preferred_element_type=F32)
            mu = jnp.mean(o, axis=-1, keepdims=True)
            dev = o - mu
            var = jnp.mean(dev * dev, axis=-1, keepdims=True)
            on = dev * lax.rsqrt(var + LN_EPS) * gain_ref[...]
            y_ref[sl, :] = (_silu(g_ref[sl, :].astype(F32)) * on).astype(BF16)

    pl.when(blk < N_CTX_BLOCKS)(functools.partial(block, T_CTX // CHUNK, False))
    pl.when(blk >= N_CTX_BLOCKS)(functools.partial(block, T_LAT // CHUNK, True))


def _retention(z, cos_t, sin_t, decay_logit_b, gain, s0, layer, states):
    qcol0 = (3 * D_CONV + D_POOL) // HEAD_DIM
    n_blocks = M_TOK // RET_ROWS

    def zspec(k):
        return pl.BlockSpec((RET_ROWS, HEAD_DIM), lambda h, b: (b, qcol0 + k * N_RET_HEADS + h))

    def lat_seq(b):
        return jnp.maximum(b - N_CTX_BLOCKS, 0)

    def ctx_block(b):
        return jnp.minimum(b, N_CTX_BLOCKS - 1)

    n_chunks = RET_ROWS // CHUNK
    in_specs = [
        zspec(0), zspec(1), zspec(2), zspec(3),
        pl.BlockSpec((RET_ROWS, HEAD_DIM), lambda h, b: (0, 0)),
        pl.BlockSpec((RET_ROWS, HEAD_DIM), lambda h, b: (0, 0)),
        pl.BlockSpec((2, None, 1, HEAD_DIM), lambda h, b: (0, h, 0, 0)),
        pl.BlockSpec((1, HEAD_DIM), lambda h, b: (0, h)),
        pl.BlockSpec((None, None, 2, None, HEAD_DIM, HEAD_DIM),
                     lambda h, b: (lat_seq(b), layer, 0, h, 0, 0)),
    ]
    args = [z, z, z, z, cos_t, sin_t, decay_logit_b, gain.reshape(1, D_RET), s0]
    if layer == 0:
        st_spec = pl.BlockSpec((SEQ_PER_CTX_BLOCK, DEPTH, 2, None, HEAD_DIM, HEAD_DIM),
                               lambda h, b: (ctx_block(b), 0, 0, h, 0, 0))
        aliases = {}
    else:
        st_spec = pl.BlockSpec((SEQ_PER_CTX_BLOCK, None, 2, None, HEAD_DIM, HEAD_DIM),
                               lambda h, b: (ctx_block(b), layer, 0, h, 0, 0))
        in_specs.append(pl.BlockSpec(memory_space=pl.ANY))
        args.append(states)
        aliases = {len(args) - 1: 1}
    return pl.pallas_call(
        functools.partial(_retention_kernel, layer=layer),
        grid=(N_RET_HEADS, n_blocks),
        in_specs=in_specs,
        out_specs=[pl.BlockSpec((RET_ROWS, HEAD_DIM), lambda h, b: (b, h)), st_spec],
        out_shape=[
            jax.ShapeDtypeStruct((M_TOK, D_RET), BF16),
            jax.ShapeDtypeStruct((N_CTX_SEQ, DEPTH, 2, N_RET_HEADS, HEAD_DIM, HEAD_DIM), F32),
        ],
        input_output_aliases=aliases,
        scratch_shapes=[
            pltpu.VMEM((5, CHUNK, CHUNK), F32),
            pltpu.VMEM((RET_ROWS, HEAD_DIM), BF16),
            pltpu.VMEM((n_chunks, 2 * HEAD_DIM, HEAD_DIM), F32),
            pltpu.VMEM((2, n_chunks, HEAD_DIM, HEAD_DIM), BF16),
        ],
        compiler_params=_cparams(("arbitrary", "arbitrary")),
        name="retention",
    )(*args)


def _layer_norm_rows(r, g, b):
    mu = jnp.mean(r, axis=-1, keepdims=True)
    dev = r - mu
    var = jnp.mean(dev * dev, axis=-1, keepdims=True)
    return dev * lax.rsqrt(var + LN_EPS) * g + b


def _top2_of4(vals):
    top1 = jnp.maximum(jnp.maximum(vals[0], vals[1]), jnp.maximum(vals[2], vals[3]))
    idx1 = jnp.where(vals[0] == top1, 0, jnp.where(vals[1] == top1, 1, jnp.where(vals[2] == top1, 2, 3)))
    neg = jnp.float32(-jnp.inf)
    rest = [jnp.where(idx1 == j, neg, vals[j]) for j in range(4)]
    top2 = jnp.maximum(jnp.maximum(rest[0], rest[1]), jnp.maximum(rest[2], rest[3]))
    idx2 = jnp.where(rest[0] == top2, 0, jnp.where(rest[1] == top2, 1, jnp.where(rest[2] == top2, 2, 3)))
    return top1, idx1, top2, idx2


def _store_token_tiles(ref, val, tok0=0):
    n = val.shape[0]
    base = tok0 * TOK_STRIDE
    for c in range(TOK_ROWS):
        ref[pl.ds(base + c, n, stride=TOK_STRIDE), :] = val[:, c * LANES:(c + 1) * LANES]
    ref[pl.ds(base + TOK_ROWS, n, stride=TOK_STRIDE), :] = jnp.zeros((n, LANES), val.dtype)


def _load_token_tiles(ref, tok0, n):
    return jnp.concatenate(
        [ref[pl.ds(tok0 * TOK_STRIDE + c, n, stride=TOK_STRIDE), :] for c in range(TOK_ROWS)], axis=1)


def _outproj_kernel(ycp_ref, yret_ref, x_ref, wo_hbm, gate1_ref, lng_ref, lnb_ref, sc2_ref, sh2_ref,
                    wr_ref, rb_ref, x1_ref, h2_ref, ei_ref, ewt_ref, rk_ref, cnt_ref,
                    carry_scr, wo_ref, wstage, wsem, wr_hi, wr_lo, *, layer):
    i = pl.program_id(0)

    @pl.when(i == 0)
    def _():
        carry_scr[...] = jnp.zeros_like(carry_scr)
        wr = wr_ref[...]
        hi = wr.astype(BF16)
        wr_hi[...] = hi
        wr_lo[...] = (wr - hi.astype(F32)).astype(BF16)
        rows = wstage.shape[1]
        n_chunks = D_MODEL // rows

        def chunk_copy(c):
            return pltpu.make_async_copy(wo_hbm.at[layer, pl.ds(c * rows, rows), :], wstage.at[c % 2],
                                         wsem.at[c % 2])
        chunk_copy(0).start()
        for c in range(n_chunks):
            if c + 1 < n_chunks:
                chunk_copy(c + 1).start()
            chunk_copy(c).wait()
            wo_ref[c * rows:(c + 1) * rows, :] = wstage[c % 2].astype(BF16)

    s_i = lax.broadcasted_iota(I32, (SUB_OP, SUB_OP), 0)
    t_i = lax.broadcasted_iota(I32, (SUB_OP, SUB_OP), 1)
    tri = jnp.where(s_i < t_i, 1.0, 0.0).astype(BF16)
    half_k = D_CONV + D_POOL
    n_sub = x_ref.shape[0] // SUB_OP
    ys = []
    for sub in range(n_sub):
        rows_sl = slice(sub * SUB_OP, (sub + 1) * SUB_OP)
        y = jnp.dot(ycp_ref[rows_sl, :], wo_ref[0:half_k, :], preferred_element_type=F32)
        ys.append(y + jnp.dot(yret_ref[rows_sl, :], wo_ref[half_k:, :], preferred_element_type=F32))
    lts = [_ln_router_subtile(sub, ys[sub], x_ref, gate1_ref, lng_ref, lnb_ref, sc2_ref, sh2_ref,
                              wr_hi, wr_lo, x1_ref, h2_ref) for sub in range(n_sub)]
    carry = carry_scr[:, 0:1]
    for sub in range(n_sub):
        carry = _route_subtile(sub, lts[sub], carry, tri, rb_ref, ei_ref, ewt_ref, rk_ref)
    carry_scr[...] = jnp.broadcast_to(carry, carry_scr.shape)
    cnt_ref[...] = carry_scr[...]


def _ln_router_subtile(sub, y, x_ref, gate1_ref, lng_ref, lnb_ref, sc2_ref, sh2_ref, wr_hi, wr_lo,
                       x1_ref, h2_ref):
    tm = SUB_OP
    rows_sl = slice(sub * tm, (sub + 1) * tm)
    x1 = _layer_norm_rows(DEEPNORM_ALPHA * x_ref[rows_sl, :] + gate1_ref[...] * y,
                          lng_ref[...], lnb_ref[...])
    x1_ref[rows_sl, :] = x1
    h2 = x1 * (1.0 + sc2_ref[...]) + sh2_ref[...]
    _store_token_tiles(h2_ref, h2, tok0=sub * tm)

    h_hi = h2.astype(BF16)
    h_lo = (h2 - h_hi.astype(F32)).astype(BF16)
    logits = (jnp.dot(h_hi, wr_hi[...], preferred_element_type=F32)
              + jnp.dot(h_lo, wr_hi[...], preferred_element_type=F32)
              + jnp.dot(h_hi, wr_lo[...], preferred_element_type=F32))
    return logits.T


def _route_subtile(sub, lt, carry, tri, rb_ref, ei_ref, ewt_ref, rk_ref):
    tm = SUB_OP
    rows_sl = slice(sub * tm, (sub + 1) * tm)
    rows = [lt[e:e + 1, :] for e in range(N_EXPERTS)]

    mx = rows[0]
    for e in range(1, N_EXPERTS):
        mx = jnp.maximum(mx, rows[e])
    ex = [jnp.exp(r - mx) for r in rows]
    den = ex[0]
    for e in range(1, N_EXPERTS):
        den = den + ex[e]
    score = [x / den for x in ex]
    biased = [score[e] + rb_ref[e] for e in range(N_EXPERTS)]

    best = None
    for gi in range(N_EXPERT_GROUPS):
        t1, i1, t2, i2 = _top2_of4(biased[gi * EXPERTS_PER_GROUP:(gi + 1) * EXPERTS_PER_GROUP])
        gs = t1 + t2
        e1 = gi * EXPERTS_PER_GROUP + i1
        e2 = gi * EXPERTS_PER_GROUP + i2
        if best is None:
            best = (gs, e1, e2)
        else:
            take = gs > best[0]
            best = (jnp.where(take, gs, best[0]), jnp.where(take, e1, best[1]), jnp.where(take, e2, best[2]))
    _, e1, e2 = best
    zero = jnp.zeros_like(score[0])
    w1 = zero
    w2 = zero
    for e in range(N_EXPERTS):
        w1 = w1 + jnp.where(e1 == e, score[e], 0.0)
        w2 = w2 + jnp.where(e2 == e, score[e], 0.0)
    wsum = w1 + w2
    ei_ref[0:1, rows_sl] = e1
    ei_ref[1:2, rows_sl] = e2
    ewt_ref[rows_sl, :] = jnp.concatenate(
        [w1 / wsum, w2 / wsum, jnp.zeros((LANES - 2, tm), F32)], axis=0).T

    onehot = jnp.concatenate(
        [jnp.where((e1 == e) | (e2 == e), 1.0, 0.0) for e in range(N_EXPERTS)], axis=0)
    prefix = jnp.dot(onehot.astype(BF16), tri, preferred_element_type=F32) + carry
    r1 = zero
    r2 = zero
    for e in range(N_EXPERTS):
        r1 = r1 + jnp.where(e1 == e, prefix[e:e + 1, :], 0.0)
        r2 = r2 + jnp.where(e2 == e, prefix[e:e + 1, :], 0.0)
    rk_ref[0:1, rows_sl] = r1.astype(I32)
    rk_ref[1:2, rows_sl] = r2.astype(I32)
    return carry + jnp.sum(onehot, axis=1, keepdims=True)


def _out_proj(ycp, yret, x, w_out, layer, ada5, ln_g, ln_b, w_router_pad, router_bias):
    tm = TM_OP
    ada_spec = lambda chunk: pl.BlockSpec(
        (None, None, 1, D_MODEL), lambda i: (_ada_row(i, tm), chunk, 0, 0))
    vec_spec = pl.BlockSpec((1, D_MODEL), lambda i: (0, 0))
    route_spec = pl.BlockSpec((2, tm), lambda i: (0, i))
    return pl.pallas_call(
        functools.partial(_outproj_kernel, layer=layer),
        grid=(M_TOK // tm,),
        in_specs=[
            pl.BlockSpec((tm, D_CONV + D_POOL), lambda i: (i, 0)),
            pl.BlockSpec((tm, D_RET), lambda i: (i, 0)),
            pl.BlockSpec((tm, D_MODEL), lambda i: (i, 0)),
            pl.BlockSpec(memory_space=pl.ANY),
            ada_spec(2),
            vec_spec, vec_spec,
            ada_spec(4),
            ada_spec(3),
            pl.BlockSpec((D_MODEL, LANES), lambda i: (0, 0)),
            pl.BlockSpec(memory_space=pltpu.SMEM),
        ],
        out_specs=[
            pl.BlockSpec((tm, D_MODEL), lambda i: (i, 0)),
            pl.BlockSpec((tm * TOK_STRIDE, LANES), lambda i: (i, 0)),
            route_spec,
            pl.BlockSpec((tm, LANES), lambda i: (i, 0)),
            route_spec,
            pl.BlockSpec((N_EXPERTS, LANES), lambda i: (0, 0)),
        ],
        out_shape=[
            jax.ShapeDtypeStruct((M_TOK, D_MODEL), F32),
            jax.ShapeDtypeStruct((M_TOK * TOK_STRIDE, LANES), F32),
            jax.ShapeDtypeStruct((2, M_TOK), I32),
            jax.ShapeDtypeStruct((M_TOK, LANES), F32),
            jax.ShapeDtypeStruct((2, M_TOK), I32),
            jax.ShapeDtypeStruct((N_EXPERTS, LANES), F32),
        ],
        scratch_shapes=[
            pltpu.VMEM((N_EXPERTS, LANES), F32),
            pltpu.VMEM((D_MODEL, D_MODEL), BF16),
            pltpu.VMEM((2, W_OUT_STAGE_ROWS, D_MODEL), F32),
            pltpu.SemaphoreType.DMA((2,)),
            pltpu.VMEM((D_MODEL, LANES), BF16),
            pltpu.VMEM((D_MODEL, LANES), BF16),
        ],
        compiler_params=_cparams(("arbitrary",)),
        name="out_proj_router",
    )(ycp, yret, x, w_out, ada5, ln_g.reshape(1, D_MODEL), ln_b.reshape(1, D_MODEL),
      ada5, ada5, w_router_pad, router_bias)


def _pos_kernel(ei_ref, rk_ref, cnt_ref, pos_ref):
    ei = ei_ref[...]
    pos = rk_ref[...]
    start = jnp.zeros((1, 1), F32)
    for e in range(N_EXPERTS):
        pos = pos + jnp.where(ei == e, start.astype(I32), 0)
        n_tiles = jnp.floor((cnt_ref[e:e + 1, 0:1] + (TM_EXP - 1.0)) * (1.0 / TM_EXP))
        start = start + n_tiles * TM_EXP
    pos_ref[...] = pos


def _pair_rows(ei, rk, cnt):
    return pl.pallas_call(
        _pos_kernel,
        out_shape=jax.ShapeDtypeStruct((2, M_TOK), I32),
        name="pair_rows",
    )(ei, rk, cnt)


def _route_kernel(cnt_ref, pos_ref, src_ref, te_ref, nv_ref):
    def zero(r, carry):
        src_ref[r] = 0
        return carry

    tile = jnp.int32(0)
    for e in range(N_EXPERTS):
        n_tiles = lax.shift_right_logical(cnt_ref[e] + (TM_EXP - 1), TM_EXP.bit_length() - 1)

        def mark(j, carry, e=e, tile=tile):
            te_ref[tile + j] = e
            return carry
        lax.fori_loop(0, n_tiles, mark, 0)
        lax.fori_loop(tile * TM_EXP + cnt_ref[e], (tile + n_tiles) * TM_EXP, zero, 0)
        tile = tile + n_tiles
    nv_ref[0] = tile
    last_expert = te_ref[tile - 1]

    def mark_unused(j, carry):
        te_ref[j] = last_expert
        return carry
    lax.fori_loop(tile, NT_EXP, mark_unused, 0)
    lax.fori_loop(tile * TM_EXP, NP_EXP, zero, 0)

    def place(t, carry):
        src_ref[pos_ref[t]] = t
        src_ref[pos_ref[M_TOK + t]] = t
        return carry
    lax.fori_loop(0, M_TOK, place, 0, unroll=16)


def _route_tables(cnt, pos):
    smem = pl.BlockSpec(memory_space=pltpu.SMEM)
    return pl.pallas_call(
        _route_kernel,
        in_specs=[smem, smem],
        out_specs=[smem, smem, smem],
        out_shape=[
            jax.ShapeDtypeStruct((NP_EXP,), I32),
            jax.ShapeDtypeStruct((NT_EXP,), I32),
            jax.ShapeDtypeStruct((1,), I32),
        ],
        name="route_tables",
    )(cnt, pos)


GATHER_PRIORITY = 1


def _row_gather_start(src_hbm, buf, sem, idx_ref, base, n_tok, tok0=0, priority=0):
    for r in range(n_tok):
        pltpu.make_async_copy(src_hbm.at[pl.ds(idx_ref[base + r] * TOK_STRIDE, TOK_ROWS), :],
                              buf.at[pl.ds((tok0 + r) * TOK_STRIDE, TOK_ROWS), :], sem).start(r % 2)


def _row_gather_wait(src_hbm, buf, sem):
    n_rows = buf.shape[0] // TOK_STRIDE * TOK_ROWS
    pltpu.make_async_copy(src_hbm.at[pl.ds(0, n_rows), :], buf.at[pl.ds(0, n_rows), :], sem).wait()


N_XBUF = 4


def _experts_kernel(te_ref, nv_ref, src_ref, h2_hbm, wg_ref, wu_ref, wd_ref, o_ref, *scratch):
    bufs = scratch[:N_XBUF]
    gsem, wg_bf, wu_bf, wd_bf = scratch[N_XBUF:]
    i = pl.program_id(0)
    n_valid = nv_ref[0]

    def start_tile(tile, slot):
        tile = jnp.minimum(tile, NT_EXP - 1)
        _row_gather_start(h2_hbm, bufs[slot], gsem.at[slot], src_ref, tile * TM_EXP, TM_EXP,
                          priority=GATHER_PRIORITY)

    @pl.when(i == 0)
    def _():
        for t in range(N_XBUF - 1):
            start_tile(t, t)

    @pl.when((i < n_valid) & ((i == 0) | (te_ref[i] != te_ref[jnp.maximum(i - 1, 0)])))
    def _():
        wg_bf[...] = wg_ref[...].astype(BF16)
        wu_bf[...] = wu_ref[...].astype(BF16)
        wd_bf[...] = wd_ref[...].astype(BF16)

    def step(cur):
        _row_gather_wait(h2_hbm, bufs[cur], gsem.at[cur])
        start_tile(i + N_XBUF - 1, (cur + N_XBUF - 1) % N_XBUF)
        x = _load_token_tiles(bufs[cur], 0, TM_EXP).astype(BF16)
        g = jnp.dot(x, wg_bf[...], preferred_element_type=F32)
        u = jnp.dot(x, wu_bf[...], preferred_element_type=F32)
        a = (_silu(g) * u).astype(BF16)
        _store_token_tiles(o_ref, jnp.dot(a, wd_bf[...], preferred_element_type=F32))

        @pl.when(i == n_valid - 1)
        def _():
            for ahead in range(1, N_XBUF):
                slot = (cur + ahead) % N_XBUF
                _row_gather_wait(h2_hbm, bufs[slot], gsem.at[slot])

    for cur in range(N_XBUF):
        pl.when((i < n_valid) & (i % N_XBUF == cur))(functools.partial(step, cur))

    @pl.when(i >= n_valid)
    def _():
        o_ref[...] = jnp.zeros_like(o_ref)


def _experts(h2, tile_expert, n_valid, src_tok, wg, wu, wd, layer):
    grid_spec = pltpu.PrefetchScalarGridSpec(
        num_scalar_prefetch=3,
        grid=(NT_EXP,),
        in_specs=[
            pl.BlockSpec(memory_space=pl.ANY),
            pl.BlockSpec((None, None, D_MODEL, D_EXPERT), lambda i, te, nv, src: (layer, te[i], 0, 0)),
            pl.BlockSpec((None, None, D_MODEL, D_EXPERT), lambda i, te, nv, src: (layer, te[i], 0, 0)),
            pl.BlockSpec((None, None, D_EXPERT, D_MODEL), lambda i, te, nv, src: (layer, te[i], 0, 0)),
        ],
        out_specs=pl.BlockSpec((TM_EXP * TOK_STRIDE, LANES), lambda i, te, nv, src: (i, 0)),
        scratch_shapes=[pltpu.VMEM((TM_EXP * TOK_STRIDE, LANES), F32) for _ in range(N_XBUF)] + [
            pltpu.SemaphoreType.DMA((N_XBUF,)),
            pltpu.VMEM((D_MODEL, D_EXPERT), BF16),
            pltpu.VMEM((D_MODEL, D_EXPERT), BF16),
            pltpu.VMEM((D_EXPERT, D_MODEL), BF16),
        ],
    )
    return pl.pallas_call(
        _experts_kernel,
        grid_spec=grid_spec,
        out_shape=jax.ShapeDtypeStruct((NP_EXP * TOK_STRIDE, LANES), F32),
        compiler_params=_cparams(("arbitrary",)),
        name="experts",
    )(tile_expert, n_valid, src_tok, h2, wg, wu, wd)


N_RBUF = 3


def _final_kernel(*refs, emit_h):
    n_out = 2
    if emit_h:
        (pos_ref, x1_ref, ys_hbm, ewt_ref, gate2_ref, lng_ref, lnb_ref, sc1_ref, sh1_ref,
         x_ref, h_ref) = refs[:9 + n_out]
    else:
        (pos_ref, x1_ref, ys_hbm, ewt_ref, gate2_ref, lng_ref, lnb_ref, xc_ref, xl_ref) = refs[:7 + n_out]
    bufs = refs[-N_RBUF - 1:-1]
    sem = refs[-1]
    tm = TM_OUT
    i = pl.program_id(0)
    n_blocks = M_TOK // tm

    def start(tile, slot):
        tile = jnp.minimum(tile, n_blocks - 1)
        _row_gather_start(ys_hbm, bufs[slot], sem.at[slot], pos_ref, tile * tm, tm, tok0=0)
        _row_gather_start(ys_hbm, bufs[slot], sem.at[slot], pos_ref, M_TOK + tile * tm, tm, tok0=tm)

    @pl.when(i == 0)
    def _():
        for t in range(N_RBUF - 1):
            start(t, t)

    def step(cur):
        buf = bufs[cur]
        _row_gather_wait(ys_hbm, buf, sem.at[cur])
        start(i + N_RBUF - 1, (cur + N_RBUF - 1) % N_RBUF)
        w = ewt_ref[...]
        y2 = w[:, 0:1] * _load_token_tiles(buf, 0, tm) + w[:, 1:2] * _load_token_tiles(buf, tm, tm)
        x2 = _layer_norm_rows(DEEPNORM_ALPHA * x1_ref[...] + gate2_ref[...] * y2,
                              lng_ref[...], lnb_ref[...])
        if emit_h:
            x_ref[...] = x2
            h_ref[...] = (x2 * (1.0 + sc1_ref[...]) + sh1_ref[...]).astype(BF16)
        else:
            @pl.when(i < M_CTX // tm)
            def _():
                xc_ref[...] = x2

            @pl.when(i >= M_CTX // tm)
            def _():
                xl_ref[...] = x2

        @pl.when(i == n_blocks - 1)
        def _():
            for ahead in range(1, N_RBUF):
                slot = (cur + ahead) % N_RBUF
                _row_gather_wait(ys_hbm, bufs[slot], sem.at[slot])

    for cur in range(N_RBUF):
        pl.when(i % N_RBUF == cur)(functools.partial(step, cur))


def _final(x1, ys, pos, ewt, ada5, ln_g, ln_b, ada5_next):
    tm = TM_OUT
    emit_h = ada5_next is not None
    vec_spec = pl.BlockSpec((1, D_MODEL), lambda i, pos: (0, 0))
    ada_spec = lambda chunk: pl.BlockSpec(
        (None, None, 1, D_MODEL), lambda i, pos: (_ada_row(i, tm), chunk, 0, 0))
    in_specs = [
        pl.BlockSpec((tm, D_MODEL), lambda i, pos: (i, 0)),
        pl.BlockSpec(memory_space=pl.ANY),
        pl.BlockSpec((tm, LANES), lambda i, pos: (i, 0)),
        ada_spec(5),
        vec_spec, vec_spec,
    ]
    args = [pos, x1, ys, ewt, ada5, ln_g.reshape(1, D_MODEL), ln_b.reshape(1, D_MODEL)]
    if emit_h:
        in_specs += [ada_spec(1), ada_spec(0)]
        args += [ada5_next, ada5_next]
        row_spec = pl.BlockSpec((tm, D_MODEL), lambda i, pos: (i, 0))
        out_specs = [row_spec, row_spec]
        out_shape = [jax.ShapeDtypeStruct((M_TOK, D_MODEL), F32),
                     jax.ShapeDtypeStruct((M_TOK, D_MODEL), BF16)]
    else:
        out_specs = [
            pl.BlockSpec((tm, D_MODEL), lambda i, pos: (_ctx_block(i, tm), 0)),
            pl.BlockSpec((tm, D_MODEL), lambda i, pos: (_lat_block(i, tm), 0)),
        ]
        out_shape = [jax.ShapeDtypeStruct((M_CTX, D_MODEL), F32),
                     jax.ShapeDtypeStruct((M_LAT, D_MODEL), F32)]
    grid_spec = pltpu.PrefetchScalarGridSpec(
        num_scalar_prefetch=1,
        grid=(M_TOK // tm,),
        in_specs=in_specs,
        out_specs=out_specs,
        scratch_shapes=[pltpu.VMEM((2 * tm * TOK_STRIDE, LANES), F32) for _ in range(N_RBUF)] + [
            pltpu.SemaphoreType.DMA((N_RBUF,)),
        ],
    )
    return pl.pallas_call(
        functools.partial(_final_kernel, emit_h=emit_h),
        grid_spec=grid_spec,
        out_shape=out_shape,
        compiler_params=_cparams(("arbitrary",)),
        name="final_ln",
    )(*args)


def _rope_tables():
    rows = T_LAT // GRID_W
    row = jnp.repeat(jnp.arange(rows), GRID_W).astype(F32)
    col = jnp.tile(jnp.arange(GRID_W), rows).astype(F32)
    n_freq = HEAD_DIM // 4
    inv_freq = ROPE_BASE ** (-jnp.arange(n_freq, dtype=F32) / n_freq)
    ang = jnp.concatenate([row[:, None] * inv_freq[None], col[:, None] * inv_freq[None]], axis=-1)
    cos, sin = jnp.cos(ang), jnp.sin(ang)
    return jnp.concatenate([cos, cos], axis=-1), jnp.concatenate([-sin, sin], axis=-1)


def kernel(x_prompt, x_sample, state_retention, c, c_ctx, w_ada, b_ada, w_in, w_out, conv_w, pool_w,
           pool_scale, ret_decay_logit, ret_gn_gain, ln1_g, ln1_b, ln2_g, ln2_b, w_router, router_bias,
           w_gate, w_up, w_down):
    x_ctx = x_prompt.reshape(M_CTX, D_MODEL)
    x_lat = x_sample.reshape(M_LAT, D_MODEL)
    c_all = jnp.concatenate(
        [c_ctx[None, :], c, jnp.zeros((ADA_ROWS - 1 - N_LAT_SEQ, D_MODEL), F32)], axis=0)
    ada = _ada_table(c_all, w_ada, b_ada).reshape(DEPTH, ADA_ROWS, 6, 1, D_MODEL)

    cos_lat, sin_lat = _rope_tables()
    w_router_pad = jnp.pad(w_router, ((0, 0), (0, LANES - N_EXPERTS)))

    states = None
    x, h = _modulate(x_ctx, x_lat, ada[0])
    for l in range(DEPTH):
        ada5 = ada[l]
        z = _in_proj(h, w_in, l)

        ycp = _conv_pool(z, conv_w[l], pool_w[l], pool_scale[l])

        dl = jnp.broadcast_to(ret_decay_logit[l][:, :, None, None], (2, N_RET_HEADS, 1, HEAD_DIM))
        yret, states = _retention(z, cos_lat, sin_lat, dl, ret_gn_gain[l], state_retention, l, states)

        x1, h2, ei, ewt, rk, cnt = _out_proj(ycp, yret, x, w_out, l, ada5,
                                             ln1_g[l], ln1_b[l], w_router_pad, router_bias)
        pos = _pair_rows(ei, rk, cnt).reshape(-1)
        src_tok, te, n_valid = _route_tables(cnt[:, 0].astype(I32), pos)
        ys = _experts(h2, te, n_valid, src_tok, w_gate, w_up, w_down, l)
        if l + 1 < DEPTH:
            x, h = _final(x1, ys, pos, ewt, ada5, ln2_g[l], ln2_b[l], ada[l + 1])
        else:
            x_ctx, x_lat = _final(x1, ys, pos, ewt, ada5, ln2_g[l], ln2_b[l], None)

    y_prompt = x_ctx.reshape(N_CTX_SEQ, T_CTX, D_MODEL)
    y_sample = x_lat.reshape(N_LAT_SEQ, T_LAT, D_MODEL)
    return y_prompt, y_sample, states
```
